```python
import math
import jax
import jax.numpy as jnp
from jax import lax
import numpy as np

D_MODEL = 1024
BATCH = 8
SEQ = 2048
DEPTH = 4
DEC_BATCH = 128
DEC_SEQ = 8
PAST_LEN = 2048
PAGE_SIZE = 128

N_MEM = 256
W_A = D_MODEL // 4
W_B = D_MODEL // 2
W_C = D_MODEL - W_A - W_B
W_MIX = W_A + W_B + W_C
H_A = 4
DH_A = W_A // H_A
MLSTM_CHUNK = 64
H_B = 4
DV_B = W_B // H_B
DK_B = DV_B // 2
Q_BLOCK = 128
GC = 16
G_C = W_C // GC
P_C = 64
H_X = 4
DH_X = D_MODEL // H_X
D_FF = 128 * ((8 * D_MODEL // 3 + 127) // 128)
ALPHA = (2.0 * DEPTH) ** 0.25
BETA = (8.0 * DEPTH) ** -0.25
LN_EPS = 1e-5
NORM_EPS = 1e-6

OFF_AQ = 0
OFF_AK = OFF_AQ + W_A
OFF_AV = OFF_AK + W_A
OFF_AO = OFF_AV + W_A
OFF_AI = OFF_AO + W_A
OFF_AF = OFF_AI + H_A
OFF_BQ = OFF_AF + H_A
OFF_BK = OFF_BQ + H_B * 2 * DK_B
OFF_BV = OFF_BK + H_B * 2 * DK_B
OFF_CU = OFF_BV + W_B
N_IN = OFF_CU + W_C

kernel_name = 'hybrid_mlstm_diffattn_s5_step'


def layer_norm(x, g, b):
    xf = x.astype(jnp.float32)
    mu = jnp.mean(xf, -1, keepdims=True)
    xc = xf - mu
    var = jnp.mean(xc * xc, -1, keepdims=True)
    return (xc * lax.rsqrt(var + LN_EPS) * g.astype(jnp.float32) + b.astype(jnp.float32)).astype(x.dtype)


def swiglu(x, wg, wu, wd):
    return (jax.nn.silu(x @ wg) * (x @ wu)) @ wd


def head_rms_norm(h, g):
    hf = h.astype(jnp.float32)
    hf = hf * lax.rsqrt(jnp.mean(hf * hf, -1, keepdims=True) + NORM_EPS)
    return hf.reshape(h.shape[:-2] + (-1,)) * g.astype(jnp.float32)


def _mlstm_chunk(carry, inp):
    c, n, m = carry
    q, k, v, ig, lf = inp
    L = q.shape[2]
    b = jnp.cumsum(lf, axis=-1)
    causal = jnp.tril(jnp.ones((L, L), dtype=bool))
    dmat = jnp.where(causal, b[..., :, None] - b[..., None, :] + ig[..., None, :], -jnp.inf)
    inter = b + m[..., None]
    m_row = jnp.maximum(inter, jnp.max(dmat, axis=-1))
    s = jnp.einsum('bhtd,bhsd->bhts', q, k) * jnp.exp(dmat - m_row[..., None])
    w_inter = jnp.exp(inter - m_row)
    num = jnp.einsum('bhts,bhsd->bhtd', s, v) + w_inter[..., None] * jnp.einsum('bhvk,bhtk->bhtv', c, q)
    den = jnp.sum(s, axis=-1) + w_inter * jnp.einsum('bhk,bhtk->bht', n, q)
    h = num / jnp.maximum(jnp.abs(den), jnp.exp(-m_row))[..., None]
    m_new = m_row[..., -1]
    decay = jnp.exp(b[..., -1] + m - m_new)
    wk = jnp.exp(b[..., -1:] - b + ig - m_new[..., None])
    c_new = decay[..., None, None] * c + jnp.einsum('bhs,bhsv,bhsk->bhvk', wk, v, k)
    n_new = decay[..., None] * n + jnp.einsum('bhs,bhsk->bhk', wk, k)
    return (c_new, n_new, m_new), h


def mlstm(q, k, v, ig, lf, c0, n0, m0):
    B, T, H, D = q.shape
    L = MLSTM_CHUNK if T % MLSTM_CHUNK == 0 else T
    nc = T // L

    def split_vec(a):
        return a.astype(jnp.float32).reshape(B, nc, L, H, D).transpose(1, 0, 3, 2, 4)

    def split_gate(a):
        return a.astype(jnp.float32).reshape(B, nc, L, H).transpose(1, 0, 3, 2)

    carry0 = (c0.astype(jnp.float32), n0.astype(jnp.float32), m0.astype(jnp.float32))
    (c1, n1, m1), h = lax.scan(_mlstm_chunk, carry0,
                               (split_vec(q), split_vec(k), split_vec(v), split_gate(ig), split_gate(lf)))
    h = h.transpose(1, 0, 3, 2, 4).reshape(B, T, H, D)
    return h, c1, n1, m1


def alibi_slopes():
    return 2.0 ** (-8.0 * jnp.arange(1, H_B + 1, dtype=jnp.float32) / H_B)


def diff_attend_block(q, k, v, q_pos, k_pos, lam):
    s = jnp.einsum('bqhcd,bkhcd->bchqk', q, k).astype(jnp.float32) * (DK_B ** -0.5)
    dist = (q_pos[:, None] - k_pos[None, :]).astype(jnp.float32)
    bias = jnp.where(dist >= 0, -alibi_slopes()[:, None, None] * dist, -jnp.inf)
    p = jax.nn.softmax(s + bias, axis=-1)
    a = p[:, 0] - lam * p[:, 1]
    return jnp.einsum('bhqk,bkhd->bqhd', a.astype(v.dtype), v)


def diff_attention(q, k, v, lam):
    B, Tq = q.shape[0], q.shape[1]
    Tk = k.shape[1]
    k_pos = jnp.arange(Tk, dtype=jnp.int32)
    q_pos = (Tk - Tq) + jnp.arange(Tq, dtype=jnp.int32)
    qb = Q_BLOCK if Tq % Q_BLOCK == 0 else Tq
    nb = Tq // qb
    qs = jnp.swapaxes(q.reshape((B, nb, qb) + q.shape[2:]), 0, 1)
    ps = q_pos.reshape(nb, qb)
    o = lax.map(lambda a: diff_attend_block(a[0], k, v, a[1], k_pos, lam), (qs, ps))
    return jnp.swapaxes(o, 0, 1).reshape(B, Tq, H_B, DV_B)


def s5_discretize(a_re, a_im, log_dt, b_re, b_im):
    a_re = a_re.astype(jnp.float32)
    a_im = a_im.astype(jnp.float32)
    dt = jnp.exp(log_dt.astype(jnp.float32))[:, None]
    mag = jnp.exp(a_re * dt)
    lb_re = mag * jnp.cos(a_im * dt)
    lb_im = mag * jnp.sin(a_im * dt)
    den = a_re * a_re + a_im * a_im
    xr = lb_re - 1.0
    fr = (xr * a_re + lb_im * a_im) / den
    fi = (lb_im * a_re - xr * a_im) / den
    b_re = b_re.astype(jnp.float32)
    b_im = b_im.astype(jnp.float32)
    bb_re = fr[..., None] * b_re - fi[..., None] * b_im
    bb_im = fr[..., None] * b_im + fi[..., None] * b_re
    return lb_re, lb_im, bb_re, bb_im


def _cmul_combine(e1, e2):
    a1r, a1i, b1r, b1i = e1
    a2r, a2i, b2r, b2i = e2
    return (a2r * a1r - a2i * a1i, a2r * a1i + a2i * a1r,
            a2r * b1r - a2i * b1i + b2r, a2r * b1i + a2i * b1r + b2i)


def s5_mixer(u, s0_re, s0_im, disc, c_re, c_im, d, glu_w, glu_b):
    lb_re, lb_im, bb_re, bb_im = disc
    B, T, _ = u.shape
    uf = u.astype(jnp.float32).reshape(B, T, G_C, GC)
    bu_re = jnp.einsum('gpc,btgc->btgp', bb_re, uf)
    bu_im = jnp.einsum('gpc,btgc->btgp', bb_im, uf)
    s0_re = s0_re.astype(jnp.float32)
    s0_im = s0_im.astype(jnp.float32)
    bu_re = bu_re.at[:, 0].add(lb_re * s0_re - lb_im * s0_im)
    bu_im = bu_im.at[:, 0].add(lb_re * s0_im + lb_im * s0_re)
    a_re = jnp.broadcast_to(lb_re, bu_re.shape)
    a_im = jnp.broadcast_to(lb_im, bu_im.shape)
    _, _, s_re, s_im = lax.associative_scan(_cmul_combine, (a_re, a_im, bu_re, bu_im), axis=1)
    y = (jnp.einsum('gcp,btgp->btgc', c_re.astype(jnp.float32), s_re)
         - jnp.einsum('gcp,btgp->btgc', c_im.astype(jnp.float32), s_im))
    y = y.reshape(B, T, W_C) + d.astype(jnp.float32) * u.astype(jnp.float32)
    z = jax.nn.gelu(y).astype(u.dtype) @ glu_w + glu_b
    out = z[..., :W_C] * jax.nn.sigmoid(z[..., W_C:])
    return out, s_re[:, -1], s_im[:, -1]


def token_mixer(h, l, P, k_past, v_past, c0, n0, m0, s0_re, s0_im, disc, lam_init):
    B, T, _ = h.shape
    z = h @ P['w_in'][l] + P['b_in'][l]
    qa = z[..., OFF_AQ:OFF_AK].reshape(B, T, H_A, DH_A)
    ka = z[..., OFF_AK:OFF_AV].reshape(B, T, H_A, DH_A) * (DH_A ** -0.5)
    va = z[..., OFF_AV:OFF_AO].reshape(B, T, H_A, DH_A)
    og = z[..., OFF_AO:OFF_AI]
    ig = z[..., OFF_AI:OFF_AF]
    lf = jax.nn.log_sigmoid(z[..., OFF_AF:OFF_BQ].astype(jnp.float32))
    ha, c1, n1, m1 = mlstm(qa, ka, va, ig, lf, c0, n0, m0)
    ya = jax.nn.sigmoid(og.astype(jnp.float32)) * head_rms_norm(ha, P['mlstm_norm_g'][l])
    qb = z[..., OFF_BQ:OFF_BK].reshape(B, T, H_B, 2, DK_B)
    kb_new = z[..., OFF_BK:OFF_BV].reshape(B, T, H_B, 2 * DK_B)
    vb_new = z[..., OFF_BV:OFF_CU].reshape(B, T, H_B, DV_B)
    if k_past is None:
        kb, vb = kb_new, vb_new
    else:
        kb = jnp.concatenate([k_past.astype(kb_new.dtype), kb_new], axis=1)
        vb = jnp.concatenate([v_past.astype(vb_new.dtype), vb_new], axis=1)
    lp = P['diff_lam'][l].astype(jnp.float32)
    lam = jnp.exp(jnp.dot(lp[0], lp[1])) - jnp.exp(jnp.dot(lp[2], lp[3])) + lam_init
    ob = diff_attention(qb, kb.reshape(B, kb.shape[1], H_B, 2, DK_B), vb, lam)
    yb = head_rms_norm(ob, P['diff_norm_g'][l]) * (1.0 - lam_init)
    yc, s1_re, s1_im = s5_mixer(z[..., OFF_CU:N_IN], s0_re, s0_im, disc,
                                P['ssm_c_re'][l], P['ssm_c_im'][l], P['ssm_d'][l],
                                P['ssm_glu_w'][l], P['ssm_glu_b'][l])
    y = jnp.concatenate([ya.astype(h.dtype), yb.astype(h.dtype), yc.astype(h.dtype)], axis=-1) @ P['w_out'][l]
    return y, (kb_new, vb_new, c1, n1, m1, s1_re, s1_im)


def cross_attention(x, mk, mv, wq, wo):
    B, T, _ = x.shape
    q = (x @ wq).reshape(B, T, H_X, DH_X)
    s = jnp.einsum('bthd,bmhd->bhtm', q, mk.astype(q.dtype)).astype(jnp.float32) * (DH_X ** -0.5)
    p = jax.nn.softmax(s, axis=-1)
    o = jnp.einsum('bhtm,bmhd->bthd', p.astype(x.dtype), mv.astype(x.dtype))
    return o.reshape(B, T, D_MODEL) @ wo


def trunk_layer(x, l, P, mk, mv, k_past, v_past, c0, n0, m0, s0_re, s0_im, disc):
    lam_init = 0.8 - 0.6 * math.exp(-0.3 * l)
    g = P['ln_g'][l]
    b = P['ln_b'][l]
    x = layer_norm(ALPHA * x + 0.5 * swiglu(x, P['ffn1_wg'][l], P['ffn1_wu'][l], P['ffn1_wd'][l]), g[0], b[0])
    mix, st = token_mixer(x, l, P, k_past, v_past, c0, n0, m0, s0_re, s0_im, disc, lam_init)
    x = layer_norm(ALPHA * x + mix, g[1], b[1])
    x = layer_norm(ALPHA * x + cross_attention(x, mk, mv, P['cross_wq'][l], P['cross_wo'][l]), g[2], b[2])
    x = layer_norm(ALPHA * x + 0.5 * swiglu(x, P['ffn2_wg'][l], P['ffn2_wu'][l], P['ffn2_wd'][l]), g[3], b[3])
    return x, st


def setup_inputs(seed: int = 0) -> dict:
    key = jax.random.key(seed)
    ks = iter(list(jax.random.split(key, 64)))

    def nrm(shape, scale):
        return jax.random.normal(next(ks), shape, jnp.float32) * scale

    n_pages = PAST_LEN // PAGE_SIZE
    n_used = DEC_BATCH * n_pages
    n_pool = n_used + n_used // 4
    page_table = jax.random.permutation(next(ks), n_pool)[:n_used].reshape(DEC_BATCH, n_pages).astype(jnp.int32)
    log_dt = math.log(1e-3) + jax.random.uniform(next(ks), (DEPTH, G_C), jnp.float32) * (math.log(1e-1) - math.log(1e-3))
    b_in = nrm((DEPTH, N_IN), 0.02).at[:, OFF_AF:OFF_BQ].add(jnp.linspace(3.0, 6.0, H_A))
    a_im = jnp.broadcast_to(jnp.pi * jnp.arange(P_C, dtype=jnp.float32), (DEPTH, G_C, P_C)) + nrm((DEPTH, G_C, P_C), 0.01)
    return {
        'x_prompt': nrm((BATCH, SEQ, D_MODEL), 1.0),
        'x_sample': nrm((DEC_BATCH, DEC_SEQ, D_MODEL), 1.0),
        'mem_prompt': nrm((BATCH, N_MEM, D_MODEL), 1.0),
        'cache_k': nrm((DEPTH, n_pool, PAGE_SIZE, H_B, 2 * DK_B), 1.0),
        'cache_v': nrm((DEPTH, n_pool, PAGE_SIZE, H_B, DV_B), 1.0),
        'page_table': page_table,
        'cache_mem_k': nrm((DEPTH, DEC_BATCH, N_MEM, H_X, DH_X), 1.0),
        'cache_mem_v': nrm((DEPTH, DEC_BATCH, N_MEM, H_X, DH_X), 1.0),
        'state_mlstm_c': nrm((DEPTH, DEC_BATCH, H_A, DH_A, DH_A), 1.0),
        'state_mlstm_n': nrm((DEPTH, DEC_BATCH, H_A, DH_A), 1.0),
        'state_mlstm_m': nrm((DEPTH, DEC_BATCH, H_A), 0.5),
        'state_ssm_re': nrm((DEPTH, DEC_BATCH, G_C, P_C), 0.1),
        'state_ssm_im': nrm((DEPTH, DEC_BATCH, G_C, P_C), 0.1),
        'ln_g': 1.0 + nrm((DEPTH, 4, D_MODEL), 0.02),
        'ln_b': nrm((DEPTH, 4, D_MODEL), 0.02),
        'ffn1_wg': nrm((DEPTH, D_MODEL, D_FF), D_MODEL ** -0.5),
        'ffn1_wu': nrm((DEPTH, D_MODEL, D_FF), D_MODEL ** -0.5),
        'ffn1_wd': nrm((DEPTH, D_FF, D_MODEL), BETA * D_FF ** -0.5),
        'ffn2_wg': nrm((DEPTH, D_MODEL, D_FF), D_MODEL ** -0.5),
        'ffn2_wu': nrm((DEPTH, D_MODEL, D_FF), D_MODEL ** -0.5),
        'ffn2_wd': nrm((DEPTH, D_FF, D_MODEL), BETA * D_FF ** -0.5),
        'w_in': nrm((DEPTH, D_MODEL, N_IN), D_MODEL ** -0.5),
        'b_in': b_in,
        'mlstm_norm_g': 1.0 + nrm((DEPTH, W_A), 0.02),
        'diff_lam': nrm((DEPTH, 4, DK_B), 0.1),
        'diff_norm_g': 1.0 + nrm((DEPTH, W_B), 0.02),
        'ssm_a_re': -0.5 + nrm((DEPTH, G_C, P_C), 0.01),
        'ssm_a_im': a_im,
        'ssm_log_dt': log_dt,
        'ssm_b_re': nrm((DEPTH, G_C, P_C, GC), (2.0 * GC) ** -0.5),
        'ssm_b_im': nrm((DEPTH, G_C, P_C, GC), (2.0 * GC) ** -0.5),
        'ssm_c_re': nrm((DEPTH, G_C, GC, P_C), (2.0 * P_C) ** -0.5),
        'ssm_c_im': nrm((DEPTH, G_C, GC, P_C), (2.0 * P_C) ** -0.5),
        'ssm_d': nrm((DEPTH, W_C), 1.0),
        'ssm_glu_w': nrm((DEPTH, W_C, 2 * W_C), W_C ** -0.5),
        'ssm_glu_b': nrm((DEPTH, 2 * W_C), 0.02),
        'w_out': nrm((DEPTH, W_MIX, D_MODEL), BETA * W_MIX ** -0.5),
        'cross_wq': nrm((DEPTH, D_MODEL, D_MODEL), D_MODEL ** -0.5),
        'cross_wk': nrm((DEPTH, D_MODEL, D_MODEL), D_MODEL ** -0.5),
        'cross_wv': nrm((DEPTH, D_MODEL, D_MODEL), D_MODEL ** -0.5),
        'cross_wo': nrm((DEPTH, D_MODEL, D_MODEL), BETA * D_MODEL ** -0.5),
    }


def reference(x_prompt, x_sample, mem_prompt, cache_k, cache_v, page_table, cache_mem_k, cache_mem_v,
              state_mlstm_c, state_mlstm_n, state_mlstm_m, state_ssm_re, state_ssm_im,
              ln_g, ln_b, ffn1_wg, ffn1_wu, ffn1_wd, ffn2_wg, ffn2_wu, ffn2_wd, w_in, b_in,
              mlstm_norm_g, diff_lam, diff_norm_g, ssm_a_re, ssm_a_im, ssm_log_dt, ssm_b_re, ssm_b_im,
              ssm_c_re, ssm_c_im, ssm_d, ssm_glu_w, ssm_glu_b, w_out, cross_wq, cross_wk, cross_wv, cross_wo):
    P = {'ln_g': ln_g, 'ln_b': ln_b, 'ffn1_wg': ffn1_wg, 'ffn1_wu': ffn1_wu, 'ffn1_wd': ffn1_wd,
         'ffn2_wg': ffn2_wg, 'ffn2_wu': ffn2_wu, 'ffn2_wd': ffn2_wd, 'w_in': w_in, 'b_in': b_in,
         'mlstm_norm_g': mlstm_norm_g, 'diff_lam': diff_lam, 'diff_norm_g': diff_norm_g,
         'ssm_c_re': ssm_c_re, 'ssm_c_im': ssm_c_im, 'ssm_d': ssm_d, 'ssm_glu_w': ssm_glu_w,
         'ssm_glu_b': ssm_glu_b, 'w_out': w_out, 'cross_wq': cross_wq, 'cross_wo': cross_wo}
    bp, bs = x_prompt.shape[0], x_sample.shape[0]
    past_len = page_table.shape[1] * PAGE_SIZE
    zc = jnp.zeros((bp, H_A, DH_A, DH_A), jnp.float32)
    zn = jnp.zeros((bp, H_A, DH_A), jnp.float32)
    zm = jnp.zeros((bp, H_A), jnp.float32)
    zs = jnp.zeros((bp, G_C, P_C), jnp.float32)
    yp, ys = x_prompt, x_sample
    pm_k, pm_v, p_st, s_st = [], [], [], []
    for l in range(DEPTH):
        disc = s5_discretize(ssm_a_re[l], ssm_a_im[l], ssm_log_dt[l], ssm_b_re[l], ssm_b_im[l])
        mk = (mem_prompt @ cross_wk[l]).reshape(bp, N_MEM, H_X, DH_X)
        mv = (mem_prompt @ cross_wv[l]).reshape(bp, N_MEM, H_X, DH_X)
        pm_k.append(mk)
        pm_v.append(mv)
        yp, st = trunk_layer(yp, l, P, mk, mv, None, None, zc, zn, zm, zs, zs, disc)
        p_st.append(st)
        k_past = cache_k[l][page_table].reshape(bs, past_len, H_B, 2 * DK_B)
        v_past = cache_v[l][page_table].reshape(bs, past_len, H_B, DV_B)
        ys, st = trunk_layer(ys, l, P, cache_mem_k[l], cache_mem_v[l], k_past, v_past,
                             state_mlstm_c[l], state_mlstm_n[l], state_mlstm_m[l],
                             state_ssm_re[l], state_ssm_im[l], disc)
        s_st.append(st)
    p_k, p_v, p_c, p_n, p_m, p_sr, p_si = [jnp.stack(a) for a in zip(*p_st)]
    s_k, s_v, s_c, s_n, s_m, s_sr, s_si = [jnp.stack(a) for a in zip(*s_st)]
    p_mk = jnp.stack(pm_k)
    p_mv = jnp.stack(pm_v)
    return (yp, ys, p_k, p_v, p_mk, p_mv, p_c, p_n, p_m, p_sr, p_si, s_k, s_v, s_c, s_n, s_m, s_sr, s_si)
```

```python
import functools
import math

import jax
import jax.numpy as jnp
from jax import lax
from jax.experimental import pallas as pl
from jax.experimental.pallas import tpu as pltpu

F32 = jnp.float32
BF16 = jnp.bfloat16

D_MODEL = 1024
PAGE_SIZE = 128
W_A = D_MODEL // 4
W_B = D_MODEL // 2
W_C = D_MODEL - W_A - W_B
H_A = 4
DH_A = W_A // H_A
MLSTM_CHUNK = 64
H_B = 4
DV_B = W_B // H_B
DK_B = DV_B // 2
GC = 16
G_C = W_C // GC
P_C = 64
H_X = 4
DH_X = D_MODEL // H_X
LN_EPS = 1e-5
NORM_EPS = 1e-6

OFF_AQ = 0
OFF_AK = OFF_AQ + W_A
OFF_AV = OFF_AK + W_A
OFF_AO = OFF_AV + W_A
OFF_AI = OFF_AO + W_A
OFF_AF = OFF_AI + H_A
OFF_BQ = OFF_AF + H_A
OFF_BK = OFF_BQ + H_B * 2 * DK_B
OFF_BV = OFF_BK + H_B * 2 * DK_B
OFF_CU = OFF_BV + W_B
N_IN = OFF_CU + W_C

V7X_LANES = 128
V7X_SUBLANES = 8
V7X_VMEM_LIMIT_BYTES = 56 * 1024 * 1024

TOKEN_TILE = 1024
FF_TILE = 256
PROJ_TILE = 512
ATTN_TILE = 512
XATTN_TILE = 512
S5_TILE = 512
PAGES_PER_STEP = 4
S5_SAMPLE_SEQS = 16

NEG_INF = float("-inf")


def _cparams(*sem):
    return pltpu.CompilerParams(dimension_semantics=sem, vmem_limit_bytes=V7X_VMEM_LIMIT_BYTES)


def _bdot(a, b):
    return jnp.dot(a.astype(BF16), b.astype(BF16), preferred_element_type=F32)


def _bdot_nt(a, b):
    return lax.dot_general(a.astype(BF16), b.astype(BF16), (((1,), (1,)), ((), ())),
                           preferred_element_type=F32)


def _layer_norm(y, g, b):
    mu = jnp.mean(y, axis=-1, keepdims=True)
    yc = y - mu
    var = jnp.mean(yc * yc, axis=-1, keepdims=True)
    return yc * lax.rsqrt(var + LN_EPS) * g + b


def _ffn_ln_kernel(x_ref, wg_ref, wu_ref, wd_ref, g_ref, b_ref, o_ref, xb_ref, acc_ref, *, alpha):
    j = pl.program_id(1)

    @pl.when(j == 0)
    def _():
        xb_ref[...] = x_ref[...].astype(BF16)
        acc_ref[...] = jnp.zeros_like(acc_ref)

    xb = xb_ref[...]
    hg = jnp.dot(xb, wg_ref[...], preferred_element_type=F32)
    hu = jnp.dot(xb, wu_ref[...], preferred_element_type=F32)
    h = (hg * jax.nn.sigmoid(hg)) * hu
    acc_ref[...] += jnp.dot(h.astype(BF16), wd_ref[...], preferred_element_type=F32)

    @pl.when(j == pl.num_programs(1) - 1)
    def _():
        y = alpha * x_ref[...] + 0.5 * acc_ref[...]
        o_ref[...] = _layer_norm(y, g_ref[...], b_ref[...])


def ffn_ln(x, wg, wu, wd, g, b, alpha):
    n, d = x.shape
    dff = wg.shape[1]
    tm, tf = TOKEN_TILE, FF_TILE
    return pl.pallas_call(
        functools.partial(_ffn_ln_kernel, alpha=alpha),
        grid=(n // tm, dff // tf),
        in_specs=[
            pl.BlockSpec((tm, d), lambda i, j: (i, 0)),
            pl.BlockSpec((d, tf), lambda i, j: (0, j)),
            pl.BlockSpec((d, tf), lambda i, j: (0, j)),
            pl.BlockSpec((tf, d), lambda i, j: (j, 0)),
            pl.BlockSpec((1, d), lambda i, j: (0, 0)),
            pl.BlockSpec((1, d), lambda i, j: (0, 0)),
        ],
        out_specs=pl.BlockSpec((tm, d), lambda i, j: (i, 0)),
        out_shape=jax.ShapeDtypeStruct((n, d), F32),
        scratch_shapes=[pltpu.VMEM((tm, d), BF16), pltpu.VMEM((tm, d), F32)],
        compiler_params=_cparams("parallel", "arbitrary"),
        name="ffn_ln",
    )(x, wg, wu, wd, g.reshape(1, d), b.reshape(1, d))


N_ZA = 4 * W_A
N_ZB = 3 * W_B
N_ZC = W_C
N_ZG = V7X_LANES
N_PROJ = N_ZA + N_ZB + N_ZC + N_ZG


def _proj_in_kernel(x_ref, w_ref, b_ref, za_ref, zb_ref, zc_ref, zg_ref):
    xb = x_ref[...].astype(BF16)
    off = 0
    for ref in (za_ref, zb_ref, zc_ref, zg_ref):
        width = ref.shape[1]
        ref[...] = jnp.dot(xb, w_ref[:, off:off + width], preferred_element_type=F32) + b_ref[:, off:off + width]
        off += width


def proj_in(x, w, b):
    n, d = x.shape
    tm = PROJ_TILE
    widths = (N_ZA, N_ZB, N_ZC, N_ZG)
    return pl.pallas_call(
        _proj_in_kernel,
        grid=(n // tm,),
        in_specs=[
            pl.BlockSpec((tm, d), lambda i: (i, 0)),
            pl.BlockSpec((d, N_PROJ), lambda i: (0, 0)),
            pl.BlockSpec((1, N_PROJ), lambda i: (0, 0)),
        ],
        out_specs=[pl.BlockSpec((tm, wd), lambda i: (i, 0)) for wd in widths],
        out_shape=[jax.ShapeDtypeStruct((n, wd), F32) for wd in widths],
        compiler_params=_cparams("parallel"),
        name="proj_in",
    )(x, w, b)


def _pack_w_in(w_in, b_in):
    def cols(a):
        pad = jnp.zeros(a.shape[:-1] + (N_ZG - 2 * H_A,), a.dtype)
        return jnp.concatenate([a[..., OFF_AQ:OFF_AI], a[..., OFF_BQ:OFF_CU], a[..., OFF_CU:N_IN],
                                a[..., OFF_AI:OFF_BQ], pad], axis=-1)
    return cols(w_in).astype(BF16), cols(b_in)[:, None, :]


def _stacked_matmul_kernel(x_ref, w_ref, o_ref):
    o_ref[...] = jnp.dot(x_ref[...].astype(BF16), w_ref[...], preferred_element_type=F32)


def stacked_matmul(x, w):
    m, k = x.shape
    s, _, n = w.shape
    tm = min(m, TOKEN_TILE)
    return pl.pallas_call(
        _stacked_matmul_kernel,
        grid=(m // tm, s),
        in_specs=[
            pl.BlockSpec((tm, k), lambda i, j: (i, 0)),
            pl.BlockSpec((None, k, n), lambda i, j: (j, 0, 0)),
        ],
        out_specs=pl.BlockSpec((None, tm, n), lambda i, j: (j, i, 0)),
        out_shape=jax.ShapeDtypeStruct((s, m, n), F32),
        compiler_params=_cparams("parallel", "arbitrary"),
        name="stacked_matmul",
    )(x, w)


def _out_ln_kernel(*refs, n_parts, alpha, has_next):
    parts = refs[:n_parts]
    w_ref, x_ref, g_ref, b_ref = refs[n_parts:n_parts + 4]
    rest = refs[n_parts + 4:]
    if has_next:
        wn_ref, o_ref, q_ref = rest
    else:
        (o_ref,) = rest
    acc = None
    off = 0
    for p in parts:
        width = p.shape[1]
        t = jnp.dot(p[...].astype(BF16), w_ref[off:off + width, :], preferred_element_type=F32)
        acc = t if acc is None else acc + t
        off += width
    y = _layer_norm(alpha * x_ref[...] + acc, g_ref[...], b_ref[...])
    o_ref[...] = y
    if has_next:
        q_ref[...] = jnp.dot(y.astype(BF16), wn_ref[...], preferred_element_type=F32)


def out_ln(parts, w, x, g, b, alpha, w_next=None):
    n, d = x.shape
    tm = PROJ_TILE
    has_next = w_next is not None
    in_specs = [pl.BlockSpec((tm, p.shape[1]), lambda i: (i, 0)) for p in parts]
    in_specs += [
        pl.BlockSpec(w.shape, lambda i: (0, 0)),
        pl.BlockSpec((tm, d), lambda i: (i, 0)),
        pl.BlockSpec((1, d), lambda i: (0, 0)),
        pl.BlockSpec((1, d), lambda i: (0, 0)),
    ]
    args = list(parts) + [w, x, g.reshape(1, d), b.reshape(1, d)]
    out_specs = [pl.BlockSpec((tm, d), lambda i: (i, 0))]
    out_shape = [jax.ShapeDtypeStruct((n, d), F32)]
    if has_next:
        in_specs.append(pl.BlockSpec(w_next.shape, lambda i: (0, 0)))
        args.append(w_next)
        out_specs.append(pl.BlockSpec((tm, w_next.shape[1]), lambda i: (i, 0)))
        out_shape.append(jax.ShapeDtypeStruct((n, w_next.shape[1]), F32))
    res = pl.pallas_call(
        functools.partial(_out_ln_kernel, n_parts=len(parts), alpha=alpha, has_next=has_next),
        grid=(n // tm,),
        in_specs=in_specs,
        out_specs=out_specs,
        out_shape=out_shape,
        compiler_params=_cparams("parallel"),
        name="out_ln",
    )(*args)
    return res if has_next else res[0]


def _log_sigmoid(x):
    return jnp.minimum(x, 0.0) - jnp.log1p(jnp.exp(-jnp.abs(x)))


def _mlstm_kernel(*refs, chunk, has_init):
    if has_init:
        (za_ref, zg_ref, g_ref, c0_ref, n0_ref, m0_ref, _prev,
         ya_ref, c1_ref, n1_ref, m1_ref, c_sc, n_sc, m_sc) = refs
    else:
        (za_ref, zg_ref, g_ref, ya_ref, c1_ref, n1_ref, m1_ref, c_sc, n_sc, m_sc) = refs
    L = chunk
    ci = pl.program_id(1)

    @pl.when(ci == 0)
    def _():
        if has_init:
            c_sc[...] = c0_ref[...]
            n_sc[...] = n0_ref[...]
            m_sc[...] = m0_ref[...]
        else:
            c_sc[...] = jnp.zeros_like(c_sc)
            n_sc[...] = jnp.zeros_like(n_sc)
            m_sc[...] = jnp.zeros_like(m_sc)

    mm = _bdot if L >= 2 * V7X_SUBLANES else (lambda a, b: jnp.dot(a, b, preferred_element_type=F32))
    mm_nt = _bdot_nt if L >= 2 * V7X_SUBLANES else (
        lambda a, b: lax.dot_general(a, b, (((1,), (1,)), ((), ())), preferred_element_type=F32))

    gates = zg_ref[...]
    lane = lax.broadcasted_iota(jnp.int32, gates.shape, 1)
    gl = jnp.where(lane < H_A, gates, _log_sigmoid(gates))
    row = lax.broadcasted_iota(jnp.int32, (L, L), 0)
    col = lax.broadcasted_iota(jnp.int32, (L, L), 1)
    causal = col <= row
    tril = causal.astype(F32)
    bcum = jnp.dot(tril, gl, precision=lax.Precision.HIGHEST, preferred_element_type=F32)
    mixed = jnp.where(lane < H_A, gates, bcum)
    sel_r = lax.broadcasted_iota(jnp.int32, (V7X_SUBLANES, V7X_LANES), 0)
    sel_c = lax.broadcasted_iota(jnp.int32, (V7X_SUBLANES, V7X_LANES), 1)
    sel = (sel_r == sel_c).astype(F32)
    rows = lax.dot_general(sel, mixed, (((1,), (1,)), ((), ())),
                           precision=lax.Precision.HIGHEST, preferred_element_type=F32)

    za = za_ref[...]
    outs = []
    for h in range(H_A):
        q = za[:, h * DH_A:(h + 1) * DH_A]
        k = za[:, W_A + h * DH_A:W_A + (h + 1) * DH_A] * (DH_A ** -0.5)
        v = za[:, 2 * W_A + h * DH_A:2 * W_A + (h + 1) * DH_A]
        og = za[:, 3 * W_A + h * DH_A:3 * W_A + (h + 1) * DH_A]
        ig_col = gates[:, h:h + 1]
        b_col = bcum[:, H_A + h:H_A + h + 1]
        ig_row = rows[h:h + 1, :]
        b_row = rows[H_A + h:H_A + h + 1, :]
        c = c_sc[h]
        n = n_sc[h]
        m_prev = m_sc[h]

        dmat = jnp.where(causal, b_col - b_row + ig_row, NEG_INF)
        inter = b_col + m_prev
        m_row = jnp.maximum(inter, jnp.max(dmat, axis=-1, keepdims=True))
        s = mm_nt(q, k) * jnp.exp(dmat - m_row)
        w_inter = jnp.exp(inter - m_row)
        num = mm(s, v) + w_inter * mm_nt(q, c)
        den = jnp.sum(s, axis=-1, keepdims=True) + w_inter * jnp.sum(q * n, axis=-1, keepdims=True)
        hh = num / jnp.maximum(jnp.abs(den), jnp.exp(-m_row))

        m_new = m_row[L - 1:L, :]
        b_last = b_col[L - 1:L, :]
        decay = jnp.exp(b_last + m_prev - m_new)
        wk = jnp.exp(b_last - b_col + ig_col - m_new)
        vw = wk * v
        if L >= 2 * V7X_SUBLANES:
            upd = lax.dot_general(vw.astype(BF16), k.astype(BF16), (((0,), (0,)), ((), ())),
                                  preferred_element_type=F32)
        else:
            upd = lax.dot_general(vw, k, (((0,), (0,)), ((), ())), preferred_element_type=F32)
        c_sc[h] = decay * c + upd
        n_sc[h] = decay * n + jnp.sum(wk * k, axis=0, keepdims=True)
        m_sc[h] = m_new

        hn = hh * lax.rsqrt(jnp.mean(hh * hh, axis=-1, keepdims=True) + NORM_EPS)
        outs.append(jax.nn.sigmoid(og) * hn)
    ya_ref[...] = jnp.concatenate(outs, axis=-1) * g_ref[...]

    @pl.when(ci == pl.num_programs(1) - 1)
    def _():
        c1_ref[...] = c_sc[...]
        n1_ref[...] = n_sc[...]
        m1_ref[...] = m_sc[...]


def mlstm(za, zg, norm_g, nb, t, row0, init=None, prev=None):
    n_tok = za.shape[0]
    L = MLSTM_CHUNK if t % MLSTM_CHUNK == 0 else t
    nc = t // L
    rb0 = row0 // L
    has_init = init is not None
    row_map = lambda b, c: (rb0 + b * nc + c, 0)
    st_map = lambda b, c: (b, 0, 0, 0)
    in_specs = [
        pl.BlockSpec((L, N_ZA), row_map),
        pl.BlockSpec((L, N_ZG), row_map),
        pl.BlockSpec((1, W_A), lambda b, c: (0, 0)),
    ]
    args = [za, zg, norm_g.reshape(1, W_A)]
    aliases = {}
    if has_init:
        c0, n0, m0 = init
        in_specs += [
            pl.BlockSpec((None, H_A, DH_A, DH_A), st_map),
            pl.BlockSpec((None, H_A, 1, DH_A), st_map),
            pl.BlockSpec((None, H_A, 1, 1), st_map),
            pl.BlockSpec(memory_space=pl.ANY),
        ]
        args += [c0, n0.reshape(nb, H_A, 1, DH_A), m0.reshape(nb, H_A, 1, 1), prev]
        aliases = {6: 0}
    out_specs = [
        pl.BlockSpec((L, W_A), row_map),
        pl.BlockSpec((None, H_A, DH_A, DH_A), st_map),
        pl.BlockSpec((None, H_A, 1, DH_A), st_map),
        pl.BlockSpec((None, H_A, 1, 1), st_map),
    ]
    out_shape = [
        jax.ShapeDtypeStruct((n_tok, W_A), F32),
        jax.ShapeDtypeStruct((nb, H_A, DH_A, DH_A), F32),
        jax.ShapeDtypeStruct((nb, H_A, 1, DH_A), F32),
        jax.ShapeDtypeStruct((nb, H_A, 1, 1), F32),
    ]
    ya, c1, n1, m1 = pl.pallas_call(
        functools.partial(_mlstm_kernel, chunk=L, has_init=has_init),
        grid=(nb, nc),
        in_specs=in_specs,
        out_specs=out_specs,
        out_shape=out_shape,
        scratch_shapes=[pltpu.VMEM((H_A, DH_A, DH_A), F32), pltpu.VMEM((H_A, 1, DH_A), F32),
                        pltpu.VMEM((H_A, 1, 1), F32)],
        input_output_aliases=aliases,
        compiler_params=_cparams("parallel", "arbitrary"),
        name="mlstm_init" if has_init else "mlstm",
    )(*args)
    return ya, c1, n1.reshape(nb, H_A, DH_A), m1.reshape(nb, H_A)


def _diff_lambda(lam_ref, lam_init):
    lp = lam_ref[...]
    d01 = jnp.sum(lp[0:1, :] * lp[1:2, :], axis=-1, keepdims=True)
    d23 = jnp.sum(lp[2:3, :] * lp[3:4, :], axis=-1, keepdims=True)
    return jnp.exp(d01) - jnp.exp(d23) + lam_init


def _alibi_slope(h):
    return jnp.where(h == 0, 2.0 ** -2, jnp.where(h == 1, 2.0 ** -4, jnp.where(h == 2, 2.0 ** -6, 2.0 ** -8)))


def _attn_prompt_kernel(lam_ref, g_ref, q_ref, k_ref, v_ref, o_ref, m_sc, l_sc, acc_sc, *, lam_init, tq, tk):
    h = pl.program_id(1)
    i = pl.program_id(2)
    j = pl.program_id(3)

    @pl.when(j == 0)
    def _():
        m_sc[...] = jnp.full_like(m_sc, NEG_INF)
        l_sc[...] = jnp.zeros_like(l_sc)
        acc_sc[...] = jnp.zeros_like(acc_sc)

    @pl.when(j <= i)
    def _():
        slope = _alibi_slope(h).astype(F32)
        row = lax.broadcasted_iota(jnp.int32, (tq, tk), 0)
        col = lax.broadcasted_iota(jnp.int32, (tq, tk), 1)
        rel = col - row + (j * tk - i * tq)
        bias = jnp.where(rel <= 0, slope * rel.astype(F32), NEG_INF)
        q = q_ref[...] * (DK_B ** -0.5)
        k = k_ref[...]
        vb = v_ref[...].astype(BF16)
        for c in range(2):
            s = _bdot_nt(q[:, c * DK_B:(c + 1) * DK_B], k[:, c * DK_B:(c + 1) * DK_B]) + bias
            m_old = m_sc[c]
            m_new = jnp.maximum(m_old, jnp.max(s, axis=-1, keepdims=True))
            alpha = jnp.exp(m_old - m_new)
            p = jnp.exp(s - m_new)
            l_sc[c] = alpha * l_sc[c] + jnp.sum(p, axis=-1, keepdims=True)
            acc_sc[c] = alpha * acc_sc[c] + jnp.dot(p.astype(BF16), vb, preferred_element_type=F32)
            m_sc[c] = m_new

    @pl.when(j == pl.num_programs(3) - 1)
    def _():
        lam = _diff_lambda(lam_ref, lam_init)
        ob = acc_sc[0] / l_sc[0] - lam * (acc_sc[1] / l_sc[1])
        on = ob * lax.rsqrt(jnp.mean(ob * ob, axis=-1, keepdims=True) + NORM_EPS)
        o_ref[...] = on * g_ref[...] * (1.0 - lam_init)


def diff_attn_prompt(zb, lam_p, norm_g, nb, t, lam_init):
    n_tok = zb.shape[0]
    tq = tk = ATTN_TILE
    nq = t // tq
    kv_map = lambda off: (lambda b, h, i, j: (b * nq + jnp.minimum(i, j), off + h))
    return pl.pallas_call(
        functools.partial(_attn_prompt_kernel, lam_init=lam_init, tq=tq, tk=tk),
        grid=(nb, H_B, nq, nq),
        in_specs=[
            pl.BlockSpec(lam_p.shape, lambda b, h, i, j: (0, 0)),
            pl.BlockSpec((1, DV_B), lambda b, h, i, j: (0, h)),
            pl.BlockSpec((tq, DV_B), lambda b, h, i, j: (b * nq + i, h)),
            pl.BlockSpec((tk, DV_B), kv_map(H_B)),
            pl.BlockSpec((tk, DV_B), kv_map(2 * H_B)),
        ],
        out_specs=pl.BlockSpec((tq, DV_B), lambda b, h, i, j: (b * nq + i, h)),
        out_shape=jax.ShapeDtypeStruct((n_tok, W_B), F32),
        scratch_shapes=[pltpu.VMEM((2, tq, 1), F32), pltpu.VMEM((2, tq, 1), F32),
                        pltpu.VMEM((2, tq, DV_B), F32)],
        compiler_params=_cparams("parallel", "parallel", "parallel", "arbitrary"),
        name="diff_attn_prompt",
    )(lam_p, norm_g.reshape(1, W_B), zb, zb, zb)


def _attn_sample_kernel(*refs, lam_init, past_len, t_new, pages_per_step):
    G = pages_per_step
    pt_ref, lam_ref, g_ref, q_ref, kn_ref, vn_ref = refs[:6]
    k_refs = refs[6:6 + G]
    v_refs = refs[6 + G:6 + 2 * G]
    _prev, o_ref, qbd_sc, m_sc, l_sc, acc_sc = refs[6 + 2 * G:]
    del pt_ref
    ps = pl.program_id(1)
    n_rows = 2 * H_B * t_new
    rows_per_head = 2 * t_new

    r_col = lax.broadcasted_iota(jnp.int32, (n_rows, 1), 0)
    head = r_col // rows_per_head
    slope = _alibi_slope(head).astype(F32)
    q_pos = past_len + r_col % t_new

    @pl.when(ps == 0)
    def _():
        q = q_ref[...] * (DK_B ** -0.5)
        q_rep = jnp.concatenate([q] * (2 * H_B), axis=0)
        rr = lax.broadcasted_iota(jnp.int32, q_rep.shape, 0)
        cc = lax.broadcasted_iota(jnp.int32, q_rep.shape, 1)
        qbd_sc[...] = jnp.where(rr // t_new == cc // DK_B, q_rep, 0.0).astype(BF16)
        m_sc[...] = jnp.full_like(m_sc, NEG_INF)
        l_sc[...] = jnp.zeros_like(l_sc)
        acc_sc[...] = jnp.zeros_like(acc_sc)

    def online_update(s, pv_fn):
        m_old = m_sc[...]
        m_new = jnp.maximum(m_old, jnp.max(s, axis=-1, keepdims=True))
        alpha = jnp.exp(m_old - m_new)
        p = jnp.exp(s - m_new)
        l_sc[...] = alpha * l_sc[...] + jnp.sum(p, axis=-1, keepdims=True)
        acc_sc[...] = alpha * acc_sc[...] + pv_fn(p)
        m_sc[...] = m_new

    qbd = qbd_sc[...]
    lane = lax.broadcasted_iota(jnp.int32, (n_rows, G * PAGE_SIZE), 1)
    k_pos = ps * (G * PAGE_SIZE) + lane
    s = jnp.concatenate([_bdot_nt(qbd, k_refs[g][...]) for g in range(G)], axis=-1)
    s = s + slope * (k_pos - q_pos).astype(F32)

    def pv_pages(p):
        pb = p.astype(BF16)
        outs = []
        for hh in range(H_B):
            acc = None
            for g in range(G):
                t = jnp.dot(pb[hh * rows_per_head:(hh + 1) * rows_per_head, g * PAGE_SIZE:(g + 1) * PAGE_SIZE],
                            v_refs[g][:, hh * DV_B:(hh + 1) * DV_B].astype(BF16), preferred_element_type=F32)
                acc = t if acc is None else acc + t
            outs.append(acc)
        return jnp.concatenate(outs, axis=0)

    online_update(s, pv_pages)

    @pl.when(ps == pl.num_programs(1) - 1)
    def _():
        kn = kn_ref[...]
        vn = vn_ref[...]
        qf = qbd_sc[...].astype(F32)
        sn = lax.dot_general(qf, kn.astype(BF16).astype(F32), (((1,), (1,)), ((), ())),
                             preferred_element_type=F32)
        kp = past_len + lax.broadcasted_iota(jnp.int32, (n_rows, t_new), 1)
        rel = kp - q_pos
        sn = jnp.where(rel <= 0, sn + slope * rel.astype(F32), NEG_INF)

        def pv_new(p):
            pr = p.astype(BF16).astype(F32)
            outs = []
            for hh in range(H_B):
                outs.append(jnp.dot(pr[hh * rows_per_head:(hh + 1) * rows_per_head, :],
                                    vn[:, hh * DV_B:(hh + 1) * DV_B].astype(BF16).astype(F32),
                                    preferred_element_type=F32))
            return jnp.concatenate(outs, axis=0)

        online_update(sn, pv_new)

        lam = _diff_lambda(lam_ref, lam_init)
        o = acc_sc[...] / l_sc[...]
        outs = []
        for hh in range(H_B):
            o0 = o[hh * rows_per_head:hh * rows_per_head + t_new, :]
            o1 = o[hh * rows_per_head + t_new:(hh + 1) * rows_per_head, :]
            ob = o0 - lam * o1
            outs.append(ob * lax.rsqrt(jnp.mean(ob * ob, axis=-1, keepdims=True) + NORM_EPS))
        o_ref[...] = jnp.concatenate(outs, axis=-1) * g_ref[...] * (1.0 - lam_init)


def diff_attn_sample(zb, cache_k, cache_v, layer, page_table, lam_p, norm_g, nb, t_new, row0, lam_init, prev):
    n_tok = zb.shape[0]
    n_pages = page_table.shape[1]
    G = PAGES_PER_STEP
    rb0 = row0 // t_new
    n_rows = 2 * H_B * t_new
    ck = cache_k.reshape(cache_k.shape[0], cache_k.shape[1], PAGE_SIZE, W_B)
    cv = cache_v.reshape(cache_v.shape[0], cache_v.shape[1], PAGE_SIZE, W_B)

    def page_map(g):
        return lambda b, p, pt: (layer, pt[b, p * G + g], 0, 0)

    in_specs = [
        pl.BlockSpec(lam_p.shape, lambda b, p, pt: (0, 0)),
        pl.BlockSpec((1, W_B), lambda b, p, pt: (0, 0)),
        pl.BlockSpec((t_new, W_B), lambda b, p, pt: (rb0 + b, 0)),
        pl.BlockSpec((t_new, W_B), lambda b, p, pt: (rb0 + b, 1)),
        pl.BlockSpec((t_new, W_B), lambda b, p, pt: (rb0 + b, 2)),
    ]
    in_specs += [pl.BlockSpec((None, None, PAGE_SIZE, W_B), page_map(g)) for g in range(G)]
    in_specs += [pl.BlockSpec((None, None, PAGE_SIZE, W_B), page_map(g)) for g in range(G)]
    in_specs += [pl.BlockSpec(memory_space=pl.ANY)]
    args = [page_table, lam_p, norm_g.reshape(1, W_B), zb, zb, zb] + [ck] * G + [cv] * G + [prev]
    grid_spec = pltpu.PrefetchScalarGridSpec(
        num_scalar_prefetch=1,
        grid=(nb, n_pages // G),
        in_specs=in_specs,
        out_specs=pl.BlockSpec((t_new, W_B), lambda b, p, pt: (rb0 + b, 0)),
        scratch_shapes=[pltpu.VMEM((n_rows, W_B), BF16), pltpu.VMEM((n_rows, 1), F32),
                        pltpu.VMEM((n_rows, 1), F32), pltpu.VMEM((n_rows, DV_B), F32)],
    )
    return pl.pallas_call(
        functools.partial(_attn_sample_kernel, lam_init=lam_init, past_len=n_pages * PAGE_SIZE,
                          t_new=t_new, pages_per_step=G),
        grid_spec=grid_spec,
        out_shape=jax.ShapeDtypeStruct((n_tok, W_B), F32),
        input_output_aliases={len(args) - 1: 0},
        compiler_params=_cparams("parallel", "arbitrary"),
        name="diff_attn_sample",
    )(*args)


def _s5_disc_kernel(are_ref, aim_ref, ldt_ref, bre_ref, bim_ref, lre_ref, lim_ref, bbre_ref, bbim_ref):
    a_re = are_ref[...]
    a_im = aim_ref[...]
    dt = jnp.exp(ldt_ref[...])
    mag = jnp.exp(a_re * dt)
    lb_re = mag * jnp.cos(a_im * dt)
    lb_im = mag * jnp.sin(a_im * dt)
    den = a_re * a_re + a_im * a_im
    xr = lb_re - 1.0
    fr = (xr * a_re + lb_im * a_im) / den
    fi = (lb_im * a_re - xr * a_im) / den
    lre_ref[...] = lb_re
    lim_ref[...] = lb_im
    b_re = bre_ref[...]
    b_im = bim_ref[...]
    bbre_ref[...] = fr * b_re - fi * b_im
    bbim_ref[...] = fr * b_im + fi * b_re


def s5_discretize(a_re, a_im, log_dt, b_re, b_im):
    depth = a_re.shape[0]
    gp = G_C * P_C
    flat = lambda a: a.reshape(depth, 1, gp)
    ldt = jnp.broadcast_to(log_dt[:, :, None], (depth, G_C, P_C)).reshape(depth, 1, gp)
    tr = lambda b: jnp.transpose(b, (0, 3, 1, 2)).reshape(depth, GC, gp)
    shp1 = jax.ShapeDtypeStruct((depth, 1, gp), F32)
    shpb = jax.ShapeDtypeStruct((depth, GC, gp), F32)
    return pl.pallas_call(_s5_disc_kernel, out_shape=[shp1, shp1, shpb, shpb], name="s5_discretize")(
        flat(a_re), flat(a_im), ldt, tr(b_re), tr(b_im))


def _block_diag_in(bb):
    depth, _, gp = bb.shape
    tiled = jnp.tile(bb, (1, G_C, 1)).reshape(depth, G_C, GC, gp)
    grp_r = jnp.arange(G_C)[:, None, None]
    grp_c = (jnp.arange(gp) // P_C)[None, None, :]
    return jnp.where(grp_r == grp_c, tiled, 0.0).reshape(depth, G_C * GC, gp)


def _block_diag_out(c):
    depth = c.shape[0]
    ct = jnp.transpose(c, (0, 1, 3, 2)).reshape(depth, G_C * P_C, GC)
    tiled = jnp.tile(ct, (1, 1, G_C))
    grp_r = (jnp.arange(G_C * P_C) // P_C)[:, None]
    grp_c = (jnp.arange(G_C * GC) // GC)[None, :]
    return jnp.where(grp_r == grp_c, tiled, 0.0)


def _cmul_add(a_re, a_im, x_re, x_im, y_re, y_im):
    return y_re + (a_re * x_re - a_im * x_im), y_im + (a_re * x_im + a_im * x_re)


def _block_scan(x_re, x_im, pw, row8):
    for d, (a_re, a_im) in zip((1, 2, 4), pw):
        sh_re = jnp.where(row8 >= d, pltpu.roll(x_re, d, 0), 0.0)
        sh_im = jnp.where(row8 >= d, pltpu.roll(x_im, d, 0), 0.0)
        x_re, x_im = _cmul_add(a_re, a_im, sh_re, sh_im, x_re, x_im)
    return x_re, x_im


def _s5_kernel(*refs, rows, independent, aliased):
    (u_ref, lre_ref, lim_ref, bb_ref, cc_ref, d_ref, gw_ref, gb_ref, s0re_ref, s0im_ref) = refs[:10]
    rest = refs[10:]
    if aliased:
        rest = rest[1:]
    y_ref, s1re_ref, s1im_ref, st_sc, car_sc = rest
    gp = G_C * P_C
    nblk = rows // V7X_SUBLANES
    ti = pl.program_id(1)

    lam_re = lre_ref[...]
    lam_im = lim_ref[...]
    l2_re, l2_im = lam_re * lam_re - lam_im * lam_im, 2.0 * lam_re * lam_im
    l4_re, l4_im = l2_re * l2_re - l2_im * l2_im, 2.0 * l2_re * l2_im
    pw = ((lam_re, lam_im), (l2_re, l2_im), (l4_re, l4_im))
    row8 = lax.broadcasted_iota(jnp.int32, (V7X_SUBLANES, gp), 0)
    pk_re, pk_im = _block_scan(jnp.where(row8 == 0, lam_re, 0.0), jnp.where(row8 == 0, lam_im, 0.0), pw, row8)

    u = u_ref[...]
    st_sc[...] = _bdot(u, bb_ref[...])

    if not independent:
        @pl.when(ti == 0)
        def _():
            car_sc[0:1, :] = s0re_ref[...]
            car_sc[1:2, :] = s0im_ref[...]

    def body(bi, carry):
        r0 = pl.multiple_of(bi * V7X_SUBLANES, V7X_SUBLANES)
        x_re = st_sc[pl.ds(r0, V7X_SUBLANES), 0:gp]
        x_im = st_sc[pl.ds(r0, V7X_SUBLANES), gp:2 * gp]
        x_re, x_im = _block_scan(x_re, x_im, pw, row8)
        if independent:
            c_re = s0re_ref[pl.ds(bi, 1), :]
            c_im = s0im_ref[pl.ds(bi, 1), :]
        else:
            c_re, c_im = carry
        s_re, s_im = _cmul_add(pk_re, pk_im, c_re, c_im, x_re, x_im)
        st_sc[pl.ds(r0, V7X_SUBLANES), 0:gp] = s_re
        st_sc[pl.ds(r0, V7X_SUBLANES), gp:2 * gp] = s_im
        last_re = s_re[V7X_SUBLANES - 1:V7X_SUBLANES, :]
        last_im = s_im[V7X_SUBLANES - 1:V7X_SUBLANES, :]
        if independent:
            s1re_ref[pl.ds(bi, 1), :] = last_re
            s1im_ref[pl.ds(bi, 1), :] = last_im
            return carry
        return last_re, last_im

    if independent:
        lax.fori_loop(0, nblk, body, 0)
    else:
        c_re, c_im = lax.fori_loop(0, nblk, body, (car_sc[0:1, :], car_sc[1:2, :]))
        car_sc[0:1, :] = c_re
        car_sc[1:2, :] = c_im

        @pl.when(ti == pl.num_programs(1) - 1)
        def _():
            s1re_ref[...] = c_re
            s1im_ref[...] = c_im

    y = _bdot(st_sc[...], cc_ref[...]) + d_ref[...] * u
    z = _bdot(jax.nn.gelu(y), gw_ref[...]) + gb_ref[...]
    y_ref[...] = z[:, :W_C] * jax.nn.sigmoid(z[:, W_C:])


def s5_mixer(zc, lam_re, lam_im, bb, cc, d, glu_w, glu_b, s0_re, s0_im, nb, t, row0, prev=None):
    n_tok = zc.shape[0]
    gp = G_C * P_C
    independent = t == V7X_SUBLANES
    aliased = prev is not None
    if independent:
        seqs = S5_SAMPLE_SEQS
        rows = seqs * t
        grid = (nb // seqs, 1)
        st_spec = pl.BlockSpec((seqs, gp), lambda b, i: (b, 0))
    else:
        rows = S5_TILE
        grid = (nb, t // rows)
        st_spec = pl.BlockSpec((1, gp), lambda b, i: (b, 0))
    nt = grid[1]
    rb0 = row0 // rows
    row_map = lambda b, i: (rb0 + b * nt + i, 0)
    const = lambda b, i: (0, 0)
    in_specs = [
        pl.BlockSpec((rows, W_C), row_map),
        pl.BlockSpec((1, gp), const),
        pl.BlockSpec((1, gp), const),
        pl.BlockSpec(bb.shape, const),
        pl.BlockSpec(cc.shape, const),
        pl.BlockSpec((1, W_C), const),
        pl.BlockSpec(glu_w.shape, const),
        pl.BlockSpec((1, 2 * W_C), const),
        st_spec,
        st_spec,
    ]
    args = [zc, lam_re, lam_im, bb, cc, d.reshape(1, W_C), glu_w, glu_b.reshape(1, 2 * W_C), s0_re, s0_im]
    aliases = {}
    if aliased:
        in_specs.append(pl.BlockSpec(memory_space=pl.ANY))
        args.append(prev)
        aliases = {len(args) - 1: 0}
    if independent:
        s1_shape = jax.ShapeDtypeStruct((nb, gp), F32)
    else:
        s1_shape = jax.ShapeDtypeStruct((nb, 1, gp), F32)
        st_out = pl.BlockSpec((None, 1, gp), lambda b, i: (b, 0, 0))
    out_specs = [pl.BlockSpec((rows, W_C), row_map)] + ([st_spec, st_spec] if independent else [st_out, st_out])
    if not independent:
        args[8] = s0_re.reshape(nb, 1, gp)
        args[9] = s0_im.reshape(nb, 1, gp)
        in_specs[8] = in_specs[9] = pl.BlockSpec((None, 1, gp), lambda b, i: (b, 0, 0))
    yc, s1_re, s1_im = pl.pallas_call(
        functools.partial(_s5_kernel, rows=rows, independent=independent, aliased=aliased),
        grid=grid,
        in_specs=in_specs,
        out_specs=out_specs,
        out_shape=[jax.ShapeDtypeStruct((n_tok, W_C), F32), s1_shape, s1_shape],
        scratch_shapes=[pltpu.VMEM((rows, 2 * gp), F32), pltpu.VMEM((V7X_SUBLANES, gp), F32)],
        input_output_aliases=aliases,
        compiler_params=_cparams("parallel", "arbitrary"),
        name="s5_sample" if independent else "s5_prompt",
    )(*args)
    return yc, s1_re.reshape(nb, G_C, P_C), s1_im.reshape(nb, G_C, P_C)


def _xattn_kernel(*refs, tq, aliased):
    q_ref, mk_ref, mv_ref = refs[:3]
    o_ref = refs[-1]
    q = q_ref[...] * (DH_X ** -0.5)
    pad = tq < 2 * V7X_SUBLANES
    if pad:
        q = jnp.concatenate([q, jnp.zeros((2 * V7X_SUBLANES - tq, q.shape[1]), F32)], axis=0)
    outs = []
    for h in range(H_X):
        sl = slice(h * DH_X, (h + 1) * DH_X)
        s = _bdot_nt(q[:, sl], mk_ref[:, sl])
        m = jnp.max(s, axis=-1, keepdims=True)
        p = jnp.exp(s - m)
        l = jnp.sum(p, axis=-1, keepdims=True)
        outs.append(_bdot(p, mv_ref[:, sl]) / l)
    o = jnp.concatenate(outs, axis=-1)
    o_ref[...] = o[:tq] if pad else o


def cross_attn(qx, mk, mv, nb, t, row0, prev=None):
    n_tok, d = qx.shape
    n_mem = mk.shape[0] // nb
    tq = min(t, XATTN_TILE)
    nq = t // tq
    rb0 = row0 // tq
    aliased = prev is not None
    in_specs = [
        pl.BlockSpec((tq, d), lambda b, i: (rb0 + b * nq + i, 0)),
        pl.BlockSpec((n_mem, d), lambda b, i: (b, 0)),
        pl.BlockSpec((n_mem, d), lambda b, i: (b, 0)),
    ]
    args = [qx, mk, mv]
    aliases = {}
    if aliased:
        in_specs.append(pl.BlockSpec(memory_space=pl.ANY))
        args.append(prev)
        aliases = {3: 0}
    return pl.pallas_call(
        functools.partial(_xattn_kernel, tq=tq, aliased=aliased),
        grid=(nb, nq),
        in_specs=in_specs,
        out_specs=pl.BlockSpec((tq, d), lambda b, i: (rb0 + b * nq + i, 0)),
        out_shape=jax.ShapeDtypeStruct((n_tok, d), F32),
        input_output_aliases=aliases,
        compiler_params=_cparams("parallel", "arbitrary"),
        name="cross_attn_sample" if aliased else "cross_attn_prompt",
    )(*args)


def kernel(x_prompt, x_sample, mem_prompt, cache_k, cache_v, page_table, cache_mem_k, cache_mem_v,
           state_mlstm_c, state_mlstm_n, state_mlstm_m, state_ssm_re, state_ssm_im,
           ln_g, ln_b, ffn1_wg, ffn1_wu, ffn1_wd, ffn2_wg, ffn2_wu, ffn2_wd, w_in, b_in,
           mlstm_norm_g, diff_lam, diff_norm_g, ssm_a_re, ssm_a_im, ssm_log_dt, ssm_b_re, ssm_b_im,
           ssm_c_re, ssm_c_im, ssm_d, ssm_glu_w, ssm_glu_b, w_out, cross_wq, cross_wk, cross_wv, cross_wo):
    bp, tp, d = x_prompt.shape
    bs, ts, _ = x_sample.shape
    depth = ln_g.shape[0]
    n_mem = mem_prompt.shape[1]
    n_p = bp * tp
    gp = G_C * P_C
    alpha = (2.0 * depth) ** 0.25

    cast = lambda w: w.astype(BF16)
    ffn1_wg, ffn1_wu, ffn1_wd = cast(ffn1_wg), cast(ffn1_wu), cast(ffn1_wd)
    ffn2_wg, ffn2_wu, ffn2_wd = cast(ffn2_wg), cast(ffn2_wu), cast(ffn2_wd)
    w_out_b, wq_b, wo_b, glu_w_b = cast(w_out), cast(cross_wq), cast(cross_wo), cast(ssm_glu_w)
    w_in_p, b_in_p = _pack_w_in(w_in, b_in)
    lam_re, lam_im, bb_re, bb_im = s5_discretize(ssm_a_re, ssm_a_im, ssm_log_dt, ssm_b_re, ssm_b_im)
    bb = jnp.concatenate([_block_diag_in(bb_re), _block_diag_in(bb_im)], axis=-1).astype(BF16)
    cc = jnp.concatenate([_block_diag_out(ssm_c_re), -_block_diag_out(ssm_c_im)], axis=1).astype(BF16)

    w_kv = jnp.concatenate([cast(cross_wk), cast(cross_wv)], axis=0)
    mem_kv = stacked_matmul(mem_prompt.reshape(bp * n_mem, d), w_kv)

    x = jnp.concatenate([x_prompt.reshape(n_p, d), x_sample.reshape(bs * ts, d)], axis=0)
    zeros_s = jnp.zeros((bp, gp), F32)
    p_st, s_st = [], []
    for l in range(depth):
        lam_init = 0.8 - 0.6 * math.exp(-0.3 * l)
        x = ffn_ln(x, ffn1_wg[l], ffn1_wu[l], ffn1_wd[l], ln_g[l, 0], ln_b[l, 0], alpha)
        za, zb, zc, zg = proj_in(x, w_in_p[l], b_in_p[l])

        ya, pc, pn, pm = mlstm(za, zg, mlstm_norm_g[l], bp, tp, 0)
        ya, sc, sn, sm = mlstm(za, zg, mlstm_norm_g[l], bs, ts, n_p,
                               init=(state_mlstm_c[l], state_mlstm_n[l], state_mlstm_m[l]), prev=ya)

        yb = diff_attn_prompt(zb, diff_lam[l], diff_norm_g[l], bp, tp, lam_init)
        yb = diff_attn_sample(zb, cache_k, cache_v, l, page_table, diff_lam[l], diff_norm_g[l],
                              bs, ts, n_p, lam_init, yb)

        s5_args = (lam_re[l], lam_im[l], bb[l], cc[l], ssm_d[l], glu_w_b[l], ssm_glu_b[l])
        yc, psr, psi = s5_mixer(zc, *s5_args, zeros_s, zeros_s, bp, tp, 0)
        yc, ssr, ssi = s5_mixer(zc, *s5_args, state_ssm_re[l].reshape(bs, gp), state_ssm_im[l].reshape(bs, gp),
                                bs, ts, n_p, prev=yc)

        x, qx = out_ln([ya, yb, yc], w_out_b[l], x, ln_g[l, 1], ln_b[l, 1], alpha, w_next=wq_b[l])
        o = cross_attn(qx, mem_kv[l], mem_kv[depth + l], bp, tp, 0)
        o = cross_attn(qx, cache_mem_k[l].reshape(bs * n_mem, d), cache_mem_v[l].reshape(bs * n_mem, d),
                       bs, ts, n_p, prev=o)
        x = out_ln([o], wo_b[l], x, ln_g[l, 2], ln_b[l, 2], alpha)
        x = ffn_ln(x, ffn2_wg[l], ffn2_wu[l], ffn2_wd[l], ln_g[l, 3], ln_b[l, 3], alpha)

        kb, vb = zb[:, W_B:2 * W_B], zb[:, 2 * W_B:]
        p_st.append((kb[:n_p].reshape(bp, tp, H_B, 2 * DK_B), vb[:n_p].reshape(bp, tp, H_B, DV_B),
                     pc, pn, pm, psr, psi))
        s_st.append((kb[n_p:].reshape(bs, ts, H_B, 2 * DK_B), vb[n_p:].reshape(bs, ts, H_B, DV_B),
                     sc, sn, sm, ssr, ssi))

    p_k, p_v, p_c, p_n, p_m, p_sr, p_si = [jnp.stack(a) for a in zip(*p_st)]
    s_k, s_v, s_c, s_n, s_m, s_sr, s_si = [jnp.stack(a) for a in zip(*s_st)]
    p_mk = mem_kv[:depth].reshape(depth, bp, n_mem, H_X, DH_X)
    p_mv = mem_kv[depth:].reshape(depth, bp, n_mem, H_X, DH_X)
    yp = x[:n_p].reshape(bp, tp, d)
    ys = x[n_p:].reshape(bs, ts, d)
    return (yp, ys, p_k, p_v, p_mk, p_mv, p_c, p_n, p_m, p_sr, p_si, s_k, s_v, s_c, s_n, s_m, s_sr, s_si)
```

```python
import functools
import math

import jax
import jax.numpy as jnp
from jax import lax
from jax.experimental import pallas as pl
from jax.experimental.pallas import tpu as pltpu

F32 = jnp.float32
BF16 = jnp.bfloat16

D_MODEL = 1024
PAGE_SIZE = 128
W_A = D_MODEL // 4
W_B = D_MODEL // 2
W_C = D_MODEL - W_A - W_B
H_A = 4
DH_A = W_A // H_A
MLSTM_CHUNK = 64
H_B = 4
DV_B = W_B // H_B
DK_B = DV_B // 2
GC = 16
G_C = W_C // GC
P_C = 64
H_X = 4
DH_X = D_MODEL // H_X
LN_EPS = 1e-5
NORM_EPS = 1e-6

OFF_AQ = 0
OFF_AK = OFF_AQ + W_A
OFF_AV = OFF_AK + W_A
OFF_AO = OFF_AV + W_A
OFF_AI = OFF_AO + W_A
OFF_AF = OFF_AI + H_A
OFF_BQ = OFF_AF + H_A
OFF_BK = OFF_BQ + H_B * 2 * DK_B
OFF_BV = OFF_BK + H_B * 2 * DK_B
OFF_CU = OFF_BV + W_B
N_IN = OFF_CU + W_C

V7X_LANES = 128
V7X_SUBLANES = 8
V7X_BF16_ROWS = 2 * V7X_SUBLANES
V7X_VMEM_LIMIT_BYTES = 56 * 1024 * 1024

TOKEN_TILE = 1024
FF_TILE = 256
PROJ_TILE = 512
ATTN_TILE = 512
ATTN_ROWS = 256
XATTN_TILE = 512
S5_TILE = 512
PAGES_PER_STEP = 8
S5_SAMPLE_SEQS = 16
MLSTM_UNITS = 8
ALIBI_SPLIT = 64

NEG_INF = float("-inf")


def _cparams(*sem):
    return pltpu.CompilerParams(dimension_semantics=sem, vmem_limit_bytes=V7X_VMEM_LIMIT_BYTES)


def _bdot(a, b):
    return jnp.dot(a.astype(BF16), b.astype(BF16), preferred_element_type=F32)


def _bdot_nt(a, b):
    return lax.dot_general(a.astype(BF16), b.astype(BF16), (((1,), (1,)), ((), ())),
                           preferred_element_type=F32)


def _layer_norm(y, g, b):
    mu = jnp.mean(y, axis=-1, keepdims=True)
    yc = y - mu
    var = jnp.mean(yc * yc, axis=-1, keepdims=True)
    return yc * lax.rsqrt(var + LN_EPS) * g + b


def _ffn_ln_kernel(x_ref, wg_ref, wu_ref, wd_ref, g_ref, b_ref, o_ref, h_sc, *, alpha, tf):
    xb = x_ref[...].astype(BF16)
    for c in range(wg_ref.shape[1] // tf):
        cols = slice(c * tf, (c + 1) * tf)
        hg = jnp.dot(xb, wg_ref[:, cols], preferred_element_type=F32)
        hu = jnp.dot(xb, wu_ref[:, cols], preferred_element_type=F32)
        h_sc[:, cols] = ((hg * jax.nn.sigmoid(hg)) * hu).astype(BF16)
    ff = jnp.dot(h_sc[...], wd_ref[...], preferred_element_type=F32)
    o_ref[...] = _layer_norm(alpha * x_ref[...] + 0.5 * ff, g_ref[...], b_ref[...])


def ffn_ln(x, wg, wu, wd, g, b, alpha):
    n, d = x.shape
    dff = wg.shape[1]
    tm, tf = TOKEN_TILE, FF_TILE
    resident = pl.Buffered(1)
    return pl.pallas_call(
        functools.partial(_ffn_ln_kernel, alpha=alpha, tf=tf),
        grid=(n // tm,),
        in_specs=[
            pl.BlockSpec((tm, d), lambda i: (i, 0)),
            pl.BlockSpec((d, dff), lambda i: (0, 0), pipeline_mode=resident),
            pl.BlockSpec((d, dff), lambda i: (0, 0), pipeline_mode=resident),
            pl.BlockSpec((dff, d), lambda i: (0, 0), pipeline_mode=resident),
            pl.BlockSpec((1, d), lambda i: (0, 0)),
            pl.BlockSpec((1, d), lambda i: (0, 0)),
        ],
        out_specs=pl.BlockSpec((tm, d), lambda i: (i, 0)),
        out_shape=jax.ShapeDtypeStruct((n, d), F32),
        scratch_shapes=[pltpu.VMEM((tm, dff), BF16)],
        compiler_params=_cparams("parallel"),
        name="ffn_ln",
    )(x, wg, wu, wd, g.reshape(1, d), b.reshape(1, d))


N_ZA = 4 * W_A
N_ZB = 3 * W_B
N_ZC = W_C
N_ZG = V7X_LANES
N_PROJ = N_ZA + N_ZB + N_ZC + N_ZG


def _proj_in_kernel(x_ref, w_ref, b_ref, _kp_prev, _vp_prev, za_ref, zb_ref, zc_ref, zg_ref,
                    kp_ref, vp_ref, ks_ref, vs_ref, *, prompt_tiles):
    i = pl.program_id(0)
    xb = x_ref[...].astype(BF16)
    off = 0
    for ref in (za_ref, zb_ref, zc_ref, zg_ref):
        width = ref.shape[1]
        ref[...] = jnp.dot(xb, w_ref[:, off:off + width], preferred_element_type=F32) + b_ref[:, off:off + width]
        off += width

    def write_kv(k_out, v_out):
        for h in range(H_B):
            k_out[:, h, :] = zb_ref[:, W_B + h * DV_B:W_B + (h + 1) * DV_B]
            v_out[:, h, :] = zb_ref[:, 2 * W_B + h * DV_B:2 * W_B + (h + 1) * DV_B]

    @pl.when(i < prompt_tiles)
    def _():
        write_kv(kp_ref, vp_ref)

    @pl.when(i >= prompt_tiles)
    def _():
        write_kv(ks_ref, vs_ref)


def proj_in(x, w, b, kp_prev, vp_prev, layer, n_prompt):
    n, d = x.shape
    tm = PROJ_TILE
    pt = n_prompt // tm
    widths = (N_ZA, N_ZB, N_ZC, N_ZG)
    kv_blk = (tm, H_B, DV_B)
    p_map = lambda i: (layer, jnp.minimum(i, pt - 1), 0, 0)
    s_map = lambda i: (jnp.maximum(i - pt, 0), 0, 0)
    kv_s = jax.ShapeDtypeStruct((n - n_prompt, H_B, DV_B), F32)
    return pl.pallas_call(
        functools.partial(_proj_in_kernel, prompt_tiles=pt),
        grid=(n // tm,),
        in_specs=[
            pl.BlockSpec((tm, d), lambda i: (i, 0)),
            pl.BlockSpec((d, N_PROJ), lambda i: (0, 0)),
            pl.BlockSpec((1, N_PROJ), lambda i: (0, 0)),
            pl.BlockSpec(memory_space=pl.ANY),
            pl.BlockSpec(memory_space=pl.ANY),
        ],
        out_specs=[pl.BlockSpec((tm, wd), lambda i: (i, 0)) for wd in widths]
        + [pl.BlockSpec((None,) + kv_blk, p_map)] * 2 + [pl.BlockSpec(kv_blk, s_map)] * 2,
        out_shape=[jax.ShapeDtypeStruct((n, wd), F32) for wd in widths]
        + [jax.ShapeDtypeStruct(kp_prev.shape, F32)] * 2 + [kv_s, kv_s],
        input_output_aliases={3: 4, 4: 5},
        compiler_params=_cparams("arbitrary"),
        name="proj_in",
    )(x, w, b, kp_prev, vp_prev)


def _pack_w_in(w_in, b_in):
    def cols(a):
        pad = jnp.zeros(a.shape[:-1] + (N_ZG - 2 * H_A,), a.dtype)
        return jnp.concatenate([a[..., OFF_AQ:OFF_AI], a[..., OFF_BQ:OFF_CU], a[..., OFF_CU:N_IN],
                                a[..., OFF_AI:OFF_BQ], pad], axis=-1)
    return cols(w_in).astype(BF16), cols(b_in)[:, None, :]


def _mem_kv_kernel(x_ref, wk_ref, wv_ref, mk_ref, mv_ref, mkh_ref, mvh_ref):
    xb = x_ref[...].astype(BF16)
    for w_ref, o_ref, oh_ref in ((wk_ref, mk_ref, mkh_ref), (wv_ref, mv_ref, mvh_ref)):
        r = jnp.dot(xb, w_ref[...], preferred_element_type=F32)
        for h in range(H_X):
            o_ref[:, h, :] = r[:, h * DH_X:(h + 1) * DH_X]
            oh_ref[h] = r[:, h * DH_X:(h + 1) * DH_X].astype(BF16)


def mem_kv(x, wk, wv):
    m, d = x.shape
    depth = wk.shape[0]
    tm = min(m, TOKEN_TILE)
    shp = jax.ShapeDtypeStruct((depth, m, H_X, DH_X), F32)
    shp_h = jax.ShapeDtypeStruct((depth, H_X, m, DH_X), BF16)
    w_spec = pl.BlockSpec((None, d, d), lambda i, l: (l, 0, 0))
    o_spec = pl.BlockSpec((None, tm, H_X, DH_X), lambda i, l: (l, i, 0, 0))
    oh_spec = pl.BlockSpec((None, H_X, tm, DH_X), lambda i, l: (l, 0, i, 0))
    return pl.pallas_call(
        _mem_kv_kernel,
        grid=(m // tm, depth),
        in_specs=[pl.BlockSpec((tm, d), lambda i, l: (i, 0)), w_spec, w_spec],
        out_specs=[o_spec, o_spec, oh_spec, oh_spec],
        out_shape=[shp, shp, shp_h, shp_h],
        compiler_params=_cparams("parallel", "arbitrary"),
        name="mem_kv",
    )(x, wk, wv)


def _out_ln_kernel(*refs, n_parts, alpha, has_next):
    parts = refs[:n_parts]
    w_ref, x_ref, g_ref, b_ref = refs[n_parts:n_parts + 4]
    rest = refs[n_parts + 4:]
    if has_next:
        wn_ref, o_ref, q_ref = rest
    else:
        (o_ref,) = rest
    acc = None
    off = 0
    for p in parts:
        width = p.shape[1]
        t = jnp.dot(p[...].astype(BF16), w_ref[off:off + width, :], preferred_element_type=F32)
        acc = t if acc is None else acc + t
        off += width
    y = _layer_norm(alpha * x_ref[...] + acc, g_ref[...], b_ref[...])
    o_ref[...] = y
    if has_next:
        q_ref[...] = jnp.dot(y.astype(BF16), wn_ref[...], preferred_element_type=F32)


def out_ln(parts, w, x, g, b, alpha, w_next=None):
    n, d = x.shape
    tm = PROJ_TILE
    has_next = w_next is not None
    in_specs = [pl.BlockSpec((tm, p.shape[1]), lambda i: (i, 0)) for p in parts]
    in_specs += [
        pl.BlockSpec(w.shape, lambda i: (0, 0)),
        pl.BlockSpec((tm, d), lambda i: (i, 0)),
        pl.BlockSpec((1, d), lambda i: (0, 0)),
        pl.BlockSpec((1, d), lambda i: (0, 0)),
    ]
    args = list(parts) + [w, x, g.reshape(1, d), b.reshape(1, d)]
    out_specs = [pl.BlockSpec((tm, d), lambda i: (i, 0))]
    out_shape = [jax.ShapeDtypeStruct((n, d), F32)]
    if has_next:
        in_specs.append(pl.BlockSpec(w_next.shape, lambda i: (0, 0)))
        args.append(w_next)
        out_specs.append(pl.BlockSpec((tm, w_next.shape[1]), lambda i: (i, 0)))
        out_shape.append(jax.ShapeDtypeStruct((n, w_next.shape[1]), F32))
    res = pl.pallas_call(
        functools.partial(_out_ln_kernel, n_parts=len(parts), alpha=alpha, has_next=has_next),
        grid=(n // tm,),
        in_specs=in_specs,
        out_specs=out_specs,
        out_shape=out_shape,
        compiler_params=_cparams("parallel"),
        name="out_ln",
    )(*args)
    return res if has_next else res[0]


def _log_sigmoid(x):
    return jnp.minimum(x, 0.0) - jnp.log1p(jnp.exp(-jnp.abs(x)))


def _lane_pick(row, h):
    return row[:, h:h + 1]


def _mlstm_kernel(*refs, chunk, units, sequential):
    if sequential:
        za_ref, zg_ref, g_ref, ya_ref, c1_ref, n1_ref, m1_ref, c_sc, n_sc, m_sc = refs
    else:
        za_ref, zg_ref, g_ref, c0_ref, n0_ref, m0_ref, _prev, ya_ref, c1_ref, n1_ref, m1_ref = refs
    L = chunk
    small = L < V7X_BF16_ROWS
    cast = (lambda a: a) if small else (lambda a: a.astype(BF16))
    mm = lambda a, b: jnp.dot(cast(a), cast(b), preferred_element_type=F32)
    mm_nt = lambda a, b: lax.dot_general(cast(a), cast(b), (((1,), (1,)), ((), ())), preferred_element_type=F32)
    mm_tn = lambda a, b: lax.dot_general(cast(a), cast(b), (((0,), (0,)), ((), ())), preferred_element_type=F32)

    if sequential:
        ti = pl.program_id(1)

        @pl.when(ti == 0)
        def _():
            c_sc[...] = jnp.zeros_like(c_sc)
            n_sc[...] = jnp.zeros_like(n_sc)
            m_sc[...] = jnp.zeros_like(m_sc)

    row = lax.broadcasted_iota(jnp.int32, (L, L), 0)
    col = lax.broadcasted_iota(jnp.int32, (L, L), 1)
    causal = col <= row
    tril = causal.astype(F32)
    sel_r = lax.broadcasted_iota(jnp.int32, (V7X_SUBLANES, V7X_LANES), 0)
    sel_c = lax.broadcasted_iota(jnp.int32, (V7X_SUBLANES, V7X_LANES), 1)
    sel = (sel_r == sel_c).astype(F32)
    lane_g = lax.broadcasted_iota(jnp.int32, (L, N_ZG), 1)
    lane_m = lax.broadcasted_iota(jnp.int32, (1, V7X_LANES), 1)
    norm_g = g_ref[...]

    if sequential:
        state = [(c_sc[h], n_sc[h:h + 1, :], _lane_pick(m_sc[...], h)) for h in range(H_A)]

    for u in range(units):
        rows_u = slice(u * L, (u + 1) * L)
        gates = zg_ref[rows_u, :]
        gl = jnp.where(lane_g < H_A, gates, _log_sigmoid(gates))
        bcum = jnp.dot(tril, gl, precision=lax.Precision.HIGHEST, preferred_element_type=F32)
        mixed = jnp.where(lane_g < H_A, gates, bcum)
        t_rows = lax.dot_general(sel, mixed, (((1,), (1,)), ((), ())),
                                 precision=lax.Precision.HIGHEST, preferred_element_type=F32)
        if not sequential:
            state = [(c0_ref[u, h], n0_ref[u, h:h + 1, :], _lane_pick(m0_ref[u], h)) for h in range(H_A)]
        outs = []
        new_state = []
        for h in range(H_A):
            q = za_ref[rows_u, h * DH_A:(h + 1) * DH_A]
            k = za_ref[rows_u, W_A + h * DH_A:W_A + (h + 1) * DH_A] * (DH_A ** -0.5)
            v = za_ref[rows_u, 2 * W_A + h * DH_A:2 * W_A + (h + 1) * DH_A]
            og = za_ref[rows_u, 3 * W_A + h * DH_A:3 * W_A + (h + 1) * DH_A]
            ig_col = gates[:, h:h + 1]
            b_col = bcum[:, H_A + h:H_A + h + 1]
            ig_row = t_rows[h:h + 1, :]
            b_row = t_rows[H_A + h:H_A + h + 1, :]
            c, n, m_prev = state[h]

            dmat = jnp.where(causal, b_col - b_row + ig_row, NEG_INF)
            a = jnp.max(dmat, axis=-1, keepdims=True)
            s = mm_nt(q, k) * jnp.exp(dmat - a)
            n_loc = mm(s, v)
            d_loc = jnp.sum(s, axis=-1, keepdims=True)
            a_last = a[L - 1:L, :]
            b_last = b_col[L - 1:L, :]
            wk = jnp.exp(b_last - b_col + ig_col - a_last)
            u_loc = mm_tn(wk * v, k)
            nu_loc = jnp.sum(wk * k, axis=0, keepdims=True)

            inter = b_col + m_prev
            m_row = jnp.maximum(inter, a)
            r = jnp.exp(a - m_row)
            w_inter = jnp.exp(inter - m_row)
            num = r * n_loc + w_inter * mm_nt(q, c)
            den = r * d_loc + w_inter * jnp.sum(q * n, axis=-1, keepdims=True)
            hh = num / jnp.maximum(jnp.abs(den), jnp.exp(-m_row))
            m_new = m_row[L - 1:L, :]
            decay = jnp.exp(b_last + m_prev - m_new)
            e_loc = jnp.exp(a_last - m_new)
            new_state.append((decay * c + e_loc * u_loc, decay * n + e_loc * nu_loc, m_new))

            hn = hh * lax.rsqrt(jnp.mean(hh * hh, axis=-1, keepdims=True) + NORM_EPS)
            outs.append(jax.nn.sigmoid(og) * hn)
        ya_ref[rows_u, :] = jnp.concatenate(outs, axis=-1) * norm_g
        state = new_state
        if not sequential:
            m_out = jnp.zeros((1, V7X_LANES), F32)
            for h in range(H_A):
                c1_ref[u, h] = state[h][0]
                n1_ref[u, h:h + 1, :] = state[h][1]
                m_out = jnp.where(lane_m == h, state[h][2], m_out)
            m1_ref[u] = m_out

    if sequential:
        m_out = jnp.zeros((1, V7X_LANES), F32)
        for h in range(H_A):
            c_sc[h] = state[h][0]
            n_sc[h:h + 1, :] = state[h][1]
            m_out = jnp.where(lane_m == h, state[h][2], m_out)
        m_sc[...] = m_out

        @pl.when(ti == pl.num_programs(1) - 1)
        def _():
            c1_ref[0] = c_sc[...]
            n1_ref[0] = n_sc[...]
            m1_ref[0] = m_sc[...]


def mlstm(za, zg, norm_g, nb, t, row0, init=None, prev=None):
    n_tok = za.shape[0]
    L = MLSTM_CHUNK if t % MLSTM_CHUNK == 0 else t
    nc = t // L
    units = MLSTM_UNITS
    rows = units * L
    sequential = init is None
    if sequential:
        grid = (nb, nc // units)
        seq_blk = 1
        row_map = lambda b, c: (row0 // rows + b * (nc // units) + c, 0)
    else:
        assert nc == 1
        grid = (nb // units, 1)
        seq_blk = units
        row_map = lambda b, c: (row0 // rows + b, 0)
    st4 = lambda b, c: (b, 0, 0, 0)
    st3 = lambda b, c: (b, 0, 0)
    in_specs = [
        pl.BlockSpec((rows, N_ZA), row_map),
        pl.BlockSpec((rows, N_ZG), row_map),
        pl.BlockSpec((1, W_A), lambda b, c: (0, 0)),
    ]
    args = [za, zg, norm_g.reshape(1, W_A)]
    aliases = {}
    scratch = []
    if sequential:
        scratch = [pltpu.VMEM((H_A, DH_A, DH_A), F32), pltpu.VMEM((H_A, DH_A), F32),
                   pltpu.VMEM((1, V7X_LANES), F32)]
    else:
        c0, n0, m0 = init
        m0p = jnp.pad(m0, ((0, 0), (0, V7X_LANES - H_A))).reshape(nb, 1, V7X_LANES)
        in_specs += [
            pl.BlockSpec((seq_blk, H_A, DH_A, DH_A), st4),
            pl.BlockSpec((seq_blk, H_A, DH_A), st3),
            pl.BlockSpec((seq_blk, 1, V7X_LANES), st3),
            pl.BlockSpec(memory_space=pl.ANY),
        ]
        args += [c0, n0, m0p, prev]
        aliases = {6: 0}
    out_specs = [
        pl.BlockSpec((rows, W_A), row_map),
        pl.BlockSpec((seq_blk, H_A, DH_A, DH_A), st4),
        pl.BlockSpec((seq_blk, H_A, DH_A), st3),
        pl.BlockSpec((seq_blk, 1, V7X_LANES), st3),
    ]
    out_shape = [
        jax.ShapeDtypeStruct((n_tok, W_A), F32),
        jax.ShapeDtypeStruct((nb, H_A, DH_A, DH_A), F32),
        jax.ShapeDtypeStruct((nb, H_A, DH_A), F32),
        jax.ShapeDtypeStruct((nb, 1, V7X_LANES), F32),
    ]
    ya, c1, n1, m1 = pl.pallas_call(
        functools.partial(_mlstm_kernel, chunk=L, units=units, sequential=sequential),
        grid=grid,
        in_specs=in_specs,
        out_specs=out_specs,
        out_shape=out_shape,
        scratch_shapes=scratch,
        input_output_aliases=aliases,
        compiler_params=_cparams("parallel", "arbitrary"),
        name="mlstm_prompt" if sequential else "mlstm_sample",
    )(*args)
    return ya, c1, n1, m1[:, 0, :H_A]


def _diff_lambda(lam_ref, lam_init):
    lp = lam_ref[...]
    d01 = jnp.sum(lp[0:1, :] * lp[1:2, :], axis=-1, keepdims=True)
    d23 = jnp.sum(lp[2:3, :] * lp[3:4, :], axis=-1, keepdims=True)
    return jnp.exp(d01) - jnp.exp(d23) + lam_init


def _alibi_slope(h):
    return jnp.where(h == 0, 2.0 ** -2, jnp.where(h == 1, 2.0 ** -4, jnp.where(h == 2, 2.0 ** -6, 2.0 ** -8)))


def _alibi_tables(t):
    slopes = (2.0 ** (-8.0 * jnp.arange(1, H_B + 1, dtype=F32) / H_B))[:, None]
    pos = jnp.arange(t, dtype=jnp.int32)
    hi = ((pos // ALIBI_SPLIT) * ALIBI_SPLIT).astype(F32)[None, :]
    lo = (pos % ALIBI_SPLIT).astype(F32)[None, :]
    ones = jnp.ones((H_B, t), F32)
    pad = jnp.zeros((H_B, t, DV_B - 4), F32)
    aq = jnp.concatenate([jnp.stack([slopes * ones, slopes * ones, -slopes * hi, -slopes * lo], axis=-1), pad], -1)
    ak = jnp.concatenate([jnp.stack([hi * ones, lo * ones, ones, ones], axis=-1), pad], -1)
    return aq.astype(BF16), ak.astype(BF16)


def _attn_prompt_kernel(lam_ref, g_ref, q_ref, aq_ref, k_ref, ak_ref, v_ref, o_ref,
                        qs_sc, m_sc, acc_sc, *, lam_init, tq, tk):
    i = pl.program_id(2)
    j = pl.program_id(3)
    rq = ATTN_ROWS

    @pl.when(j == 0)
    def _():
        q = q_ref[...] * (DK_B ** -0.5)
        lane = lax.broadcasted_iota(jnp.int32, q.shape, 1)
        qs_sc[0:tq, 0:DV_B] = jnp.where(lane < DK_B, q, 0.0).astype(BF16)
        qs_sc[tq:2 * tq, 0:DV_B] = jnp.where(lane >= DK_B, q, 0.0).astype(BF16)
        qs_sc[0:tq, DV_B:2 * DV_B] = aq_ref[...]
        qs_sc[tq:2 * tq, DV_B:2 * DV_B] = aq_ref[...]
        m_sc[...] = jnp.full_like(m_sc, NEG_INF)
        acc_sc[...] = jnp.zeros_like(acc_sc)

    def step(masked):
        kaug = jnp.concatenate([k_ref[...].astype(BF16), ak_ref[...]], axis=1)
        vaug = jnp.concatenate([v_ref[...].astype(BF16), jnp.ones((tk, V7X_LANES), BF16)], axis=1)
        for r in range(2 * tq // rq):
            rows = slice(r * rq, (r + 1) * rq)
            s = lax.dot_general(qs_sc[rows, :], kaug, (((1,), (1,)), ((), ())), preferred_element_type=F32)
            if masked:
                qi = (r * rq) % tq + lax.broadcasted_iota(jnp.int32, (rq, tk), 0)
                kj = lax.broadcasted_iota(jnp.int32, (rq, tk), 1)
                s = jnp.where(kj <= qi, s, NEG_INF)
            m_old = m_sc[rows, :]
            m_new = jnp.maximum(m_old, jnp.max(s, axis=-1, keepdims=True))
            alpha = jnp.exp(m_old - m_new)
            p = jnp.exp(s - pltpu.repeat(m_new, tk // V7X_LANES, 1))
            acc_sc[rows, :] = (pltpu.repeat(alpha, 2, 1) * acc_sc[rows, :]
                               + jnp.dot(p.astype(BF16), vaug, preferred_element_type=F32))
            m_sc[rows, :] = m_new

    @pl.when(j < i)
    def _():
        step(False)

    @pl.when(j == i)
    def _():
        step(True)

    @pl.when(j == pl.num_programs(3) - 1)
    def _():
        lam = _diff_lambda(lam_ref, lam_init)
        o0 = acc_sc[0:tq, 0:DV_B] / acc_sc[0:tq, DV_B:2 * DV_B]
        o1 = acc_sc[tq:2 * tq, 0:DV_B] / acc_sc[tq:2 * tq, DV_B:2 * DV_B]
        ob = o0 - lam * o1
        on = ob * lax.rsqrt(jnp.mean(ob * ob, axis=-1, keepdims=True) + NORM_EPS)
        o_ref[...] = on * g_ref[...] * (1.0 - lam_init)


def diff_attn_prompt(zb, aq, ak, lam_p, norm_g, nb, t, lam_init):
    n_tok = zb.shape[0]
    tq = tk = ATTN_TILE
    nq = t // tq
    kv_map = lambda off: (lambda b, h, i, j: (b * nq + jnp.minimum(i, j), off + h))
    return pl.pallas_call(
        functools.partial(_attn_prompt_kernel, lam_init=lam_init, tq=tq, tk=tk),
        grid=(nb, H_B, nq, nq),
        in_specs=[
            pl.BlockSpec(lam_p.shape, lambda b, h, i, j: (0, 0)),
            pl.BlockSpec((1, DV_B), lambda b, h, i, j: (0, h)),
            pl.BlockSpec((tq, DV_B), lambda b, h, i, j: (b * nq + i, h)),
            pl.BlockSpec((None, tq, DV_B), lambda b, h, i, j: (h, i, 0)),
            pl.BlockSpec((tk, DV_B), kv_map(H_B)),
            pl.BlockSpec((None, tk, DV_B), lambda b, h, i, j: (h, jnp.minimum(i, j), 0)),
            pl.BlockSpec((tk, DV_B), kv_map(2 * H_B)),
        ],
        out_specs=pl.BlockSpec((tq, DV_B), lambda b, h, i, j: (b * nq + i, h)),
        out_shape=jax.ShapeDtypeStruct((n_tok, W_B), F32),
        scratch_shapes=[pltpu.VMEM((2 * tq, 2 * DV_B), BF16), pltpu.VMEM((2 * tq, V7X_LANES), F32),
                        pltpu.VMEM((2 * tq, 2 * DV_B), F32)],
        compiler_params=_cparams("parallel", "parallel", "parallel", "arbitrary"),
        name="diff_attn_prompt",
    )(lam_p, norm_g.reshape(1, W_B), zb, aq, zb, ak, zb)


def _attn_sample_kernel(*refs, lam_init, past_len, t_new, pages_per_step):
    G = pages_per_step
    pt_ref, lam_ref, g_ref, q_ref, kn_ref, vn_ref = refs[:6]
    k_refs = refs[6:6 + G]
    v_refs = refs[6 + G:6 + 2 * G]
    _prev, o_ref, qs_sc, bias_sc, m_sc, l_sc, acc_sc = refs[6 + 2 * G:]
    del pt_ref
    ps = pl.program_id(1)
    rows_per_head = 2 * t_new
    n_rows = H_B * rows_per_head
    page_rows = PAGE_SIZE * H_B
    reps = page_rows // V7X_LANES

    r_lane = lax.broadcasted_iota(jnp.int32, (n_rows, V7X_LANES), 0)
    slope = _alibi_slope(r_lane // rows_per_head).astype(F32)

    @pl.when(ps == 0)
    def _():
        q = q_ref[...] * (DK_B ** -0.5)
        lane = lax.broadcasted_iota(jnp.int32, (t_new, DV_B), 1)
        for h in range(H_B):
            qh = q[:, h * DV_B:(h + 1) * DV_B]
            qs_sc[h * rows_per_head:(h + 1) * rows_per_head, :] = jnp.concatenate(
                [jnp.where(lane < DK_B, qh, 0.0), jnp.where(lane >= DK_B, qh, 0.0)], axis=0).astype(BF16)
        rr = lax.broadcasted_iota(jnp.int32, (n_rows, page_rows), 0)
        cc = lax.broadcasted_iota(jnp.int32, (n_rows, page_rows), 1)
        rel = cc // H_B - (past_len + rr % t_new)
        bias = _alibi_slope(rr // rows_per_head).astype(F32) * rel.astype(F32)
        bias_sc[...] = jnp.where(cc % H_B == rr // rows_per_head, bias, NEG_INF)
        m_sc[...] = jnp.full_like(m_sc, NEG_INF)
        l_sc[...] = jnp.zeros_like(l_sc)
        acc_sc[...] = jnp.zeros_like(acc_sc)

    qs = qs_sc[...]
    m, l, acc = m_sc[...], l_sc[...], acc_sc[...]
    s_pages = []
    m_new = m
    for g in range(G):
        base = ((ps * G + g) * PAGE_SIZE).astype(F32)
        s = _bdot_nt(qs, k_refs[g][...]) + (bias_sc[...] + pltpu.repeat(slope * base, reps, 1))
        m_new = jnp.maximum(m_new, jnp.max(s, axis=-1, keepdims=True))
        s_pages.append(s)
    alpha = jnp.exp(m - m_new)
    l = alpha * l
    acc = alpha * acc
    for g in range(G):
        p = jnp.exp(s_pages[g] - pltpu.repeat(m_new, reps, 1))
        l = l + jnp.sum(p, axis=-1, keepdims=True)
        acc = acc + _bdot(p, v_refs[g][...])
    m = m_new

    @pl.when(ps < pl.num_programs(1) - 1)
    def _():
        m_sc[...] = m
        l_sc[...] = l
        acc_sc[...] = acc

    @pl.when(ps == pl.num_programs(1) - 1)
    def _():
        rnd = lambda a: a.astype(BF16).astype(F32)
        kn = rnd(kn_ref[...])
        vn = rnd(vn_ref[...])
        qf = qs.astype(F32)
        sn = jnp.concatenate(
            [lax.dot_general(qf[h * rows_per_head:(h + 1) * rows_per_head, :], kn[:, h * DV_B:(h + 1) * DV_B],
                             (((1,), (1,)), ((), ())), preferred_element_type=F32) for h in range(H_B)],
            axis=0)
        rr = lax.broadcasted_iota(jnp.int32, (n_rows, t_new), 0)
        rel = lax.broadcasted_iota(jnp.int32, (n_rows, t_new), 1) - rr % t_new
        sn = jnp.where(rel <= 0, sn + _alibi_slope(rr // rows_per_head).astype(F32) * rel.astype(F32), NEG_INF)
        m_fin = jnp.maximum(m, jnp.max(sn, axis=-1, keepdims=True))
        alpha = jnp.exp(m - m_fin)
        pn = rnd(jnp.exp(sn - m_fin[:, 0:1]))
        l_fin = alpha * l + jnp.sum(pn, axis=-1, keepdims=True)
        pv = jnp.concatenate(
            [jnp.dot(pn[h * rows_per_head:(h + 1) * rows_per_head, :], vn[:, h * DV_B:(h + 1) * DV_B],
                     preferred_element_type=F32) for h in range(H_B)], axis=0)
        o = (alpha * acc + pv) / l_fin

        lam = _diff_lambda(lam_ref, lam_init)
        outs = []
        for h in range(H_B):
            o0 = o[h * rows_per_head:h * rows_per_head + t_new, :]
            o1 = o[h * rows_per_head + t_new:(h + 1) * rows_per_head, :]
            ob = o0 - lam * o1
            outs.append(ob * lax.rsqrt(jnp.mean(ob * ob, axis=-1, keepdims=True) + NORM_EPS))
        o_ref[...] = jnp.concatenate(outs, axis=-1) * g_ref[...] * (1.0 - lam_init)


def diff_attn_sample(zb, cache_k, cache_v, layer, page_table, lam_p, norm_g, nb, t_new, row0, lam_init, prev):
    n_tok = zb.shape[0]
    n_pages = page_table.shape[1]
    G = PAGES_PER_STEP
    rb0 = row0 // t_new
    n_rows = 2 * H_B * t_new
    page_rows = PAGE_SIZE * H_B
    ck = cache_k.reshape(cache_k.shape[0], cache_k.shape[1], page_rows, DV_B)
    cv = cache_v.reshape(cache_v.shape[0], cache_v.shape[1], page_rows, DV_B)

    def page_map(g):
        return lambda b, p, pt: (layer, pt[b, p * G + g], 0, 0)

    page_blk = (None, None, page_rows, DV_B)
    in_specs = [
        pl.BlockSpec(lam_p.shape, lambda b, p, pt: (0, 0)),
        pl.BlockSpec((1, W_B), lambda b, p, pt: (0, 0)),
        pl.BlockSpec((t_new, W_B), lambda b, p, pt: (rb0 + b, 0)),
        pl.BlockSpec((t_new, W_B), lambda b, p, pt: (rb0 + b, 1)),
        pl.BlockSpec((t_new, W_B), lambda b, p, pt: (rb0 + b, 2)),
    ]
    in_specs += [pl.BlockSpec(page_blk, page_map(g)) for g in range(G)]
    in_specs += [pl.BlockSpec(page_blk, page_map(g)) for g in range(G)]
    in_specs += [pl.BlockSpec(memory_space=pl.ANY)]
    args = [page_table, lam_p, norm_g.reshape(1, W_B), zb, zb, zb] + [ck] * G + [cv] * G + [prev]
    grid_spec = pltpu.PrefetchScalarGridSpec(
        num_scalar_prefetch=1,
        grid=(nb, n_pages // G),
        in_specs=in_specs,
        out_specs=pl.BlockSpec((t_new, W_B), lambda b, p, pt: (rb0 + b, 0)),
        scratch_shapes=[pltpu.VMEM((n_rows, DV_B), BF16), pltpu.VMEM((n_rows, page_rows), F32),
                        pltpu.VMEM((n_rows, V7X_LANES), F32), pltpu.VMEM((n_rows, V7X_LANES), F32),
                        pltpu.VMEM((n_rows, DV_B), F32)],
    )
    return pl.pallas_call(
        functools.partial(_attn_sample_kernel, lam_init=lam_init, past_len=n_pages * PAGE_SIZE,
                          t_new=t_new, pages_per_step=G),
        grid_spec=grid_spec,
        out_shape=jax.ShapeDtypeStruct((n_tok, W_B), F32),
        input_output_aliases={len(args) - 1: 0},
        compiler_params=_cparams("parallel", "arbitrary"),
        name="diff_attn_sample",
    )(*args)


def _s5_disc_kernel(are_ref, aim_ref, ldt_ref, bre_ref, bim_ref, lre_ref, lim_ref, bbre_ref, bbim_ref):
    a_re = are_ref[...]
    a_im = aim_ref[...]
    dt = jnp.exp(ldt_ref[...])
    mag = jnp.exp(a_re * dt)
    lb_re = mag * jnp.cos(a_im * dt)
    lb_im = mag * jnp.sin(a_im * dt)
    den = a_re * a_re + a_im * a_im
    xr = lb_re - 1.0
    fr = (xr * a_re + lb_im * a_im) / den
    fi = (lb_im * a_re - xr * a_im) / den
    lre_ref[...] = lb_re
    lim_ref[...] = lb_im
    b_re = bre_ref[...]
    b_im = bim_ref[...]
    bbre_ref[...] = fr * b_re - fi * b_im
    bbim_ref[...] = fr * b_im + fi * b_re


def s5_discretize(a_re, a_im, log_dt, b_re, b_im):
    depth = a_re.shape[0]
    gp = G_C * P_C
    flat = lambda a: a.reshape(depth, 1, gp)
    ldt = jnp.broadcast_to(log_dt[:, :, None], (depth, G_C, P_C)).reshape(depth, 1, gp)
    tr = lambda b: jnp.transpose(b, (0, 3, 1, 2)).reshape(depth, GC, gp)
    shp1 = jax.ShapeDtypeStruct((depth, 1, gp), F32)
    shpb = jax.ShapeDtypeStruct((depth, GC, gp), F32)
    return pl.pallas_call(_s5_disc_kernel, out_shape=[shp1, shp1, shpb, shpb], name="s5_discretize")(
        flat(a_re), flat(a_im), ldt, tr(b_re), tr(b_im))


def _block_diag_in(bb):
    depth, _, gp = bb.shape
    tiled = jnp.tile(bb, (1, G_C, 1)).reshape(depth, G_C, GC, gp)
    grp_r = jnp.arange(G_C)[:, None, None]
    grp_c = (jnp.arange(gp) // P_C)[None, None, :]
    return jnp.where(grp_r == grp_c, tiled, 0.0).reshape(depth, G_C * GC, gp)


def _block_diag_out(c):
    depth = c.shape[0]
    ct = jnp.transpose(c, (0, 1, 3, 2)).reshape(depth, G_C * P_C, GC)
    tiled = jnp.tile(ct, (1, 1, G_C))
    grp_r = (jnp.arange(G_C * P_C) // P_C)[:, None]
    grp_c = (jnp.arange(G_C * GC) // GC)[None, :]
    return jnp.where(grp_r == grp_c, tiled, 0.0)


def _cmul_add(a_re, a_im, x_re, x_im, y_re, y_im):
    return y_re + (a_re * x_re - a_im * x_im), y_im + (a_re * x_im + a_im * x_re)


def _block_scan(x_re, x_im, pw, row8):
    for d, (a_re, a_im) in zip((1, 2, 4), pw):
        sh_re = jnp.where(row8 >= d, pltpu.roll(x_re, d, 0), 0.0)
        sh_im = jnp.where(row8 >= d, pltpu.roll(x_im, d, 0), 0.0)
        x_re, x_im = _cmul_add(a_re, a_im, sh_re, sh_im, x_re, x_im)
    return x_re, x_im


def _s5_kernel(*refs, rows, independent, aliased):
    (u_ref, lre_ref, lim_ref, bb_ref, cc_ref, d_ref, gw_ref, gb_ref, s0re_ref, s0im_ref) = refs[:10]
    rest = refs[10:]
    if aliased:
        rest = rest[1:]
    y_ref, s1re_ref, s1im_ref, st_sc, car_sc = rest
    gp = G_C * P_C
    nblk = rows // V7X_SUBLANES
    ti = pl.program_id(1)

    lam_re = lre_ref[...]
    lam_im = lim_ref[...]
    l2_re, l2_im = lam_re * lam_re - lam_im * lam_im, 2.0 * lam_re * lam_im
    l4_re, l4_im = l2_re * l2_re - l2_im * l2_im, 2.0 * l2_re * l2_im
    pw = ((lam_re, lam_im), (l2_re, l2_im), (l4_re, l4_im))
    row8 = lax.broadcasted_iota(jnp.int32, (V7X_SUBLANES, gp), 0)
    pk_re, pk_im = _block_scan(jnp.where(row8 == 0, lam_re, 0.0), jnp.where(row8 == 0, lam_im, 0.0), pw, row8)

    u = u_ref[...]
    st_sc[...] = _bdot(u, bb_ref[...])

    if not independent:
        @pl.when(ti == 0)
        def _():
            car_sc[0:1, :] = s0re_ref[...]
            car_sc[1:2, :] = s0im_ref[...]

    def body(bi, carry):
        r0 = pl.multiple_of(bi * V7X_SUBLANES, V7X_SUBLANES)
        x_re = st_sc[pl.ds(r0, V7X_SUBLANES), 0:gp]
        x_im = st_sc[pl.ds(r0, V7X_SUBLANES), gp:2 * gp]
        x_re, x_im = _block_scan(x_re, x_im, pw, row8)
        if independent:
            c_re = s0re_ref[pl.ds(bi, 1), :]
            c_im = s0im_ref[pl.ds(bi, 1), :]
        else:
            c_re, c_im = carry
        s_re, s_im = _cmul_add(pk_re, pk_im, c_re, c_im, x_re, x_im)
        st_sc[pl.ds(r0, V7X_SUBLANES), 0:gp] = s_re
        st_sc[pl.ds(r0, V7X_SUBLANES), gp:2 * gp] = s_im
        last_re = s_re[V7X_SUBLANES - 1:V7X_SUBLANES, :]
        last_im = s_im[V7X_SUBLANES - 1:V7X_SUBLANES, :]
        if independent:
            s1re_ref[pl.ds(bi, 1), :] = last_re
            s1im_ref[pl.ds(bi, 1), :] = last_im
            return carry
        return last_re, last_im

    if independent:
        lax.fori_loop(0, nblk, body, 0)
    else:
        c_re, c_im = lax.fori_loop(0, nblk, body, (car_sc[0:1, :], car_sc[1:2, :]))
        car_sc[0:1, :] = c_re
        car_sc[1:2, :] = c_im

        @pl.when(ti == pl.num_programs(1) - 1)
        def _():
            s1re_ref[...] = c_re
            s1im_ref[...] = c_im

    y = _bdot(st_sc[...], cc_ref[...]) + d_ref[...] * u
    z = _bdot(jax.nn.gelu(y), gw_ref[...]) + gb_ref[...]
    y_ref[...] = z[:, :W_C] * jax.nn.sigmoid(z[:, W_C:])


def s5_mixer(zc, lam_re, lam_im, bb, cc, d, glu_w, glu_b, s0_re, s0_im, nb, t, row0, prev=None):
    n_tok = zc.shape[0]
    gp = G_C * P_C
    independent = t == V7X_SUBLANES
    aliased = prev is not None
    if independent:
        seqs = S5_SAMPLE_SEQS
        rows = seqs * t
        grid = (nb // seqs, 1)
        st_spec = pl.BlockSpec((seqs, gp), lambda b, i: (b, 0))
    else:
        rows = S5_TILE
        grid = (nb, t // rows)
        st_spec = pl.BlockSpec((1, gp), lambda b, i: (b, 0))
    nt = grid[1]
    rb0 = row0 // rows
    row_map = lambda b, i: (rb0 + b * nt + i, 0)
    const = lambda b, i: (0, 0)
    in_specs = [
        pl.BlockSpec((rows, W_C), row_map),
        pl.BlockSpec((1, gp), const),
        pl.BlockSpec((1, gp), const),
        pl.BlockSpec(bb.shape, const),
        pl.BlockSpec(cc.shape, const),
        pl.BlockSpec((1, W_C), const),
        pl.BlockSpec(glu_w.shape, const),
        pl.BlockSpec((1, 2 * W_C), const),
        st_spec,
        st_spec,
    ]
    args = [zc, lam_re, lam_im, bb, cc, d.reshape(1, W_C), glu_w, glu_b.reshape(1, 2 * W_C), s0_re, s0_im]
    aliases = {}
    if aliased:
        in_specs.append(pl.BlockSpec(memory_space=pl.ANY))
        args.append(prev)
        aliases = {len(args) - 1: 0}
    if independent:
        s1_shape = jax.ShapeDtypeStruct((nb, gp), F32)
    else:
        s1_shape = jax.ShapeDtypeStruct((nb, 1, gp), F32)
        st_out = pl.BlockSpec((None, 1, gp), lambda b, i: (b, 0, 0))
    out_specs = [pl.BlockSpec((rows, W_C), row_map)] + ([st_spec, st_spec] if independent else [st_out, st_out])
    if not independent:
        args[8] = s0_re.reshape(nb, 1, gp)
        args[9] = s0_im.reshape(nb, 1, gp)
        in_specs[8] = in_specs[9] = pl.BlockSpec((None, 1, gp), lambda b, i: (b, 0, 0))
    yc, s1_re, s1_im = pl.pallas_call(
        functools.partial(_s5_kernel, rows=rows, independent=independent, aliased=aliased),
        grid=grid,
        in_specs=in_specs,
        out_specs=out_specs,
        out_shape=[jax.ShapeDtypeStruct((n_tok, W_C), F32), s1_shape, s1_shape],
        scratch_shapes=[pltpu.VMEM((rows, 2 * gp), F32), pltpu.VMEM((V7X_SUBLANES, gp), F32)],
        input_output_aliases=aliases,
        compiler_params=_cparams("parallel", "arbitrary"),
        name="s5_sample" if independent else "s5_prompt",
    )(*args)
    return yc, s1_re.reshape(nb, G_C, P_C), s1_im.reshape(nb, G_C, P_C)


def _xattn_kernel(*refs, tq):
    q_ref, mk_ref, mv_ref = refs[:3]
    o_ref = refs[-1]
    q = q_ref[...] * (DH_X ** -0.5)
    pad = tq < V7X_BF16_ROWS
    if pad:
        q = jnp.concatenate([q, jnp.zeros((V7X_BF16_ROWS - tq, q.shape[1]), F32)], axis=0)
    outs = []
    for h in range(H_X):
        s = _bdot_nt(q[:, h * DH_X:(h + 1) * DH_X], mk_ref[h])
        m = jnp.max(s, axis=-1, keepdims=True)
        p = jnp.exp(s - m)
        l = jnp.sum(p, axis=-1, keepdims=True)
        outs.append(_bdot(p, mv_ref[h]) / l)
    o = jnp.concatenate(outs, axis=-1)
    o_ref[...] = o[:tq] if pad else o


def cross_attn(qx, mk, mv, layer, n_mem, nb, t, row0, prev=None):
    n_tok, d = qx.shape
    tq = min(t, XATTN_TILE)
    nq = t // tq
    rb0 = row0 // tq
    aliased = prev is not None
    mem_spec = pl.BlockSpec((None, H_X, n_mem, DH_X), lambda b, i: (layer, 0, b, 0))
    in_specs = [pl.BlockSpec((tq, d), lambda b, i: (rb0 + b * nq + i, 0)), mem_spec, mem_spec]
    args = [qx, mk, mv]
    aliases = {}
    if aliased:
        in_specs.append(pl.BlockSpec(memory_space=pl.ANY))
        args.append(prev)
        aliases = {3: 0}
    return pl.pallas_call(
        functools.partial(_xattn_kernel, tq=tq),
        grid=(nb, nq),
        in_specs=in_specs,
        out_specs=pl.BlockSpec((tq, d), lambda b, i: (rb0 + b * nq + i, 0)),
        out_shape=jax.ShapeDtypeStruct((n_tok, d), F32),
        input_output_aliases=aliases,
        compiler_params=_cparams("parallel", "arbitrary"),
        name="cross_attn_sample" if aliased else "cross_attn_prompt",
    )(*args)


def kernel(x_prompt, x_sample, mem_prompt, cache_k, cache_v, page_table, cache_mem_k, cache_mem_v,
           state_mlstm_c, state_mlstm_n, state_mlstm_m, state_ssm_re, state_ssm_im,
           ln_g, ln_b, ffn1_wg, ffn1_wu, ffn1_wd, ffn2_wg, ffn2_wu, ffn2_wd, w_in, b_in,
           mlstm_norm_g, diff_lam, diff_norm_g, ssm_a_re, ssm_a_im, ssm_log_dt, ssm_b_re, ssm_b_im,
           ssm_c_re, ssm_c_im, ssm_d, ssm_glu_w, ssm_glu_b, w_out, cross_wq, cross_wk, cross_wv, cross_wo):
    bp, tp, d = x_prompt.shape
    bs, ts, _ = x_sample.shape
    depth = ln_g.shape[0]
    n_mem = mem_prompt.shape[1]
    n_p = bp * tp
    gp = G_C * P_C
    alpha = (2.0 * depth) ** 0.25

    cast = lambda w: w.astype(BF16)
    ffn1_wg, ffn1_wu, ffn1_wd = cast(ffn1_wg), cast(ffn1_wu), cast(ffn1_wd)
    ffn2_wg, ffn2_wu, ffn2_wd = cast(ffn2_wg), cast(ffn2_wu), cast(ffn2_wd)
    w_out_b, wq_b, wo_b, glu_w_b = cast(w_out), cast(cross_wq), cast(cross_wo), cast(ssm_glu_w)
    w_in_p, b_in_p = _pack_w_in(w_in, b_in)
    lam_re, lam_im, bb_re, bb_im = s5_discretize(ssm_a_re, ssm_a_im, ssm_log_dt, ssm_b_re, ssm_b_im)
    bb = jnp.concatenate([_block_diag_in(bb_re), _block_diag_in(bb_im)], axis=-1).astype(BF16)
    cc = jnp.concatenate([_block_diag_out(ssm_c_re), -_block_diag_out(ssm_c_im)], axis=1).astype(BF16)
    alibi_q, alibi_k = _alibi_tables(tp)

    p_mk, p_mv, p_mkh, p_mvh = mem_kv(mem_prompt.reshape(bp * n_mem, d), cast(cross_wk), cast(cross_wv))
    p_mk = p_mk.reshape(depth, bp, n_mem, H_X, DH_X)
    p_mv = p_mv.reshape(depth, bp, n_mem, H_X, DH_X)
    head_major = lambda a: jnp.transpose(a, (0, 3, 1, 2, 4)).astype(BF16).reshape(depth, H_X, bs * n_mem, DH_X)
    s_mkh, s_mvh = head_major(cache_mem_k), head_major(cache_mem_v)

    x = jnp.concatenate([x_prompt.reshape(n_p, d), x_sample.reshape(bs * ts, d)], axis=0)
    zeros_s = jnp.zeros((bp, gp), F32)
    p_k = jnp.zeros((depth, n_p, H_B, DV_B), F32)
    p_v = jnp.zeros((depth, n_p, H_B, DV_B), F32)
    p_st, s_st = [], []
    for l in range(depth):
        lam_init = 0.8 - 0.6 * math.exp(-0.3 * l)
        x = ffn_ln(x, ffn1_wg[l], ffn1_wu[l], ffn1_wd[l], ln_g[l, 0], ln_b[l, 0], alpha)
        za, zb, zc, zg, p_k, p_v, s_k, s_v = proj_in(x, w_in_p[l], b_in_p[l], p_k, p_v, l, n_p)

        ya, pc, pn, pm = mlstm(za, zg, mlstm_norm_g[l], bp, tp, 0)
        ya, sc, sn, sm = mlstm(za, zg, mlstm_norm_g[l], bs, ts, n_p,
                               init=(state_mlstm_c[l], state_mlstm_n[l], state_mlstm_m[l]), prev=ya)

        yb = diff_attn_prompt(zb, alibi_q, alibi_k, diff_lam[l], diff_norm_g[l], bp, tp, lam_init)
        yb = diff_attn_sample(zb, cache_k, cache_v, l, page_table, diff_lam[l], diff_norm_g[l],
                              bs, ts, n_p, lam_init, yb)

        s5_args = (lam_re[l], lam_im[l], bb[l], cc[l], ssm_d[l], glu_w_b[l], ssm_glu_b[l])
        yc, psr, psi = s5_mixer(zc, *s5_args, zeros_s, zeros_s, bp, tp, 0)
        yc, ssr, ssi = s5_mixer(zc, *s5_args, state_ssm_re[l].reshape(bs, gp), state_ssm_im[l].reshape(bs, gp),
                                bs, ts, n_p, prev=yc)

        x, qx = out_ln([ya, yb, yc], w_out_b[l], x, ln_g[l, 1], ln_b[l, 1], alpha, w_next=wq_b[l])
        o = cross_attn(qx, p_mkh, p_mvh, l, n_mem, bp, tp, 0)
        o = cross_attn(qx, s_mkh, s_mvh, l, n_mem, bs, ts, n_p, prev=o)
        x = out_ln([o], wo_b[l], x, ln_g[l, 2], ln_b[l, 2], alpha)
        x = ffn_ln(x, ffn2_wg[l], ffn2_wu[l], ffn2_wd[l], ln_g[l, 3], ln_b[l, 3], alpha)

        p_st.append((pc, pn, pm, psr, psi))
        s_st.append((s_k.reshape(bs, ts, H_B, DV_B), s_v.reshape(bs, ts, H_B, DV_B), sc, sn, sm, ssr, ssi))

    p_c, p_n, p_m, p_sr, p_si = [jnp.stack(a) for a in zip(*p_st)]
    s_k, s_v, s_c, s_n, s_m, s_sr, s_si = [jnp.stack(a) for a in zip(*s_st)]
    p_k = p_k.reshape(depth, bp, tp, H_B, DV_B)
    p_v = p_v.reshape(depth, bp, tp, H_B, DV_B)
    yp = x[:n_p].reshape(bp, tp, d)
    ys = x[n_p:].reshape(bs, ts, d)
    return (yp, ys, p_k, p_v, p_mk, p_mv, p_c, p_n, p_m, p_sr, p_si, s_k, s_v, s_c, s_n, s_m, s_sr, s_si)
```

```python
import functools
import math

import jax
import jax.numpy as jnp
from jax import lax
from jax.experimental import pallas as pl
from jax.experimental.pallas import tpu as pltpu

F32 = jnp.float32
BF16 = jnp.bfloat16

D_MODEL = 1024
PAGE_SIZE = 128
W_A = D_MODEL // 4
W_B = D_MODEL // 2
W_C = D_MODEL - W_A - W_B
H_A = 4
DH_A = W_A // H_A
MLSTM_CHUNK = 64
H_B = 4
DV_B = W_B // H_B
DK_B = DV_B // 2
GC = 16
G_C = W_C // GC
P_C = 64
H_X = 4
DH_X = D_MODEL // H_X
LN_EPS = 1e-5
NORM_EPS = 1e-6

OFF_AQ = 0
OFF_AK = OFF_AQ + W_A
OFF_AV = OFF_AK + W_A
OFF_AO = OFF_AV + W_A
OFF_AI = OFF_AO + W_A
OFF_AF = OFF_AI + H_A
OFF_BQ = OFF_AF + H_A
OFF_BK = OFF_BQ + H_B * 2 * DK_B
OFF_BV = OFF_BK + H_B * 2 * DK_B
OFF_CU = OFF_BV + W_B
N_IN = OFF_CU + W_C

V7X_LANES = 128
V7X_SUBLANES = 8
V7X_BF16_ROWS = 2 * V7X_SUBLANES
V7X_VMEM_LIMIT_BYTES = 56 * 1024 * 1024

TOKEN_TILE = 1024
FF_TILE = 256
PROJ_TILE = 512
ATTN_TILE = 512
ATTN_ROWS = 256
XATTN_TILE = 512
S5_TILE = 512
PAGES_PER_STEP = 16
XATTN_SAMPLE_SEQS = 8
S5_SAMPLE_SEQS = 16
MLSTM_PROMPT_CHUNK = 512
MLSTM_UNITS = 8
ALIBI_SPLIT = 64

NEG_INF = float("-inf")


def _cparams(*sem):
    return pltpu.CompilerParams(dimension_semantics=sem, vmem_limit_bytes=V7X_VMEM_LIMIT_BYTES)


def _bdot(a, b):
    return jnp.dot(a.astype(BF16), b.astype(BF16), preferred_element_type=F32)


def _bdot_nt(a, b):
    return lax.dot_general(a.astype(BF16), b.astype(BF16), (((1,), (1,)), ((), ())),
                           preferred_element_type=F32)


def _layer_norm(y, g, b):
    mu = jnp.mean(y, axis=-1, keepdims=True)
    yc = y - mu
    var = jnp.mean(yc * yc, axis=-1, keepdims=True)
    return yc * lax.rsqrt(var + LN_EPS) * g + b


def _ffn_ln_kernel(x_ref, wg_ref, wu_ref, wd_ref, g_ref, b_ref, o_ref, h_sc, *, alpha, tf):
    xb = x_ref[...].astype(BF16)
    for c in range(wg_ref.shape[1] // tf):
        cols = slice(c * tf, (c + 1) * tf)
        hg = jnp.dot(xb, wg_ref[:, cols], preferred_element_type=F32)
        hu = jnp.dot(xb, wu_ref[:, cols], preferred_element_type=F32)
        h_sc[:, cols] = ((hg * jax.nn.sigmoid(hg)) * hu).astype(BF16)
    ff = jnp.dot(h_sc[...], wd_ref[...], preferred_element_type=F32)
    o_ref[...] = _layer_norm(alpha * x_ref[...] + 0.5 * ff, g_ref[...], b_ref[...])


def ffn_ln(x, wg, wu, wd, g, b, alpha):
    n, d = x.shape
    dff = wg.shape[1]
    tm, tf = TOKEN_TILE, FF_TILE
    resident = pl.Buffered(1)
    return pl.pallas_call(
        functools.partial(_ffn_ln_kernel, alpha=alpha, tf=tf),
        grid=(n // tm,),
        in_specs=[
            pl.BlockSpec((tm, d), lambda i: (i, 0)),
            pl.BlockSpec((d, dff), lambda i: (0, 0), pipeline_mode=resident),
            pl.BlockSpec((d, dff), lambda i: (0, 0), pipeline_mode=resident),
            pl.BlockSpec((dff, d), lambda i: (0, 0), pipeline_mode=resident),
            pl.BlockSpec((1, d), lambda i: (0, 0)),
            pl.BlockSpec((1, d), lambda i: (0, 0)),
        ],
        out_specs=pl.BlockSpec((tm, d), lambda i: (i, 0)),
        out_shape=jax.ShapeDtypeStruct((n, d), F32),
        scratch_shapes=[pltpu.VMEM((tm, dff), BF16)],
        compiler_params=_cparams("parallel"),
        name="ffn_ln",
    )(x, wg, wu, wd, g.reshape(1, d), b.reshape(1, d))


N_ZA = 4 * W_A
N_ZB = 3 * W_B
N_ZC = W_C
N_ZG = V7X_LANES
N_PROJ = N_ZA + N_ZB + N_ZC + N_ZG


def _proj_in_kernel(x_ref, w_ref, b_ref, _kp_prev, _vp_prev, za_ref, zb_ref, zc_ref, zg_ref,
                    kp_ref, vp_ref, ks_ref, vs_ref, *, prompt_tiles):
    i = pl.program_id(0)
    xb = x_ref[...].astype(BF16)
    off = 0
    for ref in (za_ref, zb_ref, zc_ref, zg_ref):
        width = ref.shape[1]
        ref[...] = jnp.dot(xb, w_ref[:, off:off + width], preferred_element_type=F32) + b_ref[:, off:off + width]
        off += width

    def write_kv(k_out, v_out):
        for h in range(H_B):
            k_out[:, h, :] = zb_ref[:, W_B + h * DV_B:W_B + (h + 1) * DV_B]
            v_out[:, h, :] = zb_ref[:, 2 * W_B + h * DV_B:2 * W_B + (h + 1) * DV_B]

    @pl.when(i < prompt_tiles)
    def _():
        write_kv(kp_ref, vp_ref)

    @pl.when(i >= prompt_tiles)
    def _():
        write_kv(ks_ref, vs_ref)


def proj_in(x, w, b, kp_prev, vp_prev, layer, n_prompt):
    n, d = x.shape
    tm = PROJ_TILE
    pt = n_prompt // tm
    widths = (N_ZA, N_ZB, N_ZC, N_ZG)
    kv_blk = (tm, H_B, DV_B)
    p_map = lambda i: (layer, jnp.minimum(i, pt - 1), 0, 0)
    s_map = lambda i: (jnp.maximum(i - pt, 0), 0, 0)
    kv_s = jax.ShapeDtypeStruct((n - n_prompt, H_B, DV_B), F32)
    return pl.pallas_call(
        functools.partial(_proj_in_kernel, prompt_tiles=pt),
        grid=(n // tm,),
        in_specs=[
            pl.BlockSpec((tm, d), lambda i: (i, 0)),
            pl.BlockSpec((d, N_PROJ), lambda i: (0, 0)),
            pl.BlockSpec((1, N_PROJ), lambda i: (0, 0)),
            pl.BlockSpec(memory_space=pl.ANY),
            pl.BlockSpec(memory_space=pl.ANY),
        ],
        out_specs=[pl.BlockSpec((tm, wd), lambda i: (i, 0)) for wd in widths]
        + [pl.BlockSpec((None,) + kv_blk, p_map)] * 2 + [pl.BlockSpec(kv_blk, s_map)] * 2,
        out_shape=[jax.ShapeDtypeStruct((n, wd), F32) for wd in widths]
        + [jax.ShapeDtypeStruct(kp_prev.shape, F32)] * 2 + [kv_s, kv_s],
        input_output_aliases={3: 4, 4: 5},
        compiler_params=_cparams("arbitrary"),
        name="proj_in",
    )(x, w, b, kp_prev, vp_prev)


def _pack_w_in(w_in, b_in):
    def cols(a):
        pad = jnp.zeros(a.shape[:-1] + (N_ZG - 2 * H_A,), a.dtype)
        return jnp.concatenate([a[..., OFF_AQ:OFF_AI], a[..., OFF_BQ:OFF_CU], a[..., OFF_CU:N_IN],
                                a[..., OFF_AI:OFF_BQ], pad], axis=-1)
    return cols(w_in).astype(BF16), cols(b_in)[:, None, :]


def _mem_kv_kernel(x_ref, wk_ref, wv_ref, mk_ref, mv_ref, mkh_ref, mvh_ref):
    xb = x_ref[...].astype(BF16)
    for w_ref, o_ref, oh_ref in ((wk_ref, mk_ref, mkh_ref), (wv_ref, mv_ref, mvh_ref)):
        r = jnp.dot(xb, w_ref[...], preferred_element_type=F32)
        for h in range(H_X):
            o_ref[:, h, :] = r[:, h * DH_X:(h + 1) * DH_X]
            oh_ref[h] = r[:, h * DH_X:(h + 1) * DH_X].astype(BF16)


def mem_kv(x, wk, wv):
    m, d = x.shape
    depth = wk.shape[0]
    tm = min(m, TOKEN_TILE)
    shp = jax.ShapeDtypeStruct((depth, m, H_X, DH_X), F32)
    shp_h = jax.ShapeDtypeStruct((depth, H_X, m, DH_X), BF16)
    w_spec = pl.BlockSpec((None, d, d), lambda i, l: (l, 0, 0))
    o_spec = pl.BlockSpec((None, tm, H_X, DH_X), lambda i, l: (l, i, 0, 0))
    oh_spec = pl.BlockSpec((None, H_X, tm, DH_X), lambda i, l: (l, 0, i, 0))
    return pl.pallas_call(
        _mem_kv_kernel,
        grid=(m // tm, depth),
        in_specs=[pl.BlockSpec((tm, d), lambda i, l: (i, 0)), w_spec, w_spec],
        out_specs=[o_spec, o_spec, oh_spec, oh_spec],
        out_shape=[shp, shp, shp_h, shp_h],
        compiler_params=_cparams("parallel", "arbitrary"),
        name="mem_kv",
    )(x, wk, wv)


def _out_ln_kernel(*refs, n_parts, alpha, has_next):
    parts = refs[:n_parts]
    w_ref, x_ref, g_ref, b_ref = refs[n_parts:n_parts + 4]
    rest = refs[n_parts + 4:]
    if has_next:
        wn_ref, o_ref, q_ref = rest
    else:
        (o_ref,) = rest
    acc = None
    off = 0
    for p in parts:
        width = p.shape[1]
        t = jnp.dot(p[...].astype(BF16), w_ref[off:off + width, :], preferred_element_type=F32)
        acc = t if acc is None else acc + t
        off += width
    y = _layer_norm(alpha * x_ref[...] + acc, g_ref[...], b_ref[...])
    o_ref[...] = y
    if has_next:
        q_ref[...] = jnp.dot(y.astype(BF16), wn_ref[...], preferred_element_type=F32)


def out_ln(parts, w, x, g, b, alpha, w_next=None):
    n, d = x.shape
    tm = PROJ_TILE
    has_next = w_next is not None
    in_specs = [pl.BlockSpec((tm, p.shape[1]), lambda i: (i, 0)) for p in parts]
    in_specs += [
        pl.BlockSpec(w.shape, lambda i: (0, 0)),
        pl.BlockSpec((tm, d), lambda i: (i, 0)),
        pl.BlockSpec((1, d), lambda i: (0, 0)),
        pl.BlockSpec((1, d), lambda i: (0, 0)),
    ]
    args = list(parts) + [w, x, g.reshape(1, d), b.reshape(1, d)]
    out_specs = [pl.BlockSpec((tm, d), lambda i: (i, 0))]
    out_shape = [jax.ShapeDtypeStruct((n, d), F32)]
    if has_next:
        in_specs.append(pl.BlockSpec(w_next.shape, lambda i: (0, 0)))
        args.append(w_next)
        out_specs.append(pl.BlockSpec((tm, w_next.shape[1]), lambda i: (i, 0)))
        out_shape.append(jax.ShapeDtypeStruct((n, w_next.shape[1]), F32))
    res = pl.pallas_call(
        functools.partial(_out_ln_kernel, n_parts=len(parts), alpha=alpha, has_next=has_next),
        grid=(n // tm,),
        in_specs=in_specs,
        out_specs=out_specs,
        out_shape=out_shape,
        compiler_params=_cparams("parallel"),
        name="out_ln",
    )(*args)
    return res if has_next else res[0]


def _log_sigmoid(x):
    return jnp.minimum(x, 0.0) - jnp.log1p(jnp.exp(-jnp.abs(x)))


def _lane_pick(row, h):
    return row[:, h:h + 1]


def _mlstm_kernel(*refs, chunk, units, sequential):
    if sequential:
        za_ref, zg_ref, g_ref, ya_ref, c1_ref, n1_ref, m1_ref, c_sc, n_sc, m_sc = refs
    else:
        za_ref, zg_ref, g_ref, c0_ref, n0_ref, m0_ref, _prev, ya_ref, c1_ref, n1_ref, m1_ref = refs
    L = chunk
    small = L < V7X_BF16_ROWS
    cast = (lambda a: a) if small else (lambda a: a.astype(BF16))
    mm = lambda a, b: jnp.dot(cast(a), cast(b), preferred_element_type=F32)
    mm_nt = lambda a, b: lax.dot_general(cast(a), cast(b), (((1,), (1,)), ((), ())), preferred_element_type=F32)
    mm_tn = lambda a, b: lax.dot_general(cast(a), cast(b), (((0,), (0,)), ((), ())), preferred_element_type=F32)

    if sequential:
        ti = pl.program_id(1)

        @pl.when(ti == 0)
        def _():
            c_sc[...] = jnp.zeros_like(c_sc)
            n_sc[...] = jnp.zeros_like(n_sc)
            m_sc[...] = jnp.zeros_like(m_sc)

    row = lax.broadcasted_iota(jnp.int32, (L, L), 0)
    col = lax.broadcasted_iota(jnp.int32, (L, L), 1)
    causal = col <= row
    tril = causal.astype(F32)
    sel_r = lax.broadcasted_iota(jnp.int32, (V7X_SUBLANES, V7X_LANES), 0)
    sel_c = lax.broadcasted_iota(jnp.int32, (V7X_SUBLANES, V7X_LANES), 1)
    sel = (sel_r == sel_c).astype(F32)
    lane_g = lax.broadcasted_iota(jnp.int32, (L, N_ZG), 1)
    lane_m = lax.broadcasted_iota(jnp.int32, (1, V7X_LANES), 1)
    norm_g = g_ref[...]
    rep = lambda col: jnp.broadcast_to(col, (L, V7X_LANES))
    wide = lambda x, n: pltpu.repeat(x, n // V7X_LANES, 1) if n > V7X_LANES else x[:, :n]

    if sequential:
        state = [(c_sc[h], n_sc[h:h + 1, :], _lane_pick(m_sc[...], h)) for h in range(H_A)]

    for u in range(units):
        rows_u = slice(u * L, (u + 1) * L)
        gates = zg_ref[rows_u, :]
        gl = jnp.where(lane_g < H_A, gates, _log_sigmoid(gates))
        bcum = jnp.dot(tril, gl, precision=lax.Precision.HIGHEST, preferred_element_type=F32)
        mixed = jnp.where(lane_g < H_A, gates, bcum)
        t_rows = lax.dot_general(sel, mixed, (((1,), (1,)), ((), ())),
                                 precision=lax.Precision.HIGHEST, preferred_element_type=F32)
        if not sequential:
            state = [(c0_ref[u, h], n0_ref[u, h:h + 1, :], _lane_pick(m0_ref[u], h)) for h in range(H_A)]
        outs = []
        new_state = []
        for h in range(H_A):
            q = za_ref[rows_u, h * DH_A:(h + 1) * DH_A]
            k = za_ref[rows_u, W_A + h * DH_A:W_A + (h + 1) * DH_A] * (DH_A ** -0.5)
            v = za_ref[rows_u, 2 * W_A + h * DH_A:2 * W_A + (h + 1) * DH_A]
            og = za_ref[rows_u, 3 * W_A + h * DH_A:3 * W_A + (h + 1) * DH_A]
            ig_c = rep(gates[:, h:h + 1])
            b_c = rep(bcum[:, H_A + h:H_A + h + 1])
            ig_row = t_rows[h:h + 1, :]
            b_row = t_rows[H_A + h:H_A + h + 1, :]
            c, n, m_prev = state[h]

            dmat = jnp.where(causal, wide(b_c, L) - b_row + ig_row, NEG_INF)
            a = rep(jnp.max(dmat, axis=-1, keepdims=True))
            s = mm_nt(q, k) * jnp.exp(dmat - wide(a, L))
            n_loc = mm(s, v)
            d_loc = rep(jnp.sum(s, axis=-1, keepdims=True))
            a_last = a[L - 1:L, :]
            b_last = b_c[L - 1:L, :]
            wk = jnp.exp(b_last - b_c + ig_c - a_last)[:, :DH_A]
            u_loc = mm_tn(wk * v, k)
            nu_loc = jnp.sum(wk * k, axis=0, keepdims=True)

            inter = b_c + m_prev
            m_row = jnp.maximum(inter, a)
            r = jnp.exp(a - m_row)
            w_inter = jnp.exp(inter - m_row)
            num = r[:, :DH_A] * n_loc + w_inter[:, :DH_A] * mm_nt(q, c)
            den = r * d_loc + w_inter * rep(jnp.sum(q * n, axis=-1, keepdims=True))
            hh = num / jnp.maximum(jnp.abs(den), jnp.exp(-m_row))[:, :DH_A]
            m_new = m_row[L - 1:L, 0:1]
            decay = jnp.exp(b_last[:, 0:1] + m_prev - m_new)
            e_loc = jnp.exp(a_last[:, 0:1] - m_new)
            new_state.append((decay * c + e_loc * u_loc, decay * n + e_loc * nu_loc, m_new))

            hn = hh * lax.rsqrt(jnp.mean(hh * hh, axis=-1, keepdims=True) + NORM_EPS)
            outs.append(jax.nn.sigmoid(og) * hn)
        ya_ref[rows_u, :] = jnp.concatenate(outs, axis=-1) * norm_g
        state = new_state
        if not sequential:
            m_out = jnp.zeros((1, V7X_LANES), F32)
            for h in range(H_A):
                c1_ref[u, h] = state[h][0]
                n1_ref[u, h:h + 1, :] = state[h][1]
                m_out = jnp.where(lane_m == h, state[h][2], m_out)
            m1_ref[u] = m_out

    if sequential:
        m_out = jnp.zeros((1, V7X_LANES), F32)
        for h in range(H_A):
            c_sc[h] = state[h][0]
            n_sc[h:h + 1, :] = state[h][1]
            m_out = jnp.where(lane_m == h, state[h][2], m_out)
        m_sc[...] = m_out

        @pl.when(ti == pl.num_programs(1) - 1)
        def _():
            c1_ref[0] = c_sc[...]
            n1_ref[0] = n_sc[...]
            m1_ref[0] = m_sc[...]


def mlstm(za, zg, norm_g, nb, t, row0, init=None, prev=None):
    n_tok = za.shape[0]
    L = MLSTM_CHUNK if t % MLSTM_CHUNK == 0 else t
    units = MLSTM_UNITS
    if init is None:
        L, units = MLSTM_PROMPT_CHUNK, 1
    nc = t // L
    rows = units * L
    sequential = init is None
    if sequential:
        grid = (nb, nc // units)
        seq_blk = 1
        row_map = lambda b, c: (row0 // rows + b * (nc // units) + c, 0)
    else:
        assert nc == 1
        grid = (nb // units, 1)
        seq_blk = units
        row_map = lambda b, c: (row0 // rows + b, 0)
    st4 = lambda b, c: (b, 0, 0, 0)
    st3 = lambda b, c: (b, 0, 0)
    in_specs = [
        pl.BlockSpec((rows, N_ZA), row_map),
        pl.BlockSpec((rows, N_ZG), row_map),
        pl.BlockSpec((1, W_A), lambda b, c: (0, 0)),
    ]
    args = [za, zg, norm_g.reshape(1, W_A)]
    aliases = {}
    scratch = []
    if sequential:
        scratch = [pltpu.VMEM((H_A, DH_A, DH_A), F32), pltpu.VMEM((H_A, DH_A), F32),
                   pltpu.VMEM((1, V7X_LANES), F32)]
    else:
        c0, n0, m0 = init
        m0p = jnp.pad(m0, ((0, 0), (0, V7X_LANES - H_A))).reshape(nb, 1, V7X_LANES)
        in_specs += [
            pl.BlockSpec((seq_blk, H_A, DH_A, DH_A), st4),
            pl.BlockSpec((seq_blk, H_A, DH_A), st3),
            pl.BlockSpec((seq_blk, 1, V7X_LANES), st3),
            pl.BlockSpec(memory_space=pl.ANY),
        ]
        args += [c0, n0, m0p, prev]
        aliases = {6: 0}
    out_specs = [
        pl.BlockSpec((rows, W_A), row_map),
        pl.BlockSpec((seq_blk, H_A, DH_A, DH_A), st4),
        pl.BlockSpec((seq_blk, H_A, DH_A), st3),
        pl.BlockSpec((seq_blk, 1, V7X_LANES), st3),
    ]
    out_shape = [
        jax.ShapeDtypeStruct((n_tok, W_A), F32),
        jax.ShapeDtypeStruct((nb, H_A, DH_A, DH_A), F32),
        jax.ShapeDtypeStruct((nb, H_A, DH_A), F32),
        jax.ShapeDtypeStruct((nb, 1, V7X_LANES), F32),
    ]
    ya, c1, n1, m1 = pl.pallas_call(
        functools.partial(_mlstm_kernel, chunk=L, units=units, sequential=sequential),
        grid=grid,
        in_specs=in_specs,
        out_specs=out_specs,
        out_shape=out_shape,
        scratch_shapes=scratch,
        input_output_aliases=aliases,
        compiler_params=_cparams("parallel", "arbitrary"),
        name="mlstm_prompt" if sequential else "mlstm_sample",
    )(*args)
    return ya, c1, n1, m1[:, 0, :H_A]


def _diff_lambda(lam_ref, lam_init):
    lp = lam_ref[...]
    d01 = jnp.sum(lp[0:1, :] * lp[1:2, :], axis=-1, keepdims=True)
    d23 = jnp.sum(lp[2:3, :] * lp[3:4, :], axis=-1, keepdims=True)
    return jnp.exp(d01) - jnp.exp(d23) + lam_init


def _alibi_slope(h):
    return jnp.where(h == 0, 2.0 ** -2, jnp.where(h == 1, 2.0 ** -4, jnp.where(h == 2, 2.0 ** -6, 2.0 ** -8)))


def _alibi_tables(t):
    slopes = (2.0 ** (-8.0 * jnp.arange(1, H_B + 1, dtype=F32) / H_B))[:, None]
    pos = jnp.arange(t, dtype=jnp.int32)
    hi = ((pos // ALIBI_SPLIT) * ALIBI_SPLIT).astype(F32)[None, :]
    lo = (pos % ALIBI_SPLIT).astype(F32)[None, :]
    ones = jnp.ones((H_B, t), F32)
    pad = jnp.zeros((H_B, t, DV_B - 4), F32)
    aq = jnp.concatenate([jnp.stack([slopes * ones, slopes * ones, -slopes * hi, -slopes * lo], axis=-1), pad], -1)
    ak = jnp.concatenate([jnp.stack([hi * ones, lo * ones, ones, ones], axis=-1), pad], -1)
    return aq.astype(BF16), ak.astype(BF16)


def _attn_prompt_kernel(lam_ref, g_ref, q_ref, aq_ref, k_ref, ak_ref, v_ref, o_ref,
                        qs_sc, m_sc, acc_sc, *, lam_init, tq, tk):
    i = pl.program_id(1)
    j = pl.program_id(2)
    rq = ATTN_ROWS

    @pl.when(j == 0)
    def _():
        lane = lax.broadcasted_iota(jnp.int32, (tq, DV_B), 1)
        for h in range(H_B):
            q = q_ref[:, h * DV_B:(h + 1) * DV_B] * (DK_B ** -0.5)
            qs_sc[h, 0:tq, 0:DV_B] = jnp.where(lane < DK_B, q, 0.0).astype(BF16)
            qs_sc[h, tq:2 * tq, 0:DV_B] = jnp.where(lane >= DK_B, q, 0.0).astype(BF16)
            qs_sc[h, 0:tq, DV_B:2 * DV_B] = aq_ref[h]
            qs_sc[h, tq:2 * tq, DV_B:2 * DV_B] = aq_ref[h]
        m_sc[...] = jnp.full_like(m_sc, NEG_INF)
        acc_sc[...] = jnp.zeros_like(acc_sc)

    def step(masked):
        ones = jnp.ones((tk, V7X_LANES), BF16)
        for h in range(H_B):
            kaug = jnp.concatenate([k_ref[:, h * DV_B:(h + 1) * DV_B].astype(BF16), ak_ref[h]], axis=1)
            vaug = jnp.concatenate([v_ref[:, h * DV_B:(h + 1) * DV_B].astype(BF16), ones], axis=1)
            for r in range(2 * tq // rq):
                rows = slice(r * rq, (r + 1) * rq)
                s = lax.dot_general(qs_sc[h, rows, :], kaug, (((1,), (1,)), ((), ())), preferred_element_type=F32)
                if masked:
                    qi = (r * rq) % tq + lax.broadcasted_iota(jnp.int32, (rq, tk), 0)
                    kj = lax.broadcasted_iota(jnp.int32, (rq, tk), 1)
                    s = jnp.where(kj <= qi, s, NEG_INF)
                m_old = m_sc[h, rows, :]
                m_new = jnp.maximum(m_old, jnp.max(s, axis=-1, keepdims=True))
                alpha = jnp.exp(m_old - m_new)
                p = jnp.exp(s - pltpu.repeat(m_new, tk // V7X_LANES, 1))
                acc_sc[h, rows, :] = (pltpu.repeat(alpha, 2, 1) * acc_sc[h, rows, :]
                                      + jnp.dot(p.astype(BF16), vaug, preferred_element_type=F32))
                m_sc[h, rows, :] = m_new

    @pl.when(j < i)
    def _():
        step(False)

    @pl.when(j == i)
    def _():
        step(True)

    @pl.when(j == pl.num_programs(2) - 1)
    def _():
        lam = _diff_lambda(lam_ref, lam_init)
        for h in range(H_B):
            o0 = acc_sc[h, 0:tq, 0:DV_B] / acc_sc[h, 0:tq, DV_B:2 * DV_B]
            o1 = acc_sc[h, tq:2 * tq, 0:DV_B] / acc_sc[h, tq:2 * tq, DV_B:2 * DV_B]
            ob = o0 - lam * o1
            on = ob * lax.rsqrt(jnp.mean(ob * ob, axis=-1, keepdims=True) + NORM_EPS)
            o_ref[:, h * DV_B:(h + 1) * DV_B] = on * g_ref[:, h * DV_B:(h + 1) * DV_B] * (1.0 - lam_init)


def diff_attn_prompt(zb, aq, ak, lam_p, norm_g, nb, t, lam_init):
    n_tok = zb.shape[0]
    tq = tk = ATTN_TILE
    nq = t // tq
    kv_map = lambda off: (lambda b, i, j: (b * nq + jnp.minimum(i, j), off))
    return pl.pallas_call(
        functools.partial(_attn_prompt_kernel, lam_init=lam_init, tq=tq, tk=tk),
        grid=(nb, nq, nq),
        in_specs=[
            pl.BlockSpec(lam_p.shape, lambda b, i, j: (0, 0)),
            pl.BlockSpec((1, W_B), lambda b, i, j: (0, 0)),
            pl.BlockSpec((tq, W_B), lambda b, i, j: (b * nq + i, 0)),
            pl.BlockSpec((H_B, tq, DV_B), lambda b, i, j: (0, i, 0)),
            pl.BlockSpec((tk, W_B), kv_map(1)),
            pl.BlockSpec((H_B, tk, DV_B), lambda b, i, j: (0, jnp.minimum(i, j), 0)),
            pl.BlockSpec((tk, W_B), kv_map(2)),
        ],
        out_specs=pl.BlockSpec((tq, W_B), lambda b, i, j: (b * nq + i, 0)),
        out_shape=jax.ShapeDtypeStruct((n_tok, W_B), F32),
        scratch_shapes=[pltpu.VMEM((H_B, 2 * tq, 2 * DV_B), BF16), pltpu.VMEM((H_B, 2 * tq, V7X_LANES), F32),
                        pltpu.VMEM((H_B, 2 * tq, 2 * DV_B), F32)],
        compiler_params=_cparams("parallel", "parallel", "arbitrary"),
        name="diff_attn_prompt",
    )(lam_p, norm_g.reshape(1, W_B), zb, aq, zb, ak, zb)


def _attn_sample_kernel(*refs, lam_init, past_len, t_new, pages_per_step):
    G = pages_per_step
    pt_ref, lam_ref, g_ref, q_ref, kn_ref, vn_ref = refs[:6]
    k_refs = refs[6:6 + G]
    v_refs = refs[6 + G:6 + 2 * G]
    _prev, o_ref, qs_sc, bias_sc, m_sc, l_sc, acc_sc = refs[6 + 2 * G:]
    del pt_ref
    ps = pl.program_id(1)
    rows_per_head = 2 * t_new
    n_rows = H_B * rows_per_head
    page_rows = PAGE_SIZE * H_B
    reps = page_rows // V7X_LANES

    r_lane = lax.broadcasted_iota(jnp.int32, (n_rows, V7X_LANES), 0)
    slope = _alibi_slope(r_lane // rows_per_head).astype(F32)

    @pl.when(ps == 0)
    def _():
        q = q_ref[...] * (DK_B ** -0.5)
        lane = lax.broadcasted_iota(jnp.int32, (t_new, DV_B), 1)
        for h in range(H_B):
            qh = q[:, h * DV_B:(h + 1) * DV_B]
            qs_sc[h * rows_per_head:(h + 1) * rows_per_head, :] = jnp.concatenate(
                [jnp.where(lane < DK_B, qh, 0.0), jnp.where(lane >= DK_B, qh, 0.0)], axis=0).astype(BF16)
        rr = lax.broadcasted_iota(jnp.int32, (n_rows, page_rows), 0)
        cc = lax.broadcasted_iota(jnp.int32, (n_rows, page_rows), 1)
        rel = cc // H_B - (past_len + rr % t_new)
        bias = _alibi_slope(rr // rows_per_head).astype(F32) * rel.astype(F32)
        bias_sc[...] = jnp.where(cc % H_B == rr // rows_per_head, bias, NEG_INF)
        m_sc[...] = jnp.full_like(m_sc, NEG_INF)
        l_sc[...] = jnp.zeros_like(l_sc)
        acc_sc[...] = jnp.zeros_like(acc_sc)

    qs = qs_sc[...]
    m, l, acc = m_sc[...], l_sc[...], acc_sc[...]
    s_pages = []
    m_new = m
    for g in range(G):
        base = ((ps * G + g) * PAGE_SIZE).astype(F32)
        s = _bdot_nt(qs, k_refs[g][...]) + (bias_sc[...] + pltpu.repeat(slope * base, reps, 1))
        m_new = jnp.maximum(m_new, jnp.max(s, axis=-1, keepdims=True))
        s_pages.append(s)
    alpha = jnp.exp(m - m_new)
    l = alpha * l
    acc = alpha * acc
    for g in range(G):
        p = jnp.exp(s_pages[g] - pltpu.repeat(m_new, reps, 1))
        l = l + jnp.sum(p, axis=-1, keepdims=True)
        acc = acc + _bdot(p, v_refs[g][...])
    m = m_new

    @pl.when(ps < pl.num_programs(1) - 1)
    def _():
        m_sc[...] = m
        l_sc[...] = l
        acc_sc[...] = acc

    @pl.when(ps == pl.num_programs(1) - 1)
    def _():
        rnd = lambda a: a.astype(BF16).astype(F32)
        kn = rnd(kn_ref[...])
        vn = rnd(vn_ref[...])
        qf = qs.astype(F32)
        sn = jnp.concatenate(
            [lax.dot_general(qf[h * rows_per_head:(h + 1) * rows_per_head, :], kn[:, h * DV_B:(h + 1) * DV_B],
                             (((1,), (1,)), ((), ())), preferred_element_type=F32) for h in range(H_B)],
            axis=0)
        rr = lax.broadcasted_iota(jnp.int32, (n_rows, t_new), 0)
        rel = lax.broadcasted_iota(jnp.int32, (n_rows, t_new), 1) - rr % t_new
        sn = jnp.where(rel <= 0, sn + _alibi_slope(rr // rows_per_head).astype(F32) * rel.astype(F32), NEG_INF)
        m_fin = jnp.maximum(m, jnp.max(sn, axis=-1, keepdims=True))
        alpha = jnp.exp(m - m_fin)
        pn = rnd(jnp.exp(sn - m_fin[:, 0:1]))
        l_fin = alpha * l + jnp.sum(pn, axis=-1, keepdims=True)
        pv = jnp.concatenate(
            [jnp.dot(pn[h * rows_per_head:(h + 1) * rows_per_head, :], vn[:, h * DV_B:(h + 1) * DV_B],
                     preferred_element_type=F32) for h in range(H_B)], axis=0)
        o = (alpha * acc + pv) / l_fin

        lam = _diff_lambda(lam_ref, lam_init)
        outs = []
        for h in range(H_B):
            o0 = o[h * rows_per_head:h * rows_per_head + t_new, :]
            o1 = o[h * rows_per_head + t_new:(h + 1) * rows_per_head, :]
            ob = o0 - lam * o1
            outs.append(ob * lax.rsqrt(jnp.mean(ob * ob, axis=-1, keepdims=True) + NORM_EPS))
        o_ref[...] = jnp.concatenate(outs, axis=-1) * g_ref[...] * (1.0 - lam_init)


def diff_attn_sample(zb, cache_k, cache_v, layer, page_table, lam_p, norm_g, nb, t_new, row0, lam_init, prev):
    n_tok = zb.shape[0]
    n_pages = page_table.shape[1]
    G = PAGES_PER_STEP
    rb0 = row0 // t_new
    n_rows = 2 * H_B * t_new
    page_rows = PAGE_SIZE * H_B
    ck = cache_k.reshape(cache_k.shape[0], cache_k.shape[1], page_rows, DV_B)
    cv = cache_v.reshape(cache_v.shape[0], cache_v.shape[1], page_rows, DV_B)

    def page_map(g):
        return lambda b, p, pt: (layer, pt[b, p * G + g], 0, 0)

    page_blk = (None, None, page_rows, DV_B)
    in_specs = [
        pl.BlockSpec(lam_p.shape, lambda b, p, pt: (0, 0)),
        pl.BlockSpec((1, W_B), lambda b, p, pt: (0, 0)),
        pl.BlockSpec((t_new, W_B), lambda b, p, pt: (rb0 + b, 0)),
        pl.BlockSpec((t_new, W_B), lambda b, p, pt: (rb0 + b, 1)),
        pl.BlockSpec((t_new, W_B), lambda b, p, pt: (rb0 + b, 2)),
    ]
    in_specs += [pl.BlockSpec(page_blk, page_map(g)) for g in range(G)]
    in_specs += [pl.BlockSpec(page_blk, page_map(g)) for g in range(G)]
    in_specs += [pl.BlockSpec(memory_space=pl.ANY)]
    args = [page_table, lam_p, norm_g.reshape(1, W_B), zb, zb, zb] + [ck] * G + [cv] * G + [prev]
    grid_spec = pltpu.PrefetchScalarGridSpec(
        num_scalar_prefetch=1,
        grid=(nb, n_pages // G),
        in_specs=in_specs,
        out_specs=pl.BlockSpec((t_new, W_B), lambda b, p, pt: (rb0 + b, 0)),
        scratch_shapes=[pltpu.VMEM((n_rows, DV_B), BF16), pltpu.VMEM((n_rows, page_rows), F32),
                        pltpu.VMEM((n_rows, V7X_LANES), F32), pltpu.VMEM((n_rows, V7X_LANES), F32),
                        pltpu.VMEM((n_rows, DV_B), F32)],
    )
    return pl.pallas_call(
        functools.partial(_attn_sample_kernel, lam_init=lam_init, past_len=n_pages * PAGE_SIZE,
                          t_new=t_new, pages_per_step=G),
        grid_spec=grid_spec,
        out_shape=jax.ShapeDtypeStruct((n_tok, W_B), F32),
        input_output_aliases={len(args) - 1: 0},
        compiler_params=_cparams("parallel", "arbitrary"),
        name="diff_attn_sample",
    )(*args)


def _s5_disc_kernel(are_ref, aim_ref, ldt_ref, bre_ref, bim_ref, lre_ref, lim_ref, bbre_ref, bbim_ref):
    a_re = are_ref[...]
    a_im = aim_ref[...]
    dt = jnp.exp(ldt_ref[...])
    mag = jnp.exp(a_re * dt)
    lb_re = mag * jnp.cos(a_im * dt)
    lb_im = mag * jnp.sin(a_im * dt)
    den = a_re * a_re + a_im * a_im
    xr = lb_re - 1.0
    fr = (xr * a_re + lb_im * a_im) / den
    fi = (lb_im * a_re - xr * a_im) / den
    lre_ref[...] = lb_re
    lim_ref[...] = lb_im
    b_re = bre_ref[...]
    b_im = bim_ref[...]
    bbre_ref[...] = fr * b_re - fi * b_im
    bbim_ref[...] = fr * b_im + fi * b_re


def s5_discretize(a_re, a_im, log_dt, b_re, b_im):
    depth = a_re.shape[0]
    gp = G_C * P_C
    flat = lambda a: a.reshape(depth, 1, gp)
    ldt = jnp.broadcast_to(log_dt[:, :, None], (depth, G_C, P_C)).reshape(depth, 1, gp)
    tr = lambda b: jnp.transpose(b, (0, 3, 1, 2)).reshape(depth, GC, gp)
    shp1 = jax.ShapeDtypeStruct((depth, 1, gp), F32)
    shpb = jax.ShapeDtypeStruct((depth, GC, gp), F32)
    return pl.pallas_call(_s5_disc_kernel, out_shape=[shp1, shp1, shpb, shpb], name="s5_discretize")(
        flat(a_re), flat(a_im), ldt, tr(b_re), tr(b_im))


def _block_diag_in(bb):
    depth, _, gp = bb.shape
    tiled = jnp.tile(bb, (1, G_C, 1)).reshape(depth, G_C, GC, gp)
    grp_r = jnp.arange(G_C)[:, None, None]
    grp_c = (jnp.arange(gp) // P_C)[None, None, :]
    return jnp.where(grp_r == grp_c, tiled, 0.0).reshape(depth, G_C * GC, gp)


def _block_diag_out(c):
    depth = c.shape[0]
    ct = jnp.transpose(c, (0, 1, 3, 2)).reshape(depth, G_C * P_C, GC)
    tiled = jnp.tile(ct, (1, 1, G_C))
    grp_r = (jnp.arange(G_C * P_C) // P_C)[:, None]
    grp_c = (jnp.arange(G_C * GC) // GC)[None, :]
    return jnp.where(grp_r == grp_c, tiled, 0.0)


def _cmul_add(a_re, a_im, x_re, x_im, y_re, y_im):
    return y_re + (a_re * x_re - a_im * x_im), y_im + (a_re * x_im + a_im * x_re)


def _block_scan(x_re, x_im, pw, row8):
    for d, (a_re, a_im) in zip((1, 2, 4), pw):
        sh_re = jnp.where(row8 >= d, pltpu.roll(x_re, d, 0), 0.0)
        sh_im = jnp.where(row8 >= d, pltpu.roll(x_im, d, 0), 0.0)
        x_re, x_im = _cmul_add(a_re, a_im, sh_re, sh_im, x_re, x_im)
    return x_re, x_im


def _s5_kernel(*refs, rows, independent, aliased):
    (u_ref, lre_ref, lim_ref, bb_ref, cc_ref, d_ref, gw_ref, gb_ref, s0re_ref, s0im_ref) = refs[:10]
    rest = refs[10:]
    if aliased:
        rest = rest[1:]
    y_ref, s1re_ref, s1im_ref, st_sc, car_sc = rest
    gp = G_C * P_C
    nblk = rows // V7X_SUBLANES
    ti = pl.program_id(1)

    lam_re = lre_ref[...]
    lam_im = lim_ref[...]
    l2_re, l2_im = lam_re * lam_re - lam_im * lam_im, 2.0 * lam_re * lam_im
    l4_re, l4_im = l2_re * l2_re - l2_im * l2_im, 2.0 * l2_re * l2_im
    pw = ((lam_re, lam_im), (l2_re, l2_im), (l4_re, l4_im))
    row8 = lax.broadcasted_iota(jnp.int32, (V7X_SUBLANES, gp), 0)
    pk_re, pk_im = _block_scan(jnp.where(row8 == 0, lam_re, 0.0), jnp.where(row8 == 0, lam_im, 0.0), pw, row8)

    u = u_ref[...]
    st_sc[...] = _bdot(u, bb_ref[...])

    if not independent:
        @pl.when(ti == 0)
        def _():
            car_sc[0:1, :] = s0re_ref[...]
            car_sc[1:2, :] = s0im_ref[...]

    def body(bi, carry):
        r0 = pl.multiple_of(bi * V7X_SUBLANES, V7X_SUBLANES)
        x_re = st_sc[pl.ds(r0, V7X_SUBLANES), 0:gp]
        x_im = st_sc[pl.ds(r0, V7X_SUBLANES), gp:2 * gp]
        x_re, x_im = _block_scan(x_re, x_im, pw, row8)
        if independent:
            c_re = s0re_ref[pl.ds(bi, 1), :]
            c_im = s0im_ref[pl.ds(bi, 1), :]
        else:
            c_re, c_im = carry
        s_re, s_im = _cmul_add(pk_re, pk_im, c_re, c_im, x_re, x_im)
        st_sc[pl.ds(r0, V7X_SUBLANES), 0:gp] = s_re
        st_sc[pl.ds(r0, V7X_SUBLANES), gp:2 * gp] = s_im
        last_re = s_re[V7X_SUBLANES - 1:V7X_SUBLANES, :]
        last_im = s_im[V7X_SUBLANES - 1:V7X_SUBLANES, :]
        if independent:
            s1re_ref[pl.ds(bi, 1), :] = last_re
            s1im_ref[pl.ds(bi, 1), :] = last_im
            return carry
        return last_re, last_im

    if independent:
        lax.fori_loop(0, nblk, body, 0)
    else:
        c_re, c_im = lax.fori_loop(0, nblk, body, (car_sc[0:1, :], car_sc[1:2, :]))
        car_sc[0:1, :] = c_re
        car_sc[1:2, :] = c_im

        @pl.when(ti == pl.num_programs(1) - 1)
        def _():
            s1re_ref[...] = c_re
            s1im_ref[...] = c_im

    y = _bdot(st_sc[...], cc_ref[...]) + d_ref[...] * u
    z = _bdot(jax.nn.gelu(y), gw_ref[...]) + gb_ref[...]
    y_ref[...] = z[:, :W_C] * jax.nn.sigmoid(z[:, W_C:])


def s5_mixer(zc, lam_re, lam_im, bb, cc, d, glu_w, glu_b, s0_re, s0_im, nb, t, row0, prev=None):
    n_tok = zc.shape[0]
    gp = G_C * P_C
    independent = t == V7X_SUBLANES
    aliased = prev is not None
    if independent:
        seqs = S5_SAMPLE_SEQS
        rows = seqs * t
        grid = (nb // seqs, 1)
        st_spec = pl.BlockSpec((seqs, gp), lambda b, i: (b, 0))
    else:
        rows = S5_TILE
        grid = (nb, t // rows)
        st_spec = pl.BlockSpec((1, gp), lambda b, i: (b, 0))
    nt = grid[1]
    rb0 = row0 // rows
    row_map = lambda b, i: (rb0 + b * nt + i, 0)
    const = lambda b, i: (0, 0)
    in_specs = [
        pl.BlockSpec((rows, W_C), row_map),
        pl.BlockSpec((1, gp), const),
        pl.BlockSpec((1, gp), const),
        pl.BlockSpec(bb.shape, const),
        pl.BlockSpec(cc.shape, const),
        pl.BlockSpec((1, W_C), const),
        pl.BlockSpec(glu_w.shape, const),
        pl.BlockSpec((1, 2 * W_C), const),
        st_spec,
        st_spec,
    ]
    args = [zc, lam_re, lam_im, bb, cc, d.reshape(1, W_C), glu_w, glu_b.reshape(1, 2 * W_C), s0_re, s0_im]
    aliases = {}
    if aliased:
        in_specs.append(pl.BlockSpec(memory_space=pl.ANY))
        args.append(prev)
        aliases = {len(args) - 1: 0}
    if independent:
        s1_shape = jax.ShapeDtypeStruct((nb, gp), F32)
    else:
        s1_shape = jax.ShapeDtypeStruct((nb, 1, gp), F32)
        st_out = pl.BlockSpec((None, 1, gp), lambda b, i: (b, 0, 0))
    out_specs = [pl.BlockSpec((rows, W_C), row_map)] + ([st_spec, st_spec] if independent else [st_out, st_out])
    if not independent:
        args[8] = s0_re.reshape(nb, 1, gp)
        args[9] = s0_im.reshape(nb, 1, gp)
        in_specs[8] = in_specs[9] = pl.BlockSpec((None, 1, gp), lambda b, i: (b, 0, 0))
    yc, s1_re, s1_im = pl.pallas_call(
        functools.partial(_s5_kernel, rows=rows, independent=independent, aliased=aliased),
        grid=grid,
        in_specs=in_specs,
        out_specs=out_specs,
        out_shape=[jax.ShapeDtypeStruct((n_tok, W_C), F32), s1_shape, s1_shape],
        scratch_shapes=[pltpu.VMEM((rows, 2 * gp), F32), pltpu.VMEM((V7X_SUBLANES, gp), F32)],
        input_output_aliases=aliases,
        compiler_params=_cparams("parallel", "arbitrary"),
        name="s5_sample" if independent else "s5_prompt",
    )(*args)
    return yc, s1_re.reshape(nb, G_C, P_C), s1_im.reshape(nb, G_C, P_C)


def _xattn_kernel(*refs, tq, seqs, n_mem):
    q_ref, mk_ref, mv_ref = refs[:3]
    o_ref = refs[-1]
    rows = seqs * tq
    q = q_ref[...] * (DH_X ** -0.5)
    pad = rows < V7X_BF16_ROWS
    if pad:
        q = jnp.concatenate([q, jnp.zeros((V7X_BF16_ROWS - rows, q.shape[1]), F32)], axis=0)
    if seqs > 1:
        rr = lax.broadcasted_iota(jnp.int32, (q.shape[0], seqs * n_mem), 0)
        cc = lax.broadcasted_iota(jnp.int32, (q.shape[0], seqs * n_mem), 1)
        own = rr // tq == cc // n_mem
    outs = []
    for h in range(H_X):
        s = _bdot_nt(q[:, h * DH_X:(h + 1) * DH_X], mk_ref[h])
        if seqs > 1:
            s = jnp.where(own, s, NEG_INF)
        m = jnp.max(s, axis=-1, keepdims=True)
        p = jnp.exp(s - m)
        l = jnp.sum(p, axis=-1, keepdims=True)
        outs.append(_bdot(p, mv_ref[h]) / l)
    o = jnp.concatenate(outs, axis=-1)
    o_ref[...] = o[:rows] if pad else o


def cross_attn(qx, mk, mv, layer, n_mem, nb, t, row0, prev=None):
    n_tok, d = qx.shape
    tq = min(t, XATTN_TILE)
    nq = t // tq
    seqs = XATTN_SAMPLE_SEQS if nq == 1 and tq < XATTN_TILE else 1
    rows = seqs * tq
    rb0 = row0 // rows
    aliased = prev is not None
    mem_spec = pl.BlockSpec((None, H_X, seqs * n_mem, DH_X), lambda b, i: (layer, 0, b, 0))
    in_specs = [pl.BlockSpec((rows, d), lambda b, i: (rb0 + b * nq + i, 0)), mem_spec, mem_spec]
    args = [qx, mk, mv]
    aliases = {}
    if aliased:
        in_specs.append(pl.BlockSpec(memory_space=pl.ANY))
        args.append(prev)
        aliases = {3: 0}
    return pl.pallas_call(
        functools.partial(_xattn_kernel, tq=tq, seqs=seqs, n_mem=n_mem),
        grid=(nb // seqs, nq),
        in_specs=in_specs,
        out_specs=pl.BlockSpec((rows, d), lambda b, i: (rb0 + b * nq + i, 0)),
        out_shape=jax.ShapeDtypeStruct((n_tok, d), F32),
        input_output_aliases=aliases,
        compiler_params=_cparams("parallel", "arbitrary"),
        name="cross_attn_sample" if aliased else "cross_attn_prompt",
    )(*args)


def kernel(x_prompt, x_sample, mem_prompt, cache_k, cache_v, page_table, cache_mem_k, cache_mem_v,
           state_mlstm_c, state_mlstm_n, state_mlstm_m, state_ssm_re, state_ssm_im,
           ln_g, ln_b, ffn1_wg, ffn1_wu, ffn1_wd, ffn2_wg, ffn2_wu, ffn2_wd, w_in, b_in,
           mlstm_norm_g, diff_lam, diff_norm_g, ssm_a_re, ssm_a_im, ssm_log_dt, ssm_b_re, ssm_b_im,
           ssm_c_re, ssm_c_im, ssm_d, ssm_glu_w, ssm_glu_b, w_out, cross_wq, cross_wk, cross_wv, cross_wo):
    bp, tp, d = x_prompt.shape
    bs, ts, _ = x_sample.shape
    depth = ln_g.shape[0]
    n_mem = mem_prompt.shape[1]
    n_p = bp * tp
    gp = G_C * P_C
    alpha = (2.0 * depth) ** 0.25

    cast = lambda w: w.astype(BF16)
    ffn1_wg, ffn1_wu, ffn1_wd = cast(ffn1_wg), cast(ffn1_wu), cast(ffn1_wd)
    ffn2_wg, ffn2_wu, ffn2_wd = cast(ffn2_wg), cast(ffn2_wu), cast(ffn2_wd)
    w_out_b, wq_b, wo_b, glu_w_b = cast(w_out), cast(cross_wq), cast(cross_wo), cast(ssm_glu_w)
    w_in_p, b_in_p = _pack_w_in(w_in, b_in)
    lam_re, lam_im, bb_re, bb_im = s5_discretize(ssm_a_re, ssm_a_im, ssm_log_dt, ssm_b_re, ssm_b_im)
    bb = jnp.concatenate([_block_diag_in(bb_re), _block_diag_in(bb_im)], axis=-1).astype(BF16)
    cc = jnp.concatenate([_block_diag_out(ssm_c_re), -_block_diag_out(ssm_c_im)], axis=1).astype(BF16)
    alibi_q, alibi_k = _alibi_tables(tp)

    p_mk, p_mv, p_mkh, p_mvh = mem_kv(mem_prompt.reshape(bp * n_mem, d), cast(cross_wk), cast(cross_wv))
    p_mk = p_mk.reshape(depth, bp, n_mem, H_X, DH_X)
    p_mv = p_mv.reshape(depth, bp, n_mem, H_X, DH_X)
    head_major = lambda a: jnp.transpose(a, (0, 3, 1, 2, 4)).astype(BF16).reshape(depth, H_X, bs * n_mem, DH_X)
    s_mkh, s_mvh = head_major(cache_mem_k), head_major(cache_mem_v)

    x = jnp.concatenate([x_prompt.reshape(n_p, d), x_sample.reshape(bs * ts, d)], axis=0)
    zeros_s = jnp.zeros((bp, gp), F32)
    p_k = jnp.zeros((depth, n_p, H_B, DV_B), F32)
    p_v = jnp.zeros((depth, n_p, H_B, DV_B), F32)
    p_st, s_st = [], []
    for l in range(depth):
        lam_init = 0.8 - 0.6 * math.exp(-0.3 * l)
        x = ffn_ln(x, ffn1_wg[l], ffn1_wu[l], ffn1_wd[l], ln_g[l, 0], ln_b[l, 0], alpha)
        za, zb, zc, zg, p_k, p_v, s_k, s_v = proj_in(x, w_in_p[l], b_in_p[l], p_k, p_v, l, n_p)

        ya, pc, pn, pm = mlstm(za, zg, mlstm_norm_g[l], bp, tp, 0)
        ya, sc, sn, sm = mlstm(za, zg, mlstm_norm_g[l], bs, ts, n_p,
                               init=(state_mlstm_c[l], state_mlstm_n[l], state_mlstm_m[l]), prev=ya)

        yb = diff_attn_prompt(zb, alibi_q, alibi_k, diff_lam[l], diff_norm_g[l], bp, tp, lam_init)
        yb = diff_attn_sample(zb, cache_k, cache_v, l, page_table, diff_lam[l], diff_norm_g[l],
                              bs, ts, n_p, lam_init, yb)

        s5_args = (lam_re[l], lam_im[l], bb[l], cc[l], ssm_d[l], glu_w_b[l], ssm_glu_b[l])
        yc, psr, psi = s5_mixer(zc, *s5_args, zeros_s, zeros_s, bp, tp, 0)
        yc, ssr, ssi = s5_mixer(zc, *s5_args, state_ssm_re[l].reshape(bs, gp), state_ssm_im[l].reshape(bs, gp),
                                bs, ts, n_p, prev=yc)

        x, qx = out_ln([ya, yb, yc], w_out_b[l], x, ln_g[l, 1], ln_b[l, 1], alpha, w_next=wq_b[l])
        o = cross_attn(qx, p_mkh, p_mvh, l, n_mem, bp, tp, 0)
        o = cross_attn(qx, s_mkh, s_mvh, l, n_mem, bs, ts, n_p, prev=o)
        x = out_ln([o], wo_b[l], x, ln_g[l, 2], ln_b[l, 2], alpha)
        x = ffn_ln(x, ffn2_wg[l], ffn2_wu[l], ffn2_wd[l], ln_g[l, 3], ln_b[l, 3], alpha)

        p_st.append((pc, pn, pm, psr, psi))
        s_st.append((s_k.reshape(bs, ts, H_B, DV_B), s_v.reshape(bs, ts, H_B, DV_B), sc, sn, sm, ssr, ssi))

    p_c, p_n, p_m, p_sr, p_si = [jnp.stack(a) for a in zip(*p_st)]
    s_k, s_v, s_c, s_n, s_m, s_sr, s_si = [jnp.stack(a) for a in zip(*s_st)]
    p_k = p_k.reshape(depth, bp, tp, H_B, DV_B)
    p_v = p_v.reshape(depth, bp, tp, H_B, DV_B)
    yp = x[:n_p].reshape(bp, tp, d)
    ys = x[n_p:].reshape(bs, ts, d)
    return (yp, ys, p_k, p_v, p_mk, p_mv, p_c, p_n, p_m, p_sr, p_si, s_k, s_v, s_c, s_n, s_m, s_sr, s_si)
```

```python
import functools
import math

import jax
import jax.numpy as jnp
from jax import lax
from jax.experimental import pallas as pl
from jax.experimental.pallas import tpu as pltpu

F32 = jnp.float32
BF16 = jnp.bfloat16

D_MODEL = 1024
PAGE_SIZE = 128
W_A = D_MODEL // 4
W_B = D_MODEL // 2
W_C = D_MODEL - W_A - W_B
H_A = 4
DH_A = W_A // H_A
MLSTM_CHUNK = 64
H_B = 4
DV_B = W_B // H_B
DK_B = DV_B // 2
GC = 16
G_C = W_C // GC
P_C = 64
H_X = 4
DH_X = D_MODEL // H_X
LN_EPS = 1e-5
NORM_EPS = 1e-6

OFF_AQ = 0
OFF_AK = OFF_AQ + W_A
OFF_AV = OFF_AK + W_A
OFF_AO = OFF_AV + W_A
OFF_AI = OFF_AO + W_A
OFF_AF = OFF_AI + H_A
OFF_BQ = OFF_AF + H_A
OFF_BK = OFF_BQ + H_B * 2 * DK_B
OFF_BV = OFF_BK + H_B * 2 * DK_B
OFF_CU = OFF_BV + W_B
N_IN = OFF_CU + W_C

V7X_LANES = 128
V7X_SUBLANES = 8
V7X_BF16_ROWS = 2 * V7X_SUBLANES
V7X_VMEM_LIMIT_BYTES = 56 * 1024 * 1024

TOKEN_TILE = 1024
FF_TILE = 256
PROJ_TILE = 512
ATTN_TILE = 512
ATTN_ROWS = 256
XATTN_TILE = 512
S5_TILE = 512
PAGES_PER_STEP = 16
XATTN_SAMPLE_SEQS = 4
S5_SAMPLE_SEQS = 16
MLSTM_PROMPT_CHUNK = 512
MLSTM_UNITS = 8
ALIBI_SPLIT = 64

NEG_INF = float("-inf")


def _cparams(*sem):
    return pltpu.CompilerParams(dimension_semantics=sem, vmem_limit_bytes=V7X_VMEM_LIMIT_BYTES)


def _bdot(a, b):
    return jnp.dot(a.astype(BF16), b.astype(BF16), preferred_element_type=F32)


def _bdot_nt(a, b):
    return lax.dot_general(a.astype(BF16), b.astype(BF16), (((1,), (1,)), ((), ())),
                           preferred_element_type=F32)


def _tile_lanes(x, n):
    return x if n == 1 else jnp.concatenate([x] * n, axis=1)


def _layer_norm(y, g, b):
    mu = jnp.mean(y, axis=-1, keepdims=True)
    yc = y - mu
    var = jnp.mean(yc * yc, axis=-1, keepdims=True)
    return yc * lax.rsqrt(var + LN_EPS) * g + b


def _ffn_ln_kernel(x_ref, wg_ref, wu_ref, wd_ref, g_ref, b_ref, o_ref, h_sc, *, alpha, tf):
    xb = x_ref[...].astype(BF16)
    for c in range(wg_ref.shape[1] // tf):
        cols = slice(c * tf, (c + 1) * tf)
        hg = jnp.dot(xb, wg_ref[:, cols], preferred_element_type=F32)
        hu = jnp.dot(xb, wu_ref[:, cols], preferred_element_type=F32)
        h_sc[:, cols] = ((hg * jax.nn.sigmoid(hg)) * hu).astype(BF16)
    ff = jnp.dot(h_sc[...], wd_ref[...], preferred_element_type=F32)
    o_ref[...] = _layer_norm(alpha * x_ref[...] + 0.5 * ff, g_ref[...], b_ref[...])


def ffn_ln(x, wg, wu, wd, g, b, alpha):
    n, d = x.shape
    dff = wg.shape[1]
    tm, tf = TOKEN_TILE, FF_TILE
    resident = pl.Buffered(1)
    return pl.pallas_call(
        functools.partial(_ffn_ln_kernel, alpha=alpha, tf=tf),
        grid=(n // tm,),
        in_specs=[
            pl.BlockSpec((tm, d), lambda i: (i, 0)),
            pl.BlockSpec((d, dff), lambda i: (0, 0), pipeline_mode=resident),
            pl.BlockSpec((d, dff), lambda i: (0, 0), pipeline_mode=resident),
            pl.BlockSpec((dff, d), lambda i: (0, 0), pipeline_mode=resident),
            pl.BlockSpec((1, d), lambda i: (0, 0)),
            pl.BlockSpec((1, d), lambda i: (0, 0)),
        ],
        out_specs=pl.BlockSpec((tm, d), lambda i: (i, 0)),
        out_shape=jax.ShapeDtypeStruct((n, d), F32),
        scratch_shapes=[pltpu.VMEM((tm, dff), BF16)],
        compiler_params=_cparams("parallel"),
        name="ffn_ln",
    )(x, wg, wu, wd, g.reshape(1, d), b.reshape(1, d))


N_ZA = 4 * W_A
N_ZB = 3 * W_B
N_ZC = W_C
N_ZG = V7X_LANES
N_PROJ = N_ZA + N_ZB + N_ZC + N_ZG


def _proj_in_kernel(x_ref, w_ref, b_ref, _kp_prev, _vp_prev, za_ref, zb_ref, zc_ref, zg_ref,
                    kp_ref, vp_ref, ks_ref, vs_ref, *, prompt_tiles):
    i = pl.program_id(0)
    xb = x_ref[...].astype(BF16)
    off = 0
    for ref in (za_ref, zb_ref, zc_ref, zg_ref):
        width = ref.shape[1]
        ref[...] = jnp.dot(xb, w_ref[:, off:off + width], preferred_element_type=F32) + b_ref[:, off:off + width]
        off += width

    def write_kv(k_out, v_out):
        for h in range(H_B):
            k_out[:, h, :] = zb_ref[:, W_B + h * DV_B:W_B + (h + 1) * DV_B]
            v_out[:, h, :] = zb_ref[:, 2 * W_B + h * DV_B:2 * W_B + (h + 1) * DV_B]

    @pl.when(i < prompt_tiles)
    def _():
        write_kv(kp_ref, vp_ref)

    @pl.when(i >= prompt_tiles)
    def _():
        write_kv(ks_ref, vs_ref)


def proj_in(x, w, b, kp_prev, vp_prev, layer, n_prompt):
    n, d = x.shape
    tm = PROJ_TILE
    pt = n_prompt // tm
    widths = (N_ZA, N_ZB, N_ZC, N_ZG)
    kv_blk = (tm, H_B, DV_B)
    p_map = lambda i: (layer, jnp.minimum(i, pt - 1), 0, 0)
    s_map = lambda i: (jnp.maximum(i - pt, 0), 0, 0)
    kv_s = jax.ShapeDtypeStruct((n - n_prompt, H_B, DV_B), F32)
    return pl.pallas_call(
        functools.partial(_proj_in_kernel, prompt_tiles=pt),
        grid=(n // tm,),
        in_specs=[
            pl.BlockSpec((tm, d), lambda i: (i, 0)),
            pl.BlockSpec((d, N_PROJ), lambda i: (0, 0)),
            pl.BlockSpec((1, N_PROJ), lambda i: (0, 0)),
            pl.BlockSpec(memory_space=pl.ANY),
            pl.BlockSpec(memory_space=pl.ANY),
        ],
        out_specs=[pl.BlockSpec((tm, wd), lambda i: (i, 0)) for wd in widths]
        + [pl.BlockSpec((None,) + kv_blk, p_map)] * 2 + [pl.BlockSpec(kv_blk, s_map)] * 2,
        out_shape=[jax.ShapeDtypeStruct((n, wd), F32) for wd in widths]
        + [jax.ShapeDtypeStruct(kp_prev.shape, F32)] * 2 + [kv_s, kv_s],
        input_output_aliases={3: 4, 4: 5},
        compiler_params=_cparams("arbitrary"),
        name="proj_in",
    )(x, w, b, kp_prev, vp_prev)


def _pack_w_in(w_in, b_in):
    def cols(a):
        pad = jnp.zeros(a.shape[:-1] + (N_ZG - 2 * H_A,), a.dtype)
        return jnp.concatenate([a[..., OFF_AQ:OFF_AI], a[..., OFF_BQ:OFF_CU], a[..., OFF_CU:N_IN],
                                a[..., OFF_AI:OFF_BQ], pad], axis=-1)
    return cols(w_in).astype(BF16), cols(b_in)[:, None, :]


def _mem_kv_kernel(x_ref, wk_ref, wv_ref, mk_ref, mv_ref, mkh_ref, mvh_ref):
    xb = x_ref[...].astype(BF16)
    for w_ref, o_ref, oh_ref in ((wk_ref, mk_ref, mkh_ref), (wv_ref, mv_ref, mvh_ref)):
        r = jnp.dot(xb, w_ref[...], preferred_element_type=F32)
        for h in range(H_X):
            o_ref[:, h, :] = r[:, h * DH_X:(h + 1) * DH_X]
            oh_ref[h] = r[:, h * DH_X:(h + 1) * DH_X].astype(BF16)


def mem_kv(x, wk, wv):
    m, d = x.shape
    depth = wk.shape[0]
    tm = min(m, TOKEN_TILE)
    shp = jax.ShapeDtypeStruct((depth, m, H_X, DH_X), F32)
    shp_h = jax.ShapeDtypeStruct((depth, H_X, m, DH_X), BF16)
    w_spec = pl.BlockSpec((None, d, d), lambda i, l: (l, 0, 0))
    o_spec = pl.BlockSpec((None, tm, H_X, DH_X), lambda i, l: (l, i, 0, 0))
    oh_spec = pl.BlockSpec((None, H_X, tm, DH_X), lambda i, l: (l, 0, i, 0))
    return pl.pallas_call(
        _mem_kv_kernel,
        grid=(m // tm, depth),
        in_specs=[pl.BlockSpec((tm, d), lambda i, l: (i, 0)), w_spec, w_spec],
        out_specs=[o_spec, o_spec, oh_spec, oh_spec],
        out_shape=[shp, shp, shp_h, shp_h],
        compiler_params=_cparams("parallel", "arbitrary"),
        name="mem_kv",
    )(x, wk, wv)


def _out_ln_kernel(*refs, n_parts, alpha, has_next):
    parts = refs[:n_parts]
    w_ref, x_ref, g_ref, b_ref = refs[n_parts:n_parts + 4]
    rest = refs[n_parts + 4:]
    if has_next:
        wn_ref, o_ref, q_ref = rest
    else:
        (o_ref,) = rest
    acc = None
    off = 0
    for p in parts:
        width = p.shape[1]
        t = jnp.dot(p[...].astype(BF16), w_ref[off:off + width, :], preferred_element_type=F32)
        acc = t if acc is None else acc + t
        off += width
    y = _layer_norm(alpha * x_ref[...] + acc, g_ref[...], b_ref[...])
    o_ref[...] = y
    if has_next:
        q_ref[...] = jnp.dot(y.astype(BF16), wn_ref[...], preferred_element_type=F32).astype(q_ref.dtype)


def out_ln(parts, w, x, g, b, alpha, w_next=None):
    n, d = x.shape
    tm = PROJ_TILE
    has_next = w_next is not None
    in_specs = [pl.BlockSpec((tm, p.shape[1]), lambda i: (i, 0)) for p in parts]
    in_specs += [
        pl.BlockSpec(w.shape, lambda i: (0, 0)),
        pl.BlockSpec((tm, d), lambda i: (i, 0)),
        pl.BlockSpec((1, d), lambda i: (0, 0)),
        pl.BlockSpec((1, d), lambda i: (0, 0)),
    ]
    args = list(parts) + [w, x, g.reshape(1, d), b.reshape(1, d)]
    out_specs = [pl.BlockSpec((tm, d), lambda i: (i, 0))]
    out_shape = [jax.ShapeDtypeStruct((n, d), F32)]
    if has_next:
        in_specs.append(pl.BlockSpec(w_next.shape, lambda i: (0, 0)))
        args.append(w_next)
        out_specs.append(pl.BlockSpec((tm, w_next.shape[1]), lambda i: (i, 0)))
        out_shape.append(jax.ShapeDtypeStruct((n, w_next.shape[1]), BF16))
    res = pl.pallas_call(
        functools.partial(_out_ln_kernel, n_parts=len(parts), alpha=alpha, has_next=has_next),
        grid=(n // tm,),
        in_specs=in_specs,
        out_specs=out_specs,
        out_shape=out_shape,
        compiler_params=_cparams("parallel"),
        name="out_ln",
    )(*args)
    return res if has_next else res[0]


def _log_sigmoid(x):
    return jnp.minimum(x, 0.0) - jnp.log1p(jnp.exp(-jnp.abs(x)))


def _lane_pick(row, h):
    return row[:, h:h + 1]


def _mlstm_kernel(*refs, chunk, units, sequential):
    if sequential:
        za_ref, zg_ref, g_ref, ya_ref, c1_ref, n1_ref, m1_ref, c_sc, n_sc, m_sc = refs
    else:
        za_ref, zg_ref, g_ref, c0_ref, n0_ref, m0_ref, _prev, ya_ref, c1_ref, n1_ref, m1_ref = refs
    L = chunk
    small = L < V7X_BF16_ROWS
    cast = (lambda a: a) if small else (lambda a: a.astype(BF16))
    mm = lambda a, b: jnp.dot(cast(a), cast(b), preferred_element_type=F32)
    mm_nt = lambda a, b: lax.dot_general(cast(a), cast(b), (((1,), (1,)), ((), ())), preferred_element_type=F32)
    mm_tn = lambda a, b: lax.dot_general(cast(a), cast(b), (((0,), (0,)), ((), ())), preferred_element_type=F32)

    if sequential:
        ti = pl.program_id(1)

        @pl.when(ti == 0)
        def _():
            c_sc[...] = jnp.zeros_like(c_sc)
            n_sc[...] = jnp.zeros_like(n_sc)
            m_sc[...] = jnp.zeros_like(m_sc)

    row = lax.broadcasted_iota(jnp.int32, (L, L), 0)
    col = lax.broadcasted_iota(jnp.int32, (L, L), 1)
    causal = col <= row
    tril = causal.astype(F32)
    sel_r = lax.broadcasted_iota(jnp.int32, (V7X_SUBLANES, V7X_LANES), 0)
    sel_c = lax.broadcasted_iota(jnp.int32, (V7X_SUBLANES, V7X_LANES), 1)
    sel = (sel_r == sel_c).astype(F32)
    lane_g = lax.broadcasted_iota(jnp.int32, (L, N_ZG), 1)
    lane_m = lax.broadcasted_iota(jnp.int32, (1, V7X_LANES), 1)
    norm_g = g_ref[...]
    rep = lambda col: jnp.broadcast_to(col, (L, V7X_LANES))
    wide = lambda x, n: _tile_lanes(x, n // V7X_LANES) if n > V7X_LANES else x[:, :n]

    if sequential:
        state = [(c_sc[h], n_sc[h:h + 1, :], _lane_pick(m_sc[...], h)) for h in range(H_A)]

    for u in range(units):
        rows_u = slice(u * L, (u + 1) * L)
        gates = zg_ref[rows_u, :]
        gl = jnp.where(lane_g < H_A, gates, _log_sigmoid(gates))
        bcum = jnp.dot(tril, gl, precision=lax.Precision.HIGHEST, preferred_element_type=F32)
        mixed = jnp.where(lane_g < H_A, gates, bcum)
        t_rows = lax.dot_general(sel, mixed, (((1,), (1,)), ((), ())),
                                 precision=lax.Precision.HIGHEST, preferred_element_type=F32)
        if not sequential:
            state = [(c0_ref[u, h], n0_ref[u, h:h + 1, :], _lane_pick(m0_ref[u], h)) for h in range(H_A)]
        outs = []
        new_state = []
        for h in range(H_A):
            q = za_ref[rows_u, h * DH_A:(h + 1) * DH_A]
            k = za_ref[rows_u, W_A + h * DH_A:W_A + (h + 1) * DH_A] * (DH_A ** -0.5)
            v = za_ref[rows_u, 2 * W_A + h * DH_A:2 * W_A + (h + 1) * DH_A]
            og = za_ref[rows_u, 3 * W_A + h * DH_A:3 * W_A + (h + 1) * DH_A]
            ig_c = rep(gates[:, h:h + 1])
            b_c = rep(bcum[:, H_A + h:H_A + h + 1])
            ig_row = t_rows[h:h + 1, :]
            b_row = t_rows[H_A + h:H_A + h + 1, :]
            c, n, m_prev = state[h]

            dmat = jnp.where(causal, wide(b_c, L) - b_row + ig_row, NEG_INF)
            a = rep(jnp.max(dmat, axis=-1, keepdims=True))
            s = mm_nt(q, k) * jnp.exp(dmat - wide(a, L))
            n_loc = mm(s, v)
            d_loc = rep(jnp.sum(s, axis=-1, keepdims=True))
            a_last = a[L - 1:L, :]
            b_last = b_c[L - 1:L, :]
            wk = jnp.exp(b_last - b_c + ig_c - a_last)[:, :DH_A]
            u_loc = mm_tn(wk * v, k)
            nu_loc = jnp.sum(wk * k, axis=0, keepdims=True)

            inter = b_c + m_prev
            m_row = jnp.maximum(inter, a)
            r = jnp.exp(a - m_row)
            w_inter = jnp.exp(inter - m_row)
            num = r[:, :DH_A] * n_loc + w_inter[:, :DH_A] * mm_nt(q, c)
            den = r * d_loc + w_inter * rep(jnp.sum(q * n, axis=-1, keepdims=True))
            hh = num / jnp.maximum(jnp.abs(den), jnp.exp(-m_row))[:, :DH_A]
            m_new = m_row[L - 1:L, 0:1]
            decay = jnp.exp(b_last[:, 0:1] + m_prev - m_new)
            e_loc = jnp.exp(a_last[:, 0:1] - m_new)
            new_state.append((decay * c + e_loc * u_loc, decay * n + e_loc * nu_loc, m_new))

            hn = hh * lax.rsqrt(jnp.mean(hh * hh, axis=-1, keepdims=True) + NORM_EPS)
            outs.append(jax.nn.sigmoid(og) * hn)
        ya_ref[rows_u, :] = (jnp.concatenate(outs, axis=-1) * norm_g).astype(ya_ref.dtype)
        state = new_state
        if not sequential:
            m_out = jnp.zeros((1, V7X_LANES), F32)
            for h in range(H_A):
                c1_ref[u, h] = state[h][0]
                n1_ref[u, h:h + 1, :] = state[h][1]
                m_out = jnp.where(lane_m == h, state[h][2], m_out)
            m1_ref[u] = m_out

    if sequential:
        m_out = jnp.zeros((1, V7X_LANES), F32)
        for h in range(H_A):
            c_sc[h] = state[h][0]
            n_sc[h:h + 1, :] = state[h][1]
            m_out = jnp.where(lane_m == h, state[h][2], m_out)
        m_sc[...] = m_out

        @pl.when(ti == pl.num_programs(1) - 1)
        def _():
            c1_ref[0] = c_sc[...]
            n1_ref[0] = n_sc[...]
            m1_ref[0] = m_sc[...]


def mlstm(za, zg, norm_g, nb, t, row0, init=None, prev=None):
    n_tok = za.shape[0]
    L = MLSTM_CHUNK if t % MLSTM_CHUNK == 0 else t
    units = MLSTM_UNITS
    if init is None:
        L, units = MLSTM_PROMPT_CHUNK, 1
    nc = t // L
    rows = units * L
    sequential = init is None
    if sequential:
        grid = (nb, nc // units)
        seq_blk = 1
        row_map = lambda b, c: (row0 // rows + b * (nc // units) + c, 0)
    else:
        assert nc == 1
        grid = (nb // units, 1)
        seq_blk = units
        row_map = lambda b, c: (row0 // rows + b, 0)
    st4 = lambda b, c: (b, 0, 0, 0)
    st3 = lambda b, c: (b, 0, 0)
    in_specs = [
        pl.BlockSpec((rows, N_ZA), row_map),
        pl.BlockSpec((rows, N_ZG), row_map),
        pl.BlockSpec((1, W_A), lambda b, c: (0, 0)),
    ]
    args = [za, zg, norm_g.reshape(1, W_A)]
    aliases = {}
    scratch = []
    if sequential:
        scratch = [pltpu.VMEM((H_A, DH_A, DH_A), F32), pltpu.VMEM((H_A, DH_A), F32),
                   pltpu.VMEM((1, V7X_LANES), F32)]
    else:
        c0, n0, m0 = init
        m0p = jnp.pad(m0, ((0, 0), (0, V7X_LANES - H_A))).reshape(nb, 1, V7X_LANES)
        in_specs += [
            pl.BlockSpec((seq_blk, H_A, DH_A, DH_A), st4),
            pl.BlockSpec((seq_blk, H_A, DH_A), st3),
            pl.BlockSpec((seq_blk, 1, V7X_LANES), st3),
            pl.BlockSpec(memory_space=pl.ANY),
        ]
        args += [c0, n0, m0p, prev]
        aliases = {6: 0}
    out_specs = [
        pl.BlockSpec((rows, W_A), row_map),
        pl.BlockSpec((seq_blk, H_A, DH_A, DH_A), st4),
        pl.BlockSpec((seq_blk, H_A, DH_A), st3),
        pl.BlockSpec((seq_blk, 1, V7X_LANES), st3),
    ]
    out_shape = [
        jax.ShapeDtypeStruct((n_tok, W_A), BF16),
        jax.ShapeDtypeStruct((nb, H_A, DH_A, DH_A), F32),
        jax.ShapeDtypeStruct((nb, H_A, DH_A), F32),
        jax.ShapeDtypeStruct((nb, 1, V7X_LANES), F32),
    ]
    ya, c1, n1, m1 = pl.pallas_call(
        functools.partial(_mlstm_kernel, chunk=L, units=units, sequential=sequential),
        grid=grid,
        in_specs=in_specs,
        out_specs=out_specs,
        out_shape=out_shape,
        scratch_shapes=scratch,
        input_output_aliases=aliases,
        compiler_params=_cparams("parallel", "arbitrary"),
        name="mlstm_prompt" if sequential else "mlstm_sample",
    )(*args)
    return ya, c1, n1, m1[:, 0, :H_A]


def _diff_lambda(lam_ref, lam_init):
    lp = lam_ref[...]
    d01 = jnp.sum(lp[0:1, :] * lp[1:2, :], axis=-1, keepdims=True)
    d23 = jnp.sum(lp[2:3, :] * lp[3:4, :], axis=-1, keepdims=True)
    return jnp.exp(d01) - jnp.exp(d23) + lam_init


def _alibi_slope(h):
    return jnp.where(h == 0, 2.0 ** -2, jnp.where(h == 1, 2.0 ** -4, jnp.where(h == 2, 2.0 ** -6, 2.0 ** -8)))


def _alibi_tables(t):
    slopes = (2.0 ** (-8.0 * jnp.arange(1, H_B + 1, dtype=F32) / H_B))[:, None]
    pos = jnp.arange(t, dtype=jnp.int32)
    hi = ((pos // ALIBI_SPLIT) * ALIBI_SPLIT).astype(F32)[None, :]
    lo = (pos % ALIBI_SPLIT).astype(F32)[None, :]
    ones = jnp.ones((H_B, t), F32)
    pad = jnp.zeros((H_B, t, DV_B - 4), F32)
    aq = jnp.concatenate([jnp.stack([slopes * ones, slopes * ones, -slopes * hi, -slopes * lo], axis=-1), pad], -1)
    ak = jnp.concatenate([jnp.stack([hi * ones, lo * ones, ones, ones], axis=-1), pad], -1)
    return aq.astype(BF16), ak.astype(BF16)


def _attn_prompt_kernel(lam_ref, g_ref, q_ref, aq_ref, k_ref, ak_ref, v_ref, o_ref,
                        qs_sc, m_sc, acc_sc, *, lam_init, tq, tk):
    i = pl.program_id(1)
    j = pl.program_id(2)
    rq = ATTN_ROWS

    @pl.when(j == 0)
    def _():
        lane = lax.broadcasted_iota(jnp.int32, (tq, DV_B), 1)
        for h in range(H_B):
            q = q_ref[:, h * DV_B:(h + 1) * DV_B] * (DK_B ** -0.5)
            qs_sc[h, 0:tq, 0:DV_B] = jnp.where(lane < DK_B, q, 0.0).astype(BF16)
            qs_sc[h, tq:2 * tq, 0:DV_B] = jnp.where(lane >= DK_B, q, 0.0).astype(BF16)
            qs_sc[h, 0:tq, DV_B:2 * DV_B] = aq_ref[h]
            qs_sc[h, tq:2 * tq, DV_B:2 * DV_B] = aq_ref[h]
        m_sc[...] = jnp.full_like(m_sc, NEG_INF)
        acc_sc[...] = jnp.zeros_like(acc_sc)

    def step(masked):
        ones = jnp.ones((tk, V7X_LANES), BF16)
        for h in range(H_B):
            kaug = jnp.concatenate([k_ref[:, h * DV_B:(h + 1) * DV_B].astype(BF16), ak_ref[h]], axis=1)
            vaug = jnp.concatenate([v_ref[:, h * DV_B:(h + 1) * DV_B].astype(BF16), ones], axis=1)
            for r in range(2 * tq // rq):
                rows = slice(r * rq, (r + 1) * rq)
                s = lax.dot_general(qs_sc[h, rows, :], kaug, (((1,), (1,)), ((), ())), preferred_element_type=F32)
                if masked:
                    qi = (r * rq) % tq + lax.broadcasted_iota(jnp.int32, (rq, tk), 0)
                    kj = lax.broadcasted_iota(jnp.int32, (rq, tk), 1)
                    s = jnp.where(kj <= qi, s, NEG_INF)
                m_old = m_sc[h, rows, :]
                m_new = jnp.maximum(m_old, jnp.max(s, axis=-1, keepdims=True))
                alpha = jnp.exp(m_old - m_new)
                p = jnp.exp(s - _tile_lanes(m_new, tk // V7X_LANES))
                acc_sc[h, rows, :] = (_tile_lanes(alpha, 2) * acc_sc[h, rows, :]
                                      + jnp.dot(p.astype(BF16), vaug, preferred_element_type=F32))
                m_sc[h, rows, :] = m_new

    @pl.when(j < i)
    def _():
        step(False)

    @pl.when(j == i)
    def _():
        step(True)

    @pl.when(j == pl.num_programs(2) - 1)
    def _():
        lam = _diff_lambda(lam_ref, lam_init)
        for h in range(H_B):
            o0 = acc_sc[h, 0:tq, 0:DV_B] / acc_sc[h, 0:tq, DV_B:2 * DV_B]
            o1 = acc_sc[h, tq:2 * tq, 0:DV_B] / acc_sc[h, tq:2 * tq, DV_B:2 * DV_B]
            ob = o0 - lam * o1
            on = ob * lax.rsqrt(jnp.mean(ob * ob, axis=-1, keepdims=True) + NORM_EPS)
            o_ref[:, h * DV_B:(h + 1) * DV_B] = on * g_ref[:, h * DV_B:(h + 1) * DV_B] * (1.0 - lam_init)


def diff_attn_prompt(zb, aq, ak, lam_p, norm_g, nb, t, lam_init):
    n_tok = zb.shape[0]
    tq = tk = ATTN_TILE
    nq = t // tq
    kv_map = lambda off: (lambda b, i, j: (b * nq + jnp.minimum(i, j), off))
    return pl.pallas_call(
        functools.partial(_attn_prompt_kernel, lam_init=lam_init, tq=tq, tk=tk),
        grid=(nb, nq, nq),
        in_specs=[
            pl.BlockSpec(lam_p.shape, lambda b, i, j: (0, 0)),
            pl.BlockSpec((1, W_B), lambda b, i, j: (0, 0)),
            pl.BlockSpec((tq, W_B), lambda b, i, j: (b * nq + i, 0)),
            pl.BlockSpec((H_B, tq, DV_B), lambda b, i, j: (0, i, 0)),
            pl.BlockSpec((tk, W_B), kv_map(1)),
            pl.BlockSpec((H_B, tk, DV_B), lambda b, i, j: (0, jnp.minimum(i, j), 0)),
            pl.BlockSpec((tk, W_B), kv_map(2)),
        ],
        out_specs=pl.BlockSpec((tq, W_B), lambda b, i, j: (b * nq + i, 0)),
        out_shape=jax.ShapeDtypeStruct((n_tok, W_B), F32),
        scratch_shapes=[pltpu.VMEM((H_B, 2 * tq, 2 * DV_B), BF16), pltpu.VMEM((H_B, 2 * tq, V7X_LANES), F32),
                        pltpu.VMEM((H_B, 2 * tq, 2 * DV_B), F32)],
        compiler_params=_cparams("parallel", "parallel", "arbitrary"),
        name="diff_attn_prompt",
    )(lam_p, norm_g.reshape(1, W_B), zb, aq, zb, ak, zb)


def _attn_sample_kernel(*refs, lam_init, past_len, t_new, pages_per_step):
    G = pages_per_step
    pt_ref, lam_ref, g_ref, q_ref, kn_ref, vn_ref = refs[:6]
    k_refs = refs[6:6 + G]
    v_refs = refs[6 + G:6 + 2 * G]
    _prev, o_ref, qs_sc, bias_sc, m_sc, l_sc, acc_sc = refs[6 + 2 * G:]
    del pt_ref
    ps = pl.program_id(1)
    rows_per_head = 2 * t_new
    n_rows = H_B * rows_per_head
    page_rows = PAGE_SIZE * H_B
    reps = page_rows // V7X_LANES

    r_lane = lax.broadcasted_iota(jnp.int32, (n_rows, V7X_LANES), 0)
    slope = _alibi_slope(r_lane // rows_per_head).astype(F32)

    @pl.when(ps == 0)
    def _():
        q = q_ref[...] * (DK_B ** -0.5)
        lane = lax.broadcasted_iota(jnp.int32, (t_new, DV_B), 1)
        for h in range(H_B):
            qh = q[:, h * DV_B:(h + 1) * DV_B]
            qs_sc[h * rows_per_head:(h + 1) * rows_per_head, :] = jnp.concatenate(
                [jnp.where(lane < DK_B, qh, 0.0), jnp.where(lane >= DK_B, qh, 0.0)], axis=0).astype(BF16)
        rr = lax.broadcasted_iota(jnp.int32, (n_rows, page_rows), 0)
        cc = lax.broadcasted_iota(jnp.int32, (n_rows, page_rows), 1)
        rel = cc // H_B - (past_len + rr % t_new)
        bias = _alibi_slope(rr // rows_per_head).astype(F32) * rel.astype(F32)
        bias_sc[...] = jnp.where(cc % H_B == rr // rows_per_head, bias, NEG_INF)
        m_sc[...] = jnp.full_like(m_sc, NEG_INF)
        l_sc[...] = jnp.zeros_like(l_sc)
        acc_sc[...] = jnp.zeros_like(acc_sc)

    qs = qs_sc[...]
    m, l, acc = m_sc[...], l_sc[...], acc_sc[...]
    s_pages = []
    m_new = m
    for g in range(G):
        base = ((ps * G + g) * PAGE_SIZE).astype(F32)
        s = _bdot_nt(qs, k_refs[g][...]) + (bias_sc[...] + _tile_lanes(slope * base, reps))
        m_new = jnp.maximum(m_new, jnp.max(s, axis=-1, keepdims=True))
        s_pages.append(s)
    alpha = jnp.exp(m - m_new)
    l = alpha * l
    acc = alpha * acc
    for g in range(G):
        p = jnp.exp(s_pages[g] - _tile_lanes(m_new, reps))
        l = l + jnp.sum(p, axis=-1, keepdims=True)
        acc = acc + _bdot(p, v_refs[g][...])
    m = m_new

    @pl.when(ps < pl.num_programs(1) - 1)
    def _():
        m_sc[...] = m
        l_sc[...] = l
        acc_sc[...] = acc

    @pl.when(ps == pl.num_programs(1) - 1)
    def _():
        rnd = lambda a: a.astype(BF16).astype(F32)
        kn = rnd(kn_ref[...])
        vn = rnd(vn_ref[...])
        qf = qs.astype(F32)
        sn = jnp.concatenate(
            [lax.dot_general(qf[h * rows_per_head:(h + 1) * rows_per_head, :], kn[:, h * DV_B:(h + 1) * DV_B],
                             (((1,), (1,)), ((), ())), preferred_element_type=F32) for h in range(H_B)],
            axis=0)
        rr = lax.broadcasted_iota(jnp.int32, (n_rows, t_new), 0)
        rel = lax.broadcasted_iota(jnp.int32, (n_rows, t_new), 1) - rr % t_new
        sn = jnp.where(rel <= 0, sn + _alibi_slope(rr // rows_per_head).astype(F32) * rel.astype(F32), NEG_INF)
        m_fin = jnp.maximum(m, jnp.max(sn, axis=-1, keepdims=True))
        alpha = jnp.exp(m - m_fin)
        pn = rnd(jnp.exp(sn - m_fin[:, 0:1]))
        l_fin = alpha * l + jnp.sum(pn, axis=-1, keepdims=True)
        pv = jnp.concatenate(
            [jnp.dot(pn[h * rows_per_head:(h + 1) * rows_per_head, :], vn[:, h * DV_B:(h + 1) * DV_B],
                     preferred_element_type=F32) for h in range(H_B)], axis=0)
        o = (alpha * acc + pv) / l_fin

        lam = _diff_lambda(lam_ref, lam_init)
        outs = []
        for h in range(H_B):
            o0 = o[h * rows_per_head:h * rows_per_head + t_new, :]
            o1 = o[h * rows_per_head + t_new:(h + 1) * rows_per_head, :]
            ob = o0 - lam * o1
            outs.append(ob * lax.rsqrt(jnp.mean(ob * ob, axis=-1, keepdims=True) + NORM_EPS))
        o_ref[...] = jnp.concatenate(outs, axis=-1) * g_ref[...] * (1.0 - lam_init)


def diff_attn_sample(zb, cache_k, cache_v, layer, page_table, lam_p, norm_g, nb, t_new, row0, lam_init, prev):
    n_tok = zb.shape[0]
    n_pages = page_table.shape[1]
    G = PAGES_PER_STEP
    rb0 = row0 // t_new
    n_rows = 2 * H_B * t_new
    page_rows = PAGE_SIZE * H_B
    ck = cache_k.reshape(cache_k.shape[0], cache_k.shape[1], page_rows, DV_B)
    cv = cache_v.reshape(cache_v.shape[0], cache_v.shape[1], page_rows, DV_B)

    def page_map(g):
        return lambda b, p, pt: (layer, pt[b, p * G + g], 0, 0)

    page_blk = (None, None, page_rows, DV_B)
    in_specs = [
        pl.BlockSpec(lam_p.shape, lambda b, p, pt: (0, 0)),
        pl.BlockSpec((1, W_B), lambda b, p, pt: (0, 0)),
        pl.BlockSpec((t_new, W_B), lambda b, p, pt: (rb0 + b, 0)),
        pl.BlockSpec((t_new, W_B), lambda b, p, pt: (rb0 + b, 1)),
        pl.BlockSpec((t_new, W_B), lambda b, p, pt: (rb0 + b, 2)),
    ]
    in_specs += [pl.BlockSpec(page_blk, page_map(g)) for g in range(G)]
    in_specs += [pl.BlockSpec(page_blk, page_map(g)) for g in range(G)]
    in_specs += [pl.BlockSpec(memory_space=pl.ANY)]
    args = [page_table, lam_p, norm_g.reshape(1, W_B), zb, zb, zb] + [ck] * G + [cv] * G + [prev]
    grid_spec = pltpu.PrefetchScalarGridSpec(
        num_scalar_prefetch=1,
        grid=(nb, n_pages // G),
        in_specs=in_specs,
        out_specs=pl.BlockSpec((t_new, W_B), lambda b, p, pt: (rb0 + b, 0)),
        scratch_shapes=[pltpu.VMEM((n_rows, DV_B), BF16), pltpu.VMEM((n_rows, page_rows), F32),
                        pltpu.VMEM((n_rows, V7X_LANES), F32), pltpu.VMEM((n_rows, V7X_LANES), F32),
                        pltpu.VMEM((n_rows, DV_B), F32)],
    )
    return pl.pallas_call(
        functools.partial(_attn_sample_kernel, lam_init=lam_init, past_len=n_pages * PAGE_SIZE,
                          t_new=t_new, pages_per_step=G),
        grid_spec=grid_spec,
        out_shape=jax.ShapeDtypeStruct((n_tok, W_B), F32),
        input_output_aliases={len(args) - 1: 0},
        compiler_params=_cparams("parallel", "arbitrary"),
        name="diff_attn_sample",
    )(*args)


def _s5_disc_kernel(are_ref, aim_ref, ldt_ref, bre_ref, bim_ref, lre_ref, lim_ref, bbre_ref, bbim_ref):
    a_re = are_ref[...]
    a_im = aim_ref[...]
    dt = jnp.exp(ldt_ref[...])
    mag = jnp.exp(a_re * dt)
    lb_re = mag * jnp.cos(a_im * dt)
    lb_im = mag * jnp.sin(a_im * dt)
    den = a_re * a_re + a_im * a_im
    xr = lb_re - 1.0
    fr = (xr * a_re + lb_im * a_im) / den
    fi = (lb_im * a_re - xr * a_im) / den
    lre_ref[...] = lb_re
    lim_ref[...] = lb_im
    b_re = bre_ref[...]
    b_im = bim_ref[...]
    bbre_ref[...] = fr * b_re - fi * b_im
    bbim_ref[...] = fr * b_im + fi * b_re


def s5_discretize(a_re, a_im, log_dt, b_re, b_im):
    depth = a_re.shape[0]
    gp = G_C * P_C
    flat = lambda a: a.reshape(depth, 1, gp)
    ldt = jnp.broadcast_to(log_dt[:, :, None], (depth, G_C, P_C)).reshape(depth, 1, gp)
    tr = lambda b: jnp.transpose(b, (0, 3, 1, 2)).reshape(depth, GC, gp)
    shp1 = jax.ShapeDtypeStruct((depth, 1, gp), F32)
    shpb = jax.ShapeDtypeStruct((depth, GC, gp), F32)
    return pl.pallas_call(_s5_disc_kernel, out_shape=[shp1, shp1, shpb, shpb], name="s5_discretize")(
        flat(a_re), flat(a_im), ldt, tr(b_re), tr(b_im))


def _block_diag_in(bb):
    depth, _, gp = bb.shape
    tiled = jnp.tile(bb, (1, G_C, 1)).reshape(depth, G_C, GC, gp)
    grp_r = jnp.arange(G_C)[:, None, None]
    grp_c = (jnp.arange(gp) // P_C)[None, None, :]
    return jnp.where(grp_r == grp_c, tiled, 0.0).reshape(depth, G_C * GC, gp)


def _block_diag_out(c):
    depth = c.shape[0]
    ct = jnp.transpose(c, (0, 1, 3, 2)).reshape(depth, G_C * P_C, GC)
    tiled = jnp.tile(ct, (1, 1, G_C))
    grp_r = (jnp.arange(G_C * P_C) // P_C)[:, None]
    grp_c = (jnp.arange(G_C * GC) // GC)[None, :]
    return jnp.where(grp_r == grp_c, tiled, 0.0)


def _cmul_add(a_re, a_im, x_re, x_im, y_re, y_im):
    return y_re + (a_re * x_re - a_im * x_im), y_im + (a_re * x_im + a_im * x_re)


def _block_scan(x_re, x_im, pw, row8):
    for d, (a_re, a_im) in zip((1, 2, 4), pw):
        sh_re = jnp.where(row8 >= d, pltpu.roll(x_re, d, 0), 0.0)
        sh_im = jnp.where(row8 >= d, pltpu.roll(x_im, d, 0), 0.0)
        x_re, x_im = _cmul_add(a_re, a_im, sh_re, sh_im, x_re, x_im)
    return x_re, x_im


def _s5_kernel(*refs, rows, independent, aliased):
    (u_ref, lre_ref, lim_ref, bb_ref, cc_ref, d_ref, gw_ref, gb_ref, s0re_ref, s0im_ref) = refs[:10]
    rest = refs[10:]
    if aliased:
        rest = rest[1:]
    y_ref, s1re_ref, s1im_ref, st_sc, car_sc = rest
    gp = G_C * P_C
    nblk = rows // V7X_SUBLANES
    ti = pl.program_id(1)

    lam_re = lre_ref[...]
    lam_im = lim_ref[...]
    l2_re, l2_im = lam_re * lam_re - lam_im * lam_im, 2.0 * lam_re * lam_im
    l4_re, l4_im = l2_re * l2_re - l2_im * l2_im, 2.0 * l2_re * l2_im
    pw = ((lam_re, lam_im), (l2_re, l2_im), (l4_re, l4_im))
    row8 = lax.broadcasted_iota(jnp.int32, (V7X_SUBLANES, gp), 0)
    pk_re, pk_im = _block_scan(jnp.where(row8 == 0, lam_re, 0.0), jnp.where(row8 == 0, lam_im, 0.0), pw, row8)

    u = u_ref[...]
    st_sc[...] = _bdot(u, bb_ref[...])

    if not independent:
        @pl.when(ti == 0)
        def _():
            car_sc[0:1, :] = s0re_ref[...]
            car_sc[1:2, :] = s0im_ref[...]

    def body(bi, carry):
        r0 = pl.multiple_of(bi * V7X_SUBLANES, V7X_SUBLANES)
        x_re = st_sc[pl.ds(r0, V7X_SUBLANES), 0:gp]
        x_im = st_sc[pl.ds(r0, V7X_SUBLANES), gp:2 * gp]
        x_re, x_im = _block_scan(x_re, x_im, pw, row8)
        if independent:
            c_re = s0re_ref[pl.ds(bi, 1), :]
            c_im = s0im_ref[pl.ds(bi, 1), :]
        else:
            c_re, c_im = carry
        s_re, s_im = _cmul_add(pk_re, pk_im, c_re, c_im, x_re, x_im)
        st_sc[pl.ds(r0, V7X_SUBLANES), 0:gp] = s_re
        st_sc[pl.ds(r0, V7X_SUBLANES), gp:2 * gp] = s_im
        last_re = s_re[V7X_SUBLANES - 1:V7X_SUBLANES, :]
        last_im = s_im[V7X_SUBLANES - 1:V7X_SUBLANES, :]
        if independent:
            s1re_ref[pl.ds(bi, 1), :] = last_re
            s1im_ref[pl.ds(bi, 1), :] = last_im
            return carry
        return last_re, last_im

    if independent:
        lax.fori_loop(0, nblk, body, 0)
    else:
        c_re, c_im = lax.fori_loop(0, nblk, body, (car_sc[0:1, :], car_sc[1:2, :]))
        car_sc[0:1, :] = c_re
        car_sc[1:2, :] = c_im

        @pl.when(ti == pl.num_programs(1) - 1)
        def _():
            s1re_ref[...] = c_re
            s1im_ref[...] = c_im

    y = _bdot(st_sc[...], cc_ref[...]) + d_ref[...] * u
    z = _bdot(jax.nn.gelu(y), gw_ref[...]) + gb_ref[...]
    y_ref[...] = (z[:, :W_C] * jax.nn.sigmoid(z[:, W_C:])).astype(y_ref.dtype)


def s5_mixer(zc, lam_re, lam_im, bb, cc, d, glu_w, glu_b, s0_re, s0_im, nb, t, row0, prev=None):
    n_tok = zc.shape[0]
    gp = G_C * P_C
    independent = t == V7X_SUBLANES
    aliased = prev is not None
    if independent:
        seqs = S5_SAMPLE_SEQS
        rows = seqs * t
        grid = (nb // seqs, 1)
        st_spec = pl.BlockSpec((seqs, gp), lambda b, i: (b, 0))
    else:
        rows = S5_TILE
        grid = (nb, t // rows)
        st_spec = pl.BlockSpec((1, gp), lambda b, i: (b, 0))
    nt = grid[1]
    rb0 = row0 // rows
    row_map = lambda b, i: (rb0 + b * nt + i, 0)
    const = lambda b, i: (0, 0)
    in_specs = [
        pl.BlockSpec((rows, W_C), row_map),
        pl.BlockSpec((1, gp), const),
        pl.BlockSpec((1, gp), const),
        pl.BlockSpec(bb.shape, const),
        pl.BlockSpec(cc.shape, const),
        pl.BlockSpec((1, W_C), const),
        pl.BlockSpec(glu_w.shape, const),
        pl.BlockSpec((1, 2 * W_C), const),
        st_spec,
        st_spec,
    ]
    args = [zc, lam_re, lam_im, bb, cc, d.reshape(1, W_C), glu_w, glu_b.reshape(1, 2 * W_C), s0_re, s0_im]
    aliases = {}
    if aliased:
        in_specs.append(pl.BlockSpec(memory_space=pl.ANY))
        args.append(prev)
        aliases = {len(args) - 1: 0}
    if independent:
        s1_shape = jax.ShapeDtypeStruct((nb, gp), F32)
    else:
        s1_shape = jax.ShapeDtypeStruct((nb, 1, gp), F32)
        st_out = pl.BlockSpec((None, 1, gp), lambda b, i: (b, 0, 0))
    out_specs = [pl.BlockSpec((rows, W_C), row_map)] + ([st_spec, st_spec] if independent else [st_out, st_out])
    if not independent:
        args[8] = s0_re.reshape(nb, 1, gp)
        args[9] = s0_im.reshape(nb, 1, gp)
        in_specs[8] = in_specs[9] = pl.BlockSpec((None, 1, gp), lambda b, i: (b, 0, 0))
    yc, s1_re, s1_im = pl.pallas_call(
        functools.partial(_s5_kernel, rows=rows, independent=independent, aliased=aliased),
        grid=grid,
        in_specs=in_specs,
        out_specs=out_specs,
        out_shape=[jax.ShapeDtypeStruct((n_tok, W_C), BF16), s1_shape, s1_shape],
        scratch_shapes=[pltpu.VMEM((rows, 2 * gp), F32), pltpu.VMEM((V7X_SUBLANES, gp), F32)],
        input_output_aliases=aliases,
        compiler_params=_cparams("parallel", "arbitrary"),
        name="s5_sample" if independent else "s5_prompt",
    )(*args)
    return yc, s1_re.reshape(nb, G_C, P_C), s1_im.reshape(nb, G_C, P_C)


def _xattn_kernel(q_ref, mk_ref, mv_ref, o_ref):
    q = q_ref[...] * (DH_X ** -0.5)
    outs = []
    for h in range(H_X):
        s = _bdot_nt(q[:, h * DH_X:(h + 1) * DH_X], mk_ref[h])
        m = jnp.max(s, axis=-1, keepdims=True)
        p = jnp.exp(s - m)
        l = jnp.sum(p, axis=-1, keepdims=True)
        outs.append(_bdot(p, mv_ref[h]) / l)
    o_ref[...] = jnp.concatenate(outs, axis=-1).astype(o_ref.dtype)


def _xattn_native_kernel(q_ref, mk_ref, mv_ref, _prev, o_ref, bias_sc, *, tq, seqs, n_mem):
    rows = seqs * tq
    n_keys = seqs * n_mem * H_X

    @pl.when(pl.program_id(0) == 0)
    def _():
        rr = lax.broadcasted_iota(jnp.int32, (H_X * rows, n_keys), 0)
        cc = lax.broadcasted_iota(jnp.int32, (H_X * rows, n_keys), 1)
        own = (cc // (n_mem * H_X) == (rr % rows) // tq) & (cc % H_X == rr // rows)
        bias_sc[...] = jnp.where(own, 0.0, NEG_INF)

    q = q_ref[...] * (DH_X ** -0.5)
    q_all = jnp.concatenate([q[:, h * DH_X:(h + 1) * DH_X] for h in range(H_X)], axis=0)
    k2 = mk_ref[...].reshape(n_keys, DH_X)
    v2 = mv_ref[...].reshape(n_keys, DH_X)
    s = _bdot_nt(q_all, k2) + bias_sc[...]
    m = jnp.max(s, axis=-1, keepdims=True)
    p = jnp.exp(s - m)
    l = jnp.sum(p, axis=-1, keepdims=True)
    o = _bdot(p, v2) / l
    o_ref[...] = jnp.concatenate([o[h * rows:(h + 1) * rows, :] for h in range(H_X)], axis=-1).astype(o_ref.dtype)


def cross_attn_native(qx, mem_k, mem_v, layer, nb, t, row0, prev):
    n_tok, d = qx.shape
    n_mem = mem_k.shape[2]
    seqs = XATTN_SAMPLE_SEQS
    rows = seqs * t
    rb0 = row0 // rows
    mem_spec = pl.BlockSpec((None, seqs, n_mem, H_X, DH_X), lambda b: (layer, b, 0, 0, 0))
    return pl.pallas_call(
        functools.partial(_xattn_native_kernel, tq=t, seqs=seqs, n_mem=n_mem),
        grid=(nb // seqs,),
        in_specs=[pl.BlockSpec((rows, d), lambda b: (rb0 + b, 0)), mem_spec, mem_spec,
                  pl.BlockSpec(memory_space=pl.ANY)],
        out_specs=pl.BlockSpec((rows, d), lambda b: (rb0 + b, 0)),
        out_shape=jax.ShapeDtypeStruct((n_tok, d), prev.dtype),
        scratch_shapes=[pltpu.VMEM((H_X * rows, seqs * n_mem * H_X), F32)],
        input_output_aliases={3: 0},
        compiler_params=_cparams("arbitrary"),
        name="cross_attn_sample",
    )(qx, mem_k, mem_v, prev)


def cross_attn(qx, mk, mv, layer, n_mem, nb, t, out_dtype):
    n_tok, d = qx.shape
    tq = XATTN_TILE
    nq = t // tq
    mem_spec = pl.BlockSpec((None, H_X, n_mem, DH_X), lambda b, i: (layer, 0, b, 0))
    return pl.pallas_call(
        _xattn_kernel,
        grid=(nb, nq),
        in_specs=[pl.BlockSpec((tq, d), lambda b, i: (b * nq + i, 0)), mem_spec, mem_spec],
        out_specs=pl.BlockSpec((tq, d), lambda b, i: (b * nq + i, 0)),
        out_shape=jax.ShapeDtypeStruct((n_tok, d), out_dtype),
        compiler_params=_cparams("parallel", "arbitrary"),
        name="cross_attn_prompt",
    )(qx, mk, mv)


def kernel(x_prompt, x_sample, mem_prompt, cache_k, cache_v, page_table, cache_mem_k, cache_mem_v,
           state_mlstm_c, state_mlstm_n, state_mlstm_m, state_ssm_re, state_ssm_im,
           ln_g, ln_b, ffn1_wg, ffn1_wu, ffn1_wd, ffn2_wg, ffn2_wu, ffn2_wd, w_in, b_in,
           mlstm_norm_g, diff_lam, diff_norm_g, ssm_a_re, ssm_a_im, ssm_log_dt, ssm_b_re, ssm_b_im,
           ssm_c_re, ssm_c_im, ssm_d, ssm_glu_w, ssm_glu_b, w_out, cross_wq, cross_wk, cross_wv, cross_wo):
    bp, tp, d = x_prompt.shape
    bs, ts, _ = x_sample.shape
    depth = ln_g.shape[0]
    n_mem = mem_prompt.shape[1]
    n_p = bp * tp
    gp = G_C * P_C
    alpha = (2.0 * depth) ** 0.25

    cast = lambda w: w.astype(BF16)
    ffn1_wg, ffn1_wu, ffn1_wd = cast(ffn1_wg), cast(ffn1_wu), cast(ffn1_wd)
    ffn2_wg, ffn2_wu, ffn2_wd = cast(ffn2_wg), cast(ffn2_wu), cast(ffn2_wd)
    w_out_b, wq_b, wo_b, glu_w_b = cast(w_out), cast(cross_wq), cast(cross_wo), cast(ssm_glu_w)
    w_in_p, b_in_p = _pack_w_in(w_in, b_in)
    lam_re, lam_im, bb_re, bb_im = s5_discretize(ssm_a_re, ssm_a_im, ssm_log_dt, ssm_b_re, ssm_b_im)
    bb = jnp.concatenate([_block_diag_in(bb_re), _block_diag_in(bb_im)], axis=-1).astype(BF16)
    cc = jnp.concatenate([_block_diag_out(ssm_c_re), -_block_diag_out(ssm_c_im)], axis=1).astype(BF16)
    alibi_q, alibi_k = _alibi_tables(tp)

    p_mk, p_mv, p_mkh, p_mvh = mem_kv(mem_prompt.reshape(bp * n_mem, d), cast(cross_wk), cast(cross_wv))
    p_mk = p_mk.reshape(depth, bp, n_mem, H_X, DH_X)
    p_mv = p_mv.reshape(depth, bp, n_mem, H_X, DH_X)

    x = jnp.concatenate([x_prompt.reshape(n_p, d), x_sample.reshape(bs * ts, d)], axis=0)
    zeros_s = jnp.zeros((bp, gp), F32)
    p_k = jnp.zeros((depth, n_p, H_B, DV_B), F32)
    p_v = jnp.zeros((depth, n_p, H_B, DV_B), F32)
    p_st, s_st = [], []
    for l in range(depth):
        lam_init = 0.8 - 0.6 * math.exp(-0.3 * l)
        x = ffn_ln(x, ffn1_wg[l], ffn1_wu[l], ffn1_wd[l], ln_g[l, 0], ln_b[l, 0], alpha)
        za, zb, zc, zg, p_k, p_v, s_k, s_v = proj_in(x, w_in_p[l], b_in_p[l], p_k, p_v, l, n_p)

        ya, pc, pn, pm = mlstm(za, zg, mlstm_norm_g[l], bp, tp, 0)
        ya, sc, sn, sm = mlstm(za, zg, mlstm_norm_g[l], bs, ts, n_p,
                               init=(state_mlstm_c[l], state_mlstm_n[l], state_mlstm_m[l]), prev=ya)

        yb = diff_attn_prompt(zb, alibi_q, alibi_k, diff_lam[l], diff_norm_g[l], bp, tp, lam_init)
        yb = diff_attn_sample(zb, cache_k, cache_v, l, page_table, diff_lam[l], diff_norm_g[l],
                              bs, ts, n_p, lam_init, yb)

        s5_args = (lam_re[l], lam_im[l], bb[l], cc[l], ssm_d[l], glu_w_b[l], ssm_glu_b[l])
        yc, psr, psi = s5_mixer(zc, *s5_args, zeros_s, zeros_s, bp, tp, 0)
        yc, ssr, ssi = s5_mixer(zc, *s5_args, state_ssm_re[l].reshape(bs, gp), state_ssm_im[l].reshape(bs, gp),
                                bs, ts, n_p, prev=yc)

        x, qx = out_ln([ya, yb, yc], w_out_b[l], x, ln_g[l, 1], ln_b[l, 1], alpha, w_next=wq_b[l])
        o = cross_attn(qx, p_mkh, p_mvh, l, n_mem, bp, tp, BF16)
        o = cross_attn_native(qx, cache_mem_k, cache_mem_v, l, bs, ts, n_p, o)
        x = out_ln([o], wo_b[l], x, ln_g[l, 2], ln_b[l, 2], alpha)
        x = ffn_ln(x, ffn2_wg[l], ffn2_wu[l], ffn2_wd[l], ln_g[l, 3], ln_b[l, 3], alpha)

        p_st.append((pc, pn, pm, psr, psi))
        s_st.append((s_k.reshape(bs, ts, H_B, DV_B), s_v.reshape(bs, ts, H_B, DV_B), sc, sn, sm, ssr, ssi))

    p_c, p_n, p_m, p_sr, p_si = [jnp.stack(a) for a in zip(*p_st)]
    s_k, s_v, s_c, s_n, s_m, s_sr, s_si = [jnp.stack(a) for a in zip(*s_st)]
    p_k = p_k.reshape(depth, bp, tp, H_B, DV_B)
    p_v = p_v.reshape(depth, bp, tp, H_B, DV_B)
    yp = x[:n_p].reshape(bp, tp, d)
    ys = x[n_p:].reshape(bs, ts, d)
    return (yp, ys, p_k, p_v, p_mk, p_mv, p_c, p_n, p_m, p_sr, p_si, s_k, s_v, s_c, s_n, s_m, s_sr, s_si)
```

```python
import functools
import math

import jax
import jax.numpy as jnp
from jax import lax
from jax.experimental import pallas as pl
from jax.experimental.pallas import tpu as pltpu

F32 = jnp.float32
BF16 = jnp.bfloat16

D_MODEL = 1024
PAGE_SIZE = 128
W_A = D_MODEL // 4
W_B = D_MODEL // 2
W_C = D_MODEL - W_A - W_B
H_A = 4
DH_A = W_A // H_A
MLSTM_CHUNK = 64
H_B = 4
DV_B = W_B // H_B
DK_B = DV_B // 2
GC = 16
G_C = W_C // GC
P_C = 64
H_X = 4
DH_X = D_MODEL // H_X
LN_EPS = 1e-5
NORM_EPS = 1e-6

OFF_AQ = 0
OFF_AK = OFF_AQ + W_A
OFF_AV = OFF_AK + W_A
OFF_AO = OFF_AV + W_A
OFF_AI = OFF_AO + W_A
OFF_AF = OFF_AI + H_A
OFF_BQ = OFF_AF + H_A
OFF_BK = OFF_BQ + H_B * 2 * DK_B
OFF_BV = OFF_BK + H_B * 2 * DK_B
OFF_CU = OFF_BV + W_B
N_IN = OFF_CU + W_C

V7X_LANES = 128
V7X_SUBLANES = 8
V7X_BF16_ROWS = 2 * V7X_SUBLANES
V7X_VMEM_LIMIT_BYTES = 56 * 1024 * 1024

TOKEN_TILE = 1024
FF_TILE = 256
PROJ_TILE = 512
ATTN_TILE = 512
ATTN_ROWS = 256
XATTN_TILE = 512
S5_TILE = 512
PAGES_PER_STEP = 16
XATTN_SAMPLE_SEQS = 4
S5_SAMPLE_SEQS = 16
MLSTM_PROMPT_CHUNK = 512
MLSTM_UNITS = 16
ALIBI_SPLIT = 64

NEG_INF = float("-inf")


def _cparams(*sem):
    return pltpu.CompilerParams(dimension_semantics=sem, vmem_limit_bytes=V7X_VMEM_LIMIT_BYTES)


def _bdot(a, b):
    return jnp.dot(a.astype(BF16), b.astype(BF16), preferred_element_type=F32)


def _bdot_nt(a, b):
    return lax.dot_general(a.astype(BF16), b.astype(BF16), (((1,), (1,)), ((), ())),
                           preferred_element_type=F32)


def _tile_lanes(x, n):
    return x if n == 1 else jnp.concatenate([x] * n, axis=1)


def _layer_norm(y, g, b):
    mu = jnp.mean(y, axis=-1, keepdims=True)
    yc = y - mu
    var = jnp.mean(yc * yc, axis=-1, keepdims=True)
    return yc * lax.rsqrt(var + LN_EPS) * g + b


def _ffn_ln_kernel(x_ref, wg_ref, wu_ref, wd_ref, g_ref, b_ref, o_ref, h_sc, *, alpha, tf):
    xb = x_ref[...].astype(BF16)
    for c in range(wg_ref.shape[1] // tf):
        cols = slice(c * tf, (c + 1) * tf)
        hg = jnp.dot(xb, wg_ref[:, cols], preferred_element_type=F32)
        hu = jnp.dot(xb, wu_ref[:, cols], preferred_element_type=F32)
        h_sc[:, cols] = ((hg * jax.nn.sigmoid(hg)) * hu).astype(BF16)
    ff = jnp.dot(h_sc[...], wd_ref[...], preferred_element_type=F32)
    o_ref[...] = _layer_norm(alpha * x_ref[...] + 0.5 * ff, g_ref[...], b_ref[...])


def ffn_ln(x, wg, wu, wd, layer, g, b, alpha):
    n, d = x.shape
    dff = wg.shape[2]
    tm, tf = TOKEN_TILE, FF_TILE
    resident = pl.Buffered(1)
    return pl.pallas_call(
        functools.partial(_ffn_ln_kernel, alpha=alpha, tf=tf),
        grid=(n // tm,),
        in_specs=[
            pl.BlockSpec((tm, d), lambda i: (i, 0)),
            pl.BlockSpec((None, d, dff), lambda i: (layer, 0, 0), pipeline_mode=resident),
            pl.BlockSpec((None, d, dff), lambda i: (layer, 0, 0), pipeline_mode=resident),
            pl.BlockSpec((None, dff, d), lambda i: (layer, 0, 0), pipeline_mode=resident),
            pl.BlockSpec((1, d), lambda i: (0, 0)),
            pl.BlockSpec((1, d), lambda i: (0, 0)),
        ],
        out_specs=pl.BlockSpec((tm, d), lambda i: (i, 0)),
        out_shape=jax.ShapeDtypeStruct((n, d), F32),
        scratch_shapes=[pltpu.VMEM((tm, dff), BF16)],
        compiler_params=_cparams("parallel"),
        name="ffn_ln",
    )(x, wg, wu, wd, g.reshape(1, d), b.reshape(1, d))


N_ZA = 4 * W_A
N_ZB = 3 * W_B
N_ZC = W_C
N_ZG = V7X_LANES
N_PROJ = N_ZA + N_ZB + N_ZC + N_ZG


def _proj_in_kernel(x_ref, w_ref, b_ref, _kp_prev, _vp_prev, za_ref, zb_ref, zc_ref, zg_ref,
                    kp_ref, vp_ref, ks_ref, vs_ref, *, prompt_tiles):
    i = pl.program_id(0)
    xb = x_ref[...].astype(BF16)
    off = 0
    for ref in (za_ref, zb_ref, zc_ref, zg_ref):
        width = ref.shape[1]
        ref[...] = jnp.dot(xb, w_ref[:, off:off + width], preferred_element_type=F32) + b_ref[:, off:off + width]
        off += width

    def write_kv(k_out, v_out):
        for h in range(H_B):
            k_out[:, h, :] = zb_ref[:, W_B + h * DV_B:W_B + (h + 1) * DV_B]
            v_out[:, h, :] = zb_ref[:, 2 * W_B + h * DV_B:2 * W_B + (h + 1) * DV_B]

    @pl.when(i < prompt_tiles)
    def _():
        write_kv(kp_ref, vp_ref)

    @pl.when(i >= prompt_tiles)
    def _():
        write_kv(ks_ref, vs_ref)


def proj_in(x, w, b, kp_prev, vp_prev, layer, n_prompt):
    n, d = x.shape
    tm = PROJ_TILE
    pt = n_prompt // tm
    widths = (N_ZA, N_ZB, N_ZC, N_ZG)
    kv_blk = (tm, H_B, DV_B)
    p_map = lambda i: (layer, jnp.minimum(i, pt - 1), 0, 0)
    s_map = lambda i: (jnp.maximum(i - pt, 0), 0, 0)
    kv_s = jax.ShapeDtypeStruct((n - n_prompt, H_B, DV_B), F32)
    return pl.pallas_call(
        functools.partial(_proj_in_kernel, prompt_tiles=pt),
        grid=(n // tm,),
        in_specs=[
            pl.BlockSpec((tm, d), lambda i: (i, 0)),
            pl.BlockSpec((None, d, N_PROJ), lambda i: (layer, 0, 0)),
            pl.BlockSpec((None, 1, N_PROJ), lambda i: (layer, 0, 0)),
            pl.BlockSpec(memory_space=pl.ANY),
            pl.BlockSpec(memory_space=pl.ANY),
        ],
        out_specs=[pl.BlockSpec((tm, wd), lambda i: (i, 0)) for wd in widths]
        + [pl.BlockSpec((None,) + kv_blk, p_map)] * 2 + [pl.BlockSpec(kv_blk, s_map)] * 2,
        out_shape=[jax.ShapeDtypeStruct((n, wd), F32) for wd in widths]
        + [jax.ShapeDtypeStruct(kp_prev.shape, F32)] * 2 + [kv_s, kv_s],
        input_output_aliases={3: 4, 4: 5},
        compiler_params=_cparams("arbitrary"),
        name="proj_in",
    )(x, w, b, kp_prev, vp_prev)


def _pack_w_in(w_in, b_in):
    def cols(a):
        pad = jnp.zeros(a.shape[:-1] + (N_ZG - 2 * H_A,), a.dtype)
        return jnp.concatenate([a[..., OFF_AQ:OFF_AI], a[..., OFF_BQ:OFF_CU], a[..., OFF_CU:N_IN],
                                a[..., OFF_AI:OFF_BQ], pad], axis=-1)
    return cols(w_in).astype(BF16), cols(b_in)[:, None, :]


def _mem_kv_kernel(x_ref, wk_ref, wv_ref, mk_ref, mv_ref, mkh_ref, mvh_ref):
    xb = x_ref[...].astype(BF16)
    for w_ref, o_ref, oh_ref in ((wk_ref, mk_ref, mkh_ref), (wv_ref, mv_ref, mvh_ref)):
        r = jnp.dot(xb, w_ref[...], preferred_element_type=F32)
        for h in range(H_X):
            o_ref[:, h, :] = r[:, h * DH_X:(h + 1) * DH_X]
            oh_ref[h] = r[:, h * DH_X:(h + 1) * DH_X].astype(BF16)


def mem_kv(x, wk, wv):
    m, d = x.shape
    depth = wk.shape[0]
    tm = min(m, TOKEN_TILE)
    shp = jax.ShapeDtypeStruct((depth, m, H_X, DH_X), F32)
    shp_h = jax.ShapeDtypeStruct((depth, H_X, m, DH_X), BF16)
    w_spec = pl.BlockSpec((None, d, d), lambda i, l: (l, 0, 0))
    o_spec = pl.BlockSpec((None, tm, H_X, DH_X), lambda i, l: (l, i, 0, 0))
    oh_spec = pl.BlockSpec((None, H_X, tm, DH_X), lambda i, l: (l, 0, i, 0))
    return pl.pallas_call(
        _mem_kv_kernel,
        grid=(m // tm, depth),
        in_specs=[pl.BlockSpec((tm, d), lambda i, l: (i, 0)), w_spec, w_spec],
        out_specs=[o_spec, o_spec, oh_spec, oh_spec],
        out_shape=[shp, shp, shp_h, shp_h],
        compiler_params=_cparams("parallel", "arbitrary"),
        name="mem_kv",
    )(x, wk, wv)


def _out_ln_kernel(*refs, n_parts, alpha, has_next):
    parts = refs[:n_parts]
    w_ref, x_ref, g_ref, b_ref = refs[n_parts:n_parts + 4]
    rest = refs[n_parts + 4:]
    if has_next:
        wn_ref, o_ref, q_ref = rest
    else:
        (o_ref,) = rest
    acc = None
    off = 0
    for p in parts:
        width = p.shape[1]
        t = jnp.dot(p[...].astype(BF16), w_ref[off:off + width, :], preferred_element_type=F32)
        acc = t if acc is None else acc + t
        off += width
    y = _layer_norm(alpha * x_ref[...] + acc, g_ref[...], b_ref[...])
    o_ref[...] = y
    if has_next:
        q_ref[...] = jnp.dot(y.astype(BF16), wn_ref[...], preferred_element_type=F32).astype(q_ref.dtype)


def out_ln(parts, w, layer, x, g, b, alpha, w_next=None):
    n, d = x.shape
    tm = PROJ_TILE
    has_next = w_next is not None
    in_specs = [pl.BlockSpec((tm, p.shape[1]), lambda i: (i, 0)) for p in parts]
    in_specs += [
        pl.BlockSpec((None,) + w.shape[1:], lambda i: (layer, 0, 0)),
        pl.BlockSpec((tm, d), lambda i: (i, 0)),
        pl.BlockSpec((1, d), lambda i: (0, 0)),
        pl.BlockSpec((1, d), lambda i: (0, 0)),
    ]
    args = list(parts) + [w, x, g.reshape(1, d), b.reshape(1, d)]
    out_specs = [pl.BlockSpec((tm, d), lambda i: (i, 0))]
    out_shape = [jax.ShapeDtypeStruct((n, d), F32)]
    if has_next:
        in_specs.append(pl.BlockSpec((None,) + w_next.shape[1:], lambda i: (layer, 0, 0)))
        args.append(w_next)
        out_specs.append(pl.BlockSpec((tm, w_next.shape[2]), lambda i: (i, 0)))
        out_shape.append(jax.ShapeDtypeStruct((n, w_next.shape[2]), BF16))
    res = pl.pallas_call(
        functools.partial(_out_ln_kernel, n_parts=len(parts), alpha=alpha, has_next=has_next),
        grid=(n // tm,),
        in_specs=in_specs,
        out_specs=out_specs,
        out_shape=out_shape,
        compiler_params=_cparams("parallel"),
        name="out_ln",
    )(*args)
    return res if has_next else res[0]


def _log_sigmoid(x):
    return jnp.minimum(x, 0.0) - jnp.log1p(jnp.exp(-jnp.abs(x)))


def _lane_pick(row, h):
    return row[:, h:h + 1]


def _mlstm_kernel(*refs, chunk, units, sequential):
    if sequential:
        za_ref, zg_ref, g_ref, ya_ref, c1_ref, n1_ref, m1_ref, c_sc, n_sc, m_sc = refs
    else:
        za_ref, zg_ref, g_ref, c0_ref, n0_ref, m0_ref, _prev, ya_ref, c1_ref, n1_ref, m1_ref = refs
    L = chunk
    small = L < V7X_BF16_ROWS
    cast = (lambda a: a) if small else (lambda a: a.astype(BF16))
    mm = lambda a, b: jnp.dot(cast(a), cast(b), preferred_element_type=F32)
    mm_nt = lambda a, b: lax.dot_general(cast(a), cast(b), (((1,), (1,)), ((), ())), preferred_element_type=F32)
    mm_tn = lambda a, b: lax.dot_general(cast(a), cast(b), (((0,), (0,)), ((), ())), preferred_element_type=F32)

    if sequential:
        ti = pl.program_id(1)

        @pl.when(ti == 0)
        def _():
            c_sc[...] = jnp.zeros_like(c_sc)
            n_sc[...] = jnp.zeros_like(n_sc)
            m_sc[...] = jnp.zeros_like(m_sc)

    row = lax.broadcasted_iota(jnp.int32, (L, L), 0)
    col = lax.broadcasted_iota(jnp.int32, (L, L), 1)
    causal = col <= row
    tril = causal.astype(F32)
    sel_r = lax.broadcasted_iota(jnp.int32, (V7X_SUBLANES, V7X_LANES), 0)
    sel_c = lax.broadcasted_iota(jnp.int32, (V7X_SUBLANES, V7X_LANES), 1)
    sel = (sel_r == sel_c).astype(F32)
    lane_g = lax.broadcasted_iota(jnp.int32, (L, N_ZG), 1)
    lane_m = lax.broadcasted_iota(jnp.int32, (1, V7X_LANES), 1)
    norm_g = g_ref[...]
    rep = lambda col: jnp.broadcast_to(col, (L, V7X_LANES))
    wide = lambda x, n: _tile_lanes(x, n // V7X_LANES) if n > V7X_LANES else x[:, :n]

    if sequential:
        state = [(c_sc[h], n_sc[h:h + 1, :], _lane_pick(m_sc[...], h)) for h in range(H_A)]

    for u in range(units):
        rows_u = slice(u * L, (u + 1) * L)
        gates = zg_ref[rows_u, :]
        gl = jnp.where(lane_g < H_A, gates, _log_sigmoid(gates))
        bcum = jnp.dot(tril, gl, precision=lax.Precision.HIGHEST, preferred_element_type=F32)
        mixed = jnp.where(lane_g < H_A, gates, bcum)
        t_rows = lax.dot_general(sel, mixed, (((1,), (1,)), ((), ())),
                                 precision=lax.Precision.HIGHEST, preferred_element_type=F32)
        if not sequential:
            state = [(c0_ref[u, h], n0_ref[u, h:h + 1, :], _lane_pick(m0_ref[u], h)) for h in range(H_A)]
        outs = []
        new_state = []
        for h in range(H_A):
            q = za_ref[rows_u, h * DH_A:(h + 1) * DH_A]
            k = za_ref[rows_u, W_A + h * DH_A:W_A + (h + 1) * DH_A] * (DH_A ** -0.5)
            v = za_ref[rows_u, 2 * W_A + h * DH_A:2 * W_A + (h + 1) * DH_A]
            og = za_ref[rows_u, 3 * W_A + h * DH_A:3 * W_A + (h + 1) * DH_A]
            ig_c = rep(gates[:, h:h + 1])
            b_c = rep(bcum[:, H_A + h:H_A + h + 1])
            ig_row = t_rows[h:h + 1, :]
            b_row = t_rows[H_A + h:H_A + h + 1, :]
            c, n, m_prev = state[h]

            dmat = jnp.where(causal, wide(b_c, L) - b_row + ig_row, NEG_INF)
            a = rep(jnp.max(dmat, axis=-1, keepdims=True))
            s = mm_nt(q, k) * jnp.exp(dmat - wide(a, L))
            n_loc = mm(s, v)
            d_loc = rep(jnp.sum(s, axis=-1, keepdims=True))
            a_last = a[L - 1:L, :]
            b_last = b_c[L - 1:L, :]
            wk = jnp.exp(b_last - b_c + ig_c - a_last)[:, :DH_A]
            u_loc = mm_tn(wk * v, k)
            nu_loc = jnp.sum(wk * k, axis=0, keepdims=True)

            inter = b_c + m_prev
            m_row = jnp.maximum(inter, a)
            r = jnp.exp(a - m_row)
            w_inter = jnp.exp(inter - m_row)
            num = r[:, :DH_A] * n_loc + w_inter[:, :DH_A] * mm_nt(q, c)
            den = r * d_loc + w_inter * rep(jnp.sum(q * n, axis=-1, keepdims=True))
            hh = num / jnp.maximum(jnp.abs(den), jnp.exp(-m_row))[:, :DH_A]
            m_new = m_row[L - 1:L, 0:1]
            decay = jnp.exp(b_last[:, 0:1] + m_prev - m_new)
            e_loc = jnp.exp(a_last[:, 0:1] - m_new)
            new_state.append((decay * c + e_loc * u_loc, decay * n + e_loc * nu_loc, m_new))

            hn = hh * lax.rsqrt(jnp.mean(hh * hh, axis=-1, keepdims=True) + NORM_EPS)
            outs.append(jax.nn.sigmoid(og) * hn)
        ya_ref[rows_u, :] = (jnp.concatenate(outs, axis=-1) * norm_g).astype(ya_ref.dtype)
        state = new_state
        if not sequential:
            m_out = jnp.zeros((1, V7X_LANES), F32)
            for h in range(H_A):
                c1_ref[u, h] = state[h][0]
                n1_ref[u, h:h + 1, :] = state[h][1]
                m_out = jnp.where(lane_m == h, state[h][2], m_out)
            m1_ref[u] = m_out

    if sequential:
        m_out = jnp.zeros((1, V7X_LANES), F32)
        for h in range(H_A):
            c_sc[h] = state[h][0]
            n_sc[h:h + 1, :] = state[h][1]
            m_out = jnp.where(lane_m == h, state[h][2], m_out)
        m_sc[...] = m_out

        @pl.when(ti == pl.num_programs(1) - 1)
        def _():
            c1_ref[0] = c_sc[...]
            n1_ref[0] = n_sc[...]
            m1_ref[0] = m_sc[...]


def _mlstm_sample_kernel(za_ref, zg_ref, g_ref, c0_ref, n0_ref, m0_ref, _prev, ya_ref, c1_ref, n1_ref, m1_ref,
                         *, t_len, seqs):
    R = seqs * t_len
    row = lax.broadcasted_iota(jnp.int32, (R, R), 0)
    col = lax.broadcasted_iota(jnp.int32, (R, R), 1)
    mask = (row // t_len == col // t_len) & (col <= row)
    tril = mask.astype(F32)
    sel_r = lax.broadcasted_iota(jnp.int32, (V7X_SUBLANES, V7X_LANES), 0)
    sel_c = lax.broadcasted_iota(jnp.int32, (V7X_SUBLANES, V7X_LANES), 1)
    sel = (sel_r == sel_c).astype(F32)
    lane_g = lax.broadcasted_iota(jnp.int32, (R, N_ZG), 1)
    lane_m = lax.broadcasted_iota(jnp.int32, (seqs, 1, V7X_LANES), 2)
    rep = lambda c_: jnp.broadcast_to(c_, (R, V7X_LANES))
    wide = lambda x, n: _tile_lanes(x, n // V7X_LANES) if n > V7X_LANES else x[:, :n]
    per_seq = lambda x: x.reshape(seqs, t_len, x.shape[-1])
    last_rows = lambda x: jnp.broadcast_to(per_seq(x)[:, t_len - 1:t_len, :], (seqs, t_len, x.shape[-1])
                                           ).reshape(R, x.shape[-1])
    seq_rows = lambda x: jnp.broadcast_to(x, (seqs, t_len, x.shape[-1])).reshape(R, x.shape[-1])
    state_rows = lambda x: jnp.broadcast_to(x, (seqs, DH_A, x.shape[-1])).reshape(seqs * DH_A, x.shape[-1])
    own_r = lax.broadcasted_iota(jnp.int32, (R, seqs * DH_A), 0) // t_len
    own_c = lax.broadcasted_iota(jnp.int32, (R, seqs * DH_A), 1) // DH_A
    own = own_r == own_c
    odd_seq = (lax.broadcasted_iota(jnp.int32, (R, DH_A), 0) // t_len) % 2 == 1

    gates = zg_ref[...]
    gl = jnp.where(lane_g < H_A, gates, _log_sigmoid(gates))
    bcum = jnp.dot(tril, gl, precision=lax.Precision.HIGHEST, preferred_element_type=F32)
    mixed = jnp.where(lane_g < H_A, gates, bcum)
    t_rows = lax.dot_general(sel, mixed, (((1,), (1,)), ((), ())),
                             precision=lax.Precision.HIGHEST, preferred_element_type=F32)
    m0 = m0_ref[...]
    norm_g = g_ref[...]
    outs = []
    m_out = jnp.zeros((seqs, 1, V7X_LANES), F32)
    for h in range(H_A):
        q = za_ref[:, h * DH_A:(h + 1) * DH_A]
        k = za_ref[:, W_A + h * DH_A:W_A + (h + 1) * DH_A] * (DH_A ** -0.5)
        v = za_ref[:, 2 * W_A + h * DH_A:2 * W_A + (h + 1) * DH_A]
        og = za_ref[:, 3 * W_A + h * DH_A:3 * W_A + (h + 1) * DH_A]
        ig_c = rep(gates[:, h:h + 1])
        b_c = rep(bcum[:, H_A + h:H_A + h + 1])
        ig_row = t_rows[h:h + 1, :]
        b_row = t_rows[H_A + h:H_A + h + 1, :]
        m_prev = rep(seq_rows(m0)[:, h:h + 1])
        c_stack = c0_ref[:, h].reshape(seqs * DH_A, DH_A)
        n_rows = seq_rows(n0_ref[:, h:h + 1, :])

        dmat = jnp.where(mask, wide(b_c, R) - b_row + ig_row, NEG_INF)
        a = rep(jnp.max(dmat, axis=-1, keepdims=True))
        s = _bdot_nt(q, k) * jnp.exp(dmat - wide(a, R))
        n_loc = _bdot(s, v)
        d_loc = rep(jnp.sum(s, axis=-1, keepdims=True))
        a_last = last_rows(a)
        b_last = last_rows(b_c)
        wk = jnp.exp(b_last - b_c + ig_c - a_last)[:, :DH_A]
        vw = wk * v
        vw2 = jnp.concatenate([vw, vw], axis=1)
        u_stack = lax.dot_general(jnp.where(own, _tile_lanes(vw2, seqs // 2), 0.0).astype(BF16), k.astype(BF16),
                                  (((0,), (0,)), ((), ())), preferred_element_type=F32)
        nu = jnp.sum(per_seq(wk * k), axis=1, keepdims=True)

        inter = b_c + m_prev
        m_row = jnp.maximum(inter, a)
        r = jnp.exp(a - m_row)
        w_inter = jnp.exp(inter - m_row)
        x_all = jnp.where(own, _bdot_nt(q, c_stack), 0.0)
        fold = x_all[:, 0:V7X_LANES]
        for j in range(1, seqs * DH_A // V7X_LANES):
            fold = fold + x_all[:, j * V7X_LANES:(j + 1) * V7X_LANES]
        qc = jnp.where(odd_seq, fold[:, DH_A:], fold[:, :DH_A])
        num = r[:, :DH_A] * n_loc + w_inter[:, :DH_A] * qc
        den = r * d_loc + w_inter * rep(jnp.sum(q * n_rows, axis=-1, keepdims=True))
        hh = num / jnp.maximum(jnp.abs(den), jnp.exp(-m_row))[:, :DH_A]
        hn = hh * lax.rsqrt(jnp.mean(hh * hh, axis=-1, keepdims=True) + NORM_EPS)
        outs.append(jax.nn.sigmoid(og) * hn)

        m_new = per_seq(m_row)[:, t_len - 1:t_len, :]
        decay = jnp.exp(per_seq(b_c + m_prev)[:, t_len - 1:t_len, :] - m_new)
        e_loc = jnp.exp(per_seq(a)[:, t_len - 1:t_len, :] - m_new)
        c_new = (state_rows(decay)[:, :DH_A] * c_stack + state_rows(e_loc)[:, :DH_A] * u_stack)
        c1_ref[:, h] = c_new.reshape(seqs, DH_A, DH_A)
        n1_ref[:, h:h + 1, :] = decay[:, :, :DH_A] * n0_ref[:, h:h + 1, :] + e_loc[:, :, :DH_A] * nu
        m_out = jnp.where(lane_m == h, m_new, m_out)
    ya_ref[...] = (jnp.concatenate(outs, axis=-1) * norm_g).astype(ya_ref.dtype)
    m1_ref[...] = m_out


def mlstm(za, zg, norm_g, nb, t, row0, init=None, layer=0, prev=None):
    n_tok = za.shape[0]
    L = MLSTM_CHUNK if t % MLSTM_CHUNK == 0 else t
    units = MLSTM_UNITS
    if init is None:
        L, units = MLSTM_PROMPT_CHUNK, 1
    nc = t // L
    rows = units * L
    sequential = init is None
    if sequential:
        grid = (nb, nc // units)
        seq_blk = 1
        row_map = lambda b, c: (row0 // rows + b * (nc // units) + c, 0)
    else:
        assert nc == 1
        grid = (nb // units, 1)
        seq_blk = units
        row_map = lambda b, c: (row0 // rows + b, 0)
    st4 = lambda b, c: (b, 0, 0, 0)
    st3 = lambda b, c: (b, 0, 0)
    in_specs = [
        pl.BlockSpec((rows, N_ZA), row_map),
        pl.BlockSpec((rows, N_ZG), row_map),
        pl.BlockSpec((1, W_A), lambda b, c: (0, 0)),
    ]
    args = [za, zg, norm_g.reshape(1, W_A)]
    aliases = {}
    scratch = []
    if sequential:
        scratch = [pltpu.VMEM((H_A, DH_A, DH_A), F32), pltpu.VMEM((H_A, DH_A), F32),
                   pltpu.VMEM((1, V7X_LANES), F32)]
    else:
        c0, n0, m0 = init
        m0p = jnp.pad(m0, ((0, 0), (0, V7X_LANES - H_A))).reshape(nb, 1, V7X_LANES)
        in_specs += [
            pl.BlockSpec((None, seq_blk, H_A, DH_A, DH_A), lambda b, c: (layer, b, 0, 0, 0)),
            pl.BlockSpec((seq_blk, H_A, DH_A), st3),
            pl.BlockSpec((seq_blk, 1, V7X_LANES), st3),
            pl.BlockSpec(memory_space=pl.ANY),
        ]
        args += [c0, n0, m0p, prev]
        aliases = {6: 0}
    out_specs = [
        pl.BlockSpec((rows, W_A), row_map),
        pl.BlockSpec((seq_blk, H_A, DH_A, DH_A), st4),
        pl.BlockSpec((seq_blk, H_A, DH_A), st3),
        pl.BlockSpec((seq_blk, 1, V7X_LANES), st3),
    ]
    out_shape = [
        jax.ShapeDtypeStruct((n_tok, W_A), BF16),
        jax.ShapeDtypeStruct((nb, H_A, DH_A, DH_A), F32),
        jax.ShapeDtypeStruct((nb, H_A, DH_A), F32),
        jax.ShapeDtypeStruct((nb, 1, V7X_LANES), F32),
    ]
    ya, c1, n1, m1 = pl.pallas_call(
        (functools.partial(_mlstm_kernel, chunk=L, units=units, sequential=True) if sequential
         else functools.partial(_mlstm_sample_kernel, t_len=L, seqs=units)),
        grid=grid,
        in_specs=in_specs,
        out_specs=out_specs,
        out_shape=out_shape,
        scratch_shapes=scratch,
        input_output_aliases=aliases,
        compiler_params=_cparams("parallel", "arbitrary"),
        name="mlstm_prompt" if sequential else "mlstm_sample",
    )(*args)
    return ya, c1, n1, m1[:, 0, :H_A]


def _diff_lambda(lam_ref, lam_init):
    lp = lam_ref[...]
    d01 = jnp.sum(lp[0:1, :] * lp[1:2, :], axis=-1, keepdims=True)
    d23 = jnp.sum(lp[2:3, :] * lp[3:4, :], axis=-1, keepdims=True)
    return jnp.exp(d01) - jnp.exp(d23) + lam_init


def _alibi_slope(h):
    return jnp.where(h == 0, 2.0 ** -2, jnp.where(h == 1, 2.0 ** -4, jnp.where(h == 2, 2.0 ** -6, 2.0 ** -8)))


def _alibi_tables(t):
    slopes = (2.0 ** (-8.0 * jnp.arange(1, H_B + 1, dtype=F32) / H_B))[:, None]
    pos = jnp.arange(t, dtype=jnp.int32)
    hi = ((pos // ALIBI_SPLIT) * ALIBI_SPLIT).astype(F32)[None, :]
    lo = (pos % ALIBI_SPLIT).astype(F32)[None, :]
    ones = jnp.ones((H_B, t), F32)
    pad = jnp.zeros((H_B, t, DV_B - 4), F32)
    aq = jnp.concatenate([jnp.stack([slopes * ones, slopes * ones, -slopes * hi, -slopes * lo], axis=-1), pad], -1)
    ak = jnp.concatenate([jnp.stack([hi * ones, lo * ones, ones, ones], axis=-1), pad], -1)
    return aq.astype(BF16), ak.astype(BF16)


def _attn_prompt_kernel(lam_ref, g_ref, q_ref, aq_ref, k_ref, ak_ref, v_ref, o_ref,
                        qs_sc, m_sc, acc_sc, *, lam_init, tq, tk):
    i = pl.program_id(1)
    j = pl.program_id(2)
    rq = ATTN_ROWS

    @pl.when(j == 0)
    def _():
        lane = lax.broadcasted_iota(jnp.int32, (tq, DV_B), 1)
        for h in range(H_B):
            q = q_ref[:, h * DV_B:(h + 1) * DV_B] * (DK_B ** -0.5)
            qs_sc[h, 0:tq, 0:DV_B] = jnp.where(lane < DK_B, q, 0.0).astype(BF16)
            qs_sc[h, tq:2 * tq, 0:DV_B] = jnp.where(lane >= DK_B, q, 0.0).astype(BF16)
            qs_sc[h, 0:tq, DV_B:2 * DV_B] = aq_ref[h]
            qs_sc[h, tq:2 * tq, DV_B:2 * DV_B] = aq_ref[h]
        m_sc[...] = jnp.full_like(m_sc, NEG_INF)
        acc_sc[...] = jnp.zeros_like(acc_sc)

    def step(masked):
        ones = jnp.ones((tk, V7X_LANES), BF16)
        for h in range(H_B):
            kaug = jnp.concatenate([k_ref[:, h * DV_B:(h + 1) * DV_B].astype(BF16), ak_ref[h]], axis=1)
            vaug = jnp.concatenate([v_ref[:, h * DV_B:(h + 1) * DV_B].astype(BF16), ones], axis=1)
            for r in range(2 * tq // rq):
                rows = slice(r * rq, (r + 1) * rq)
                s = lax.dot_general(qs_sc[h, rows, :], kaug, (((1,), (1,)), ((), ())), preferred_element_type=F32)
                if masked:
                    qi = (r * rq) % tq + lax.broadcasted_iota(jnp.int32, (rq, tk), 0)
                    kj = lax.broadcasted_iota(jnp.int32, (rq, tk), 1)
                    s = jnp.where(kj <= qi, s, NEG_INF)
                m_old = m_sc[h, rows, :]
                m_new = jnp.maximum(m_old, jnp.max(s, axis=-1, keepdims=True))
                alpha = jnp.exp(m_old - m_new)
                p = jnp.exp(s - _tile_lanes(m_new, tk // V7X_LANES))
                acc_sc[h, rows, :] = (_tile_lanes(alpha, 2) * acc_sc[h, rows, :]
                                      + jnp.dot(p.astype(BF16), vaug, preferred_element_type=F32))
                m_sc[h, rows, :] = m_new

    @pl.when(j < i)
    def _():
        step(False)

    @pl.when(j == i)
    def _():
        step(True)

    @pl.when(j == pl.num_programs(2) - 1)
    def _():
        lam = _diff_lambda(lam_ref, lam_init)
        for h in range(H_B):
            o0 = acc_sc[h, 0:tq, 0:DV_B] / acc_sc[h, 0:tq, DV_B:2 * DV_B]
            o1 = acc_sc[h, tq:2 * tq, 0:DV_B] / acc_sc[h, tq:2 * tq, DV_B:2 * DV_B]
            ob = o0 - lam * o1
            on = ob * lax.rsqrt(jnp.mean(ob * ob, axis=-1, keepdims=True) + NORM_EPS)
            o_ref[:, h * DV_B:(h + 1) * DV_B] = on * g_ref[:, h * DV_B:(h + 1) * DV_B] * (1.0 - lam_init)


def diff_attn_prompt(zb, aq, ak, lam_p, norm_g, nb, t, lam_init):
    n_tok = zb.shape[0]
    tq = tk = ATTN_TILE
    nq = t // tq
    kv_map = lambda off: (lambda b, i, j: (b * nq + jnp.minimum(i, j), off))
    return pl.pallas_call(
        functools.partial(_attn_prompt_kernel, lam_init=lam_init, tq=tq, tk=tk),
        grid=(nb, nq, nq),
        in_specs=[
            pl.BlockSpec(lam_p.shape, lambda b, i, j: (0, 0)),
            pl.BlockSpec((1, W_B), lambda b, i, j: (0, 0)),
            pl.BlockSpec((tq, W_B), lambda b, i, j: (b * nq + i, 0)),
            pl.BlockSpec((H_B, tq, DV_B), lambda b, i, j: (0, i, 0)),
            pl.BlockSpec((tk, W_B), kv_map(1)),
            pl.BlockSpec((H_B, tk, DV_B), lambda b, i, j: (0, jnp.minimum(i, j), 0)),
            pl.BlockSpec((tk, W_B), kv_map(2)),
        ],
        out_specs=pl.BlockSpec((tq, W_B), lambda b, i, j: (b * nq + i, 0)),
        out_shape=jax.ShapeDtypeStruct((n_tok, W_B), F32),
        scratch_shapes=[pltpu.VMEM((H_B, 2 * tq, 2 * DV_B), BF16), pltpu.VMEM((H_B, 2 * tq, V7X_LANES), F32),
                        pltpu.VMEM((H_B, 2 * tq, 2 * DV_B), F32)],
        compiler_params=_cparams("parallel", "parallel", "arbitrary"),
        name="diff_attn_prompt",
    )(lam_p, norm_g.reshape(1, W_B), zb, aq, zb, ak, zb)


def _attn_sample_kernel(*refs, lam_init, past_len, t_new, pages_per_step):
    G = pages_per_step
    pt_ref, lam_ref, g_ref, q_ref, kn_ref, vn_ref = refs[:6]
    k_refs = refs[6:6 + G]
    v_refs = refs[6 + G:6 + 2 * G]
    _prev, o_ref, qs_sc, bias_sc, m_sc, l_sc, acc_sc = refs[6 + 2 * G:]
    del pt_ref
    ps = pl.program_id(1)
    rows_per_head = 2 * t_new
    n_rows = H_B * rows_per_head
    page_rows = PAGE_SIZE * H_B
    reps = page_rows // V7X_LANES

    r_lane = lax.broadcasted_iota(jnp.int32, (n_rows, V7X_LANES), 0)
    slope = _alibi_slope(r_lane // rows_per_head).astype(F32)

    @pl.when(ps == 0)
    def _():
        q = q_ref[...] * (DK_B ** -0.5)
        lane = lax.broadcasted_iota(jnp.int32, (t_new, DV_B), 1)
        for h in range(H_B):
            qh = q[:, h * DV_B:(h + 1) * DV_B]
            qs_sc[h * rows_per_head:(h + 1) * rows_per_head, :] = jnp.concatenate(
                [jnp.where(lane < DK_B, qh, 0.0), jnp.where(lane >= DK_B, qh, 0.0)], axis=0).astype(BF16)
        rr = lax.broadcasted_iota(jnp.int32, (n_rows, page_rows), 0)
        cc = lax.broadcasted_iota(jnp.int32, (n_rows, page_rows), 1)
        rel = cc // H_B - (past_len + rr % t_new)
        bias = _alibi_slope(rr // rows_per_head).astype(F32) * rel.astype(F32)
        bias_sc[...] = jnp.where(cc % H_B == rr // rows_per_head, bias, NEG_INF)
        m_sc[...] = jnp.full_like(m_sc, NEG_INF)
        l_sc[...] = jnp.zeros_like(l_sc)
        acc_sc[...] = jnp.zeros_like(acc_sc)

    qs = qs_sc[...]
    m, l, acc = m_sc[...], l_sc[...], acc_sc[...]
    s_pages = []
    m_new = m
    for g in range(G):
        base = ((ps * G + g) * PAGE_SIZE).astype(F32)
        s = _bdot_nt(qs, k_refs[g][...]) + (bias_sc[...] + _tile_lanes(slope * base, reps))
        m_new = jnp.maximum(m_new, jnp.max(s, axis=-1, keepdims=True))
        s_pages.append(s)
    alpha = jnp.exp(m - m_new)
    l = alpha * l
    acc = alpha * acc
    for g in range(G):
        p = jnp.exp(s_pages[g] - _tile_lanes(m_new, reps))
        l = l + jnp.sum(p, axis=-1, keepdims=True)
        acc = acc + _bdot(p, v_refs[g][...])
    m = m_new

    @pl.when(ps < pl.num_programs(1) - 1)
    def _():
        m_sc[...] = m
        l_sc[...] = l
        acc_sc[...] = acc

    @pl.when(ps == pl.num_programs(1) - 1)
    def _():
        rnd = lambda a: a.astype(BF16).astype(F32)
        kn = rnd(kn_ref[...])
        vn = rnd(vn_ref[...])
        qf = qs.astype(F32)
        sn = jnp.concatenate(
            [lax.dot_general(qf[h * rows_per_head:(h + 1) * rows_per_head, :], kn[:, h * DV_B:(h + 1) * DV_B],
                             (((1,), (1,)), ((), ())), preferred_element_type=F32) for h in range(H_B)],
            axis=0)
        rr = lax.broadcasted_iota(jnp.int32, (n_rows, t_new), 0)
        rel = lax.broadcasted_iota(jnp.int32, (n_rows, t_new), 1) - rr % t_new
        sn = jnp.where(rel <= 0, sn + _alibi_slope(rr // rows_per_head).astype(F32) * rel.astype(F32), NEG_INF)
        m_fin = jnp.maximum(m, jnp.max(sn, axis=-1, keepdims=True))
        alpha = jnp.exp(m - m_fin)
        pn = rnd(jnp.exp(sn - m_fin[:, 0:1]))
        l_fin = alpha * l + jnp.sum(pn, axis=-1, keepdims=True)
        pv = jnp.concatenate(
            [jnp.dot(pn[h * rows_per_head:(h + 1) * rows_per_head, :], vn[:, h * DV_B:(h + 1) * DV_B],
                     preferred_element_type=F32) for h in range(H_B)], axis=0)
        o = (alpha * acc + pv) / l_fin

        lam = _diff_lambda(lam_ref, lam_init)
        outs = []
        for h in range(H_B):
            o0 = o[h * rows_per_head:h * rows_per_head + t_new, :]
            o1 = o[h * rows_per_head + t_new:(h + 1) * rows_per_head, :]
            ob = o0 - lam * o1
            outs.append(ob * lax.rsqrt(jnp.mean(ob * ob, axis=-1, keepdims=True) + NORM_EPS))
        o_ref[...] = jnp.concatenate(outs, axis=-1) * g_ref[...] * (1.0 - lam_init)


def diff_attn_sample(zb, cache_k, cache_v, layer, page_table, lam_p, norm_g, nb, t_new, row0, lam_init, prev):
    n_tok = zb.shape[0]
    n_pages = page_table.shape[1]
    G = PAGES_PER_STEP
    rb0 = row0 // t_new
    n_rows = 2 * H_B * t_new
    page_rows = PAGE_SIZE * H_B
    ck = cache_k.reshape(cache_k.shape[0], cache_k.shape[1], page_rows, DV_B)
    cv = cache_v.reshape(cache_v.shape[0], cache_v.shape[1], page_rows, DV_B)

    def page_map(g):
        return lambda b, p, pt: (layer, pt[b, p * G + g], 0, 0)

    page_blk = (None, None, page_rows, DV_B)
    in_specs = [
        pl.BlockSpec(lam_p.shape, lambda b, p, pt: (0, 0)),
        pl.BlockSpec((1, W_B), lambda b, p, pt: (0, 0)),
        pl.BlockSpec((t_new, W_B), lambda b, p, pt: (rb0 + b, 0)),
        pl.BlockSpec((t_new, W_B), lambda b, p, pt: (rb0 + b, 1)),
        pl.BlockSpec((t_new, W_B), lambda b, p, pt: (rb0 + b, 2)),
    ]
    in_specs += [pl.BlockSpec(page_blk, page_map(g)) for g in range(G)]
    in_specs += [pl.BlockSpec(page_blk, page_map(g)) for g in range(G)]
    in_specs += [pl.BlockSpec(memory_space=pl.ANY)]
    args = [page_table, lam_p, norm_g.reshape(1, W_B), zb, zb, zb] + [ck] * G + [cv] * G + [prev]
    grid_spec = pltpu.PrefetchScalarGridSpec(
        num_scalar_prefetch=1,
        grid=(nb, n_pages // G),
        in_specs=in_specs,
        out_specs=pl.BlockSpec((t_new, W_B), lambda b, p, pt: (rb0 + b, 0)),
        scratch_shapes=[pltpu.VMEM((n_rows, DV_B), BF16), pltpu.VMEM((n_rows, page_rows), F32),
                        pltpu.VMEM((n_rows, V7X_LANES), F32), pltpu.VMEM((n_rows, V7X_LANES), F32),
                        pltpu.VMEM((n_rows, DV_B), F32)],
    )
    return pl.pallas_call(
        functools.partial(_attn_sample_kernel, lam_init=lam_init, past_len=n_pages * PAGE_SIZE,
                          t_new=t_new, pages_per_step=G),
        grid_spec=grid_spec,
        out_shape=jax.ShapeDtypeStruct((n_tok, W_B), F32),
        input_output_aliases={len(args) - 1: 0},
        compiler_params=_cparams("parallel", "arbitrary"),
        name="diff_attn_sample",
    )(*args)


def _s5_disc_kernel(are_ref, aim_ref, ldt_ref, bre_ref, bim_ref, lre_ref, lim_ref, bbre_ref, bbim_ref):
    a_re = are_ref[...]
    a_im = aim_ref[...]
    dt = jnp.exp(ldt_ref[...])
    mag = jnp.exp(a_re * dt)
    lb_re = mag * jnp.cos(a_im * dt)
    lb_im = mag * jnp.sin(a_im * dt)
    den = a_re * a_re + a_im * a_im
    xr = lb_re - 1.0
    fr = (xr * a_re + lb_im * a_im) / den
    fi = (lb_im * a_re - xr * a_im) / den
    lre_ref[...] = lb_re
    lim_ref[...] = lb_im
    b_re = bre_ref[...]
    b_im = bim_ref[...]
    bbre_ref[...] = fr * b_re - fi * b_im
    bbim_ref[...] = fr * b_im + fi * b_re


def s5_discretize(a_re, a_im, log_dt, b_re, b_im):
    depth = a_re.shape[0]
    gp = G_C * P_C
    flat = lambda a: a.reshape(depth, 1, gp)
    ldt = jnp.broadcast_to(log_dt[:, :, None], (depth, G_C, P_C)).reshape(depth, 1, gp)
    tr = lambda b: jnp.transpose(b, (0, 3, 1, 2)).reshape(depth, GC, gp)
    shp1 = jax.ShapeDtypeStruct((depth, 1, gp), F32)
    shpb = jax.ShapeDtypeStruct((depth, GC, gp), F32)
    return pl.pallas_call(_s5_disc_kernel, out_shape=[shp1, shp1, shpb, shpb], name="s5_discretize")(
        flat(a_re), flat(a_im), ldt, tr(b_re), tr(b_im))


def _block_diag_in(bb):
    depth, _, gp = bb.shape
    tiled = jnp.tile(bb, (1, G_C, 1)).reshape(depth, G_C, GC, gp)
    grp_r = jnp.arange(G_C)[:, None, None]
    grp_c = (jnp.arange(gp) // P_C)[None, None, :]
    return jnp.where(grp_r == grp_c, tiled, 0.0).reshape(depth, G_C * GC, gp)


def _block_diag_out(c):
    depth = c.shape[0]
    ct = jnp.transpose(c, (0, 1, 3, 2)).reshape(depth, G_C * P_C, GC)
    tiled = jnp.tile(ct, (1, 1, G_C))
    grp_r = (jnp.arange(G_C * P_C) // P_C)[:, None]
    grp_c = (jnp.arange(G_C * GC) // GC)[None, :]
    return jnp.where(grp_r == grp_c, tiled, 0.0)


def _cmul_add(a_re, a_im, x_re, x_im, y_re, y_im):
    return y_re + (a_re * x_re - a_im * x_im), y_im + (a_re * x_im + a_im * x_re)


def _block_scan(x_re, x_im, pw, row8):
    for d, (a_re, a_im) in zip((1, 2, 4), pw):
        sh_re = jnp.where(row8 >= d, pltpu.roll(x_re, d, 0), 0.0)
        sh_im = jnp.where(row8 >= d, pltpu.roll(x_im, d, 0), 0.0)
        x_re, x_im = _cmul_add(a_re, a_im, sh_re, sh_im, x_re, x_im)
    return x_re, x_im


def _s5_kernel(*refs, rows, independent, aliased):
    (u_ref, lre_ref, lim_ref, bb_ref, cc_ref, d_ref, gw_ref, gb_ref, s0re_ref, s0im_ref) = refs[:10]
    rest = refs[10:]
    if aliased:
        rest = rest[1:]
    y_ref, s1re_ref, s1im_ref, st_sc, car_sc = rest
    gp = G_C * P_C
    nblk = rows // V7X_SUBLANES
    ti = pl.program_id(1)

    lam_re = lre_ref[...]
    lam_im = lim_ref[...]
    l2_re, l2_im = lam_re * lam_re - lam_im * lam_im, 2.0 * lam_re * lam_im
    l4_re, l4_im = l2_re * l2_re - l2_im * l2_im, 2.0 * l2_re * l2_im
    pw = ((lam_re, lam_im), (l2_re, l2_im), (l4_re, l4_im))
    row8 = lax.broadcasted_iota(jnp.int32, (V7X_SUBLANES, gp), 0)
    pk_re, pk_im = _block_scan(jnp.where(row8 == 0, lam_re, 0.0), jnp.where(row8 == 0, lam_im, 0.0), pw, row8)

    u = u_ref[...]
    st_sc[...] = _bdot(u, bb_ref[...])

    if not independent:
        @pl.when(ti == 0)
        def _():
            car_sc[0:1, :] = s0re_ref[...]
            car_sc[1:2, :] = s0im_ref[...]

    def body(bi, carry):
        r0 = pl.multiple_of(bi * V7X_SUBLANES, V7X_SUBLANES)
        x_re = st_sc[pl.ds(r0, V7X_SUBLANES), 0:gp]
        x_im = st_sc[pl.ds(r0, V7X_SUBLANES), gp:2 * gp]
        x_re, x_im = _block_scan(x_re, x_im, pw, row8)
        if independent:
            c_re = s0re_ref[pl.ds(bi, 1), :]
            c_im = s0im_ref[pl.ds(bi, 1), :]
        else:
            c_re, c_im = carry
        s_re, s_im = _cmul_add(pk_re, pk_im, c_re, c_im, x_re, x_im)
        st_sc[pl.ds(r0, V7X_SUBLANES), 0:gp] = s_re
        st_sc[pl.ds(r0, V7X_SUBLANES), gp:2 * gp] = s_im
        last_re = s_re[V7X_SUBLANES - 1:V7X_SUBLANES, :]
        last_im = s_im[V7X_SUBLANES - 1:V7X_SUBLANES, :]
        if independent:
            s1re_ref[pl.ds(bi, 1), :] = last_re
            s1im_ref[pl.ds(bi, 1), :] = last_im
            return carry
        return last_re, last_im

    if independent:
        lax.fori_loop(0, nblk, body, 0)
    else:
        c_re, c_im = lax.fori_loop(0, nblk, body, (car_sc[0:1, :], car_sc[1:2, :]))
        car_sc[0:1, :] = c_re
        car_sc[1:2, :] = c_im

        @pl.when(ti == pl.num_programs(1) - 1)
        def _():
            s1re_ref[...] = c_re
            s1im_ref[...] = c_im

    y = _bdot(st_sc[...], cc_ref[...]) + d_ref[...] * u
    z = _bdot(jax.nn.gelu(y), gw_ref[...]) + gb_ref[...]
    y_ref[...] = (z[:, :W_C] * jax.nn.sigmoid(z[:, W_C:])).astype(y_ref.dtype)


def s5_mixer(zc, lam_re, lam_im, bb, cc, layer, d, glu_w, glu_b, s0_re, s0_im, nb, t, row0, prev=None):
    n_tok = zc.shape[0]
    gp = G_C * P_C
    independent = t == V7X_SUBLANES
    aliased = prev is not None
    if independent:
        seqs = S5_SAMPLE_SEQS
        rows = seqs * t
        grid = (nb // seqs, 1)
        st_spec = pl.BlockSpec((seqs, gp), lambda b, i: (b, 0))
    else:
        rows = S5_TILE
        grid = (nb, t // rows)
        st_spec = pl.BlockSpec((1, gp), lambda b, i: (b, 0))
    nt = grid[1]
    rb0 = row0 // rows
    row_map = lambda b, i: (rb0 + b * nt + i, 0)
    const = lambda b, i: (0, 0)
    by_layer = lambda b, i: (layer, 0, 0)
    in_specs = [
        pl.BlockSpec((rows, W_C), row_map),
        pl.BlockSpec((None, 1, gp), by_layer),
        pl.BlockSpec((None, 1, gp), by_layer),
        pl.BlockSpec((None,) + bb.shape[1:], by_layer),
        pl.BlockSpec((None,) + cc.shape[1:], by_layer),
        pl.BlockSpec((1, W_C), const),
        pl.BlockSpec((None,) + glu_w.shape[1:], by_layer),
        pl.BlockSpec((1, 2 * W_C), const),
        st_spec,
        st_spec,
    ]
    args = [zc, lam_re, lam_im, bb, cc, d.reshape(1, W_C), glu_w, glu_b.reshape(1, 2 * W_C), s0_re, s0_im]
    aliases = {}
    if aliased:
        in_specs.append(pl.BlockSpec(memory_space=pl.ANY))
        args.append(prev)
        aliases = {len(args) - 1: 0}
    if independent:
        s1_shape = jax.ShapeDtypeStruct((nb, gp), F32)
    else:
        s1_shape = jax.ShapeDtypeStruct((nb, 1, gp), F32)
        st_out = pl.BlockSpec((None, 1, gp), lambda b, i: (b, 0, 0))
    out_specs = [pl.BlockSpec((rows, W_C), row_map)] + ([st_spec, st_spec] if independent else [st_out, st_out])
    if not independent:
        args[8] = s0_re.reshape(nb, 1, gp)
        args[9] = s0_im.reshape(nb, 1, gp)
        in_specs[8] = in_specs[9] = pl.BlockSpec((None, 1, gp), lambda b, i: (b, 0, 0))
    yc, s1_re, s1_im = pl.pallas_call(
        functools.partial(_s5_kernel, rows=rows, independent=independent, aliased=aliased),
        grid=grid,
        in_specs=in_specs,
        out_specs=out_specs,
        out_shape=[jax.ShapeDtypeStruct((n_tok, W_C), BF16), s1_shape, s1_shape],
        scratch_shapes=[pltpu.VMEM((rows, 2 * gp), F32), pltpu.VMEM((V7X_SUBLANES, gp), F32)],
        input_output_aliases=aliases,
        compiler_params=_cparams("parallel", "arbitrary"),
        name="s5_sample" if independent else "s5_prompt",
    )(*args)
    return yc, s1_re.reshape(nb, G_C, P_C), s1_im.reshape(nb, G_C, P_C)


def _xattn_kernel(q_ref, mk_ref, mv_ref, o_ref):
    q = q_ref[...] * (DH_X ** -0.5)
    outs = []
    for h in range(H_X):
        s = _bdot_nt(q[:, h * DH_X:(h + 1) * DH_X], mk_ref[h])
        m = jnp.max(s, axis=-1, keepdims=True)
        p = jnp.exp(s - m)
        l = jnp.sum(p, axis=-1, keepdims=True)
        outs.append(_bdot(p, mv_ref[h]) / l)
    o_ref[...] = jnp.concatenate(outs, axis=-1).astype(o_ref.dtype)


def _xattn_native_kernel(q_ref, mk_ref, mv_ref, _prev, o_ref, bias_sc, *, tq, seqs, n_mem):
    rows = seqs * tq
    n_keys = seqs * n_mem * H_X

    @pl.when(pl.program_id(0) == 0)
    def _():
        rr = lax.broadcasted_iota(jnp.int32, (H_X * rows, n_keys), 0)
        cc = lax.broadcasted_iota(jnp.int32, (H_X * rows, n_keys), 1)
        own = (cc // (n_mem * H_X) == (rr % rows) // tq) & (cc % H_X == rr // rows)
        bias_sc[...] = jnp.where(own, 0.0, NEG_INF)

    q = q_ref[...] * (DH_X ** -0.5)
    q_all = jnp.concatenate([q[:, h * DH_X:(h + 1) * DH_X] for h in range(H_X)], axis=0)
    k2 = mk_ref[...].reshape(n_keys, DH_X)
    v2 = mv_ref[...].reshape(n_keys, DH_X)
    s = _bdot_nt(q_all, k2) + bias_sc[...]
    m = jnp.max(s, axis=-1, keepdims=True)
    p = jnp.exp(s - m)
    l = jnp.sum(p, axis=-1, keepdims=True)
    o = _bdot(p, v2) / l
    o_ref[...] = jnp.concatenate([o[h * rows:(h + 1) * rows, :] for h in range(H_X)], axis=-1).astype(o_ref.dtype)


def cross_attn_native(qx, mem_k, mem_v, layer, nb, t, row0, prev):
    n_tok, d = qx.shape
    n_mem = mem_k.shape[2]
    seqs = XATTN_SAMPLE_SEQS
    rows = seqs * t
    rb0 = row0 // rows
    mem_spec = pl.BlockSpec((None, seqs, n_mem, H_X, DH_X), lambda b: (layer, b, 0, 0, 0))
    return pl.pallas_call(
        functools.partial(_xattn_native_kernel, tq=t, seqs=seqs, n_mem=n_mem),
        grid=(nb // seqs,),
        in_specs=[pl.BlockSpec((rows, d), lambda b: (rb0 + b, 0)), mem_spec, mem_spec,
                  pl.BlockSpec(memory_space=pl.ANY)],
        out_specs=pl.BlockSpec((rows, d), lambda b: (rb0 + b, 0)),
        out_shape=jax.ShapeDtypeStruct((n_tok, d), prev.dtype),
        scratch_shapes=[pltpu.VMEM((H_X * rows, seqs * n_mem * H_X), F32)],
        input_output_aliases={3: 0},
        compiler_params=_cparams("arbitrary"),
        name="cross_attn_sample",
    )(qx, mem_k, mem_v, prev)


def cross_attn(qx, mk, mv, layer, n_mem, nb, t, out_dtype):
    n_tok, d = qx.shape
    tq = XATTN_TILE
    nq = t // tq
    mem_spec = pl.BlockSpec((None, H_X, n_mem, DH_X), lambda b, i: (layer, 0, b, 0))
    return pl.pallas_call(
        _xattn_kernel,
        grid=(nb, nq),
        in_specs=[pl.BlockSpec((tq, d), lambda b, i: (b * nq + i, 0)), mem_spec, mem_spec],
        out_specs=pl.BlockSpec((tq, d), lambda b, i: (b * nq + i, 0)),
        out_shape=jax.ShapeDtypeStruct((n_tok, d), out_dtype),
        compiler_params=_cparams("parallel", "arbitrary"),
        name="cross_attn_prompt",
    )(qx, mk, mv)


def kernel(x_prompt, x_sample, mem_prompt, cache_k, cache_v, page_table, cache_mem_k, cache_mem_v,
           state_mlstm_c, state_mlstm_n, state_mlstm_m, state_ssm_re, state_ssm_im,
           ln_g, ln_b, ffn1_wg, ffn1_wu, ffn1_wd, ffn2_wg, ffn2_wu, ffn2_wd, w_in, b_in,
           mlstm_norm_g, diff_lam, diff_norm_g, ssm_a_re, ssm_a_im, ssm_log_dt, ssm_b_re, ssm_b_im,
           ssm_c_re, ssm_c_im, ssm_d, ssm_glu_w, ssm_glu_b, w_out, cross_wq, cross_wk, cross_wv, cross_wo):
    bp, tp, d = x_prompt.shape
    bs, ts, _ = x_sample.shape
    depth = ln_g.shape[0]
    n_mem = mem_prompt.shape[1]
    n_p = bp * tp
    gp = G_C * P_C
    alpha = (2.0 * depth) ** 0.25

    cast = lambda w: w.astype(BF16)
    ffn1_wg, ffn1_wu, ffn1_wd = cast(ffn1_wg), cast(ffn1_wu), cast(ffn1_wd)
    ffn2_wg, ffn2_wu, ffn2_wd = cast(ffn2_wg), cast(ffn2_wu), cast(ffn2_wd)
    w_out_b, wq_b, wo_b, glu_w_b = cast(w_out), cast(cross_wq), cast(cross_wo), cast(ssm_glu_w)
    w_in_p, b_in_p = _pack_w_in(w_in, b_in)
    lam_re, lam_im, bb_re, bb_im = s5_discretize(ssm_a_re, ssm_a_im, ssm_log_dt, ssm_b_re, ssm_b_im)
    bb = jnp.concatenate([_block_diag_in(bb_re), _block_diag_in(bb_im)], axis=-1).astype(BF16)
    cc = jnp.concatenate([_block_diag_out(ssm_c_re), -_block_diag_out(ssm_c_im)], axis=1).astype(BF16)
    alibi_q, alibi_k = _alibi_tables(tp)

    p_mk, p_mv, p_mkh, p_mvh = mem_kv(mem_prompt.reshape(bp * n_mem, d), cast(cross_wk), cast(cross_wv))
    p_mk = p_mk.reshape(depth, bp, n_mem, H_X, DH_X)
    p_mv = p_mv.reshape(depth, bp, n_mem, H_X, DH_X)

    x = jnp.concatenate([x_prompt.reshape(n_p, d), x_sample.reshape(bs * ts, d)], axis=0)
    zeros_s = jnp.zeros((bp, gp), F32)
    p_k = jnp.zeros((depth, n_p, H_B, DV_B), F32)
    p_v = jnp.zeros((depth, n_p, H_B, DV_B), F32)
    p_st, s_st = [], []
    for l in range(depth):
        lam_init = 0.8 - 0.6 * math.exp(-0.3 * l)
        x = ffn_ln(x, ffn1_wg, ffn1_wu, ffn1_wd, l, ln_g[l, 0], ln_b[l, 0], alpha)
        za, zb, zc, zg, p_k, p_v, s_k, s_v = proj_in(x, w_in_p, b_in_p, p_k, p_v, l, n_p)

        ya, pc, pn, pm = mlstm(za, zg, mlstm_norm_g[l], bp, tp, 0)
        ya, sc, sn, sm = mlstm(za, zg, mlstm_norm_g[l], bs, ts, n_p,
                               init=(state_mlstm_c, state_mlstm_n[l], state_mlstm_m[l]), layer=l, prev=ya)

        yb = diff_attn_prompt(zb, alibi_q, alibi_k, diff_lam[l], diff_norm_g[l], bp, tp, lam_init)
        yb = diff_attn_sample(zb, cache_k, cache_v, l, page_table, diff_lam[l], diff_norm_g[l],
                              bs, ts, n_p, lam_init, yb)

        s5_args = (lam_re, lam_im, bb, cc, l, ssm_d[l], glu_w_b, ssm_glu_b[l])
        yc, psr, psi = s5_mixer(zc, *s5_args, zeros_s, zeros_s, bp, tp, 0)
        yc, ssr, ssi = s5_mixer(zc, *s5_args, state_ssm_re[l].reshape(bs, gp), state_ssm_im[l].reshape(bs, gp),
                                bs, ts, n_p, prev=yc)

        x, qx = out_ln([ya, yb, yc], w_out_b, l, x, ln_g[l, 1], ln_b[l, 1], alpha, w_next=wq_b)
        o = cross_attn(qx, p_mkh, p_mvh, l, n_mem, bp, tp, BF16)
        o = cross_attn_native(qx, cache_mem_k, cache_mem_v, l, bs, ts, n_p, o)
        x = out_ln([o], wo_b, l, x, ln_g[l, 2], ln_b[l, 2], alpha)
        x = ffn_ln(x, ffn2_wg, ffn2_wu, ffn2_wd, l, ln_g[l, 3], ln_b[l, 3], alpha)

        p_st.append((pc, pn, pm, psr, psi))
        s_st.append((s_k.reshape(bs, ts, H_B, DV_B), s_v.reshape(bs, ts, H_B, DV_B), sc, sn, sm, ssr, ssi))

    p_c, p_n, p_m, p_sr, p_si = [jnp.stack(a) for a in zip(*p_st)]
    s_k, s_v, s_c, s_n, s_m, s_sr, s_si = [jnp.stack(a) for a in zip(*s_st)]
    p_k = p_k.reshape(depth, bp, tp, H_B, DV_B)
    p_v = p_v.reshape(depth, bp, tp, H_B, DV_B)
    yp = x[:n_p].reshape(bp, tp, d)
    ys = x[n_p:].reshape(bs, ts, d)
    return (yp, ys, p_k, p_v, p_mk, p_mv, p_c, p_n, p_m, p_sr, p_si, s_k, s_v, s_c, s_n, s_m, s_sr, s_si)
```

```python
import functools
import math

import jax
import jax.numpy as jnp
from jax import lax
from jax.experimental import pallas as pl
from jax.experimental.pallas import tpu as pltpu

F32 = jnp.float32
BF16 = jnp.bfloat16

D_MODEL = 1024
PAGE_SIZE = 128
W_A = D_MODEL // 4
W_B = D_MODEL // 2
W_C = D_MODEL - W_A - W_B
H_A = 4
DH_A = W_A // H_A
MLSTM_CHUNK = 64
H_B = 4
DV_B = W_B // H_B
DK_B = DV_B // 2
GC = 16
G_C = W_C // GC
P_C = 64
H_X = 4
DH_X = D_MODEL // H_X
LN_EPS = 1e-5
NORM_EPS = 1e-6

OFF_AQ = 0
OFF_AK = OFF_AQ + W_A
OFF_AV = OFF_AK + W_A
OFF_AO = OFF_AV + W_A
OFF_AI = OFF_AO + W_A
OFF_AF = OFF_AI + H_A
OFF_BQ = OFF_AF + H_A
OFF_BK = OFF_BQ + H_B * 2 * DK_B
OFF_BV = OFF_BK + H_B * 2 * DK_B
OFF_CU = OFF_BV + W_B
N_IN = OFF_CU + W_C

V7X_LANES = 128
V7X_SUBLANES = 8
V7X_BF16_ROWS = 2 * V7X_SUBLANES
V7X_VMEM_LIMIT_BYTES = 56 * 1024 * 1024
V7X_VMEM_LIMIT_LARGE_BYTES = 60 * 1024 * 1024

TOKEN_TILE = 1024
FF_TILE = 256
PROJ_TILE = 512
ATTN_TILE = 512
ATTN_ROWS = 256
XATTN_TILE = 512
S5_TILE = 512
PAGES_PER_STEP = 16
XATTN_SAMPLE_SEQS = 4
S5_SAMPLE_SEQS = 16
MLSTM_PROMPT_CHUNK = 512
MLSTM_UNITS = 16
ALIBI_SPLIT = 64

NEG_INF = float("-inf")


def _cparams(*sem, vmem_limit=V7X_VMEM_LIMIT_BYTES):
    return pltpu.CompilerParams(dimension_semantics=sem, vmem_limit_bytes=vmem_limit)


def _bdot(a, b):
    return jnp.dot(a.astype(BF16), b.astype(BF16), preferred_element_type=F32)


def _bdot_nt(a, b):
    return lax.dot_general(a.astype(BF16), b.astype(BF16), (((1,), (1,)), ((), ())),
                           preferred_element_type=F32)


def _tile_lanes(x, n):
    return x if n == 1 else jnp.concatenate([x] * n, axis=1)


def _layer_norm(y, g, b):
    mu = jnp.mean(y, axis=-1, keepdims=True)
    yc = y - mu
    var = jnp.mean(yc * yc, axis=-1, keepdims=True)
    return yc * lax.rsqrt(var + LN_EPS) * g + b


def _swiglu(xb, wg_ref, wu_ref, wd_ref, h_sc, tf):
    for c in range(wg_ref.shape[1] // tf):
        cols = slice(c * tf, (c + 1) * tf)
        hg = jnp.dot(xb, wg_ref[:, cols], preferred_element_type=F32)
        hu = jnp.dot(xb, wu_ref[:, cols], preferred_element_type=F32)
        h_sc[:, cols] = ((hg * jax.nn.sigmoid(hg)) * hu).astype(BF16)
    return jnp.dot(h_sc[...], wd_ref[...], preferred_element_type=F32)


def _ffn_ln_kernel(*refs, alpha, tf, n_src, first_tiles):
    srcs = refs[:n_src]
    wg_ref, wu_ref, wd_ref, g_ref, b_ref, o_ref, h_sc = refs[n_src:]

    def run(x_ref):
        x = x_ref[...]
        ff = _swiglu(x.astype(BF16), wg_ref, wu_ref, wd_ref, h_sc, tf)
        o_ref[...] = _layer_norm(alpha * x + 0.5 * ff, g_ref[...], b_ref[...])

    if n_src == 1:
        run(srcs[0])
    else:
        i = pl.program_id(0)
        pl.when(i < first_tiles)(lambda: run(srcs[0]))
        pl.when(i >= first_tiles)(lambda: run(srcs[1]))


def ffn_ln(xs, wg, wu, wd, layer, g, b, alpha):
    d = xs[0].shape[1]
    dff = wg.shape[2]
    tm, tf = TOKEN_TILE, FF_TILE
    first = xs[0].shape[0] // tm
    n = sum(x.shape[0] for x in xs)
    src_specs = [pl.BlockSpec((tm, d), lambda i: (jnp.minimum(i, first - 1), 0))]
    if len(xs) == 2:
        src_specs.append(pl.BlockSpec((tm, d), lambda i: (jnp.maximum(i - first, 0), 0)))
    resident = pl.Buffered(1)
    return pl.pallas_call(
        functools.partial(_ffn_ln_kernel, alpha=alpha, tf=tf, n_src=len(xs), first_tiles=first),
        grid=(n // tm,),
        in_specs=src_specs + [
            pl.BlockSpec((None, d, dff), lambda i: (layer, 0, 0), pipeline_mode=resident),
            pl.BlockSpec((None, d, dff), lambda i: (layer, 0, 0), pipeline_mode=resident),
            pl.BlockSpec((None, dff, d), lambda i: (layer, 0, 0), pipeline_mode=resident),
            pl.BlockSpec((1, d), lambda i: (0, 0)),
            pl.BlockSpec((1, d), lambda i: (0, 0)),
        ],
        out_specs=pl.BlockSpec((tm, d), lambda i: (i, 0)),
        out_shape=jax.ShapeDtypeStruct((n, d), F32),
        scratch_shapes=[pltpu.VMEM((tm, dff), BF16)],
        compiler_params=_cparams("arbitrary"),
        name="ffn_ln",
    )(*xs, wg, wu, wd, g.reshape(1, d), b.reshape(1, d))


def _out_ffn_ln_kernel(o_ref, wo_ref, x_ref, g1_ref, b1_ref, wg_ref, wu_ref, wd_ref, g2_ref, b2_ref, *rest,
                       alpha, tf, split):
    outs, h_sc = rest[:-1], rest[-1]
    y1 = _layer_norm(alpha * x_ref[...] + jnp.dot(o_ref[...].astype(BF16), wo_ref[...], preferred_element_type=F32),
                     g1_ref[...], b1_ref[...])
    ff = _swiglu(y1.astype(BF16), wg_ref, wu_ref, wd_ref, h_sc, tf)
    y2 = _layer_norm(alpha * y1 + 0.5 * ff, g2_ref[...], b2_ref[...])
    if split is None:
        outs[0][...] = y2
    else:
        i = pl.program_id(0)

        @pl.when(i < split)
        def _():
            outs[0][...] = y2

        @pl.when(i >= split)
        def _():
            outs[1][...] = y2


def out_ffn_ln(o, wo, wg, wu, wd, layer, x, g1, b1, g2, b2, alpha, split_rows=None):
    n, d = x.shape
    dff = wg.shape[2]
    tm, tf = TOKEN_TILE, FF_TILE
    resident = pl.Buffered(1)
    by_layer = lambda i: (layer, 0, 0)
    vec = pl.BlockSpec((1, d), lambda i: (0, 0))
    if split_rows is None:
        split = None
        out_specs = [pl.BlockSpec((tm, d), lambda i: (i, 0))]
        out_shape = [jax.ShapeDtypeStruct((n, d), F32)]
    else:
        split = split_rows // tm
        out_specs = [pl.BlockSpec((tm, d), lambda i: (jnp.minimum(i, split - 1), 0)),
                     pl.BlockSpec((tm, d), lambda i: (jnp.maximum(i - split, 0), 0))]
        out_shape = [jax.ShapeDtypeStruct((split_rows, d), F32), jax.ShapeDtypeStruct((n - split_rows, d), F32)]
    res = pl.pallas_call(
        functools.partial(_out_ffn_ln_kernel, alpha=alpha, tf=tf, split=split),
        grid=(n // tm,),
        in_specs=[
            pl.BlockSpec((tm, d), lambda i: (i, 0)),
            pl.BlockSpec((None, d, d), by_layer, pipeline_mode=resident),
            pl.BlockSpec((tm, d), lambda i: (i, 0)),
            vec, vec,
            pl.BlockSpec((None, d, dff), by_layer, pipeline_mode=resident),
            pl.BlockSpec((None, d, dff), by_layer, pipeline_mode=resident),
            pl.BlockSpec((None, dff, d), by_layer, pipeline_mode=resident),
            vec, vec,
        ],
        out_specs=out_specs,
        out_shape=out_shape,
        scratch_shapes=[pltpu.VMEM((tm, dff), BF16)],
        compiler_params=_cparams("arbitrary", vmem_limit=V7X_VMEM_LIMIT_LARGE_BYTES),
        name="out_ffn_ln",
    )(o, wo, x, g1.reshape(1, d), b1.reshape(1, d), wg, wu, wd, g2.reshape(1, d), b2.reshape(1, d))
    return res[0] if split_rows is None else res


N_ZA = 4 * W_A
N_ZB = 3 * W_B
N_ZC = W_C
N_ZG = V7X_LANES
N_PROJ = N_ZA + N_ZB + N_ZC + N_ZG


def _proj_in_kernel(x_ref, w_ref, b_ref, _kp_prev, _vp_prev, za_ref, zb_ref, zc_ref, zg_ref,
                    kp_ref, vp_ref, ks_ref, vs_ref, *, prompt_tiles):
    i = pl.program_id(0)
    xb = x_ref[...].astype(BF16)
    off = 0
    for ref in (za_ref, zb_ref, zc_ref, zg_ref):
        width = ref.shape[1]
        ref[...] = jnp.dot(xb, w_ref[:, off:off + width], preferred_element_type=F32) + b_ref[:, off:off + width]
        off += width

    def write_kv(k_out, v_out):
        for h in range(H_B):
            k_out[:, h, :] = zb_ref[:, W_B + h * DV_B:W_B + (h + 1) * DV_B]
            v_out[:, h, :] = zb_ref[:, 2 * W_B + h * DV_B:2 * W_B + (h + 1) * DV_B]

    @pl.when(i < prompt_tiles)
    def _():
        write_kv(kp_ref, vp_ref)

    @pl.when(i >= prompt_tiles)
    def _():
        write_kv(ks_ref, vs_ref)


def proj_in(x, w, b, kp_prev, vp_prev, layer, n_prompt):
    n, d = x.shape
    tm = PROJ_TILE
    pt = n_prompt // tm
    widths = (N_ZA, N_ZB, N_ZC, N_ZG)
    kv_blk = (tm, H_B, DV_B)
    p_map = lambda i: (layer, jnp.minimum(i, pt - 1), 0, 0)
    s_map = lambda i: (jnp.maximum(i - pt, 0), 0, 0)
    kv_s = jax.ShapeDtypeStruct((n - n_prompt, H_B, DV_B), F32)
    return pl.pallas_call(
        functools.partial(_proj_in_kernel, prompt_tiles=pt),
        grid=(n // tm,),
        in_specs=[
            pl.BlockSpec((tm, d), lambda i: (i, 0)),
            pl.BlockSpec((None, d, N_PROJ), lambda i: (layer, 0, 0)),
            pl.BlockSpec((None, 1, N_PROJ), lambda i: (layer, 0, 0)),
            pl.BlockSpec(memory_space=pl.ANY),
            pl.BlockSpec(memory_space=pl.ANY),
        ],
        out_specs=[pl.BlockSpec((tm, wd), lambda i: (i, 0)) for wd in widths]
        + [pl.BlockSpec((None,) + kv_blk, p_map)] * 2 + [pl.BlockSpec(kv_blk, s_map)] * 2,
        out_shape=[jax.ShapeDtypeStruct((n, wd), F32) for wd in widths]
        + [jax.ShapeDtypeStruct(kp_prev.shape, F32)] * 2 + [kv_s, kv_s],
        input_output_aliases={3: 4, 4: 5},
        compiler_params=_cparams("arbitrary"),
        name="proj_in",
    )(x, w, b, kp_prev, vp_prev)


def _pack_w_in(w_in, b_in):
    def cols(a):
        pad = jnp.zeros(a.shape[:-1] + (N_ZG - 2 * H_A,), a.dtype)
        return jnp.concatenate([a[..., OFF_AQ:OFF_AI], a[..., OFF_BQ:OFF_CU], a[..., OFF_CU:N_IN],
                                a[..., OFF_AI:OFF_BQ], pad], axis=-1)
    return cols(w_in).astype(BF16), cols(b_in)[:, None, :]


def _mem_kv_kernel(x_ref, wk_ref, wv_ref, mk_ref, mv_ref, mkh_ref, mvh_ref):
    xb = x_ref[...].astype(BF16)
    for w_ref, o_ref, oh_ref in ((wk_ref, mk_ref, mkh_ref), (wv_ref, mv_ref, mvh_ref)):
        r = jnp.dot(xb, w_ref[...], preferred_element_type=F32)
        for h in range(H_X):
            o_ref[:, h, :] = r[:, h * DH_X:(h + 1) * DH_X]
            oh_ref[h] = r[:, h * DH_X:(h + 1) * DH_X].astype(BF16)


def mem_kv(x, wk, wv):
    m, d = x.shape
    depth = wk.shape[0]
    tm = min(m, TOKEN_TILE)
    shp = jax.ShapeDtypeStruct((depth, m, H_X, DH_X), F32)
    shp_h = jax.ShapeDtypeStruct((depth, H_X, m, DH_X), BF16)
    w_spec = pl.BlockSpec((None, d, d), lambda i, l: (l, 0, 0))
    o_spec = pl.BlockSpec((None, tm, H_X, DH_X), lambda i, l: (l, i, 0, 0))
    oh_spec = pl.BlockSpec((None, H_X, tm, DH_X), lambda i, l: (l, 0, i, 0))
    return pl.pallas_call(
        _mem_kv_kernel,
        grid=(m // tm, depth),
        in_specs=[pl.BlockSpec((tm, d), lambda i, l: (i, 0)), w_spec, w_spec],
        out_specs=[o_spec, o_spec, oh_spec, oh_spec],
        out_shape=[shp, shp, shp_h, shp_h],
        compiler_params=_cparams("parallel", "arbitrary"),
        name="mem_kv",
    )(x, wk, wv)


def _out_ln_kernel(*refs, n_parts, alpha, has_next):
    parts = refs[:n_parts]
    w_ref, x_ref, g_ref, b_ref = refs[n_parts:n_parts + 4]
    rest = refs[n_parts + 4:]
    if has_next:
        wn_ref, o_ref, q_ref = rest
    else:
        (o_ref,) = rest
    acc = None
    off = 0
    for p in parts:
        width = p.shape[1]
        t = jnp.dot(p[...].astype(BF16), w_ref[off:off + width, :], preferred_element_type=F32)
        acc = t if acc is None else acc + t
        off += width
    y = _layer_norm(alpha * x_ref[...] + acc, g_ref[...], b_ref[...])
    o_ref[...] = y
    if has_next:
        q_ref[...] = jnp.dot(y.astype(BF16), wn_ref[...], preferred_element_type=F32).astype(q_ref.dtype)


def out_ln(parts, w, layer, x, g, b, alpha, w_next=None):
    n, d = x.shape
    tm = PROJ_TILE
    has_next = w_next is not None
    in_specs = [pl.BlockSpec((tm, p.shape[1]), lambda i: (i, 0)) for p in parts]
    in_specs += [
        pl.BlockSpec((None,) + w.shape[1:], lambda i: (layer, 0, 0)),
        pl.BlockSpec((tm, d), lambda i: (i, 0)),
        pl.BlockSpec((1, d), lambda i: (0, 0)),
        pl.BlockSpec((1, d), lambda i: (0, 0)),
    ]
    args = list(parts) + [w, x, g.reshape(1, d), b.reshape(1, d)]
    out_specs = [pl.BlockSpec((tm, d), lambda i: (i, 0))]
    out_shape = [jax.ShapeDtypeStruct((n, d), F32)]
    if has_next:
        in_specs.append(pl.BlockSpec((None,) + w_next.shape[1:], lambda i: (layer, 0, 0)))
        args.append(w_next)
        out_specs.append(pl.BlockSpec((tm, w_next.shape[2]), lambda i: (i, 0)))
        out_shape.append(jax.ShapeDtypeStruct((n, w_next.shape[2]), BF16))
    res = pl.pallas_call(
        functools.partial(_out_ln_kernel, n_parts=len(parts), alpha=alpha, has_next=has_next),
        grid=(n // tm,),
        in_specs=in_specs,
        out_specs=out_specs,
        out_shape=out_shape,
        compiler_params=_cparams("parallel"),
        name="out_ln",
    )(*args)
    return res if has_next else res[0]


def _log_sigmoid(x):
    return jnp.minimum(x, 0.0) - jnp.log1p(jnp.exp(-jnp.abs(x)))


def _lane_pick(row, h):
    return row[:, h:h + 1]


def _mlstm_kernel(*refs, chunk, units, sequential):
    if sequential:
        za_ref, zg_ref, g_ref, ya_ref, c1_ref, n1_ref, m1_ref, c_sc, n_sc, m_sc = refs
    else:
        za_ref, zg_ref, g_ref, c0_ref, n0_ref, m0_ref, _prev, ya_ref, c1_ref, n1_ref, m1_ref = refs
    L = chunk
    small = L < V7X_BF16_ROWS
    cast = (lambda a: a) if small else (lambda a: a.astype(BF16))
    mm = lambda a, b: jnp.dot(cast(a), cast(b), preferred_element_type=F32)
    mm_nt = lambda a, b: lax.dot_general(cast(a), cast(b), (((1,), (1,)), ((), ())), preferred_element_type=F32)
    mm_tn = lambda a, b: lax.dot_general(cast(a), cast(b), (((0,), (0,)), ((), ())), preferred_element_type=F32)

    if sequential:
        ti = pl.program_id(1)

        @pl.when(ti == 0)
        def _():
            c_sc[...] = jnp.zeros_like(c_sc)
            n_sc[...] = jnp.zeros_like(n_sc)
            m_sc[...] = jnp.zeros_like(m_sc)

    row = lax.broadcasted_iota(jnp.int32, (L, L), 0)
    col = lax.broadcasted_iota(jnp.int32, (L, L), 1)
    causal = col <= row
    tril = causal.astype(F32)
    sel_r = lax.broadcasted_iota(jnp.int32, (V7X_SUBLANES, V7X_LANES), 0)
    sel_c = lax.broadcasted_iota(jnp.int32, (V7X_SUBLANES, V7X_LANES), 1)
    sel = (sel_r == sel_c).astype(F32)
    lane_g = lax.broadcasted_iota(jnp.int32, (L, N_ZG), 1)
    lane_m = lax.broadcasted_iota(jnp.int32, (1, V7X_LANES), 1)
    norm_g = g_ref[...]
    rep = lambda col: jnp.broadcast_to(col, (L, V7X_LANES))
    wide = lambda x, n: _tile_lanes(x, n // V7X_LANES) if n > V7X_LANES else x[:, :n]

    if sequential:
        state = [(c_sc[h], n_sc[h:h + 1, :], _lane_pick(m_sc[...], h)) for h in range(H_A)]

    for u in range(units):
        rows_u = slice(u * L, (u + 1) * L)
        gates = zg_ref[rows_u, :]
        gl = jnp.where(lane_g < H_A, gates, _log_sigmoid(gates))
        bcum = jnp.dot(tril, gl, precision=lax.Precision.HIGHEST, preferred_element_type=F32)
        mixed = jnp.where(lane_g < H_A, gates, bcum)
        t_rows = lax.dot_general(sel, mixed, (((1,), (1,)), ((), ())),
                                 precision=lax.Precision.HIGHEST, preferred_element_type=F32)
        if not sequential:
            state = [(c0_ref[u, h], n0_ref[u, h:h + 1, :], _lane_pick(m0_ref[u], h)) for h in range(H_A)]
        outs = []
        new_state = []
        for h in range(H_A):
            q = za_ref[rows_u, h * DH_A:(h + 1) * DH_A]
            k = za_ref[rows_u, W_A + h * DH_A:W_A + (h + 1) * DH_A] * (DH_A ** -0.5)
            v = za_ref[rows_u, 2 * W_A + h * DH_A:2 * W_A + (h + 1) * DH_A]
            og = za_ref[rows_u, 3 * W_A + h * DH_A:3 * W_A + (h + 1) * DH_A]
            ig_c = rep(gates[:, h:h + 1])
            b_c = rep(bcum[:, H_A + h:H_A + h + 1])
            ig_row = t_rows[h:h + 1, :]
            b_row = t_rows[H_A + h:H_A + h + 1, :]
            c, n, m_prev = state[h]

            dmat = jnp.where(causal, wide(b_c, L) - b_row + ig_row, NEG_INF)
            a = rep(jnp.max(dmat, axis=-1, keepdims=True))
            s = mm_nt(q, k) * jnp.exp(dmat - wide(a, L))
            n_loc = mm(s, v)
            d_loc = rep(jnp.sum(s, axis=-1, keepdims=True))
            a_last = a[L - 1:L, :]
            b_last = b_c[L - 1:L, :]
            wk = jnp.exp(b_last - b_c + ig_c - a_last)[:, :DH_A]
            u_loc = mm_tn(wk * v, k)
            nu_loc = jnp.sum(wk * k, axis=0, keepdims=True)

            inter = b_c + m_prev
            m_row = jnp.maximum(inter, a)
            r = jnp.exp(a - m_row)
            w_inter = jnp.exp(inter - m_row)
            num = r[:, :DH_A] * n_loc + w_inter[:, :DH_A] * mm_nt(q, c)
            den = r * d_loc + w_inter * rep(jnp.sum(q * n, axis=-1, keepdims=True))
            hh = num / jnp.maximum(jnp.abs(den), jnp.exp(-m_row))[:, :DH_A]
            m_new = m_row[L - 1:L, 0:1]
            decay = jnp.exp(b_last[:, 0:1] + m_prev - m_new)
            e_loc = jnp.exp(a_last[:, 0:1] - m_new)
            new_state.append((decay * c + e_loc * u_loc, decay * n + e_loc * nu_loc, m_new))

            hn = hh * lax.rsqrt(jnp.mean(hh * hh, axis=-1, keepdims=True) + NORM_EPS)
            outs.append(jax.nn.sigmoid(og) * hn)
        ya_ref[rows_u, :] = (jnp.concatenate(outs, axis=-1) * norm_g).astype(ya_ref.dtype)
        state = new_state
        if not sequential:
            m_out = jnp.zeros((1, V7X_LANES), F32)
            for h in range(H_A):
                c1_ref[u, h] = state[h][0]
                n1_ref[u, h:h + 1, :] = state[h][1]
                m_out = jnp.where(lane_m == h, state[h][2], m_out)
            m1_ref[u] = m_out

    if sequential:
        m_out = jnp.zeros((1, V7X_LANES), F32)
        for h in range(H_A):
            c_sc[h] = state[h][0]
            n_sc[h:h + 1, :] = state[h][1]
            m_out = jnp.where(lane_m == h, state[h][2], m_out)
        m_sc[...] = m_out

        @pl.when(ti == pl.num_programs(1) - 1)
        def _():
            c1_ref[0] = c_sc[...]
            n1_ref[0] = n_sc[...]
            m1_ref[0] = m_sc[...]


def _mlstm_sample_kernel(za_ref, zg_ref, g_ref, c0_ref, n0_ref, m0_ref, _prev, ya_ref, c1_ref, n1_ref, m1_ref,
                         *, t_len, seqs):
    R = seqs * t_len
    row = lax.broadcasted_iota(jnp.int32, (R, R), 0)
    col = lax.broadcasted_iota(jnp.int32, (R, R), 1)
    mask = (row // t_len == col // t_len) & (col <= row)
    tril = mask.astype(F32)
    sel_r = lax.broadcasted_iota(jnp.int32, (V7X_SUBLANES, V7X_LANES), 0)
    sel_c = lax.broadcasted_iota(jnp.int32, (V7X_SUBLANES, V7X_LANES), 1)
    sel = (sel_r == sel_c).astype(F32)
    lane_g = lax.broadcasted_iota(jnp.int32, (R, N_ZG), 1)
    lane_m = lax.broadcasted_iota(jnp.int32, (seqs, 1, V7X_LANES), 2)
    rep = lambda c_: jnp.broadcast_to(c_, (R, V7X_LANES))
    wide = lambda x, n: _tile_lanes(x, n // V7X_LANES) if n > V7X_LANES else x[:, :n]
    per_seq = lambda x: x.reshape(seqs, t_len, x.shape[-1])
    last_rows = lambda x: jnp.broadcast_to(per_seq(x)[:, t_len - 1:t_len, :], (seqs, t_len, x.shape[-1])
                                           ).reshape(R, x.shape[-1])
    seq_rows = lambda x: jnp.broadcast_to(x, (seqs, t_len, x.shape[-1])).reshape(R, x.shape[-1])
    state_rows = lambda x: jnp.broadcast_to(x, (seqs, DH_A, x.shape[-1])).reshape(seqs * DH_A, x.shape[-1])
    own_r = lax.broadcasted_iota(jnp.int32, (R, seqs * DH_A), 0) // t_len
    own_c = lax.broadcasted_iota(jnp.int32, (R, seqs * DH_A), 1) // DH_A
    own = own_r == own_c
    odd_seq = (lax.broadcasted_iota(jnp.int32, (R, DH_A), 0) // t_len) % 2 == 1

    gates = zg_ref[...]
    gl = jnp.where(lane_g < H_A, gates, _log_sigmoid(gates))
    bcum = jnp.dot(tril, gl, precision=lax.Precision.HIGHEST, preferred_element_type=F32)
    mixed = jnp.where(lane_g < H_A, gates, bcum)
    t_rows = lax.dot_general(sel, mixed, (((1,), (1,)), ((), ())),
                             precision=lax.Precision.HIGHEST, preferred_element_type=F32)
    m0 = m0_ref[...]
    norm_g = g_ref[...]
    outs = []
    m_out = jnp.zeros((seqs, 1, V7X_LANES), F32)
    for h in range(H_A):
        q = za_ref[:, h * DH_A:(h + 1) * DH_A]
        k = za_ref[:, W_A + h * DH_A:W_A + (h + 1) * DH_A] * (DH_A ** -0.5)
        v = za_ref[:, 2 * W_A + h * DH_A:2 * W_A + (h + 1) * DH_A]
        og = za_ref[:, 3 * W_A + h * DH_A:3 * W_A + (h + 1) * DH_A]
        ig_c = rep(gates[:, h:h + 1])
        b_c = rep(bcum[:, H_A + h:H_A + h + 1])
        ig_row = t_rows[h:h + 1, :]
        b_row = t_rows[H_A + h:H_A + h + 1, :]
        m_prev = rep(seq_rows(m0)[:, h:h + 1])
        c_stack = c0_ref[:, h].reshape(seqs * DH_A, DH_A)
        n_rows = seq_rows(n0_ref[:, h:h + 1, :])

        dmat = jnp.where(mask, wide(b_c, R) - b_row + ig_row, NEG_INF)
        a = rep(jnp.max(dmat, axis=-1, keepdims=True))
        s = _bdot_nt(q, k) * jnp.exp(dmat - wide(a, R))
        n_loc = _bdot(s, v)
        d_loc = rep(jnp.sum(s, axis=-1, keepdims=True))
        a_last = last_rows(a)
        b_last = last_rows(b_c)
        wk = jnp.exp(b_last - b_c + ig_c - a_last)[:, :DH_A]
        vw = wk * v
        vw2 = jnp.concatenate([vw, vw], axis=1)
        u_stack = lax.dot_general(jnp.where(own, _tile_lanes(vw2, seqs // 2), 0.0).astype(BF16), k.astype(BF16),
                                  (((0,), (0,)), ((), ())), preferred_element_type=F32)
        nu = jnp.sum(per_seq(wk * k), axis=1, keepdims=True)

        inter = b_c + m_prev
        m_row = jnp.maximum(inter, a)
        r = jnp.exp(a - m_row)
        w_inter = jnp.exp(inter - m_row)
        x_all = jnp.where(own, _bdot_nt(q, c_stack), 0.0)
        fold = x_all[:, 0:V7X_LANES]
        for j in range(1, seqs * DH_A // V7X_LANES):
            fold = fold + x_all[:, j * V7X_LANES:(j + 1) * V7X_LANES]
        qc = jnp.where(odd_seq, fold[:, DH_A:], fold[:, :DH_A])
        num = r[:, :DH_A] * n_loc + w_inter[:, :DH_A] * qc
        den = r * d_loc + w_inter * rep(jnp.sum(q * n_rows, axis=-1, keepdims=True))
        hh = num / jnp.maximum(jnp.abs(den), jnp.exp(-m_row))[:, :DH_A]
        hn = hh * lax.rsqrt(jnp.mean(hh * hh, axis=-1, keepdims=True) + NORM_EPS)
        outs.append(jax.nn.sigmoid(og) * hn)

        m_new = per_seq(m_row)[:, t_len - 1:t_len, :]
        decay = jnp.exp(per_seq(b_c + m_prev)[:, t_len - 1:t_len, :] - m_new)
        e_loc = jnp.exp(per_seq(a)[:, t_len - 1:t_len, :] - m_new)
        c_new = (state_rows(decay)[:, :DH_A] * c_stack + state_rows(e_loc)[:, :DH_A] * u_stack)
        c1_ref[:, h] = c_new.reshape(seqs, DH_A, DH_A)
        n1_ref[:, h:h + 1, :] = decay[:, :, :DH_A] * n0_ref[:, h:h + 1, :] + e_loc[:, :, :DH_A] * nu
        m_out = jnp.where(lane_m == h, m_new, m_out)
    ya_ref[...] = (jnp.concatenate(outs, axis=-1) * norm_g).astype(ya_ref.dtype)
    m1_ref[...] = m_out


def mlstm(za, zg, norm_g, nb, t, row0, init=None, layer=0, prev=None):
    n_tok = za.shape[0]
    L = MLSTM_CHUNK if t % MLSTM_CHUNK == 0 else t
    units = MLSTM_UNITS
    if init is None:
        L, units = MLSTM_PROMPT_CHUNK, 1
    nc = t // L
    rows = units * L
    sequential = init is None
    if sequential:
        grid = (nb, nc // units)
        seq_blk = 1
        row_map = lambda b, c: (row0 // rows + b * (nc // units) + c, 0)
    else:
        assert nc == 1
        grid = (nb // units, 1)
        seq_blk = units
        row_map = lambda b, c: (row0 // rows + b, 0)
    st4 = lambda b, c: (b, 0, 0, 0)
    st3 = lambda b, c: (b, 0, 0)
    in_specs = [
        pl.BlockSpec((rows, N_ZA), row_map),
        pl.BlockSpec((rows, N_ZG), row_map),
        pl.BlockSpec((1, W_A), lambda b, c: (0, 0)),
    ]
    args = [za, zg, norm_g.reshape(1, W_A)]
    aliases = {}
    scratch = []
    if sequential:
        scratch = [pltpu.VMEM((H_A, DH_A, DH_A), F32), pltpu.VMEM((H_A, DH_A), F32),
                   pltpu.VMEM((1, V7X_LANES), F32)]
    else:
        c0, n0, m0 = init
        m0p = jnp.pad(m0, ((0, 0), (0, V7X_LANES - H_A))).reshape(nb, 1, V7X_LANES)
        in_specs += [
            pl.BlockSpec((None, seq_blk, H_A, DH_A, DH_A), lambda b, c: (layer, b, 0, 0, 0)),
            pl.BlockSpec((seq_blk, H_A, DH_A), st3),
            pl.BlockSpec((seq_blk, 1, V7X_LANES), st3),
            pl.BlockSpec(memory_space=pl.ANY),
        ]
        args += [c0, n0, m0p, prev]
        aliases = {6: 0}
    out_specs = [
        pl.BlockSpec((rows, W_A), row_map),
        pl.BlockSpec((seq_blk, H_A, DH_A, DH_A), st4),
        pl.BlockSpec((seq_blk, H_A, DH_A), st3),
        pl.BlockSpec((seq_blk, 1, V7X_LANES), st3),
    ]
    out_shape = [
        jax.ShapeDtypeStruct((n_tok, W_A), BF16),
        jax.ShapeDtypeStruct((nb, H_A, DH_A, DH_A), F32),
        jax.ShapeDtypeStruct((nb, H_A, DH_A), F32),
        jax.ShapeDtypeStruct((nb, 1, V7X_LANES), F32),
    ]
    ya, c1, n1, m1 = pl.pallas_call(
        (functools.partial(_mlstm_kernel, chunk=L, units=units, sequential=True) if sequential
         else functools.partial(_mlstm_sample_kernel, t_len=L, seqs=units)),
        grid=grid,
        in_specs=in_specs,
        out_specs=out_specs,
        out_shape=out_shape,
        scratch_shapes=scratch,
        input_output_aliases=aliases,
        compiler_params=_cparams("parallel", "arbitrary"),
        name="mlstm_prompt" if sequential else "mlstm_sample",
    )(*args)
    return ya, c1, n1, m1[:, 0, :H_A]


def _diff_lambda(lam_ref, lam_init):
    lp = lam_ref[...]
    d01 = jnp.sum(lp[0:1, :] * lp[1:2, :], axis=-1, keepdims=True)
    d23 = jnp.sum(lp[2:3, :] * lp[3:4, :], axis=-1, keepdims=True)
    return jnp.exp(d01) - jnp.exp(d23) + lam_init


def _alibi_slope(h):
    return jnp.where(h == 0, 2.0 ** -2, jnp.where(h == 1, 2.0 ** -4, jnp.where(h == 2, 2.0 ** -6, 2.0 ** -8)))


def _alibi_tables(t):
    slopes = (2.0 ** (-8.0 * jnp.arange(1, H_B + 1, dtype=F32) / H_B))[:, None]
    pos = jnp.arange(t, dtype=jnp.int32)
    hi = ((pos // ALIBI_SPLIT) * ALIBI_SPLIT).astype(F32)[None, :]
    lo = (pos % ALIBI_SPLIT).astype(F32)[None, :]
    ones = jnp.ones((H_B, t), F32)
    pad = jnp.zeros((H_B, t, DV_B - 4), F32)
    aq = jnp.concatenate([jnp.stack([slopes * ones, slopes * ones, -slopes * hi, -slopes * lo], axis=-1), pad], -1)
    ak = jnp.concatenate([jnp.stack([hi * ones, lo * ones, ones, ones], axis=-1), pad], -1)
    return aq.astype(BF16), ak.astype(BF16)


def _attn_prompt_kernel(lam_ref, g_ref, q_ref, aq_ref, k_ref, ak_ref, v_ref, o_ref,
                        qs_sc, m_sc, acc_sc, *, lam_init, tq, tk):
    i = pl.program_id(1)
    j = pl.program_id(2)
    rq = ATTN_ROWS

    @pl.when(j == 0)
    def _():
        lane = lax.broadcasted_iota(jnp.int32, (tq, DV_B), 1)
        for h in range(H_B):
            q = q_ref[:, h * DV_B:(h + 1) * DV_B] * (DK_B ** -0.5)
            qs_sc[h, 0:tq, 0:DV_B] = jnp.where(lane < DK_B, q, 0.0).astype(BF16)
            qs_sc[h, tq:2 * tq, 0:DV_B] = jnp.where(lane >= DK_B, q, 0.0).astype(BF16)
            qs_sc[h, 0:tq, DV_B:2 * DV_B] = aq_ref[h]
            qs_sc[h, tq:2 * tq, DV_B:2 * DV_B] = aq_ref[h]
        m_sc[...] = jnp.full_like(m_sc, NEG_INF)
        acc_sc[...] = jnp.zeros_like(acc_sc)

    def step(masked):
        ones = jnp.ones((tk, V7X_LANES), BF16)
        for h in range(H_B):
            kaug = jnp.concatenate([k_ref[:, h * DV_B:(h + 1) * DV_B].astype(BF16), ak_ref[h]], axis=1)
            vaug = jnp.concatenate([v_ref[:, h * DV_B:(h + 1) * DV_B].astype(BF16), ones], axis=1)
            for r in range(2 * tq // rq):
                rows = slice(r * rq, (r + 1) * rq)
                s = lax.dot_general(qs_sc[h, rows, :], kaug, (((1,), (1,)), ((), ())), preferred_element_type=F32)
                if masked:
                    qi = (r * rq) % tq + lax.broadcasted_iota(jnp.int32, (rq, tk), 0)
                    kj = lax.broadcasted_iota(jnp.int32, (rq, tk), 1)
                    s = jnp.where(kj <= qi, s, NEG_INF)
                m_old = m_sc[h, rows, :]
                m_new = jnp.maximum(m_old, jnp.max(s, axis=-1, keepdims=True))
                alpha = jnp.exp(m_old - m_new)
                p = jnp.exp(s - _tile_lanes(m_new, tk // V7X_LANES))
                acc_sc[h, rows, :] = (_tile_lanes(alpha, 2) * acc_sc[h, rows, :]
                                      + jnp.dot(p.astype(BF16), vaug, preferred_element_type=F32))
                m_sc[h, rows, :] = m_new

    @pl.when(j < i)
    def _():
        step(False)

    @pl.when(j == i)
    def _():
        step(True)

    @pl.when(j == pl.num_programs(2) - 1)
    def _():
        lam = _diff_lambda(lam_ref, lam_init)
        for h in range(H_B):
            o0 = acc_sc[h, 0:tq, 0:DV_B] / acc_sc[h, 0:tq, DV_B:2 * DV_B]
            o1 = acc_sc[h, tq:2 * tq, 0:DV_B] / acc_sc[h, tq:2 * tq, DV_B:2 * DV_B]
            ob = o0 - lam * o1
            on = ob * lax.rsqrt(jnp.mean(ob * ob, axis=-1, keepdims=True) + NORM_EPS)
            o_ref[:, h * DV_B:(h + 1) * DV_B] = on * g_ref[:, h * DV_B:(h + 1) * DV_B] * (1.0 - lam_init)


def diff_attn_prompt(zb, aq, ak, lam_p, norm_g, nb, t, lam_init):
    n_tok = zb.shape[0]
    tq = tk = ATTN_TILE
    nq = t // tq
    kv_map = lambda off: (lambda b, i, j: (b * nq + jnp.minimum(i, j), off))
    return pl.pallas_call(
        functools.partial(_attn_prompt_kernel, lam_init=lam_init, tq=tq, tk=tk),
        grid=(nb, nq, nq),
        in_specs=[
            pl.BlockSpec(lam_p.shape, lambda b, i, j: (0, 0)),
            pl.BlockSpec((1, W_B), lambda b, i, j: (0, 0)),
            pl.BlockSpec((tq, W_B), lambda b, i, j: (b * nq + i, 0)),
            pl.BlockSpec((H_B, tq, DV_B), lambda b, i, j: (0, i, 0)),
            pl.BlockSpec((tk, W_B), kv_map(1)),
            pl.BlockSpec((H_B, tk, DV_B), lambda b, i, j: (0, jnp.minimum(i, j), 0)),
            pl.BlockSpec((tk, W_B), kv_map(2)),
        ],
        out_specs=pl.BlockSpec((tq, W_B), lambda b, i, j: (b * nq + i, 0)),
        out_shape=jax.ShapeDtypeStruct((n_tok, W_B), F32),
        scratch_shapes=[pltpu.VMEM((H_B, 2 * tq, 2 * DV_B), BF16), pltpu.VMEM((H_B, 2 * tq, V7X_LANES), F32),
                        pltpu.VMEM((H_B, 2 * tq, 2 * DV_B), F32)],
        compiler_params=_cparams("parallel", "parallel", "arbitrary"),
        name="diff_attn_prompt",
    )(lam_p, norm_g.reshape(1, W_B), zb, aq, zb, ak, zb)


def _attn_sample_kernel(*refs, lam_init, past_len, t_new, pages_per_step):
    G = pages_per_step
    pt_ref, lam_ref, g_ref, q_ref, kn_ref, vn_ref = refs[:6]
    k_refs = refs[6:6 + G]
    v_refs = refs[6 + G:6 + 2 * G]
    _prev, o_ref, qs_sc, bias_sc, m_sc, l_sc, acc_sc = refs[6 + 2 * G:]
    del pt_ref
    ps = pl.program_id(1)
    rows_per_head = 2 * t_new
    n_rows = H_B * rows_per_head
    page_rows = PAGE_SIZE * H_B
    reps = page_rows // V7X_LANES

    r_lane = lax.broadcasted_iota(jnp.int32, (n_rows, V7X_LANES), 0)
    slope = _alibi_slope(r_lane // rows_per_head).astype(F32)

    @pl.when(ps == 0)
    def _():
        q = q_ref[...] * (DK_B ** -0.5)
        lane = lax.broadcasted_iota(jnp.int32, (t_new, DV_B), 1)
        for h in range(H_B):
            qh = q[:, h * DV_B:(h + 1) * DV_B]
            qs_sc[h * rows_per_head:(h + 1) * rows_per_head, :] = jnp.concatenate(
                [jnp.where(lane < DK_B, qh, 0.0), jnp.where(lane >= DK_B, qh, 0.0)], axis=0).astype(BF16)
        rr = lax.broadcasted_iota(jnp.int32, (n_rows, page_rows), 0)
        cc = lax.broadcasted_iota(jnp.int32, (n_rows, page_rows), 1)
        rel = cc // H_B - (past_len + rr % t_new)
        bias = _alibi_slope(rr // rows_per_head).astype(F32) * rel.astype(F32)
        bias_sc[...] = jnp.where(cc % H_B == rr // rows_per_head, bias, NEG_INF)
        m_sc[...] = jnp.full_like(m_sc, NEG_INF)
        l_sc[...] = jnp.zeros_like(l_sc)
        acc_sc[...] = jnp.zeros_like(acc_sc)

    qs = qs_sc[...]
    m, l, acc = m_sc[...], l_sc[...], acc_sc[...]
    s_pages = []
    m_new = m
    for g in range(G):
        base = ((ps * G + g) * PAGE_SIZE).astype(F32)
        s = _bdot_nt(qs, k_refs[g][...]) + (bias_sc[...] + _tile_lanes(slope * base, reps))
        m_new = jnp.maximum(m_new, jnp.max(s, axis=-1, keepdims=True))
        s_pages.append(s)
    alpha = jnp.exp(m - m_new)
    l = alpha * l
    acc = alpha * acc
    for g in range(G):
        p = jnp.exp(s_pages[g] - _tile_lanes(m_new, reps))
        l = l + jnp.sum(p, axis=-1, keepdims=True)
        acc = acc + _bdot(p, v_refs[g][...])
    m = m_new

    @pl.when(ps < pl.num_programs(1) - 1)
    def _():
        m_sc[...] = m
        l_sc[...] = l
        acc_sc[...] = acc

    @pl.when(ps == pl.num_programs(1) - 1)
    def _():
        rnd = lambda a: a.astype(BF16).astype(F32)
        kn = rnd(kn_ref[...])
        vn = rnd(vn_ref[...])
        qf = qs.astype(F32)
        sn = jnp.concatenate(
            [lax.dot_general(qf[h * rows_per_head:(h + 1) * rows_per_head, :], kn[:, h * DV_B:(h + 1) * DV_B],
                             (((1,), (1,)), ((), ())), preferred_element_type=F32) for h in range(H_B)],
            axis=0)
        rr = lax.broadcasted_iota(jnp.int32, (n_rows, t_new), 0)
        rel = lax.broadcasted_iota(jnp.int32, (n_rows, t_new), 1) - rr % t_new
        sn = jnp.where(rel <= 0, sn + _alibi_slope(rr // rows_per_head).astype(F32) * rel.astype(F32), NEG_INF)
        m_fin = jnp.maximum(m, jnp.max(sn, axis=-1, keepdims=True))
        alpha = jnp.exp(m - m_fin)
        pn = rnd(jnp.exp(sn - m_fin[:, 0:1]))
        l_fin = alpha * l + jnp.sum(pn, axis=-1, keepdims=True)
        pv = jnp.concatenate(
            [jnp.dot(pn[h * rows_per_head:(h + 1) * rows_per_head, :], vn[:, h * DV_B:(h + 1) * DV_B],
                     preferred_element_type=F32) for h in range(H_B)], axis=0)
        o = (alpha * acc + pv) / l_fin

        lam = _diff_lambda(lam_ref, lam_init)
        outs = []
        for h in range(H_B):
            o0 = o[h * rows_per_head:h * rows_per_head + t_new, :]
            o1 = o[h * rows_per_head + t_new:(h + 1) * rows_per_head, :]
            ob = o0 - lam * o1
            outs.append(ob * lax.rsqrt(jnp.mean(ob * ob, axis=-1, keepdims=True) + NORM_EPS))
        o_ref[...] = jnp.concatenate(outs, axis=-1) * g_ref[...] * (1.0 - lam_init)


def diff_attn_sample(zb, cache_k, cache_v, layer, page_table, lam_p, norm_g, nb, t_new, row0, lam_init, prev):
    n_tok = zb.shape[0]
    n_pages = page_table.shape[1]
    G = PAGES_PER_STEP
    rb0 = row0 // t_new
    n_rows = 2 * H_B * t_new
    page_rows = PAGE_SIZE * H_B
    ck = cache_k.reshape(cache_k.shape[0], cache_k.shape[1], page_rows, DV_B)
    cv = cache_v.reshape(cache_v.shape[0], cache_v.shape[1], page_rows, DV_B)

    def page_map(g):
        return lambda b, p, pt: (layer, pt[b, p * G + g], 0, 0)

    page_blk = (None, None, page_rows, DV_B)
    in_specs = [
        pl.BlockSpec(lam_p.shape, lambda b, p, pt: (0, 0)),
        pl.BlockSpec((1, W_B), lambda b, p, pt: (0, 0)),
        pl.BlockSpec((t_new, W_B), lambda b, p, pt: (rb0 + b, 0)),
        pl.BlockSpec((t_new, W_B), lambda b, p, pt: (rb0 + b, 1)),
        pl.BlockSpec((t_new, W_B), lambda b, p, pt: (rb0 + b, 2)),
    ]
    in_specs += [pl.BlockSpec(page_blk, page_map(g)) for g in range(G)]
    in_specs += [pl.BlockSpec(page_blk, page_map(g)) for g in range(G)]
    in_specs += [pl.BlockSpec(memory_space=pl.ANY)]
    args = [page_table, lam_p, norm_g.reshape(1, W_B), zb, zb, zb] + [ck] * G + [cv] * G + [prev]
    grid_spec = pltpu.PrefetchScalarGridSpec(
        num_scalar_prefetch=1,
        grid=(nb, n_pages // G),
        in_specs=in_specs,
        out_specs=pl.BlockSpec((t_new, W_B), lambda b, p, pt: (rb0 + b, 0)),
        scratch_shapes=[pltpu.VMEM((n_rows, DV_B), BF16), pltpu.VMEM((n_rows, page_rows), F32),
                        pltpu.VMEM((n_rows, V7X_LANES), F32), pltpu.VMEM((n_rows, V7X_LANES), F32),
                        pltpu.VMEM((n_rows, DV_B), F32)],
    )
    return pl.pallas_call(
        functools.partial(_attn_sample_kernel, lam_init=lam_init, past_len=n_pages * PAGE_SIZE,
                          t_new=t_new, pages_per_step=G),
        grid_spec=grid_spec,
        out_shape=jax.ShapeDtypeStruct((n_tok, W_B), F32),
        input_output_aliases={len(args) - 1: 0},
        compiler_params=_cparams("parallel", "arbitrary"),
        name="diff_attn_sample",
    )(*args)


def _s5_disc_kernel(are_ref, aim_ref, ldt_ref, bre_ref, bim_ref, lre_ref, lim_ref, bbre_ref, bbim_ref):
    a_re = are_ref[...]
    a_im = aim_ref[...]
    dt = jnp.exp(ldt_ref[...])
    mag = jnp.exp(a_re * dt)
    lb_re = mag * jnp.cos(a_im * dt)
    lb_im = mag * jnp.sin(a_im * dt)
    den = a_re * a_re + a_im * a_im
    xr = lb_re - 1.0
    fr = (xr * a_re + lb_im * a_im) / den
    fi = (lb_im * a_re - xr * a_im) / den
    lre_ref[...] = lb_re
    lim_ref[...] = lb_im
    b_re = bre_ref[...]
    b_im = bim_ref[...]
    bbre_ref[...] = fr * b_re - fi * b_im
    bbim_ref[...] = fr * b_im + fi * b_re


def s5_discretize(a_re, a_im, log_dt, b_re, b_im):
    depth = a_re.shape[0]
    gp = G_C * P_C
    flat = lambda a: a.reshape(depth, 1, gp)
    ldt = jnp.broadcast_to(log_dt[:, :, None], (depth, G_C, P_C)).reshape(depth, 1, gp)
    tr = lambda b: jnp.transpose(b, (0, 3, 1, 2)).reshape(depth, GC, gp)
    shp1 = jax.ShapeDtypeStruct((depth, 1, gp), F32)
    shpb = jax.ShapeDtypeStruct((depth, GC, gp), F32)
    return pl.pallas_call(_s5_disc_kernel, out_shape=[shp1, shp1, shpb, shpb], name="s5_discretize")(
        flat(a_re), flat(a_im), ldt, tr(b_re), tr(b_im))


def _block_diag_in(bb):
    depth, _, gp = bb.shape
    tiled = jnp.tile(bb, (1, G_C, 1)).reshape(depth, G_C, GC, gp)
    grp_r = jnp.arange(G_C)[:, None, None]
    grp_c = (jnp.arange(gp) // P_C)[None, None, :]
    return jnp.where(grp_r == grp_c, tiled, 0.0).reshape(depth, G_C * GC, gp)


def _block_diag_out(c):
    depth = c.shape[0]
    ct = jnp.transpose(c, (0, 1, 3, 2)).reshape(depth, G_C * P_C, GC)
    tiled = jnp.tile(ct, (1, 1, G_C))
    grp_r = (jnp.arange(G_C * P_C) // P_C)[:, None]
    grp_c = (jnp.arange(G_C * GC) // GC)[None, :]
    return jnp.where(grp_r == grp_c, tiled, 0.0)


def _cmul_add(a_re, a_im, x_re, x_im, y_re, y_im):
    return y_re + (a_re * x_re - a_im * x_im), y_im + (a_re * x_im + a_im * x_re)


def _block_scan(x_re, x_im, pw, row8):
    for d, (a_re, a_im) in zip((1, 2, 4), pw):
        sh_re = jnp.where(row8 >= d, pltpu.roll(x_re, d, 0), 0.0)
        sh_im = jnp.where(row8 >= d, pltpu.roll(x_im, d, 0), 0.0)
        x_re, x_im = _cmul_add(a_re, a_im, sh_re, sh_im, x_re, x_im)
    return x_re, x_im


def _s5_kernel(*refs, rows, independent, aliased):
    (u_ref, lre_ref, lim_ref, bb_ref, cc_ref, d_ref, gw_ref, gb_ref, s0re_ref, s0im_ref) = refs[:10]
    rest = refs[10:]
    if aliased:
        rest = rest[1:]
    y_ref, s1re_ref, s1im_ref, st_sc, car_sc = rest
    gp = G_C * P_C
    nblk = rows // V7X_SUBLANES
    ti = pl.program_id(1)

    lam_re = lre_ref[...]
    lam_im = lim_ref[...]
    l2_re, l2_im = lam_re * lam_re - lam_im * lam_im, 2.0 * lam_re * lam_im
    l4_re, l4_im = l2_re * l2_re - l2_im * l2_im, 2.0 * l2_re * l2_im
    pw = ((lam_re, lam_im), (l2_re, l2_im), (l4_re, l4_im))
    row8 = lax.broadcasted_iota(jnp.int32, (V7X_SUBLANES, gp), 0)
    pk_re, pk_im = _block_scan(jnp.where(row8 == 0, lam_re, 0.0), jnp.where(row8 == 0, lam_im, 0.0), pw, row8)

    u = u_ref[...]
    st_sc[...] = _bdot(u, bb_ref[...])

    if not independent:
        @pl.when(ti == 0)
        def _():
            car_sc[0:1, :] = s0re_ref[...]
            car_sc[1:2, :] = s0im_ref[...]

    def body(bi, carry):
        r0 = pl.multiple_of(bi * V7X_SUBLANES, V7X_SUBLANES)
        x_re = st_sc[pl.ds(r0, V7X_SUBLANES), 0:gp]
        x_im = st_sc[pl.ds(r0, V7X_SUBLANES), gp:2 * gp]
        x_re, x_im = _block_scan(x_re, x_im, pw, row8)
        if independent:
            c_re = s0re_ref[pl.ds(bi, 1), :]
            c_im = s0im_ref[pl.ds(bi, 1), :]
        else:
            c_re, c_im = carry
        s_re, s_im = _cmul_add(pk_re, pk_im, c_re, c_im, x_re, x_im)
        st_sc[pl.ds(r0, V7X_SUBLANES), 0:gp] = s_re
        st_sc[pl.ds(r0, V7X_SUBLANES), gp:2 * gp] = s_im
        last_re = s_re[V7X_SUBLANES - 1:V7X_SUBLANES, :]
        last_im = s_im[V7X_SUBLANES - 1:V7X_SUBLANES, :]
        if independent:
            s1re_ref[pl.ds(bi, 1), :] = last_re
            s1im_ref[pl.ds(bi, 1), :] = last_im
            return carry
        return last_re, last_im

    if independent:
        lax.fori_loop(0, nblk, body, 0)
    else:
        c_re, c_im = lax.fori_loop(0, nblk, body, (car_sc[0:1, :], car_sc[1:2, :]))
        car_sc[0:1, :] = c_re
        car_sc[1:2, :] = c_im

        @pl.when(ti == pl.num_programs(1) - 1)
        def _():
            s1re_ref[...] = c_re
            s1im_ref[...] = c_im

    y = _bdot(st_sc[...], cc_ref[...]) + d_ref[...] * u
    z = _bdot(jax.nn.gelu(y), gw_ref[...]) + gb_ref[...]
    y_ref[...] = (z[:, :W_C] * jax.nn.sigmoid(z[:, W_C:])).astype(y_ref.dtype)


def s5_mixer(zc, lam_re, lam_im, bb, cc, layer, d, glu_w, glu_b, s0_re, s0_im, nb, t, row0, prev=None):
    n_tok = zc.shape[0]
    gp = G_C * P_C
    independent = t == V7X_SUBLANES
    aliased = prev is not None
    if independent:
        seqs = S5_SAMPLE_SEQS
        rows = seqs * t
        grid = (nb // seqs, 1)
        st_spec = pl.BlockSpec((seqs, gp), lambda b, i: (b, 0))
    else:
        rows = S5_TILE
        grid = (nb, t // rows)
        st_spec = pl.BlockSpec((1, gp), lambda b, i: (b, 0))
    nt = grid[1]
    rb0 = row0 // rows
    row_map = lambda b, i: (rb0 + b * nt + i, 0)
    const = lambda b, i: (0, 0)
    by_layer = lambda b, i: (layer, 0, 0)
    in_specs = [
        pl.BlockSpec((rows, W_C), row_map),
        pl.BlockSpec((None, 1, gp), by_layer),
        pl.BlockSpec((None, 1, gp), by_layer),
        pl.BlockSpec((None,) + bb.shape[1:], by_layer),
        pl.BlockSpec((None,) + cc.shape[1:], by_layer),
        pl.BlockSpec((1, W_C), const),
        pl.BlockSpec((None,) + glu_w.shape[1:], by_layer),
        pl.BlockSpec((1, 2 * W_C), const),
        st_spec,
        st_spec,
    ]
    args = [zc, lam_re, lam_im, bb, cc, d.reshape(1, W_C), glu_w, glu_b.reshape(1, 2 * W_C), s0_re, s0_im]
    aliases = {}
    if aliased:
        in_specs.append(pl.BlockSpec(memory_space=pl.ANY))
        args.append(prev)
        aliases = {len(args) - 1: 0}
    if independent:
        s1_shape = jax.ShapeDtypeStruct((nb, gp), F32)
    else:
        s1_shape = jax.ShapeDtypeStruct((nb, 1, gp), F32)
        st_out = pl.BlockSpec((None, 1, gp), lambda b, i: (b, 0, 0))
    out_specs = [pl.BlockSpec((rows, W_C), row_map)] + ([st_spec, st_spec] if independent else [st_out, st_out])
    if not independent:
        args[8] = s0_re.reshape(nb, 1, gp)
        args[9] = s0_im.reshape(nb, 1, gp)
        in_specs[8] = in_specs[9] = pl.BlockSpec((None, 1, gp), lambda b, i: (b, 0, 0))
    yc, s1_re, s1_im = pl.pallas_call(
        functools.partial(_s5_kernel, rows=rows, independent=independent, aliased=aliased),
        grid=grid,
        in_specs=in_specs,
        out_specs=out_specs,
        out_shape=[jax.ShapeDtypeStruct((n_tok, W_C), BF16), s1_shape, s1_shape],
        scratch_shapes=[pltpu.VMEM((rows, 2 * gp), F32), pltpu.VMEM((V7X_SUBLANES, gp), F32)],
        input_output_aliases=aliases,
        compiler_params=_cparams("parallel", "arbitrary"),
        name="s5_sample" if independent else "s5_prompt",
    )(*args)
    return yc, s1_re.reshape(nb, G_C, P_C), s1_im.reshape(nb, G_C, P_C)


def _xattn_kernel(q_ref, mk_ref, mv_ref, o_ref):
    q = q_ref[...] * (DH_X ** -0.5)
    outs = []
    for h in range(H_X):
        s = _bdot_nt(q[:, h * DH_X:(h + 1) * DH_X], mk_ref[h])
        m = jnp.max(s, axis=-1, keepdims=True)
        p = jnp.exp(s - m)
        l = jnp.sum(p, axis=-1, keepdims=True)
        outs.append(_bdot(p, mv_ref[h]) / l)
    o_ref[...] = jnp.concatenate(outs, axis=-1).astype(o_ref.dtype)


def _xattn_native_kernel(q_ref, mk_ref, mv_ref, _prev, o_ref, bias_sc, *, tq, seqs, n_mem):
    rows = seqs * tq
    n_keys = seqs * n_mem * H_X

    @pl.when(pl.program_id(0) == 0)
    def _():
        rr = lax.broadcasted_iota(jnp.int32, (H_X * rows, n_keys), 0)
        cc = lax.broadcasted_iota(jnp.int32, (H_X * rows, n_keys), 1)
        own = (cc // (n_mem * H_X) == (rr % rows) // tq) & (cc % H_X == rr // rows)
        bias_sc[...] = jnp.where(own, 0.0, NEG_INF)

    q = q_ref[...] * (DH_X ** -0.5)
    q_all = jnp.concatenate([q[:, h * DH_X:(h + 1) * DH_X] for h in range(H_X)], axis=0)
    k2 = mk_ref[...].reshape(n_keys, DH_X)
    v2 = mv_ref[...].reshape(n_keys, DH_X)
    s = _bdot_nt(q_all, k2) + bias_sc[...]
    m = jnp.max(s, axis=-1, keepdims=True)
    p = jnp.exp(s - m)
    l = jnp.sum(p, axis=-1, keepdims=True)
    o = _bdot(p, v2) / l
    o_ref[...] = jnp.concatenate([o[h * rows:(h + 1) * rows, :] for h in range(H_X)], axis=-1).astype(o_ref.dtype)


def cross_attn_native(qx, mem_k, mem_v, layer, nb, t, row0, prev):
    n_tok, d = qx.shape
    n_mem = mem_k.shape[2]
    seqs = XATTN_SAMPLE_SEQS
    rows = seqs * t
    rb0 = row0 // rows
    mem_spec = pl.BlockSpec((None, seqs, n_mem, H_X, DH_X), lambda b: (layer, b, 0, 0, 0))
    return pl.pallas_call(
        functools.partial(_xattn_native_kernel, tq=t, seqs=seqs, n_mem=n_mem),
        grid=(nb // seqs,),
        in_specs=[pl.BlockSpec((rows, d), lambda b: (rb0 + b, 0)), mem_spec, mem_spec,
                  pl.BlockSpec(memory_space=pl.ANY)],
        out_specs=pl.BlockSpec((rows, d), lambda b: (rb0 + b, 0)),
        out_shape=jax.ShapeDtypeStruct((n_tok, d), prev.dtype),
        scratch_shapes=[pltpu.VMEM((H_X * rows, seqs * n_mem * H_X), F32)],
        input_output_aliases={3: 0},
        compiler_params=_cparams("arbitrary"),
        name="cross_attn_sample",
    )(qx, mem_k, mem_v, prev)


def cross_attn(qx, mk, mv, layer, n_mem, nb, t, out_dtype):
    n_tok, d = qx.shape
    tq = XATTN_TILE
    nq = t // tq
    mem_spec = pl.BlockSpec((None, H_X, n_mem, DH_X), lambda b, i: (layer, 0, b, 0))
    return pl.pallas_call(
        _xattn_kernel,
        grid=(nb, nq),
        in_specs=[pl.BlockSpec((tq, d), lambda b, i: (b * nq + i, 0)), mem_spec, mem_spec],
        out_specs=pl.BlockSpec((tq, d), lambda b, i: (b * nq + i, 0)),
        out_shape=jax.ShapeDtypeStruct((n_tok, d), out_dtype),
        compiler_params=_cparams("parallel", "arbitrary"),
        name="cross_attn_prompt",
    )(qx, mk, mv)


def kernel(x_prompt, x_sample, mem_prompt, cache_k, cache_v, page_table, cache_mem_k, cache_mem_v,
           state_mlstm_c, state_mlstm_n, state_mlstm_m, state_ssm_re, state_ssm_im,
           ln_g, ln_b, ffn1_wg, ffn1_wu, ffn1_wd, ffn2_wg, ffn2_wu, ffn2_wd, w_in, b_in,
           mlstm_norm_g, diff_lam, diff_norm_g, ssm_a_re, ssm_a_im, ssm_log_dt, ssm_b_re, ssm_b_im,
           ssm_c_re, ssm_c_im, ssm_d, ssm_glu_w, ssm_glu_b, w_out, cross_wq, cross_wk, cross_wv, cross_wo):
    bp, tp, d = x_prompt.shape
    bs, ts, _ = x_sample.shape
    depth = ln_g.shape[0]
    n_mem = mem_prompt.shape[1]
    n_p = bp * tp
    gp = G_C * P_C
    alpha = (2.0 * depth) ** 0.25

    cast = lambda w: w.astype(BF16)
    ffn1_wg, ffn1_wu, ffn1_wd = cast(ffn1_wg), cast(ffn1_wu), cast(ffn1_wd)
    ffn2_wg, ffn2_wu, ffn2_wd = cast(ffn2_wg), cast(ffn2_wu), cast(ffn2_wd)
    w_out_b, wq_b, wo_b, glu_w_b = cast(w_out), cast(cross_wq), cast(cross_wo), cast(ssm_glu_w)
    w_in_p, b_in_p = _pack_w_in(w_in, b_in)
    lam_re, lam_im, bb_re, bb_im = s5_discretize(ssm_a_re, ssm_a_im, ssm_log_dt, ssm_b_re, ssm_b_im)
    bb = jnp.concatenate([_block_diag_in(bb_re), _block_diag_in(bb_im)], axis=-1).astype(BF16)
    cc = jnp.concatenate([_block_diag_out(ssm_c_re), -_block_diag_out(ssm_c_im)], axis=1).astype(BF16)
    alibi_q, alibi_k = _alibi_tables(tp)

    p_mk, p_mv, p_mkh, p_mvh = mem_kv(mem_prompt.reshape(bp * n_mem, d), cast(cross_wk), cast(cross_wv))
    p_mk = p_mk.reshape(depth, bp, n_mem, H_X, DH_X)
    p_mv = p_mv.reshape(depth, bp, n_mem, H_X, DH_X)

    x = None
    zeros_s = jnp.zeros((bp, gp), F32)
    p_k = jnp.zeros((depth, n_p, H_B, DV_B), F32)
    p_v = jnp.zeros((depth, n_p, H_B, DV_B), F32)
    p_st, s_st = [], []
    for l in range(depth):
        lam_init = 0.8 - 0.6 * math.exp(-0.3 * l)
        srcs = [x_prompt.reshape(n_p, d), x_sample.reshape(bs * ts, d)] if l == 0 else [x]
        x = ffn_ln(srcs, ffn1_wg, ffn1_wu, ffn1_wd, l, ln_g[l, 0], ln_b[l, 0], alpha)
        za, zb, zc, zg, p_k, p_v, s_k, s_v = proj_in(x, w_in_p, b_in_p, p_k, p_v, l, n_p)

        ya, pc, pn, pm = mlstm(za, zg, mlstm_norm_g[l], bp, tp, 0)
        ya, sc, sn, sm = mlstm(za, zg, mlstm_norm_g[l], bs, ts, n_p,
                               init=(state_mlstm_c, state_mlstm_n[l], state_mlstm_m[l]), layer=l, prev=ya)

        yb = diff_attn_prompt(zb, alibi_q, alibi_k, diff_lam[l], diff_norm_g[l], bp, tp, lam_init)
        yb = diff_attn_sample(zb, cache_k, cache_v, l, page_table, diff_lam[l], diff_norm_g[l],
                              bs, ts, n_p, lam_init, yb)

        s5_args = (lam_re, lam_im, bb, cc, l, ssm_d[l], glu_w_b, ssm_glu_b[l])
        yc, psr, psi = s5_mixer(zc, *s5_args, zeros_s, zeros_s, bp, tp, 0)
        yc, ssr, ssi = s5_mixer(zc, *s5_args, state_ssm_re[l].reshape(bs, gp), state_ssm_im[l].reshape(bs, gp),
                                bs, ts, n_p, prev=yc)

        x, qx = out_ln([ya, yb, yc], w_out_b, l, x, ln_g[l, 1], ln_b[l, 1], alpha, w_next=wq_b)
        o = cross_attn(qx, p_mkh, p_mvh, l, n_mem, bp, tp, BF16)
        o = cross_attn_native(qx, cache_mem_k, cache_mem_v, l, bs, ts, n_p, o)
        x = out_ffn_ln(o, wo_b, ffn2_wg, ffn2_wu, ffn2_wd, l, x, ln_g[l, 2], ln_b[l, 2], ln_g[l, 3], ln_b[l, 3], alpha,
                       split_rows=n_p if l == depth - 1 else None)

        p_st.append((pc, pn, pm, psr, psi))
        s_st.append((s_k.reshape(bs, ts, H_B, DV_B), s_v.reshape(bs, ts, H_B, DV_B), sc, sn, sm, ssr, ssi))

    p_c, p_n, p_m, p_sr, p_si = [jnp.stack(a) for a in zip(*p_st)]
    s_k, s_v, s_c, s_n, s_m, s_sr, s_si = [jnp.stack(a) for a in zip(*s_st)]
    p_k = p_k.reshape(depth, bp, tp, H_B, DV_B)
    p_v = p_v.reshape(depth, bp, tp, H_B, DV_B)
    yp = x[0].reshape(bp, tp, d)
    ys = x[1].reshape(bs, ts, d)
    return (yp, ys, p_k, p_v, p_mk, p_mv, p_c, p_n, p_m, p_sr, p_si, s_k, s_v, s_c, s_n, s_m, s_sr, s_si)
```

```python
import functools
import math

import jax
import jax.numpy as jnp
from jax import lax
from jax.experimental import pallas as pl
from jax.experimental.pallas import tpu as pltpu

F32 = jnp.float32
BF16 = jnp.bfloat16

D_MODEL = 1024
PAGE_SIZE = 128
W_A = D_MODEL // 4
W_B = D_MODEL // 2
W_C = D_MODEL - W_A - W_B
H_A = 4
DH_A = W_A // H_A
H_B = 4
DV_B = W_B // H_B
DK_B = DV_B // 2
GC = 16
G_C = W_C // GC
P_C = 64
H_X = 4
DH_X = D_MODEL // H_X
LN_EPS = 1e-5
NORM_EPS = 1e-6

OFF_AQ = 0
OFF_AK = OFF_AQ + W_A
OFF_AV = OFF_AK + W_A
OFF_AO = OFF_AV + W_A
OFF_AI = OFF_AO + W_A
OFF_AF = OFF_AI + H_A
OFF_BQ = OFF_AF + H_A
OFF_BK = OFF_BQ + H_B * 2 * DK_B
OFF_BV = OFF_BK + H_B * 2 * DK_B
OFF_CU = OFF_BV + W_B
N_IN = OFF_CU + W_C

V7X_LANES = 128
V7X_SUBLANES = 8
V7X_BF16_ROWS = 2 * V7X_SUBLANES
V7X_VMEM_LIMIT_BYTES = 56 * 1024 * 1024
V7X_VMEM_LIMIT_LARGE_BYTES = 60 * 1024 * 1024

TOKEN_TILE = 1024
FF_TILE = 256
PROJ_TILE = 512
ATTN_TILE = 512
ATTN_ROWS = 256
XATTN_TILE = 512
S5_TILE = 512
PAGES_PER_STEP = 16
XATTN_SAMPLE_SEQS = 4
S5_SAMPLE_SEQS = 16
MLSTM_PROMPT_CHUNK = 512
MLSTM_UNITS = 16
ALIBI_SPLIT = 64

NEG_INF = float("-inf")


def _cparams(*sem, vmem_limit=V7X_VMEM_LIMIT_BYTES):
    return pltpu.CompilerParams(dimension_semantics=sem, vmem_limit_bytes=vmem_limit)


def _bdot(a, b):
    return jnp.dot(a.astype(BF16), b.astype(BF16), preferred_element_type=F32)


def _bdot_nt(a, b):
    return lax.dot_general(a.astype(BF16), b.astype(BF16), (((1,), (1,)), ((), ())),
                           preferred_element_type=F32)


def _tile_lanes(x, n):
    return x if n == 1 else jnp.concatenate([x] * n, axis=1)


def _layer_norm(y, g, b):
    mu = jnp.mean(y, axis=-1, keepdims=True)
    yc = y - mu
    var = jnp.mean(yc * yc, axis=-1, keepdims=True)
    return yc * lax.rsqrt(var + LN_EPS) * g + b


def _swiglu(xb, wg_ref, wu_ref, wd_ref, h_sc, tf):
    for c in range(wg_ref.shape[1] // tf):
        cols = slice(c * tf, (c + 1) * tf)
        hg = jnp.dot(xb, wg_ref[:, cols], preferred_element_type=F32)
        hu = jnp.dot(xb, wu_ref[:, cols], preferred_element_type=F32)
        h_sc[:, cols] = ((hg * jax.nn.sigmoid(hg)) * hu).astype(BF16)
    return jnp.dot(h_sc[...], wd_ref[...], preferred_element_type=F32)


def _ffn_ln_kernel(*refs, alpha, tf, n_src, first_tiles):
    srcs = refs[:n_src]
    wg_ref, wu_ref, wd_ref, g_ref, b_ref, o_ref, h_sc = refs[n_src:]

    def run(x_ref):
        x = x_ref[...]
        ff = _swiglu(x.astype(BF16), wg_ref, wu_ref, wd_ref, h_sc, tf)
        o_ref[...] = _layer_norm(alpha * x + 0.5 * ff, g_ref[...], b_ref[...])

    if n_src == 1:
        run(srcs[0])
    else:
        i = pl.program_id(0)
        pl.when(i < first_tiles)(lambda: run(srcs[0]))
        pl.when(i >= first_tiles)(lambda: run(srcs[1]))


def ffn_ln(xs, wg, wu, wd, layer, g, b, alpha):
    d = xs[0].shape[1]
    dff = wg.shape[2]
    tm, tf = TOKEN_TILE, FF_TILE
    first = xs[0].shape[0] // tm
    n = sum(x.shape[0] for x in xs)
    src_specs = [pl.BlockSpec((tm, d), lambda i: (jnp.minimum(i, first - 1), 0))]
    if len(xs) == 2:
        src_specs.append(pl.BlockSpec((tm, d), lambda i: (jnp.maximum(i - first, 0), 0)))
    resident = pl.Buffered(1)
    return pl.pallas_call(
        functools.partial(_ffn_ln_kernel, alpha=alpha, tf=tf, n_src=len(xs), first_tiles=first),
        grid=(n // tm,),
        in_specs=src_specs + [
            pl.BlockSpec((None, d, dff), lambda i: (layer, 0, 0), pipeline_mode=resident),
            pl.BlockSpec((None, d, dff), lambda i: (layer, 0, 0), pipeline_mode=resident),
            pl.BlockSpec((None, dff, d), lambda i: (layer, 0, 0), pipeline_mode=resident),
            pl.BlockSpec((1, d), lambda i: (0, 0)),
            pl.BlockSpec((1, d), lambda i: (0, 0)),
        ],
        out_specs=pl.BlockSpec((tm, d), lambda i: (i, 0)),
        out_shape=jax.ShapeDtypeStruct((n, d), F32),
        scratch_shapes=[pltpu.VMEM((tm, dff), BF16)],
        compiler_params=_cparams("arbitrary"),
        name="ffn_ln",
    )(*xs, wg, wu, wd, g.reshape(1, d), b.reshape(1, d))


def _out_ffn_ln_kernel(o_ref, wo_ref, x_ref, g1_ref, b1_ref, wg_ref, wu_ref, wd_ref, g2_ref, b2_ref, *rest,
                       alpha, tf, split):
    outs, h_sc = rest[:-1], rest[-1]
    y1 = _layer_norm(alpha * x_ref[...] + jnp.dot(o_ref[...].astype(BF16), wo_ref[...], preferred_element_type=F32),
                     g1_ref[...], b1_ref[...])
    ff = _swiglu(y1.astype(BF16), wg_ref, wu_ref, wd_ref, h_sc, tf)
    y2 = _layer_norm(alpha * y1 + 0.5 * ff, g2_ref[...], b2_ref[...])
    if split is None:
        outs[0][...] = y2
    else:
        i = pl.program_id(0)

        @pl.when(i < split)
        def _():
            outs[0][...] = y2

        @pl.when(i >= split)
        def _():
            outs[1][...] = y2


def out_ffn_ln(o, wo, wg, wu, wd, layer, x, g1, b1, g2, b2, alpha, split_rows=None):
    n, d = x.shape
    dff = wg.shape[2]
    tm, tf = TOKEN_TILE, FF_TILE
    resident = pl.Buffered(1)
    by_layer = lambda i: (layer, 0, 0)
    vec = pl.BlockSpec((1, d), lambda i: (0, 0))
    if split_rows is None:
        split = None
        out_specs = [pl.BlockSpec((tm, d), lambda i: (i, 0))]
        out_shape = [jax.ShapeDtypeStruct((n, d), F32)]
    else:
        split = split_rows // tm
        out_specs = [pl.BlockSpec((tm, d), lambda i: (jnp.minimum(i, split - 1), 0)),
                     pl.BlockSpec((tm, d), lambda i: (jnp.maximum(i - split, 0), 0))]
        out_shape = [jax.ShapeDtypeStruct((split_rows, d), F32), jax.ShapeDtypeStruct((n - split_rows, d), F32)]
    res = pl.pallas_call(
        functools.partial(_out_ffn_ln_kernel, alpha=alpha, tf=tf, split=split),
        grid=(n // tm,),
        in_specs=[
            pl.BlockSpec((tm, d), lambda i: (i, 0)),
            pl.BlockSpec((None, d, d), by_layer, pipeline_mode=resident),
            pl.BlockSpec((tm, d), lambda i: (i, 0)),
            vec, vec,
            pl.BlockSpec((None, d, dff), by_layer, pipeline_mode=resident),
            pl.BlockSpec((None, d, dff), by_layer, pipeline_mode=resident),
            pl.BlockSpec((None, dff, d), by_layer, pipeline_mode=resident),
            vec, vec,
        ],
        out_specs=out_specs,
        out_shape=out_shape,
        scratch_shapes=[pltpu.VMEM((tm, dff), BF16)],
        compiler_params=_cparams("arbitrary", vmem_limit=V7X_VMEM_LIMIT_LARGE_BYTES),
        name="out_ffn_ln",
    )(o, wo, x, g1.reshape(1, d), b1.reshape(1, d), wg, wu, wd, g2.reshape(1, d), b2.reshape(1, d))
    return res[0] if split_rows is None else res


N_ZA = 4 * W_A
N_ZB = 3 * W_B
N_ZC = W_C
N_ZG = V7X_LANES
N_PROJ = N_ZA + N_ZB + N_ZC + N_ZG


def _proj_in_kernel(x_ref, w_ref, b_ref, _kp_prev, _vp_prev, za_ref, zb_ref, zc_ref, zg_ref,
                    kp_ref, vp_ref, ks_ref, vs_ref, kv_sc, *, prompt_tiles):
    i = pl.program_id(0)
    xb = x_ref[...].astype(BF16)
    off = 0
    for ref in (za_ref, zb_ref, zc_ref, zg_ref):
        width = ref.shape[1]
        z = jnp.dot(xb, w_ref[:, off:off + width], preferred_element_type=F32) + b_ref[:, off:off + width]
        ref[...] = z.astype(ref.dtype)
        if ref is zb_ref:
            kv_sc[...] = z[:, W_B:]
        off += width

    def write_kv(k_out, v_out):
        for h in range(H_B):
            k_out[:, h, :] = kv_sc[:, h * DV_B:(h + 1) * DV_B]
            v_out[:, h, :] = kv_sc[:, W_B + h * DV_B:W_B + (h + 1) * DV_B]

    @pl.when(i < prompt_tiles)
    def _():
        write_kv(kp_ref, vp_ref)

    @pl.when(i >= prompt_tiles)
    def _():
        write_kv(ks_ref, vs_ref)


def proj_in(x, w, b, kp_prev, vp_prev, layer, n_prompt):
    n, d = x.shape
    tm = PROJ_TILE
    pt = n_prompt // tm
    widths = (N_ZA, N_ZB, N_ZC, N_ZG)
    kv_blk = (tm, H_B, DV_B)
    p_map = lambda i: (layer, jnp.minimum(i, pt - 1), 0, 0)
    s_map = lambda i: (jnp.maximum(i - pt, 0), 0, 0)
    kv_s = jax.ShapeDtypeStruct((n - n_prompt, H_B, DV_B), F32)
    return pl.pallas_call(
        functools.partial(_proj_in_kernel, prompt_tiles=pt),
        grid=(n // tm,),
        in_specs=[
            pl.BlockSpec((tm, d), lambda i: (i, 0)),
            pl.BlockSpec((None, d, N_PROJ), lambda i: (layer, 0, 0)),
            pl.BlockSpec((None, 1, N_PROJ), lambda i: (layer, 0, 0)),
            pl.BlockSpec(memory_space=pl.ANY),
            pl.BlockSpec(memory_space=pl.ANY),
        ],
        out_specs=[pl.BlockSpec((tm, wd), lambda i: (i, 0)) for wd in widths]
        + [pl.BlockSpec((None,) + kv_blk, p_map)] * 2 + [pl.BlockSpec(kv_blk, s_map)] * 2,
        out_shape=[jax.ShapeDtypeStruct((n, wd), BF16 if wd == N_ZB else F32) for wd in widths]
        + [jax.ShapeDtypeStruct(kp_prev.shape, F32)] * 2 + [kv_s, kv_s],
        scratch_shapes=[pltpu.VMEM((tm, 2 * W_B), F32)],
        input_output_aliases={3: 4, 4: 5},
        compiler_params=_cparams("arbitrary"),
        name="proj_in",
    )(x, w, b, kp_prev, vp_prev)


def _pack_w_in(w_in, b_in):
    def cols(a):
        pad = jnp.zeros(a.shape[:-1] + (N_ZG - 2 * H_A,), a.dtype)
        return jnp.concatenate([a[..., OFF_AQ:OFF_AI], a[..., OFF_BQ:OFF_CU], a[..., OFF_CU:N_IN],
                                a[..., OFF_AI:OFF_BQ], pad], axis=-1)
    return cols(w_in).astype(BF16), cols(b_in)[:, None, :]


def _mem_kv_kernel(x_ref, wk_ref, wv_ref, mk_ref, mv_ref, mkh_ref, mvh_ref):
    xb = x_ref[...].astype(BF16)
    for w_ref, o_ref, oh_ref in ((wk_ref, mk_ref, mkh_ref), (wv_ref, mv_ref, mvh_ref)):
        r = jnp.dot(xb, w_ref[...], preferred_element_type=F32)
        for h in range(H_X):
            o_ref[:, h, :] = r[:, h * DH_X:(h + 1) * DH_X]
            oh_ref[h] = r[:, h * DH_X:(h + 1) * DH_X].astype(BF16)


def mem_kv(x, wk, wv):
    m, d = x.shape
    depth = wk.shape[0]
    tm = min(m, TOKEN_TILE)
    shp = jax.ShapeDtypeStruct((depth, m, H_X, DH_X), F32)
    shp_h = jax.ShapeDtypeStruct((depth, H_X, m, DH_X), BF16)
    w_spec = pl.BlockSpec((None, d, d), lambda i, l: (l, 0, 0))
    o_spec = pl.BlockSpec((None, tm, H_X, DH_X), lambda i, l: (l, i, 0, 0))
    oh_spec = pl.BlockSpec((None, H_X, tm, DH_X), lambda i, l: (l, 0, i, 0))
    return pl.pallas_call(
        _mem_kv_kernel,
        grid=(m // tm, depth),
        in_specs=[pl.BlockSpec((tm, d), lambda i, l: (i, 0)), w_spec, w_spec],
        out_specs=[o_spec, o_spec, oh_spec, oh_spec],
        out_shape=[shp, shp, shp_h, shp_h],
        compiler_params=_cparams("parallel", "arbitrary"),
        name="mem_kv",
    )(x, wk, wv)


def _out_ln_kernel(*refs, n_parts, alpha, has_next):
    parts = refs[:n_parts]
    w_ref, x_ref, g_ref, b_ref = refs[n_parts:n_parts + 4]
    rest = refs[n_parts + 4:]
    if has_next:
        wn_ref, o_ref, q_ref = rest
    else:
        (o_ref,) = rest
    acc = None
    off = 0
    for p in parts:
        width = p.shape[1]
        t = jnp.dot(p[...].astype(BF16), w_ref[off:off + width, :], preferred_element_type=F32)
        acc = t if acc is None else acc + t
        off += width
    y = _layer_norm(alpha * x_ref[...] + acc, g_ref[...], b_ref[...])
    o_ref[...] = y
    if has_next:
        q_ref[...] = jnp.dot(y.astype(BF16), wn_ref[...], preferred_element_type=F32).astype(q_ref.dtype)


def out_ln(parts, w, layer, x, g, b, alpha, w_next=None):
    n, d = x.shape
    tm = PROJ_TILE
    has_next = w_next is not None
    in_specs = [pl.BlockSpec((tm, p.shape[1]), lambda i: (i, 0)) for p in parts]
    in_specs += [
        pl.BlockSpec((None,) + w.shape[1:], lambda i: (layer, 0, 0)),
        pl.BlockSpec((tm, d), lambda i: (i, 0)),
        pl.BlockSpec((1, d), lambda i: (0, 0)),
        pl.BlockSpec((1, d), lambda i: (0, 0)),
    ]
    args = list(parts) + [w, x, g.reshape(1, d), b.reshape(1, d)]
    out_specs = [pl.BlockSpec((tm, d), lambda i: (i, 0))]
    out_shape = [jax.ShapeDtypeStruct((n, d), F32)]
    if has_next:
        in_specs.append(pl.BlockSpec((None,) + w_next.shape[1:], lambda i: (layer, 0, 0)))
        args.append(w_next)
        out_specs.append(pl.BlockSpec((tm, w_next.shape[2]), lambda i: (i, 0)))
        out_shape.append(jax.ShapeDtypeStruct((n, w_next.shape[2]), BF16))
    res = pl.pallas_call(
        functools.partial(_out_ln_kernel, n_parts=len(parts), alpha=alpha, has_next=has_next),
        grid=(n // tm,),
        in_specs=in_specs,
        out_specs=out_specs,
        out_shape=out_shape,
        compiler_params=_cparams("parallel"),
        name="out_ln",
    )(*args)
    return res if has_next else res[0]


def _log_sigmoid(x):
    return jnp.minimum(x, 0.0) - jnp.log1p(jnp.exp(-jnp.abs(x)))


def _mlstm_prompt_kernel(za_ref, zg_ref, g_ref, ya_ref, c1_ref, n1_ref, m1_ref, c_sc, n_sc, m_sc, *, chunk):
    L = chunk
    ti = pl.program_id(1)

    @pl.when(ti == 0)
    def _():
        c_sc[...] = jnp.zeros_like(c_sc)
        n_sc[...] = jnp.zeros_like(n_sc)
        m_sc[...] = jnp.zeros_like(m_sc)

    row = lax.broadcasted_iota(jnp.int32, (L, L), 0)
    col = lax.broadcasted_iota(jnp.int32, (L, L), 1)
    causal = col <= row
    tril = causal.astype(F32)
    sel_r = lax.broadcasted_iota(jnp.int32, (V7X_SUBLANES, V7X_LANES), 0)
    sel_c = lax.broadcasted_iota(jnp.int32, (V7X_SUBLANES, V7X_LANES), 1)
    sel = (sel_r == sel_c).astype(F32)
    lane_g = lax.broadcasted_iota(jnp.int32, (L, N_ZG), 1)
    lane_m = lax.broadcasted_iota(jnp.int32, (1, V7X_LANES), 1)
    rep = lambda col_: jnp.broadcast_to(col_, (L, V7X_LANES))
    wide = lambda x, n: _tile_lanes(x, n // V7X_LANES) if n > V7X_LANES else x[:, :n]

    gates = zg_ref[...]
    gl = jnp.where(lane_g < H_A, gates, _log_sigmoid(gates))
    bcum = jnp.dot(tril, gl, precision=lax.Precision.HIGHEST, preferred_element_type=F32)
    mixed = jnp.where(lane_g < H_A, gates, bcum)
    t_rows = lax.dot_general(sel, mixed, (((1,), (1,)), ((), ())),
                             precision=lax.Precision.HIGHEST, preferred_element_type=F32)
    outs = []
    m_out = jnp.zeros((1, V7X_LANES), F32)
    for h in range(H_A):
        q = za_ref[:, h * DH_A:(h + 1) * DH_A]
        k = za_ref[:, W_A + h * DH_A:W_A + (h + 1) * DH_A] * (DH_A ** -0.5)
        v = za_ref[:, 2 * W_A + h * DH_A:2 * W_A + (h + 1) * DH_A]
        og = za_ref[:, 3 * W_A + h * DH_A:3 * W_A + (h + 1) * DH_A]
        ig_c = rep(gates[:, h:h + 1])
        b_c = rep(bcum[:, H_A + h:H_A + h + 1])
        ig_row = t_rows[h:h + 1, :]
        b_row = t_rows[H_A + h:H_A + h + 1, :]
        c, n, m_prev = c_sc[h], n_sc[h:h + 1, :], m_sc[:, h:h + 1]

        dmat = jnp.where(causal, wide(b_c, L) - b_row + ig_row, NEG_INF)
        a = rep(jnp.max(dmat, axis=-1, keepdims=True))
        s = _bdot_nt(q, k) * jnp.exp(dmat - wide(a, L))
        n_loc = _bdot(s, v)
        d_loc = rep(jnp.sum(s, axis=-1, keepdims=True))
        a_last = a[L - 1:L, :]
        b_last = b_c[L - 1:L, :]
        wk = jnp.exp(b_last - b_c + ig_c - a_last)[:, :DH_A]
        u_loc = lax.dot_general((wk * v).astype(BF16), k.astype(BF16), (((0,), (0,)), ((), ())),
                                preferred_element_type=F32)
        nu_loc = jnp.sum(wk * k, axis=0, keepdims=True)

        inter = b_c + m_prev
        m_row = jnp.maximum(inter, a)
        r = jnp.exp(a - m_row)
        w_inter = jnp.exp(inter - m_row)
        num = r[:, :DH_A] * n_loc + w_inter[:, :DH_A] * _bdot_nt(q, c)
        den = r * d_loc + w_inter * rep(jnp.sum(q * n, axis=-1, keepdims=True))
        hh = num / jnp.maximum(jnp.abs(den), jnp.exp(-m_row))[:, :DH_A]
        m_new = m_row[L - 1:L, 0:1]
        decay = jnp.exp(b_last[:, 0:1] + m_prev - m_new)
        e_loc = jnp.exp(a_last[:, 0:1] - m_new)
        c_sc[h] = decay * c + e_loc * u_loc
        n_sc[h:h + 1, :] = decay * n + e_loc * nu_loc
        m_out = jnp.where(lane_m == h, m_new, m_out)

        hn = hh * lax.rsqrt(jnp.mean(hh * hh, axis=-1, keepdims=True) + NORM_EPS)
        outs.append(jax.nn.sigmoid(og) * hn)
    ya_ref[...] = (jnp.concatenate(outs, axis=-1) * g_ref[...]).astype(ya_ref.dtype)
    m_sc[...] = m_out

    @pl.when(ti == pl.num_programs(1) - 1)
    def _():
        c1_ref[0] = c_sc[...]
        n1_ref[0] = n_sc[...]
        m1_ref[0] = m_sc[...]


def _mlstm_sample_kernel(za_ref, zg_ref, g_ref, c0_ref, n0_ref, m0_ref, _prev, ya_ref, c1_ref, n1_ref, m1_ref,
                         *, t_len, seqs):
    R = seqs * t_len
    row = lax.broadcasted_iota(jnp.int32, (R, R), 0)
    col = lax.broadcasted_iota(jnp.int32, (R, R), 1)
    mask = (row // t_len == col // t_len) & (col <= row)
    tril = mask.astype(F32)
    sel_r = lax.broadcasted_iota(jnp.int32, (V7X_SUBLANES, V7X_LANES), 0)
    sel_c = lax.broadcasted_iota(jnp.int32, (V7X_SUBLANES, V7X_LANES), 1)
    sel = (sel_r == sel_c).astype(F32)
    lane_g = lax.broadcasted_iota(jnp.int32, (R, N_ZG), 1)
    lane_m = lax.broadcasted_iota(jnp.int32, (seqs, 1, V7X_LANES), 2)
    rep = lambda c_: jnp.broadcast_to(c_, (R, V7X_LANES))
    wide = lambda x, n: _tile_lanes(x, n // V7X_LANES) if n > V7X_LANES else x[:, :n]
    per_seq = lambda x: x.reshape(seqs, t_len, x.shape[-1])
    last_rows = lambda x: jnp.broadcast_to(per_seq(x)[:, t_len - 1:t_len, :], (seqs, t_len, x.shape[-1])
                                           ).reshape(R, x.shape[-1])
    seq_rows = lambda x: jnp.broadcast_to(x, (seqs, t_len, x.shape[-1])).reshape(R, x.shape[-1])
    state_rows = lambda x: jnp.broadcast_to(x, (seqs, DH_A, x.shape[-1])).reshape(seqs * DH_A, x.shape[-1])
    own_r = lax.broadcasted_iota(jnp.int32, (R, seqs * DH_A), 0) // t_len
    own_c = lax.broadcasted_iota(jnp.int32, (R, seqs * DH_A), 1) // DH_A
    own = own_r == own_c
    odd_seq = (lax.broadcasted_iota(jnp.int32, (R, DH_A), 0) // t_len) % 2 == 1

    gates = zg_ref[...]
    gl = jnp.where(lane_g < H_A, gates, _log_sigmoid(gates))
    bcum = jnp.dot(tril, gl, precision=lax.Precision.HIGHEST, preferred_element_type=F32)
    mixed = jnp.where(lane_g < H_A, gates, bcum)
    t_rows = lax.dot_general(sel, mixed, (((1,), (1,)), ((), ())),
                             precision=lax.Precision.HIGHEST, preferred_element_type=F32)
    m0 = m0_ref[...]
    norm_g = g_ref[...]
    outs = []
    m_out = jnp.zeros((seqs, 1, V7X_LANES), F32)
    for h in range(H_A):
        q = za_ref[:, h * DH_A:(h + 1) * DH_A]
        k = za_ref[:, W_A + h * DH_A:W_A + (h + 1) * DH_A] * (DH_A ** -0.5)
        v = za_ref[:, 2 * W_A + h * DH_A:2 * W_A + (h + 1) * DH_A]
        og = za_ref[:, 3 * W_A + h * DH_A:3 * W_A + (h + 1) * DH_A]
        ig_c = rep(gates[:, h:h + 1])
        b_c = rep(bcum[:, H_A + h:H_A + h + 1])
        ig_row = t_rows[h:h + 1, :]
        b_row = t_rows[H_A + h:H_A + h + 1, :]
        m_prev = rep(seq_rows(m0)[:, h:h + 1])
        c_stack = c0_ref[:, h].reshape(seqs * DH_A, DH_A)
        n_rows = seq_rows(n0_ref[:, h:h + 1, :])

        dmat = jnp.where(mask, wide(b_c, R) - b_row + ig_row, NEG_INF)
        a = rep(jnp.max(dmat, axis=-1, keepdims=True))
        s = _bdot_nt(q, k) * jnp.exp(dmat - wide(a, R))
        n_loc = _bdot(s, v)
        d_loc = rep(jnp.sum(s, axis=-1, keepdims=True))
        a_last = last_rows(a)
        b_last = last_rows(b_c)
        wk = jnp.exp(b_last - b_c + ig_c - a_last)[:, :DH_A]
        vw = wk * v
        vw2 = jnp.concatenate([vw, vw], axis=1)
        u_stack = lax.dot_general(jnp.where(own, _tile_lanes(vw2, seqs // 2), 0.0).astype(BF16), k.astype(BF16),
                                  (((0,), (0,)), ((), ())), preferred_element_type=F32)
        nu = jnp.sum(per_seq(wk * k), axis=1, keepdims=True)

        inter = b_c + m_prev
        m_row = jnp.maximum(inter, a)
        r = jnp.exp(a - m_row)
        w_inter = jnp.exp(inter - m_row)
        x_all = jnp.where(own, _bdot_nt(q, c_stack), 0.0)
        fold = x_all[:, 0:V7X_LANES]
        for j in range(1, seqs * DH_A // V7X_LANES):
            fold = fold + x_all[:, j * V7X_LANES:(j + 1) * V7X_LANES]
        qc = jnp.where(odd_seq, fold[:, DH_A:], fold[:, :DH_A])
        num = r[:, :DH_A] * n_loc + w_inter[:, :DH_A] * qc
        den = r * d_loc + w_inter * rep(jnp.sum(q * n_rows, axis=-1, keepdims=True))
        hh = num / jnp.maximum(jnp.abs(den), jnp.exp(-m_row))[:, :DH_A]
        hn = hh * lax.rsqrt(jnp.mean(hh * hh, axis=-1, keepdims=True) + NORM_EPS)
        outs.append(jax.nn.sigmoid(og) * hn)

        m_new = per_seq(m_row)[:, t_len - 1:t_len, :]
        decay = jnp.exp(per_seq(b_c + m_prev)[:, t_len - 1:t_len, :] - m_new)
        e_loc = jnp.exp(per_seq(a)[:, t_len - 1:t_len, :] - m_new)
        c_new = (state_rows(decay)[:, :DH_A] * c_stack + state_rows(e_loc)[:, :DH_A] * u_stack)
        c1_ref[:, h] = c_new.reshape(seqs, DH_A, DH_A)
        n1_ref[:, h:h + 1, :] = decay[:, :, :DH_A] * n0_ref[:, h:h + 1, :] + e_loc[:, :, :DH_A] * nu
        m_out = jnp.where(lane_m == h, m_new, m_out)
    ya_ref[...] = (jnp.concatenate(outs, axis=-1) * norm_g).astype(ya_ref.dtype)
    m1_ref[...] = m_out


def _mlstm_out_shapes(n_tok, nb):
    return [
        jax.ShapeDtypeStruct((n_tok, W_A), BF16),
        jax.ShapeDtypeStruct((nb, H_A, DH_A, DH_A), F32),
        jax.ShapeDtypeStruct((nb, H_A, DH_A), F32),
        jax.ShapeDtypeStruct((nb, 1, V7X_LANES), F32),
    ]


def mlstm_prompt(za, zg, norm_g, nb, t):
    L = MLSTM_PROMPT_CHUNK
    nc = t // L
    row_map = lambda b, c: (b * nc + c, 0)
    st4 = lambda b, c: (b, 0, 0, 0)
    st3 = lambda b, c: (b, 0, 0)
    ya, c1, n1, m1 = pl.pallas_call(
        functools.partial(_mlstm_prompt_kernel, chunk=L),
        grid=(nb, nc),
        in_specs=[pl.BlockSpec((L, N_ZA), row_map), pl.BlockSpec((L, N_ZG), row_map),
                  pl.BlockSpec((1, W_A), lambda b, c: (0, 0))],
        out_specs=[pl.BlockSpec((L, W_A), row_map), pl.BlockSpec((1, H_A, DH_A, DH_A), st4),
                   pl.BlockSpec((1, H_A, DH_A), st3), pl.BlockSpec((1, 1, V7X_LANES), st3)],
        out_shape=_mlstm_out_shapes(za.shape[0], nb),
        scratch_shapes=[pltpu.VMEM((H_A, DH_A, DH_A), F32), pltpu.VMEM((H_A, DH_A), F32),
                        pltpu.VMEM((1, V7X_LANES), F32)],
        compiler_params=_cparams("parallel", "arbitrary"),
        name="mlstm_prompt",
    )(za, zg, norm_g.reshape(1, W_A))
    return ya, c1, n1, m1[:, 0, :H_A]


def mlstm_sample(za, zg, norm_g, nb, t, row0, c0, n0, m0, layer, prev):
    seqs = MLSTM_UNITS
    rows = seqs * t
    row_map = lambda b: (row0 // rows + b, 0)
    st4 = lambda b: (b, 0, 0, 0)
    st3 = lambda b: (b, 0, 0)
    m0p = jnp.pad(m0, ((0, 0), (0, V7X_LANES - H_A))).reshape(nb, 1, V7X_LANES)
    state_specs = [pl.BlockSpec((seqs, H_A, DH_A, DH_A), st4), pl.BlockSpec((seqs, H_A, DH_A), st3),
                   pl.BlockSpec((seqs, 1, V7X_LANES), st3)]
    ya, c1, n1, m1 = pl.pallas_call(
        functools.partial(_mlstm_sample_kernel, t_len=t, seqs=seqs),
        grid=(nb // seqs,),
        in_specs=[pl.BlockSpec((rows, N_ZA), row_map), pl.BlockSpec((rows, N_ZG), row_map),
                  pl.BlockSpec((1, W_A), lambda b: (0, 0)),
                  pl.BlockSpec((None, seqs, H_A, DH_A, DH_A), lambda b: (layer, b, 0, 0, 0))] + state_specs[1:]
        + [pl.BlockSpec(memory_space=pl.ANY)],
        out_specs=[pl.BlockSpec((rows, W_A), row_map)] + state_specs,
        out_shape=_mlstm_out_shapes(za.shape[0], nb),
        input_output_aliases={6: 0},
        compiler_params=_cparams("arbitrary"),
        name="mlstm_sample",
    )(za, zg, norm_g.reshape(1, W_A), c0, n0, m0p, prev)
    return ya, c1, n1, m1[:, 0, :H_A]


def _diff_lambda(lam_ref, lam_init):
    lp = lam_ref[...]
    d01 = jnp.sum(lp[0:1, :] * lp[1:2, :], axis=-1, keepdims=True)
    d23 = jnp.sum(lp[2:3, :] * lp[3:4, :], axis=-1, keepdims=True)
    return jnp.exp(d01) - jnp.exp(d23) + lam_init


def _alibi_slope(h):
    return jnp.where(h == 0, 2.0 ** -2, jnp.where(h == 1, 2.0 ** -4, jnp.where(h == 2, 2.0 ** -6, 2.0 ** -8)))


def _alibi_tables(t):
    slopes = (2.0 ** (-8.0 * jnp.arange(1, H_B + 1, dtype=F32) / H_B))[:, None]
    pos = jnp.arange(t, dtype=jnp.int32)
    hi = ((pos // ALIBI_SPLIT) * ALIBI_SPLIT).astype(F32)[None, :]
    lo = (pos % ALIBI_SPLIT).astype(F32)[None, :]
    ones = jnp.ones((H_B, t), F32)
    pad = jnp.zeros((H_B, t, DV_B - 4), F32)
    aq = jnp.concatenate([jnp.stack([slopes * ones, slopes * ones, -slopes * hi, -slopes * lo], axis=-1), pad], -1)
    ak = jnp.concatenate([jnp.stack([hi * ones, lo * ones, ones, ones], axis=-1), pad], -1)
    return aq.astype(BF16), ak.astype(BF16)


def _attn_prompt_kernel(lam_ref, g_ref, q_ref, aq_ref, k_ref, ak_ref, v_ref, o_ref,
                        qs_sc, m_sc, acc_sc, *, lam_init, tq, tk):
    i = pl.program_id(1)
    j = pl.program_id(2)
    rq = ATTN_ROWS

    @pl.when(j == 0)
    def _():
        lane = lax.broadcasted_iota(jnp.int32, (tq, DV_B), 1)
        for h in range(H_B):
            q = q_ref[:, h * DV_B:(h + 1) * DV_B] * (DK_B ** -0.5)
            qs_sc[h, 0:tq, 0:DV_B] = jnp.where(lane < DK_B, q, 0.0).astype(BF16)
            qs_sc[h, tq:2 * tq, 0:DV_B] = jnp.where(lane >= DK_B, q, 0.0).astype(BF16)
            qs_sc[h, 0:tq, DV_B:2 * DV_B] = aq_ref[h]
            qs_sc[h, tq:2 * tq, DV_B:2 * DV_B] = aq_ref[h]
        m_sc[...] = jnp.full_like(m_sc, NEG_INF)
        acc_sc[...] = jnp.zeros_like(acc_sc)

    def step(masked):
        ones = jnp.ones((tk, V7X_LANES), BF16)
        for h in range(H_B):
            kaug = jnp.concatenate([k_ref[:, h * DV_B:(h + 1) * DV_B].astype(BF16), ak_ref[h]], axis=1)
            vaug = jnp.concatenate([v_ref[:, h * DV_B:(h + 1) * DV_B].astype(BF16), ones], axis=1)
            for r in range(2 * tq // rq):
                rows = slice(r * rq, (r + 1) * rq)
                s = lax.dot_general(qs_sc[h, rows, :], kaug, (((1,), (1,)), ((), ())), preferred_element_type=F32)
                if masked:
                    qi = (r * rq) % tq + lax.broadcasted_iota(jnp.int32, (rq, tk), 0)
                    kj = lax.broadcasted_iota(jnp.int32, (rq, tk), 1)
                    s = jnp.where(kj <= qi, s, NEG_INF)
                m_old = m_sc[h, rows, :]
                m_new = jnp.maximum(m_old, jnp.max(s, axis=-1, keepdims=True))
                alpha = jnp.exp(m_old - m_new)
                p = jnp.exp(s - _tile_lanes(m_new, tk // V7X_LANES))
                acc_sc[h, rows, :] = (_tile_lanes(alpha, 2) * acc_sc[h, rows, :]
                                      + jnp.dot(p.astype(BF16), vaug, preferred_element_type=F32))
                m_sc[h, rows, :] = m_new

    @pl.when(j < i)
    def _():
        step(False)

    @pl.when(j == i)
    def _():
        step(True)

    @pl.when(j == pl.num_programs(2) - 1)
    def _():
        lam = _diff_lambda(lam_ref, lam_init)
        for h in range(H_B):
            o0 = acc_sc[h, 0:tq, 0:DV_B] / acc_sc[h, 0:tq, DV_B:2 * DV_B]
            o1 = acc_sc[h, tq:2 * tq, 0:DV_B] / acc_sc[h, tq:2 * tq, DV_B:2 * DV_B]
            ob = o0 - lam * o1
            on = ob * lax.rsqrt(jnp.mean(ob * ob, axis=-1, keepdims=True) + NORM_EPS)
            o_ref[:, h * DV_B:(h + 1) * DV_B] = on * g_ref[:, h * DV_B:(h + 1) * DV_B] * (1.0 - lam_init)


def diff_attn_prompt(zb, aq, ak, lam_p, norm_g, nb, t, lam_init):
    n_tok = zb.shape[0]
    tq = tk = ATTN_TILE
    nq = t // tq
    kv_map = lambda off: (lambda b, i, j: (b * nq + jnp.minimum(i, j), off))
    return pl.pallas_call(
        functools.partial(_attn_prompt_kernel, lam_init=lam_init, tq=tq, tk=tk),
        grid=(nb, nq, nq),
        in_specs=[
            pl.BlockSpec(lam_p.shape, lambda b, i, j: (0, 0)),
            pl.BlockSpec((1, W_B), lambda b, i, j: (0, 0)),
            pl.BlockSpec((tq, W_B), lambda b, i, j: (b * nq + i, 0)),
            pl.BlockSpec((H_B, tq, DV_B), lambda b, i, j: (0, i, 0)),
            pl.BlockSpec((tk, W_B), kv_map(1)),
            pl.BlockSpec((H_B, tk, DV_B), lambda b, i, j: (0, jnp.minimum(i, j), 0)),
            pl.BlockSpec((tk, W_B), kv_map(2)),
        ],
        out_specs=pl.BlockSpec((tq, W_B), lambda b, i, j: (b * nq + i, 0)),
        out_shape=jax.ShapeDtypeStruct((n_tok, W_B), F32),
        scratch_shapes=[pltpu.VMEM((H_B, 2 * tq, 2 * DV_B), BF16), pltpu.VMEM((H_B, 2 * tq, V7X_LANES), F32),
                        pltpu.VMEM((H_B, 2 * tq, 2 * DV_B), F32)],
        compiler_params=_cparams("parallel", "parallel", "arbitrary"),
        name="diff_attn_prompt",
    )(lam_p, norm_g.reshape(1, W_B), zb, aq, zb, ak, zb)


def _attn_sample_kernel(*refs, lam_init, past_len, t_new, pages_per_step):
    G = pages_per_step
    pt_ref, lam_ref, g_ref, q_ref, kn_ref, vn_ref = refs[:6]
    k_refs = refs[6:6 + G]
    v_refs = refs[6 + G:6 + 2 * G]
    _prev, o_ref, qs_sc, bias_sc, m_sc, l_sc, acc_sc = refs[6 + 2 * G:]
    del pt_ref
    ps = pl.program_id(1)
    rows_per_head = 2 * t_new
    n_rows = H_B * rows_per_head
    page_rows = PAGE_SIZE * H_B
    reps = page_rows // V7X_LANES

    r_lane = lax.broadcasted_iota(jnp.int32, (n_rows, V7X_LANES), 0)
    slope = _alibi_slope(r_lane // rows_per_head).astype(F32)

    @pl.when(ps == 0)
    def _():
        q = q_ref[...] * (DK_B ** -0.5)
        lane = lax.broadcasted_iota(jnp.int32, (t_new, DV_B), 1)
        for h in range(H_B):
            qh = q[:, h * DV_B:(h + 1) * DV_B]
            qs_sc[h * rows_per_head:(h + 1) * rows_per_head, :] = jnp.concatenate(
                [jnp.where(lane < DK_B, qh, 0.0), jnp.where(lane >= DK_B, qh, 0.0)], axis=0).astype(BF16)
        rr = lax.broadcasted_iota(jnp.int32, (n_rows, page_rows), 0)
        cc = lax.broadcasted_iota(jnp.int32, (n_rows, page_rows), 1)
        rel = cc // H_B - (past_len + rr % t_new)
        bias = _alibi_slope(rr // rows_per_head).astype(F32) * rel.astype(F32)
        bias_sc[...] = jnp.where(cc % H_B == rr // rows_per_head, bias, NEG_INF)
        m_sc[...] = jnp.full_like(m_sc, NEG_INF)
        l_sc[...] = jnp.zeros_like(l_sc)
        acc_sc[...] = jnp.zeros_like(acc_sc)

    qs = qs_sc[...]
    m, l, acc = m_sc[...], l_sc[...], acc_sc[...]
    s_pages = []
    m_new = m
    for g in range(G):
        base = ((ps * G + g) * PAGE_SIZE).astype(F32)
        s = _bdot_nt(qs, k_refs[g][...]) + (bias_sc[...] + _tile_lanes(slope * base, reps))
        m_new = jnp.maximum(m_new, jnp.max(s, axis=-1, keepdims=True))
        s_pages.append(s)
    alpha = jnp.exp(m - m_new)
    l = alpha * l
    acc = alpha * acc
    for g in range(G):
        p = jnp.exp(s_pages[g] - _tile_lanes(m_new, reps))
        l = l + jnp.sum(p, axis=-1, keepdims=True)
        acc = acc + _bdot(p, v_refs[g][...])
    m = m_new

    @pl.when(ps < pl.num_programs(1) - 1)
    def _():
        m_sc[...] = m
        l_sc[...] = l
        acc_sc[...] = acc

    @pl.when(ps == pl.num_programs(1) - 1)
    def _():
        rnd = lambda a: a.astype(BF16).astype(F32)
        kn = rnd(kn_ref[...])
        vn = rnd(vn_ref[...])
        qf = qs.astype(F32)
        sn = jnp.concatenate(
            [lax.dot_general(qf[h * rows_per_head:(h + 1) * rows_per_head, :], kn[:, h * DV_B:(h + 1) * DV_B],
                             (((1,), (1,)), ((), ())), preferred_element_type=F32) for h in range(H_B)],
            axis=0)
        rr = lax.broadcasted_iota(jnp.int32, (n_rows, t_new), 0)
        rel = lax.broadcasted_iota(jnp.int32, (n_rows, t_new), 1) - rr % t_new
        sn = jnp.where(rel <= 0, sn + _alibi_slope(rr // rows_per_head).astype(F32) * rel.astype(F32), NEG_INF)
        m_fin = jnp.maximum(m, jnp.max(sn, axis=-1, keepdims=True))
        alpha = jnp.exp(m - m_fin)
        pn = rnd(jnp.exp(sn - m_fin[:, 0:1]))
        l_fin = alpha * l + jnp.sum(pn, axis=-1, keepdims=True)
        pv = jnp.concatenate(
            [jnp.dot(pn[h * rows_per_head:(h + 1) * rows_per_head, :], vn[:, h * DV_B:(h + 1) * DV_B],
                     preferred_element_type=F32) for h in range(H_B)], axis=0)
        o = (alpha * acc + pv) / l_fin

        lam = _diff_lambda(lam_ref, lam_init)
        outs = []
        for h in range(H_B):
            o0 = o[h * rows_per_head:h * rows_per_head + t_new, :]
            o1 = o[h * rows_per_head + t_new:(h + 1) * rows_per_head, :]
            ob = o0 - lam * o1
            outs.append(ob * lax.rsqrt(jnp.mean(ob * ob, axis=-1, keepdims=True) + NORM_EPS))
        o_ref[...] = jnp.concatenate(outs, axis=-1) * g_ref[...] * (1.0 - lam_init)


def diff_attn_sample(zb, cache_k, cache_v, layer, page_table, lam_p, norm_g, nb, t_new, row0, lam_init, prev):
    n_tok = zb.shape[0]
    n_pages = page_table.shape[1]
    G = PAGES_PER_STEP
    rb0 = row0 // t_new
    n_rows = 2 * H_B * t_new
    page_rows = PAGE_SIZE * H_B
    ck = cache_k.reshape(cache_k.shape[0], cache_k.shape[1], page_rows, DV_B)
    cv = cache_v.reshape(cache_v.shape[0], cache_v.shape[1], page_rows, DV_B)

    def page_map(g):
        return lambda b, p, pt: (layer, pt[b, p * G + g], 0, 0)

    page_blk = (None, None, page_rows, DV_B)
    in_specs = [
        pl.BlockSpec(lam_p.shape, lambda b, p, pt: (0, 0)),
        pl.BlockSpec((1, W_B), lambda b, p, pt: (0, 0)),
        pl.BlockSpec((t_new, W_B), lambda b, p, pt: (rb0 + b, 0)),
        pl.BlockSpec((t_new, W_B), lambda b, p, pt: (rb0 + b, 1)),
        pl.BlockSpec((t_new, W_B), lambda b, p, pt: (rb0 + b, 2)),
    ]
    in_specs += [pl.BlockSpec(page_blk, page_map(g)) for g in range(G)]
    in_specs += [pl.BlockSpec(page_blk, page_map(g)) for g in range(G)]
    in_specs += [pl.BlockSpec(memory_space=pl.ANY)]
    args = [page_table, lam_p, norm_g.reshape(1, W_B), zb, zb, zb] + [ck] * G + [cv] * G + [prev]
    grid_spec = pltpu.PrefetchScalarGridSpec(
        num_scalar_prefetch=1,
        grid=(nb, n_pages // G),
        in_specs=in_specs,
        out_specs=pl.BlockSpec((t_new, W_B), lambda b, p, pt: (rb0 + b, 0)),
        scratch_shapes=[pltpu.VMEM((n_rows, DV_B), BF16), pltpu.VMEM((n_rows, page_rows), F32),
                        pltpu.VMEM((n_rows, V7X_LANES), F32), pltpu.VMEM((n_rows, V7X_LANES), F32),
                        pltpu.VMEM((n_rows, DV_B), F32)],
    )
    return pl.pallas_call(
        functools.partial(_attn_sample_kernel, lam_init=lam_init, past_len=n_pages * PAGE_SIZE,
                          t_new=t_new, pages_per_step=G),
        grid_spec=grid_spec,
        out_shape=jax.ShapeDtypeStruct((n_tok, W_B), F32),
        input_output_aliases={len(args) - 1: 0},
        compiler_params=_cparams("parallel", "arbitrary"),
        name="diff_attn_sample",
    )(*args)


def _s5_disc_kernel(are_ref, aim_ref, ldt_ref, bre_ref, bim_ref, lre_ref, lim_ref, bbre_ref, bbim_ref):
    a_re = are_ref[...]
    a_im = aim_ref[...]
    dt = jnp.exp(ldt_ref[...])
    mag = jnp.exp(a_re * dt)
    lb_re = mag * jnp.cos(a_im * dt)
    lb_im = mag * jnp.sin(a_im * dt)
    den = a_re * a_re + a_im * a_im
    xr = lb_re - 1.0
    fr = (xr * a_re + lb_im * a_im) / den
    fi = (lb_im * a_re - xr * a_im) / den
    lre_ref[...] = lb_re
    lim_ref[...] = lb_im
    b_re = bre_ref[...]
    b_im = bim_ref[...]
    bbre_ref[...] = fr * b_re - fi * b_im
    bbim_ref[...] = fr * b_im + fi * b_re


def s5_discretize(a_re, a_im, log_dt, b_re, b_im):
    depth = a_re.shape[0]
    gp = G_C * P_C
    flat = lambda a: a.reshape(depth, 1, gp)
    ldt = jnp.broadcast_to(log_dt[:, :, None], (depth, G_C, P_C)).reshape(depth, 1, gp)
    tr = lambda b: jnp.transpose(b, (0, 3, 1, 2)).reshape(depth, GC, gp)
    shp1 = jax.ShapeDtypeStruct((depth, 1, gp), F32)
    shpb = jax.ShapeDtypeStruct((depth, GC, gp), F32)
    return pl.pallas_call(_s5_disc_kernel, out_shape=[shp1, shp1, shpb, shpb], name="s5_discretize")(
        flat(a_re), flat(a_im), ldt, tr(b_re), tr(b_im))


def _block_diag_in(bb):
    depth, _, gp = bb.shape
    tiled = jnp.tile(bb, (1, G_C, 1)).reshape(depth, G_C, GC, gp)
    grp_r = jnp.arange(G_C)[:, None, None]
    grp_c = (jnp.arange(gp) // P_C)[None, None, :]
    return jnp.where(grp_r == grp_c, tiled, 0.0).reshape(depth, G_C * GC, gp)


def _block_diag_out(c):
    depth = c.shape[0]
    ct = jnp.transpose(c, (0, 1, 3, 2)).reshape(depth, G_C * P_C, GC)
    tiled = jnp.tile(ct, (1, 1, G_C))
    grp_r = (jnp.arange(G_C * P_C) // P_C)[:, None]
    grp_c = (jnp.arange(G_C * GC) // GC)[None, :]
    return jnp.where(grp_r == grp_c, tiled, 0.0)


def _cmul_add(a_re, a_im, x_re, x_im, y_re, y_im):
    return y_re + (a_re * x_re - a_im * x_im), y_im + (a_re * x_im + a_im * x_re)


def _block_scan(x_re, x_im, pw):
    for d, (a_re, a_im) in zip((1, 2, 4), pw):
        x_re, x_im = _cmul_add(a_re, a_im, pltpu.roll(x_re, d, 0), pltpu.roll(x_im, d, 0), x_re, x_im)
    return x_re, x_im


def _s5_kernel(*refs, rows, independent, aliased):
    (u_ref, lre_ref, lim_ref, bb_ref, cc_ref, d_ref, gw_ref, gb_ref, s0re_ref, s0im_ref) = refs[:10]
    rest = refs[10:]
    if aliased:
        rest = rest[1:]
    y_ref, s1re_ref, s1im_ref, st_sc, car_sc = rest
    gp = G_C * P_C
    nblk = rows // V7X_SUBLANES
    ti = pl.program_id(1)

    lam_re = lre_ref[...]
    lam_im = lim_ref[...]
    l2_re, l2_im = lam_re * lam_re - lam_im * lam_im, 2.0 * lam_re * lam_im
    l4_re, l4_im = l2_re * l2_re - l2_im * l2_im, 2.0 * l2_re * l2_im
    row8 = lax.broadcasted_iota(jnp.int32, (V7X_SUBLANES, gp), 0)
    pw = tuple((jnp.where(row8 >= d, a_re, 0.0), jnp.where(row8 >= d, a_im, 0.0))
               for d, (a_re, a_im) in zip((1, 2, 4), ((lam_re, lam_im), (l2_re, l2_im), (l4_re, l4_im))))
    pk_re, pk_im = _block_scan(jnp.where(row8 == 0, lam_re, 0.0), jnp.where(row8 == 0, lam_im, 0.0), pw)

    u = u_ref[...]
    st_sc[...] = _bdot(u, bb_ref[...])

    if not independent:
        @pl.when(ti == 0)
        def _():
            car_sc[0:1, :] = s0re_ref[...]
            car_sc[1:2, :] = s0im_ref[...]

    def body(bi, carry):
        r0 = pl.multiple_of(bi * V7X_SUBLANES, V7X_SUBLANES)
        x_re = st_sc[pl.ds(r0, V7X_SUBLANES), 0:gp]
        x_im = st_sc[pl.ds(r0, V7X_SUBLANES), gp:2 * gp]
        x_re, x_im = _block_scan(x_re, x_im, pw)
        if independent:
            c_re = s0re_ref[pl.ds(bi, 1), :]
            c_im = s0im_ref[pl.ds(bi, 1), :]
        else:
            c_re, c_im = carry
        s_re, s_im = _cmul_add(pk_re, pk_im, c_re, c_im, x_re, x_im)
        st_sc[pl.ds(r0, V7X_SUBLANES), 0:gp] = s_re
        st_sc[pl.ds(r0, V7X_SUBLANES), gp:2 * gp] = s_im
        last_re = s_re[V7X_SUBLANES - 1:V7X_SUBLANES, :]
        last_im = s_im[V7X_SUBLANES - 1:V7X_SUBLANES, :]
        if independent:
            s1re_ref[pl.ds(bi, 1), :] = last_re
            s1im_ref[pl.ds(bi, 1), :] = last_im
            return carry
        return last_re, last_im

    if independent:
        lax.fori_loop(0, nblk, body, 0)
    else:
        c_re, c_im = lax.fori_loop(0, nblk, body, (car_sc[0:1, :], car_sc[1:2, :]))
        car_sc[0:1, :] = c_re
        car_sc[1:2, :] = c_im

        @pl.when(ti == pl.num_programs(1) - 1)
        def _():
            s1re_ref[...] = c_re
            s1im_ref[...] = c_im

    y = _bdot(st_sc[...], cc_ref[...]) + d_ref[...] * u
    z = _bdot(jax.nn.gelu(y), gw_ref[...]) + gb_ref[...]
    y_ref[...] = (z[:, :W_C] * jax.nn.sigmoid(z[:, W_C:])).astype(y_ref.dtype)


def s5_mixer(zc, lam_re, lam_im, bb, cc, layer, d, glu_w, glu_b, s0_re, s0_im, nb, t, row0, prev=None):
    n_tok = zc.shape[0]
    gp = G_C * P_C
    independent = t == V7X_SUBLANES
    aliased = prev is not None
    if independent:
        seqs = S5_SAMPLE_SEQS
        rows = seqs * t
        grid = (nb // seqs, 1)
        st_spec = pl.BlockSpec((seqs, gp), lambda b, i: (b, 0))
    else:
        rows = S5_TILE
        grid = (nb, t // rows)
        st_spec = pl.BlockSpec((1, gp), lambda b, i: (b, 0))
    nt = grid[1]
    rb0 = row0 // rows
    row_map = lambda b, i: (rb0 + b * nt + i, 0)
    const = lambda b, i: (0, 0)
    by_layer = lambda b, i: (layer, 0, 0)
    in_specs = [
        pl.BlockSpec((rows, W_C), row_map),
        pl.BlockSpec((None, 1, gp), by_layer),
        pl.BlockSpec((None, 1, gp), by_layer),
        pl.BlockSpec((None,) + bb.shape[1:], by_layer),
        pl.BlockSpec((None,) + cc.shape[1:], by_layer),
        pl.BlockSpec((1, W_C), const),
        pl.BlockSpec((None,) + glu_w.shape[1:], by_layer),
        pl.BlockSpec((1, 2 * W_C), const),
        st_spec,
        st_spec,
    ]
    args = [zc, lam_re, lam_im, bb, cc, d.reshape(1, W_C), glu_w, glu_b.reshape(1, 2 * W_C), s0_re, s0_im]
    aliases = {}
    if aliased:
        in_specs.append(pl.BlockSpec(memory_space=pl.ANY))
        args.append(prev)
        aliases = {len(args) - 1: 0}
    if independent:
        s1_shape = jax.ShapeDtypeStruct((nb, gp), F32)
    else:
        s1_shape = jax.ShapeDtypeStruct((nb, 1, gp), F32)
        st_out = pl.BlockSpec((None, 1, gp), lambda b, i: (b, 0, 0))
    out_specs = [pl.BlockSpec((rows, W_C), row_map)] + ([st_spec, st_spec] if independent else [st_out, st_out])
    if not independent:
        args[8] = s0_re.reshape(nb, 1, gp)
        args[9] = s0_im.reshape(nb, 1, gp)
        in_specs[8] = in_specs[9] = pl.BlockSpec((None, 1, gp), lambda b, i: (b, 0, 0))
    yc, s1_re, s1_im = pl.pallas_call(
        functools.partial(_s5_kernel, rows=rows, independent=independent, aliased=aliased),
        grid=grid,
        in_specs=in_specs,
        out_specs=out_specs,
        out_shape=[jax.ShapeDtypeStruct((n_tok, W_C), BF16), s1_shape, s1_shape],
        scratch_shapes=[pltpu.VMEM((rows, 2 * gp), F32), pltpu.VMEM((V7X_SUBLANES, gp), F32)],
        input_output_aliases=aliases,
        compiler_params=_cparams("parallel", "arbitrary"),
        name="s5_sample" if independent else "s5_prompt",
    )(*args)
    return yc, s1_re.reshape(nb, G_C, P_C), s1_im.reshape(nb, G_C, P_C)


def _xattn_kernel(q_ref, mk_ref, mv_ref, o_ref):
    q = q_ref[...] * (DH_X ** -0.5)
    outs = []
    for h in range(H_X):
        s = _bdot_nt(q[:, h * DH_X:(h + 1) * DH_X], mk_ref[h])
        m = jnp.max(s, axis=-1, keepdims=True)
        p = jnp.exp(s - m)
        l = jnp.sum(p, axis=-1, keepdims=True)
        outs.append(_bdot(p, mv_ref[h]) / l)
    o_ref[...] = jnp.concatenate(outs, axis=-1).astype(o_ref.dtype)


def _xattn_native_kernel(q_ref, mk_ref, mv_ref, _prev, o_ref, bias_sc, *, tq, seqs, n_mem):
    rows = seqs * tq
    n_keys = seqs * n_mem * H_X

    @pl.when(pl.program_id(0) == 0)
    def _():
        rr = lax.broadcasted_iota(jnp.int32, (H_X * rows, n_keys), 0)
        cc = lax.broadcasted_iota(jnp.int32, (H_X * rows, n_keys), 1)
        own = (cc // (n_mem * H_X) == (rr % rows) // tq) & (cc % H_X == rr // rows)
        bias_sc[...] = jnp.where(own, 0.0, NEG_INF)

    q = q_ref[...] * (DH_X ** -0.5)
    q_all = jnp.concatenate([q[:, h * DH_X:(h + 1) * DH_X] for h in range(H_X)], axis=0)
    k2 = mk_ref[...].reshape(n_keys, DH_X)
    v2 = mv_ref[...].reshape(n_keys, DH_X)
    s = _bdot_nt(q_all, k2) + bias_sc[...]
    m = jnp.max(s, axis=-1, keepdims=True)
    p = jnp.exp(s - m)
    l = jnp.sum(p, axis=-1, keepdims=True)
    o = _bdot(p, v2) / l
    o_ref[...] = jnp.concatenate([o[h * rows:(h + 1) * rows, :] for h in range(H_X)], axis=-1).astype(o_ref.dtype)


def cross_attn_native(qx, mem_k, mem_v, layer, nb, t, row0, prev):
    n_tok, d = qx.shape
    n_mem = mem_k.shape[2]
    seqs = XATTN_SAMPLE_SEQS
    rows = seqs * t
    rb0 = row0 // rows
    mem_spec = pl.BlockSpec((None, seqs, n_mem, H_X, DH_X), lambda b: (layer, b, 0, 0, 0))
    return pl.pallas_call(
        functools.partial(_xattn_native_kernel, tq=t, seqs=seqs, n_mem=n_mem),
        grid=(nb // seqs,),
        in_specs=[pl.BlockSpec((rows, d), lambda b: (rb0 + b, 0)), mem_spec, mem_spec,
                  pl.BlockSpec(memory_space=pl.ANY)],
        out_specs=pl.BlockSpec((rows, d), lambda b: (rb0 + b, 0)),
        out_shape=jax.ShapeDtypeStruct((n_tok, d), prev.dtype),
        scratch_shapes=[pltpu.VMEM((H_X * rows, seqs * n_mem * H_X), F32)],
        input_output_aliases={3: 0},
        compiler_params=_cparams("arbitrary"),
        name="cross_attn_sample",
    )(qx, mem_k, mem_v, prev)


def cross_attn(qx, mk, mv, layer, n_mem, nb, t, out_dtype):
    n_tok, d = qx.shape
    tq = XATTN_TILE
    nq = t // tq
    mem_spec = pl.BlockSpec((None, H_X, n_mem, DH_X), lambda b, i: (layer, 0, b, 0))
    return pl.pallas_call(
        _xattn_kernel,
        grid=(nb, nq),
        in_specs=[pl.BlockSpec((tq, d), lambda b, i: (b * nq + i, 0)), mem_spec, mem_spec],
        out_specs=pl.BlockSpec((tq, d), lambda b, i: (b * nq + i, 0)),
        out_shape=jax.ShapeDtypeStruct((n_tok, d), out_dtype),
        compiler_params=_cparams("parallel", "arbitrary"),
        name="cross_attn_prompt",
    )(qx, mk, mv)


def kernel(x_prompt, x_sample, mem_prompt, cache_k, cache_v, page_table, cache_mem_k, cache_mem_v,
           state_mlstm_c, state_mlstm_n, state_mlstm_m, state_ssm_re, state_ssm_im,
           ln_g, ln_b, ffn1_wg, ffn1_wu, ffn1_wd, ffn2_wg, ffn2_wu, ffn2_wd, w_in, b_in,
           mlstm_norm_g, diff_lam, diff_norm_g, ssm_a_re, ssm_a_im, ssm_log_dt, ssm_b_re, ssm_b_im,
           ssm_c_re, ssm_c_im, ssm_d, ssm_glu_w, ssm_glu_b, w_out, cross_wq, cross_wk, cross_wv, cross_wo):
    bp, tp, d = x_prompt.shape
    bs, ts, _ = x_sample.shape
    depth = ln_g.shape[0]
    n_mem = mem_prompt.shape[1]
    n_p = bp * tp
    gp = G_C * P_C
    alpha = (2.0 * depth) ** 0.25

    cast = lambda w: w.astype(BF16)
    ffn1_wg, ffn1_wu, ffn1_wd = cast(ffn1_wg), cast(ffn1_wu), cast(ffn1_wd)
    ffn2_wg, ffn2_wu, ffn2_wd = cast(ffn2_wg), cast(ffn2_wu), cast(ffn2_wd)
    w_out_b, wq_b, wo_b, glu_w_b = cast(w_out), cast(cross_wq), cast(cross_wo), cast(ssm_glu_w)
    w_in_p, b_in_p = _pack_w_in(w_in, b_in)
    lam_re, lam_im, bb_re, bb_im = s5_discretize(ssm_a_re, ssm_a_im, ssm_log_dt, ssm_b_re, ssm_b_im)
    bb = jnp.concatenate([_block_diag_in(bb_re), _block_diag_in(bb_im)], axis=-1).astype(BF16)
    cc = jnp.concatenate([_block_diag_out(ssm_c_re), -_block_diag_out(ssm_c_im)], axis=1).astype(BF16)
    alibi_q, alibi_k = _alibi_tables(tp)

    p_mk, p_mv, p_mkh, p_mvh = mem_kv(mem_prompt.reshape(bp * n_mem, d), cast(cross_wk), cast(cross_wv))
    p_mk = p_mk.reshape(depth, bp, n_mem, H_X, DH_X)
    p_mv = p_mv.reshape(depth, bp, n_mem, H_X, DH_X)

    x = None
    zeros_s = jnp.zeros((bp, gp), F32)
    p_k = jnp.zeros((depth, n_p, H_B, DV_B), F32)
    p_v = jnp.zeros((depth, n_p, H_B, DV_B), F32)
    p_st, s_st = [], []
    for l in range(depth):
        lam_init = 0.8 - 0.6 * math.exp(-0.3 * l)
        srcs = [x_prompt.reshape(n_p, d), x_sample.reshape(bs * ts, d)] if l == 0 else [x]
        x = ffn_ln(srcs, ffn1_wg, ffn1_wu, ffn1_wd, l, ln_g[l, 0], ln_b[l, 0], alpha)
        za, zb, zc, zg, p_k, p_v, s_k, s_v = proj_in(x, w_in_p, b_in_p, p_k, p_v, l, n_p)

        ya, pc, pn, pm = mlstm_prompt(za, zg, mlstm_norm_g[l], bp, tp)
        ya, sc, sn, sm = mlstm_sample(za, zg, mlstm_norm_g[l], bs, ts, n_p,
                                      state_mlstm_c, state_mlstm_n[l], state_mlstm_m[l], l, ya)

        yb = diff_attn_prompt(zb, alibi_q, alibi_k, diff_lam[l], diff_norm_g[l], bp, tp, lam_init)
        yb = diff_attn_sample(zb, cache_k, cache_v, l, page_table, diff_lam[l], diff_norm_g[l],
                              bs, ts, n_p, lam_init, yb)

        s5_args = (lam_re, lam_im, bb, cc, l, ssm_d[l], glu_w_b, ssm_glu_b[l])
        yc, psr, psi = s5_mixer(zc, *s5_args, zeros_s, zeros_s, bp, tp, 0)
        yc, ssr, ssi = s5_mixer(zc, *s5_args, state_ssm_re[l].reshape(bs, gp), state_ssm_im[l].reshape(bs, gp),
                                bs, ts, n_p, prev=yc)

        x, qx = out_ln([ya, yb, yc], w_out_b, l, x, ln_g[l, 1], ln_b[l, 1], alpha, w_next=wq_b)
        o = cross_attn(qx, p_mkh, p_mvh, l, n_mem, bp, tp, BF16)
        o = cross_attn_native(qx, cache_mem_k, cache_mem_v, l, bs, ts, n_p, o)
        x = out_ffn_ln(o, wo_b, ffn2_wg, ffn2_wu, ffn2_wd, l, x, ln_g[l, 2], ln_b[l, 2], ln_g[l, 3], ln_b[l, 3], alpha,
                       split_rows=n_p if l == depth - 1 else None)

        p_st.append((pc, pn, pm, psr, psi))
        s_st.append((s_k.reshape(bs, ts, H_B, DV_B), s_v.reshape(bs, ts, H_B, DV_B), sc, sn, sm, ssr, ssi))

    p_c, p_n, p_m, p_sr, p_si = [jnp.stack(a) for a in zip(*p_st)]
    s_k, s_v, s_c, s_n, s_m, s_sr, s_si = [jnp.stack(a) for a in zip(*s_st)]
    p_k = p_k.reshape(depth, bp, tp, H_B, DV_B)
    p_v = p_v.reshape(depth, bp, tp, H_B, DV_B)
    yp = x[0].reshape(bp, tp, d)
    ys = x[1].reshape(bs, ts, d)
    return (yp, ys, p_k, p_v, p_mk, p_mv, p_c, p_n, p_m, p_sr, p_si, s_k, s_v, s_c, s_n, s_m, s_sr, s_si)
```

```python
import functools
import math

import jax
import jax.numpy as jnp
from jax import lax
from jax.experimental import pallas as pl
from jax.experimental.pallas import tpu as pltpu

F32 = jnp.float32
BF16 = jnp.bfloat16

D_MODEL = 1024
PAGE_SIZE = 128
W_A = D_MODEL // 4
W_B = D_MODEL // 2
W_C = D_MODEL - W_A - W_B
H_A = 4
DH_A = W_A // H_A
H_B = 4
DV_B = W_B // H_B
DK_B = DV_B // 2
GC = 16
G_C = W_C // GC
P_C = 64
H_X = 4
DH_X = D_MODEL // H_X
LN_EPS = 1e-5
NORM_EPS = 1e-6

OFF_AQ = 0
OFF_AK = OFF_AQ + W_A
OFF_AV = OFF_AK + W_A
OFF_AO = OFF_AV + W_A
OFF_AI = OFF_AO + W_A
OFF_AF = OFF_AI + H_A
OFF_BQ = OFF_AF + H_A
OFF_BK = OFF_BQ + H_B * 2 * DK_B
OFF_BV = OFF_BK + H_B * 2 * DK_B
OFF_CU = OFF_BV + W_B
N_IN = OFF_CU + W_C

V7X_LANES = 128
V7X_SUBLANES = 8
V7X_VMEM_LIMIT_BYTES = 56 * 1024 * 1024
V7X_VMEM_LIMIT_LARGE_BYTES = 60 * 1024 * 1024

TOKEN_TILE = 1024
FF_TILE = 256
PROJ_TILE = 512
OUT_TILE = 1024
ATTN_TILE = 512
ATTN_ROWS = 256
XATTN_TILE = 512
S5_TILE = 512
PAGES_PER_STEP = 16
XATTN_SAMPLE_SEQS = 4
S5_SAMPLE_SEQS = 16
MLSTM_PROMPT_CHUNK = 512
MLSTM_UNITS = 16
ALIBI_SPLIT = 64

NEG_INF = float("-inf")


def _cparams(*sem, vmem_limit=V7X_VMEM_LIMIT_BYTES):
    return pltpu.CompilerParams(dimension_semantics=sem, vmem_limit_bytes=vmem_limit)


def _bdot(a, b):
    return jnp.dot(a.astype(BF16), b.astype(BF16), preferred_element_type=F32)


def _bdot_nt(a, b):
    return lax.dot_general(a.astype(BF16), b.astype(BF16), (((1,), (1,)), ((), ())),
                           preferred_element_type=F32)


def _tile_lanes(x, n):
    return x if n == 1 else jnp.concatenate([x] * n, axis=1)


def _layer_norm(y, g, b):
    mu = jnp.mean(y, axis=-1, keepdims=True)
    yc = y - mu
    var = jnp.mean(yc * yc, axis=-1, keepdims=True)
    return yc * lax.rsqrt(var + LN_EPS) * g + b


def _swiglu(xb, wg_ref, wu_ref, wd_ref, h_sc, tf):
    for c in range(wg_ref.shape[1] // tf):
        cols = slice(c * tf, (c + 1) * tf)
        hg = jnp.dot(xb, wg_ref[:, cols], preferred_element_type=F32)
        hu = jnp.dot(xb, wu_ref[:, cols], preferred_element_type=F32)
        h_sc[:, cols] = ((hg * jax.nn.sigmoid(hg)) * hu).astype(BF16)
    return jnp.dot(h_sc[...], wd_ref[...], preferred_element_type=F32)


def _ffn_ln_kernel(*refs, alpha, tf, n_src, first_tiles):
    srcs = refs[:n_src]
    wg_ref, wu_ref, wd_ref, g_ref, b_ref, o_ref, h_sc = refs[n_src:]

    def run(x_ref):
        x = x_ref[...]
        ff = _swiglu(x.astype(BF16), wg_ref, wu_ref, wd_ref, h_sc, tf)
        o_ref[...] = _layer_norm(alpha * x + 0.5 * ff, g_ref[...], b_ref[...])

    if n_src == 1:
        run(srcs[0])
    else:
        i = pl.program_id(0)
        pl.when(i < first_tiles)(lambda: run(srcs[0]))
        pl.when(i >= first_tiles)(lambda: run(srcs[1]))


def ffn_ln(xs, wg, wu, wd, layer, g, b, alpha):
    d = xs[0].shape[1]
    dff = wg.shape[2]
    tm, tf = TOKEN_TILE, FF_TILE
    first = xs[0].shape[0] // tm
    n = sum(x.shape[0] for x in xs)
    src_specs = [pl.BlockSpec((tm, d), lambda i: (jnp.minimum(i, first - 1), 0))]
    if len(xs) == 2:
        src_specs.append(pl.BlockSpec((tm, d), lambda i: (jnp.maximum(i - first, 0), 0)))
    resident = pl.Buffered(1)
    return pl.pallas_call(
        functools.partial(_ffn_ln_kernel, alpha=alpha, tf=tf, n_src=len(xs), first_tiles=first),
        grid=(n // tm,),
        in_specs=src_specs + [
            pl.BlockSpec((None, d, dff), lambda i: (layer, 0, 0), pipeline_mode=resident),
            pl.BlockSpec((None, d, dff), lambda i: (layer, 0, 0), pipeline_mode=resident),
            pl.BlockSpec((None, dff, d), lambda i: (layer, 0, 0), pipeline_mode=resident),
            pl.BlockSpec((1, d), lambda i: (0, 0)),
            pl.BlockSpec((1, d), lambda i: (0, 0)),
        ],
        out_specs=pl.BlockSpec((tm, d), lambda i: (i, 0)),
        out_shape=jax.ShapeDtypeStruct((n, d), F32),
        scratch_shapes=[pltpu.VMEM((tm, dff), BF16)],
        compiler_params=_cparams("arbitrary"),
        name="ffn_ln",
    )(*xs, wg, wu, wd, g.reshape(1, d), b.reshape(1, d))


def _out_ffn_ln_kernel(o_ref, wo_ref, x_ref, g1_ref, b1_ref, wg_ref, wu_ref, wd_ref, g2_ref, b2_ref, *rest,
                       alpha, tf, split):
    outs, h_sc = rest[:-1], rest[-1]
    y1 = _layer_norm(alpha * x_ref[...] + jnp.dot(o_ref[...].astype(BF16), wo_ref[...], preferred_element_type=F32),
                     g1_ref[...], b1_ref[...])
    ff = _swiglu(y1.astype(BF16), wg_ref, wu_ref, wd_ref, h_sc, tf)
    y2 = _layer_norm(alpha * y1 + 0.5 * ff, g2_ref[...], b2_ref[...])
    if split is None:
        outs[0][...] = y2
    else:
        i = pl.program_id(0)

        @pl.when(i < split)
        def _():
            outs[0][...] = y2

        @pl.when(i >= split)
        def _():
            outs[1][...] = y2


def out_ffn_ln(o, wo, wg, wu, wd, layer, x, g1, b1, g2, b2, alpha, split_rows=None):
    n, d = x.shape
    dff = wg.shape[2]
    tm, tf = TOKEN_TILE, FF_TILE
    resident = pl.Buffered(1)
    by_layer = lambda i: (layer, 0, 0)
    vec = pl.BlockSpec((1, d), lambda i: (0, 0))
    if split_rows is None:
        split = None
        out_specs = [pl.BlockSpec((tm, d), lambda i: (i, 0))]
        out_shape = [jax.ShapeDtypeStruct((n, d), F32)]
    else:
        split = split_rows // tm
        out_specs = [pl.BlockSpec((tm, d), lambda i: (jnp.minimum(i, split - 1), 0)),
                     pl.BlockSpec((tm, d), lambda i: (jnp.maximum(i - split, 0), 0))]
        out_shape = [jax.ShapeDtypeStruct((split_rows, d), F32), jax.ShapeDtypeStruct((n - split_rows, d), F32)]
    res = pl.pallas_call(
        functools.partial(_out_ffn_ln_kernel, alpha=alpha, tf=tf, split=split),
        grid=(n // tm,),
        in_specs=[
            pl.BlockSpec((tm, d), lambda i: (i, 0)),
            pl.BlockSpec((None, d, d), by_layer, pipeline_mode=resident),
            pl.BlockSpec((tm, d), lambda i: (i, 0)),
            vec, vec,
            pl.BlockSpec((None, d, dff), by_layer, pipeline_mode=resident),
            pl.BlockSpec((None, d, dff), by_layer, pipeline_mode=resident),
            pl.BlockSpec((None, dff, d), by_layer, pipeline_mode=resident),
            vec, vec,
        ],
        out_specs=out_specs,
        out_shape=out_shape,
        scratch_shapes=[pltpu.VMEM((tm, dff), BF16)],
        compiler_params=_cparams("arbitrary", vmem_limit=V7X_VMEM_LIMIT_LARGE_BYTES),
        name="out_ffn_ln",
    )(o, wo, x, g1.reshape(1, d), b1.reshape(1, d), wg, wu, wd, g2.reshape(1, d), b2.reshape(1, d))
    return res[0] if split_rows is None else res


N_ZA = 4 * W_A
N_ZB = 3 * W_B
N_ZC = W_C
N_ZG = V7X_LANES
N_PROJ = N_ZA + N_ZB + N_ZC + N_ZG


def _proj_in_kernel(x_ref, w_ref, b_ref, _kp_prev, _vp_prev, za_ref, zb_ref, zc_ref, zg_ref,
                    kp_ref, vp_ref, ks_ref, vs_ref, kv_sc, *, prompt_tiles):
    i = pl.program_id(0)
    xb = x_ref[...].astype(BF16)
    off = 0
    for ref in (za_ref, zb_ref, zc_ref, zg_ref):
        width = ref.shape[1]
        z = jnp.dot(xb, w_ref[:, off:off + width], preferred_element_type=F32) + b_ref[:, off:off + width]
        ref[...] = z.astype(ref.dtype)
        if ref is zb_ref:
            kv_sc[...] = z[:, W_B:]
        off += width

    def write_kv(k_out, v_out):
        for h in range(H_B):
            k_out[:, h, :] = kv_sc[:, h * DV_B:(h + 1) * DV_B]
            v_out[:, h, :] = kv_sc[:, W_B + h * DV_B:W_B + (h + 1) * DV_B]

    @pl.when(i < prompt_tiles)
    def _():
        write_kv(kp_ref, vp_ref)

    @pl.when(i >= prompt_tiles)
    def _():
        write_kv(ks_ref, vs_ref)


def proj_in(x, w, b, kp_prev, vp_prev, layer, n_prompt):
    n, d = x.shape
    tm = PROJ_TILE
    pt = n_prompt // tm
    widths = (N_ZA, N_ZB, N_ZC, N_ZG)
    kv_blk = (tm, H_B, DV_B)
    p_map = lambda i: (layer, jnp.minimum(i, pt - 1), 0, 0)
    s_map = lambda i: (jnp.maximum(i - pt, 0), 0, 0)
    kv_s = jax.ShapeDtypeStruct((n - n_prompt, H_B, DV_B), F32)
    return pl.pallas_call(
        functools.partial(_proj_in_kernel, prompt_tiles=pt),
        grid=(n // tm,),
        in_specs=[
            pl.BlockSpec((tm, d), lambda i: (i, 0)),
            pl.BlockSpec((None, d, N_PROJ), lambda i: (layer, 0, 0)),
            pl.BlockSpec((None, 1, N_PROJ), lambda i: (layer, 0, 0)),
            pl.BlockSpec(memory_space=pl.ANY),
            pl.BlockSpec(memory_space=pl.ANY),
        ],
        out_specs=[pl.BlockSpec((tm, wd), lambda i: (i, 0)) for wd in widths]
        + [pl.BlockSpec((None,) + kv_blk, p_map)] * 2 + [pl.BlockSpec(kv_blk, s_map)] * 2,
        out_shape=[jax.ShapeDtypeStruct((n, wd), BF16 if wd == N_ZB else F32) for wd in widths]
        + [jax.ShapeDtypeStruct(kp_prev.shape, F32)] * 2 + [kv_s, kv_s],
        scratch_shapes=[pltpu.VMEM((tm, 2 * W_B), F32)],
        input_output_aliases={3: 4, 4: 5},
        compiler_params=_cparams("arbitrary"),
        name="proj_in",
    )(x, w, b, kp_prev, vp_prev)


def _pack_w_in(w_in, b_in):
    def cols(a):
        pad = jnp.zeros(a.shape[:-1] + (N_ZG - 2 * H_A,), a.dtype)
        return jnp.concatenate([a[..., OFF_AQ:OFF_AI], a[..., OFF_BQ:OFF_CU], a[..., OFF_CU:N_IN],
                                a[..., OFF_AI:OFF_BQ], pad], axis=-1)
    return cols(w_in).astype(BF16), cols(b_in)[:, None, :]


def _mem_kv_kernel(x_ref, wk_ref, wv_ref, mk_ref, mv_ref, mkh_ref, mvh_ref):
    xb = x_ref[...].astype(BF16)
    for w_ref, o_ref, oh_ref in ((wk_ref, mk_ref, mkh_ref), (wv_ref, mv_ref, mvh_ref)):
        r = jnp.dot(xb, w_ref[...], preferred_element_type=F32)
        for h in range(H_X):
            o_ref[:, h, :] = r[:, h * DH_X:(h + 1) * DH_X]
            oh_ref[h] = r[:, h * DH_X:(h + 1) * DH_X].astype(BF16)


def mem_kv(x, wk, wv):
    m, d = x.shape
    depth = wk.shape[0]
    tm = min(m, TOKEN_TILE)
    shp = jax.ShapeDtypeStruct((depth, m, H_X, DH_X), F32)
    shp_h = jax.ShapeDtypeStruct((depth, H_X, m, DH_X), BF16)
    w_spec = pl.BlockSpec((None, d, d), lambda i, l: (l, 0, 0))
    o_spec = pl.BlockSpec((None, tm, H_X, DH_X), lambda i, l: (l, i, 0, 0))
    oh_spec = pl.BlockSpec((None, H_X, tm, DH_X), lambda i, l: (l, 0, i, 0))
    return pl.pallas_call(
        _mem_kv_kernel,
        grid=(m // tm, depth),
        in_specs=[pl.BlockSpec((tm, d), lambda i, l: (i, 0)), w_spec, w_spec],
        out_specs=[o_spec, o_spec, oh_spec, oh_spec],
        out_shape=[shp, shp, shp_h, shp_h],
        compiler_params=_cparams("parallel", "arbitrary"),
        name="mem_kv",
    )(x, wk, wv)


def _mix_out_ln_kernel(ya_ref, yb_ref, yc_ref, w_ref, x_ref, g_ref, b_ref, wq_ref, o_ref, q_ref, *, alpha):
    acc = None
    off = 0
    for p in (ya_ref, yb_ref, yc_ref):
        width = p.shape[1]
        t = jnp.dot(p[...].astype(BF16), w_ref[off:off + width, :], preferred_element_type=F32)
        acc = t if acc is None else acc + t
        off += width
    y = _layer_norm(alpha * x_ref[...] + acc, g_ref[...], b_ref[...])
    o_ref[...] = y
    q_ref[...] = jnp.dot(y.astype(BF16), wq_ref[...], preferred_element_type=F32).astype(q_ref.dtype)


def mix_out_ln(ya, yb, yc, w_out, wq, layer, x, g, b, alpha):
    n, d = x.shape
    tm = OUT_TILE
    rows = lambda width: pl.BlockSpec((tm, width), lambda i: (i, 0))
    weight = lambda w: pl.BlockSpec((None,) + w.shape[1:], lambda i: (layer, 0, 0))
    vec = pl.BlockSpec((1, d), lambda i: (0, 0))
    return pl.pallas_call(
        functools.partial(_mix_out_ln_kernel, alpha=alpha),
        grid=(n // tm,),
        in_specs=[rows(ya.shape[1]), rows(yb.shape[1]), rows(yc.shape[1]), weight(w_out), rows(d), vec, vec,
                  weight(wq)],
        out_specs=[rows(d), rows(wq.shape[2])],
        out_shape=[jax.ShapeDtypeStruct((n, d), F32),
                   jax.ShapeDtypeStruct((n, wq.shape[2]), BF16)],
        compiler_params=_cparams("parallel"),
        name="mix_out_ln",
    )(ya, yb, yc, w_out, x, g.reshape(1, d), b.reshape(1, d), wq)


def _log_sigmoid(x):
    return jnp.minimum(x, 0.0) - jnp.log1p(jnp.exp(-jnp.abs(x)))


def _mlstm_prompt_kernel(za_ref, zg_ref, g_ref, ya_ref, c1_ref, n1_ref, m1_ref, c_sc, n_sc, m_sc, *, chunk):
    L = chunk
    ti = pl.program_id(1)

    @pl.when(ti == 0)
    def _():
        c_sc[...] = jnp.zeros_like(c_sc)
        n_sc[...] = jnp.zeros_like(n_sc)
        m_sc[...] = jnp.zeros_like(m_sc)

    row = lax.broadcasted_iota(jnp.int32, (L, L), 0)
    col = lax.broadcasted_iota(jnp.int32, (L, L), 1)
    causal = col <= row
    tril = causal.astype(F32)
    sel_r = lax.broadcasted_iota(jnp.int32, (V7X_SUBLANES, V7X_LANES), 0)
    sel_c = lax.broadcasted_iota(jnp.int32, (V7X_SUBLANES, V7X_LANES), 1)
    sel = (sel_r == sel_c).astype(F32)
    lane_g = lax.broadcasted_iota(jnp.int32, (L, N_ZG), 1)
    lane_m = lax.broadcasted_iota(jnp.int32, (1, V7X_LANES), 1)
    rep = lambda col_: jnp.broadcast_to(col_, (L, V7X_LANES))
    wide = lambda x, n: _tile_lanes(x, n // V7X_LANES) if n > V7X_LANES else x[:, :n]

    gates = zg_ref[...]
    gl = jnp.where(lane_g < H_A, gates, _log_sigmoid(gates))
    bcum = jnp.dot(tril, gl, precision=lax.Precision.HIGHEST, preferred_element_type=F32)
    mixed = jnp.where(lane_g < H_A, gates, bcum)
    t_rows = lax.dot_general(sel, mixed, (((1,), (1,)), ((), ())),
                             precision=lax.Precision.HIGHEST, preferred_element_type=F32)
    outs = []
    m_out = jnp.zeros((1, V7X_LANES), F32)
    for h in range(H_A):
        q = za_ref[:, h * DH_A:(h + 1) * DH_A]
        k = za_ref[:, W_A + h * DH_A:W_A + (h + 1) * DH_A] * (DH_A ** -0.5)
        v = za_ref[:, 2 * W_A + h * DH_A:2 * W_A + (h + 1) * DH_A]
        og = za_ref[:, 3 * W_A + h * DH_A:3 * W_A + (h + 1) * DH_A]
        ig_c = rep(gates[:, h:h + 1])
        b_c = rep(bcum[:, H_A + h:H_A + h + 1])
        ig_row = t_rows[h:h + 1, :]
        b_row = t_rows[H_A + h:H_A + h + 1, :]
        c, n, m_prev = c_sc[h], n_sc[h:h + 1, :], m_sc[:, h:h + 1]

        dmat = jnp.where(causal, wide(b_c, L) - b_row + ig_row, NEG_INF)
        a = rep(jnp.max(dmat, axis=-1, keepdims=True))
        s = _bdot_nt(q, k) * jnp.exp(dmat - wide(a, L))
        n_loc = _bdot(s, v)
        d_loc = rep(jnp.sum(s, axis=-1, keepdims=True))
        a_last = a[L - 1:L, :]
        b_last = b_c[L - 1:L, :]
        wk = jnp.exp(b_last - b_c + ig_c - a_last)[:, :DH_A]
        u_loc = lax.dot_general((wk * v).astype(BF16), k.astype(BF16), (((0,), (0,)), ((), ())),
                                preferred_element_type=F32)
        nu_loc = jnp.sum(wk * k, axis=0, keepdims=True)

        inter = b_c + m_prev
        m_row = jnp.maximum(inter, a)
        r = jnp.exp(a - m_row)
        w_inter = jnp.exp(inter - m_row)
        num = r[:, :DH_A] * n_loc + w_inter[:, :DH_A] * _bdot_nt(q, c)
        den = r * d_loc + w_inter * rep(jnp.sum(q * n, axis=-1, keepdims=True))
        hh = num / jnp.maximum(jnp.abs(den), jnp.exp(-m_row))[:, :DH_A]
        m_new = m_row[L - 1:L, 0:1]
        decay = jnp.exp(b_last[:, 0:1] + m_prev - m_new)
        e_loc = jnp.exp(a_last[:, 0:1] - m_new)
        c_sc[h] = decay * c + e_loc * u_loc
        n_sc[h:h + 1, :] = decay * n + e_loc * nu_loc
        m_out = jnp.where(lane_m == h, m_new, m_out)

        hn = hh * lax.rsqrt(jnp.mean(hh * hh, axis=-1, keepdims=True) + NORM_EPS)
        outs.append(jax.nn.sigmoid(og) * hn)
    ya_ref[...] = (jnp.concatenate(outs, axis=-1) * g_ref[...]).astype(ya_ref.dtype)
    m_sc[...] = m_out

    @pl.when(ti == pl.num_programs(1) - 1)
    def _():
        c1_ref[0] = c_sc[...]
        n1_ref[0] = n_sc[...]
        m1_ref[0] = m_sc[...]


def _mlstm_sample_kernel(za_ref, zg_ref, g_ref, c0_ref, n0_ref, m0_ref, _prev, ya_ref, c1_ref, n1_ref, m1_ref,
                         *, t_len, seqs):
    R = seqs * t_len
    row = lax.broadcasted_iota(jnp.int32, (R, R), 0)
    col = lax.broadcasted_iota(jnp.int32, (R, R), 1)
    mask = (row // t_len == col // t_len) & (col <= row)
    tril = mask.astype(F32)
    sel_r = lax.broadcasted_iota(jnp.int32, (V7X_SUBLANES, V7X_LANES), 0)
    sel_c = lax.broadcasted_iota(jnp.int32, (V7X_SUBLANES, V7X_LANES), 1)
    sel = (sel_r == sel_c).astype(F32)
    lane_g = lax.broadcasted_iota(jnp.int32, (R, N_ZG), 1)
    lane_m = lax.broadcasted_iota(jnp.int32, (seqs, 1, V7X_LANES), 2)
    rep = lambda c_: jnp.broadcast_to(c_, (R, V7X_LANES))
    wide = lambda x, n: _tile_lanes(x, n // V7X_LANES) if n > V7X_LANES else x[:, :n]
    per_seq = lambda x: x.reshape(seqs, t_len, x.shape[-1])
    last_rows = lambda x: jnp.broadcast_to(per_seq(x)[:, t_len - 1:t_len, :], (seqs, t_len, x.shape[-1])
                                           ).reshape(R, x.shape[-1])
    seq_rows = lambda x: jnp.broadcast_to(x, (seqs, t_len, x.shape[-1])).reshape(R, x.shape[-1])
    state_rows = lambda x: jnp.broadcast_to(x, (seqs, DH_A, x.shape[-1])).reshape(seqs * DH_A, x.shape[-1])
    own_r = lax.broadcasted_iota(jnp.int32, (R, seqs * DH_A), 0) // t_len
    own_c = lax.broadcasted_iota(jnp.int32, (R, seqs * DH_A), 1) // DH_A
    own = own_r == own_c
    odd_seq = (lax.broadcasted_iota(jnp.int32, (R, DH_A), 0) // t_len) % 2 == 1

    gates = zg_ref[...]
    gl = jnp.where(lane_g < H_A, gates, _log_sigmoid(gates))
    bcum = jnp.dot(tril, gl, precision=lax.Precision.HIGHEST, preferred_element_type=F32)
    mixed = jnp.where(lane_g < H_A, gates, bcum)
    t_rows = lax.dot_general(sel, mixed, (((1,), (1,)), ((), ())),
                             precision=lax.Precision.HIGHEST, preferred_element_type=F32)
    m0 = m0_ref[...]
    norm_g = g_ref[...]
    outs = []
    m_out = jnp.zeros((seqs, 1, V7X_LANES), F32)
    for h in range(H_A):
        q = za_ref[:, h * DH_A:(h + 1) * DH_A]
        k = za_ref[:, W_A + h * DH_A:W_A + (h + 1) * DH_A] * (DH_A ** -0.5)
        v = za_ref[:, 2 * W_A + h * DH_A:2 * W_A + (h + 1) * DH_A]
        og = za_ref[:, 3 * W_A + h * DH_A:3 * W_A + (h + 1) * DH_A]
        ig_c = rep(gates[:, h:h + 1])
        b_c = rep(bcum[:, H_A + h:H_A + h + 1])
        ig_row = t_rows[h:h + 1, :]
        b_row = t_rows[H_A + h:H_A + h + 1, :]
        m_prev = rep(seq_rows(m0)[:, h:h + 1])
        c_stack = c0_ref[:, h].reshape(seqs * DH_A, DH_A)
        n_rows = seq_rows(n0_ref[:, h:h + 1, :])

        dmat = jnp.where(mask, wide(b_c, R) - b_row + ig_row, NEG_INF)
        a = rep(jnp.max(dmat, axis=-1, keepdims=True))
        s = _bdot_nt(q, k) * jnp.exp(dmat - wide(a, R))
        n_loc = _bdot(s, v)
        d_loc = rep(jnp.sum(s, axis=-1, keepdims=True))
        a_last = last_rows(a)
        b_last = last_rows(b_c)
        wk = jnp.exp(b_last - b_c + ig_c - a_last)[:, :DH_A]
        vw = wk * v
        vw2 = jnp.concatenate([vw, vw], axis=1)
        u_stack = lax.dot_general(jnp.where(own, _tile_lanes(vw2, seqs // 2), 0.0).astype(BF16), k.astype(BF16),
                                  (((0,), (0,)), ((), ())), preferred_element_type=F32)
        nu = jnp.sum(per_seq(wk * k), axis=1, keepdims=True)

        inter = b_c + m_prev
        m_row = jnp.maximum(inter, a)
        r = jnp.exp(a - m_row)
        w_inter = jnp.exp(inter - m_row)
        x_all = jnp.where(own, _bdot_nt(q, c_stack), 0.0)
        fold = x_all[:, 0:V7X_LANES]
        for j in range(1, seqs * DH_A // V7X_LANES):
            fold = fold + x_all[:, j * V7X_LANES:(j + 1) * V7X_LANES]
        qc = jnp.where(odd_seq, fold[:, DH_A:], fold[:, :DH_A])
        num = r[:, :DH_A] * n_loc + w_inter[:, :DH_A] * qc
        den = r * d_loc + w_inter * rep(jnp.sum(q * n_rows, axis=-1, keepdims=True))
        hh = num / jnp.maximum(jnp.abs(den), jnp.exp(-m_row))[:, :DH_A]
        hn = hh * lax.rsqrt(jnp.mean(hh * hh, axis=-1, keepdims=True) + NORM_EPS)
        outs.append(jax.nn.sigmoid(og) * hn)

        m_new = per_seq(m_row)[:, t_len - 1:t_len, :]
        decay = jnp.exp(per_seq(b_c + m_prev)[:, t_len - 1:t_len, :] - m_new)
        e_loc = jnp.exp(per_seq(a)[:, t_len - 1:t_len, :] - m_new)
        c_new = (state_rows(decay)[:, :DH_A] * c_stack + state_rows(e_loc)[:, :DH_A] * u_stack)
        c1_ref[:, h] = c_new.reshape(seqs, DH_A, DH_A)
        n1_ref[:, h:h + 1, :] = decay[:, :, :DH_A] * n0_ref[:, h:h + 1, :] + e_loc[:, :, :DH_A] * nu
        m_out = jnp.where(lane_m == h, m_new, m_out)
    ya_ref[...] = (jnp.concatenate(outs, axis=-1) * norm_g).astype(ya_ref.dtype)
    m1_ref[...] = m_out


def _mlstm_out_shapes(n_tok, nb):
    return [
        jax.ShapeDtypeStruct((n_tok, W_A), BF16),
        jax.ShapeDtypeStruct((nb, H_A, DH_A, DH_A), F32),
        jax.ShapeDtypeStruct((nb, H_A, DH_A), F32),
        jax.ShapeDtypeStruct((nb, 1, V7X_LANES), F32),
    ]


def mlstm_prompt(za, zg, norm_g, nb, t):
    L = MLSTM_PROMPT_CHUNK
    nc = t // L
    row_map = lambda b, c: (b * nc + c, 0)
    st4 = lambda b, c: (b, 0, 0, 0)
    st3 = lambda b, c: (b, 0, 0)
    ya, c1, n1, m1 = pl.pallas_call(
        functools.partial(_mlstm_prompt_kernel, chunk=L),
        grid=(nb, nc),
        in_specs=[pl.BlockSpec((L, N_ZA), row_map), pl.BlockSpec((L, N_ZG), row_map),
                  pl.BlockSpec((1, W_A), lambda b, c: (0, 0))],
        out_specs=[pl.BlockSpec((L, W_A), row_map), pl.BlockSpec((1, H_A, DH_A, DH_A), st4),
                   pl.BlockSpec((1, H_A, DH_A), st3), pl.BlockSpec((1, 1, V7X_LANES), st3)],
        out_shape=_mlstm_out_shapes(za.shape[0], nb),
        scratch_shapes=[pltpu.VMEM((H_A, DH_A, DH_A), F32), pltpu.VMEM((H_A, DH_A), F32),
                        pltpu.VMEM((1, V7X_LANES), F32)],
        compiler_params=_cparams("parallel", "arbitrary"),
        name="mlstm_prompt",
    )(za, zg, norm_g.reshape(1, W_A))
    return ya, c1, n1, m1[:, 0, :H_A]


def mlstm_sample(za, zg, norm_g, nb, t, row0, c0, n0, m0, layer, prev):
    seqs = MLSTM_UNITS
    rows = seqs * t
    row_map = lambda b: (row0 // rows + b, 0)
    st4 = lambda b: (b, 0, 0, 0)
    st3 = lambda b: (b, 0, 0)
    m0p = jnp.pad(m0, ((0, 0), (0, V7X_LANES - H_A))).reshape(nb, 1, V7X_LANES)
    state_specs = [pl.BlockSpec((seqs, H_A, DH_A, DH_A), st4), pl.BlockSpec((seqs, H_A, DH_A), st3),
                   pl.BlockSpec((seqs, 1, V7X_LANES), st3)]
    ya, c1, n1, m1 = pl.pallas_call(
        functools.partial(_mlstm_sample_kernel, t_len=t, seqs=seqs),
        grid=(nb // seqs,),
        in_specs=[pl.BlockSpec((rows, N_ZA), row_map), pl.BlockSpec((rows, N_ZG), row_map),
                  pl.BlockSpec((1, W_A), lambda b: (0, 0)),
                  pl.BlockSpec((None, seqs, H_A, DH_A, DH_A), lambda b: (layer, b, 0, 0, 0))] + state_specs[1:]
        + [pl.BlockSpec(memory_space=pl.ANY)],
        out_specs=[pl.BlockSpec((rows, W_A), row_map)] + state_specs,
        out_shape=_mlstm_out_shapes(za.shape[0], nb),
        input_output_aliases={6: 0},
        compiler_params=_cparams("arbitrary"),
        name="mlstm_sample",
    )(za, zg, norm_g.reshape(1, W_A), c0, n0, m0p, prev)
    return ya, c1, n1, m1[:, 0, :H_A]


def _diff_lambda(lam_ref, lam_init):
    lp = lam_ref[...]
    d01 = jnp.sum(lp[0:1, :] * lp[1:2, :], axis=-1, keepdims=True)
    d23 = jnp.sum(lp[2:3, :] * lp[3:4, :], axis=-1, keepdims=True)
    return jnp.exp(d01) - jnp.exp(d23) + lam_init


def _alibi_slope(h):
    return jnp.where(h == 0, 2.0 ** -2, jnp.where(h == 1, 2.0 ** -4, jnp.where(h == 2, 2.0 ** -6, 2.0 ** -8)))


def _alibi_tables(t):
    slopes = (2.0 ** (-8.0 * jnp.arange(1, H_B + 1, dtype=F32) / H_B))[:, None]
    pos = jnp.arange(t, dtype=jnp.int32)
    hi = ((pos // ALIBI_SPLIT) * ALIBI_SPLIT).astype(F32)[None, :]
    lo = (pos % ALIBI_SPLIT).astype(F32)[None, :]
    ones = jnp.ones((H_B, t), F32)
    pad = jnp.zeros((H_B, t, DV_B - 4), F32)
    aq = jnp.concatenate([jnp.stack([slopes * ones, slopes * ones, -slopes * hi, -slopes * lo], axis=-1), pad], -1)
    ak = jnp.concatenate([jnp.stack([hi * ones, lo * ones, ones, ones], axis=-1), pad], -1)
    return aq.astype(BF16), ak.astype(BF16)


def _attn_prompt_kernel(lam_ref, g_ref, q_ref, aq_ref, k_ref, ak_ref, v_ref, o_ref,
                        qs_sc, m_sc, acc_sc, *, lam_init, tq, tk, nq):
    i, j = _tri_pair(pl.program_id(1), nq)
    rq = ATTN_ROWS

    @pl.when(j == 0)
    def _():
        lane = lax.broadcasted_iota(jnp.int32, (tq, DV_B), 1)
        for h in range(H_B):
            q = q_ref[:, h * DV_B:(h + 1) * DV_B] * (DK_B ** -0.5)
            qs_sc[h, 0:tq, 0:DV_B] = jnp.where(lane < DK_B, q, 0.0).astype(BF16)
            qs_sc[h, tq:2 * tq, 0:DV_B] = jnp.where(lane >= DK_B, q, 0.0).astype(BF16)
            qs_sc[h, 0:tq, DV_B:2 * DV_B] = aq_ref[h]
            qs_sc[h, tq:2 * tq, DV_B:2 * DV_B] = aq_ref[h]
        m_sc[...] = jnp.full_like(m_sc, NEG_INF)
        acc_sc[...] = jnp.zeros_like(acc_sc)

    def step(masked):
        ones = jnp.ones((tk, V7X_LANES), BF16)
        for h in range(H_B):
            kaug = jnp.concatenate([k_ref[:, h * DV_B:(h + 1) * DV_B].astype(BF16), ak_ref[h]], axis=1)
            vaug = jnp.concatenate([v_ref[:, h * DV_B:(h + 1) * DV_B].astype(BF16), ones], axis=1)
            for r in range(2 * tq // rq):
                rows = slice(r * rq, (r + 1) * rq)
                s = lax.dot_general(qs_sc[h, rows, :], kaug, (((1,), (1,)), ((), ())), preferred_element_type=F32)
                if masked:
                    qi = (r * rq) % tq + lax.broadcasted_iota(jnp.int32, (rq, tk), 0)
                    kj = lax.broadcasted_iota(jnp.int32, (rq, tk), 1)
                    s = jnp.where(kj <= qi, s, NEG_INF)
                m_old = m_sc[h, rows, :]
                m_new = jnp.maximum(m_old, jnp.max(s, axis=-1, keepdims=True))
                alpha = jnp.exp(m_old - m_new)
                p = jnp.exp(s - _tile_lanes(m_new, tk // V7X_LANES))
                acc_sc[h, rows, :] = (_tile_lanes(alpha, 2) * acc_sc[h, rows, :]
                                      + jnp.dot(p.astype(BF16), vaug, preferred_element_type=F32))
                m_sc[h, rows, :] = m_new

    @pl.when(j < i)
    def _():
        step(False)

    @pl.when(j == i)
    def _():
        step(True)

    @pl.when(j == i)
    def _():
        lam = _diff_lambda(lam_ref, lam_init)
        for h in range(H_B):
            o0 = acc_sc[h, 0:tq, 0:DV_B] / acc_sc[h, 0:tq, DV_B:2 * DV_B]
            o1 = acc_sc[h, tq:2 * tq, 0:DV_B] / acc_sc[h, tq:2 * tq, DV_B:2 * DV_B]
            ob = o0 - lam * o1
            on = ob * lax.rsqrt(jnp.mean(ob * ob, axis=-1, keepdims=True) + NORM_EPS)
            o_ref[:, h * DV_B:(h + 1) * DV_B] = on * g_ref[:, h * DV_B:(h + 1) * DV_B] * (1.0 - lam_init)


def _tri_pair(t, n):
    i = sum((t >= k * (k + 1) // 2).astype(jnp.int32) for k in range(1, n))
    return i, t - i * (i + 1) // 2


def diff_attn_prompt(zb, aq, ak, lam_p, norm_g, nb, t, lam_init):
    n_tok = zb.shape[0]
    tq = tk = ATTN_TILE
    nq = t // tq
    q_map = lambda b, s: (b * nq + _tri_pair(s, nq)[0], 0)
    kv_map = lambda off: (lambda b, s: (b * nq + _tri_pair(s, nq)[1], off))
    return pl.pallas_call(
        functools.partial(_attn_prompt_kernel, lam_init=lam_init, tq=tq, tk=tk, nq=nq),
        grid=(nb, nq * (nq + 1) // 2),
        in_specs=[
            pl.BlockSpec(lam_p.shape, lambda b, s: (0, 0)),
            pl.BlockSpec((1, W_B), lambda b, s: (0, 0)),
            pl.BlockSpec((tq, W_B), q_map),
            pl.BlockSpec((H_B, tq, DV_B), lambda b, s: (0, _tri_pair(s, nq)[0], 0)),
            pl.BlockSpec((tk, W_B), kv_map(1)),
            pl.BlockSpec((H_B, tk, DV_B), lambda b, s: (0, _tri_pair(s, nq)[1], 0)),
            pl.BlockSpec((tk, W_B), kv_map(2)),
        ],
        out_specs=pl.BlockSpec((tq, W_B), q_map),
        out_shape=jax.ShapeDtypeStruct((n_tok, W_B), F32),
        scratch_shapes=[pltpu.VMEM((H_B, 2 * tq, 2 * DV_B), BF16), pltpu.VMEM((H_B, 2 * tq, V7X_LANES), F32),
                        pltpu.VMEM((H_B, 2 * tq, 2 * DV_B), F32)],
        compiler_params=_cparams("parallel", "arbitrary"),
        name="diff_attn_prompt",
    )(lam_p, norm_g.reshape(1, W_B), zb, aq, zb, ak, zb)


def _attn_sample_kernel(*refs, lam_init, past_len, t_new, pages_per_step):
    G = pages_per_step
    pt_ref, lam_ref, g_ref, q_ref, kn_ref, vn_ref = refs[:6]
    k_refs = refs[6:6 + G]
    v_refs = refs[6 + G:6 + 2 * G]
    _prev, o_ref, qs_sc, bias_sc, m_sc, l_sc, acc_sc = refs[6 + 2 * G:]
    del pt_ref
    ps = pl.program_id(1)
    rows_per_head = 2 * t_new
    n_rows = H_B * rows_per_head
    page_rows = PAGE_SIZE * H_B
    reps = page_rows // V7X_LANES

    r_lane = lax.broadcasted_iota(jnp.int32, (n_rows, V7X_LANES), 0)
    slope = _alibi_slope(r_lane // rows_per_head).astype(F32)

    @pl.when(ps == 0)
    def _():
        q = q_ref[...] * (DK_B ** -0.5)
        lane = lax.broadcasted_iota(jnp.int32, (t_new, DV_B), 1)
        for h in range(H_B):
            qh = q[:, h * DV_B:(h + 1) * DV_B]
            qs_sc[h * rows_per_head:(h + 1) * rows_per_head, :] = jnp.concatenate(
                [jnp.where(lane < DK_B, qh, 0.0), jnp.where(lane >= DK_B, qh, 0.0)], axis=0).astype(BF16)
        rr = lax.broadcasted_iota(jnp.int32, (n_rows, page_rows), 0)
        cc = lax.broadcasted_iota(jnp.int32, (n_rows, page_rows), 1)
        rel = cc // H_B - (past_len + rr % t_new)
        bias = _alibi_slope(rr // rows_per_head).astype(F32) * rel.astype(F32)
        bias_sc[...] = jnp.where(cc % H_B == rr // rows_per_head, bias, NEG_INF)
        m_sc[...] = jnp.full_like(m_sc, NEG_INF)
        l_sc[...] = jnp.zeros_like(l_sc)
        acc_sc[...] = jnp.zeros_like(acc_sc)

    qs = qs_sc[...]
    m, l, acc = m_sc[...], l_sc[...], acc_sc[...]
    s_pages = []
    m_new = m
    for g in range(G):
        base = ((ps * G + g) * PAGE_SIZE).astype(F32)
        s = _bdot_nt(qs, k_refs[g][...]) + (bias_sc[...] + _tile_lanes(slope * base, reps))
        m_new = jnp.maximum(m_new, jnp.max(s, axis=-1, keepdims=True))
        s_pages.append(s)
    alpha = jnp.exp(m - m_new)
    l = alpha * l
    acc = alpha * acc
    for g in range(G):
        p = jnp.exp(s_pages[g] - _tile_lanes(m_new, reps))
        l = l + jnp.sum(p, axis=-1, keepdims=True)
        acc = acc + _bdot(p, v_refs[g][...])
    m = m_new

    @pl.when(ps < pl.num_programs(1) - 1)
    def _():
        m_sc[...] = m
        l_sc[...] = l
        acc_sc[...] = acc

    @pl.when(ps == pl.num_programs(1) - 1)
    def _():
        rnd = lambda a: a.astype(BF16).astype(F32)
        kn = rnd(kn_ref[...])
        vn = rnd(vn_ref[...])
        qf = qs.astype(F32)
        sn = jnp.concatenate(
            [lax.dot_general(qf[h * rows_per_head:(h + 1) * rows_per_head, :], kn[:, h * DV_B:(h + 1) * DV_B],
                             (((1,), (1,)), ((), ())), preferred_element_type=F32) for h in range(H_B)],
            axis=0)
        rr = lax.broadcasted_iota(jnp.int32, (n_rows, t_new), 0)
        rel = lax.broadcasted_iota(jnp.int32, (n_rows, t_new), 1) - rr % t_new
        sn = jnp.where(rel <= 0, sn + _alibi_slope(rr // rows_per_head).astype(F32) * rel.astype(F32), NEG_INF)
        m_fin = jnp.maximum(m, jnp.max(sn, axis=-1, keepdims=True))
        alpha = jnp.exp(m - m_fin)
        pn = rnd(jnp.exp(sn - m_fin[:, 0:1]))
        l_fin = alpha * l + jnp.sum(pn, axis=-1, keepdims=True)
        pv = jnp.concatenate(
            [jnp.dot(pn[h * rows_per_head:(h + 1) * rows_per_head, :], vn[:, h * DV_B:(h + 1) * DV_B],
                     preferred_element_type=F32) for h in range(H_B)], axis=0)
        o = (alpha * acc + pv) / l_fin

        lam = _diff_lambda(lam_ref, lam_init)
        outs = []
        for h in range(H_B):
            o0 = o[h * rows_per_head:h * rows_per_head + t_new, :]
            o1 = o[h * rows_per_head + t_new:(h + 1) * rows_per_head, :]
            ob = o0 - lam * o1
            outs.append(ob * lax.rsqrt(jnp.mean(ob * ob, axis=-1, keepdims=True) + NORM_EPS))
        o_ref[...] = jnp.concatenate(outs, axis=-1) * g_ref[...] * (1.0 - lam_init)


def diff_attn_sample(zb, cache_k, cache_v, layer, page_table, lam_p, norm_g, nb, t_new, row0, lam_init, prev):
    n_tok = zb.shape[0]
    n_pages = page_table.shape[1]
    G = PAGES_PER_STEP
    rb0 = row0 // t_new
    n_rows = 2 * H_B * t_new
    page_rows = PAGE_SIZE * H_B
    ck = cache_k.reshape(cache_k.shape[0], cache_k.shape[1], page_rows, DV_B)
    cv = cache_v.reshape(cache_v.shape[0], cache_v.shape[1], page_rows, DV_B)

    def page_map(g):
        return lambda b, p, pt: (layer, pt[b, p * G + g], 0, 0)

    page_blk = (None, None, page_rows, DV_B)
    in_specs = [
        pl.BlockSpec(lam_p.shape, lambda b, p, pt: (0, 0)),
        pl.BlockSpec((1, W_B), lambda b, p, pt: (0, 0)),
        pl.BlockSpec((t_new, W_B), lambda b, p, pt: (rb0 + b, 0)),
        pl.BlockSpec((t_new, W_B), lambda b, p, pt: (rb0 + b, 1)),
        pl.BlockSpec((t_new, W_B), lambda b, p, pt: (rb0 + b, 2)),
    ]
    in_specs += [pl.BlockSpec(page_blk, page_map(g)) for g in range(G)]
    in_specs += [pl.BlockSpec(page_blk, page_map(g)) for g in range(G)]
    in_specs += [pl.BlockSpec(memory_space=pl.ANY)]
    args = [page_table, lam_p, norm_g.reshape(1, W_B), zb, zb, zb] + [ck] * G + [cv] * G + [prev]
    grid_spec = pltpu.PrefetchScalarGridSpec(
        num_scalar_prefetch=1,
        grid=(nb, n_pages // G),
        in_specs=in_specs,
        out_specs=pl.BlockSpec((t_new, W_B), lambda b, p, pt: (rb0 + b, 0)),
        scratch_shapes=[pltpu.VMEM((n_rows, DV_B), BF16), pltpu.VMEM((n_rows, page_rows), F32),
                        pltpu.VMEM((n_rows, V7X_LANES), F32), pltpu.VMEM((n_rows, V7X_LANES), F32),
                        pltpu.VMEM((n_rows, DV_B), F32)],
    )
    return pl.pallas_call(
        functools.partial(_attn_sample_kernel, lam_init=lam_init, past_len=n_pages * PAGE_SIZE,
                          t_new=t_new, pages_per_step=G),
        grid_spec=grid_spec,
        out_shape=jax.ShapeDtypeStruct((n_tok, W_B), F32),
        input_output_aliases={len(args) - 1: 0},
        compiler_params=_cparams("parallel", "arbitrary"),
        name="diff_attn_sample",
    )(*args)


def _s5_disc_kernel(are_ref, aim_ref, ldt_ref, bre_ref, bim_ref, lre_ref, lim_ref, bbre_ref, bbim_ref):
    a_re = are_ref[...]
    a_im = aim_ref[...]
    dt = jnp.exp(ldt_ref[...])
    mag = jnp.exp(a_re * dt)
    lb_re = mag * jnp.cos(a_im * dt)
    lb_im = mag * jnp.sin(a_im * dt)
    den = a_re * a_re + a_im * a_im
    xr = lb_re - 1.0
    fr = (xr * a_re + lb_im * a_im) / den
    fi = (lb_im * a_re - xr * a_im) / den
    lre_ref[...] = lb_re
    lim_ref[...] = lb_im
    b_re = bre_ref[...]
    b_im = bim_ref[...]
    bbre_ref[...] = fr * b_re - fi * b_im
    bbim_ref[...] = fr * b_im + fi * b_re


def s5_discretize(a_re, a_im, log_dt, b_re, b_im):
    depth = a_re.shape[0]
    gp = G_C * P_C
    flat = lambda a: a.reshape(depth, 1, gp)
    ldt = jnp.broadcast_to(log_dt[:, :, None], (depth, G_C, P_C)).reshape(depth, 1, gp)
    tr = lambda b: jnp.transpose(b, (0, 3, 1, 2)).reshape(depth, GC, gp)
    shp1 = jax.ShapeDtypeStruct((depth, 1, gp), F32)
    shpb = jax.ShapeDtypeStruct((depth, GC, gp), F32)
    return pl.pallas_call(_s5_disc_kernel, out_shape=[shp1, shp1, shpb, shpb], name="s5_discretize")(
        flat(a_re), flat(a_im), ldt, tr(b_re), tr(b_im))


def _block_diag_in(bb):
    depth, _, gp = bb.shape
    tiled = jnp.tile(bb, (1, G_C, 1)).reshape(depth, G_C, GC, gp)
    grp_r = jnp.arange(G_C)[:, None, None]
    grp_c = (jnp.arange(gp) // P_C)[None, None, :]
    return jnp.where(grp_r == grp_c, tiled, 0.0).reshape(depth, G_C * GC, gp)


def _block_diag_out(c):
    depth = c.shape[0]
    ct = jnp.transpose(c, (0, 1, 3, 2)).reshape(depth, G_C * P_C, GC)
    tiled = jnp.tile(ct, (1, 1, G_C))
    grp_r = (jnp.arange(G_C * P_C) // P_C)[:, None]
    grp_c = (jnp.arange(G_C * GC) // GC)[None, :]
    return jnp.where(grp_r == grp_c, tiled, 0.0)


def _cmul_add(a_re, a_im, x_re, x_im, y_re, y_im):
    return y_re + (a_re * x_re - a_im * x_im), y_im + (a_re * x_im + a_im * x_re)


def _block_scan(x_re, x_im, pw):
    for d, (a_re, a_im) in zip((1, 2, 4), pw):
        x_re, x_im = _cmul_add(a_re, a_im, pltpu.roll(x_re, d, 0), pltpu.roll(x_im, d, 0), x_re, x_im)
    return x_re, x_im


def _s5_kernel(*refs, rows, independent, aliased):
    (u_ref, lre_ref, lim_ref, bb_ref, cc_ref, d_ref, gw_ref, gb_ref, s0re_ref, s0im_ref) = refs[:10]
    rest = refs[10:]
    if aliased:
        rest = rest[1:]
    y_ref, s1re_ref, s1im_ref, st_sc, car_sc = rest
    gp = G_C * P_C
    nblk = rows // V7X_SUBLANES
    ti = pl.program_id(1)

    lam_re = lre_ref[...]
    lam_im = lim_ref[...]
    l2_re, l2_im = lam_re * lam_re - lam_im * lam_im, 2.0 * lam_re * lam_im
    l4_re, l4_im = l2_re * l2_re - l2_im * l2_im, 2.0 * l2_re * l2_im
    row8 = lax.broadcasted_iota(jnp.int32, (V7X_SUBLANES, gp), 0)
    pw = tuple((jnp.where(row8 >= d, a_re, 0.0), jnp.where(row8 >= d, a_im, 0.0))
               for d, (a_re, a_im) in zip((1, 2, 4), ((lam_re, lam_im), (l2_re, l2_im), (l4_re, l4_im))))
    pk_re, pk_im = _block_scan(jnp.where(row8 == 0, lam_re, 0.0), jnp.where(row8 == 0, lam_im, 0.0), pw)

    u = u_ref[...]
    st_sc[...] = _bdot(u, bb_ref[...])

    if not independent:
        @pl.when(ti == 0)
        def _():
            car_sc[0:1, :] = s0re_ref[...]
            car_sc[1:2, :] = s0im_ref[...]

    def body(bi, carry):
        r0 = pl.multiple_of(bi * V7X_SUBLANES, V7X_SUBLANES)
        x_re = st_sc[pl.ds(r0, V7X_SUBLANES), 0:gp]
        x_im = st_sc[pl.ds(r0, V7X_SUBLANES), gp:2 * gp]
        x_re, x_im = _block_scan(x_re, x_im, pw)
        if independent:
            c_re = s0re_ref[pl.ds(bi, 1), :]
            c_im = s0im_ref[pl.ds(bi, 1), :]
        else:
            c_re, c_im = carry
        s_re, s_im = _cmul_add(pk_re, pk_im, c_re, c_im, x_re, x_im)
        st_sc[pl.ds(r0, V7X_SUBLANES), 0:gp] = s_re
        st_sc[pl.ds(r0, V7X_SUBLANES), gp:2 * gp] = s_im
        last_re = s_re[V7X_SUBLANES - 1:V7X_SUBLANES, :]
        last_im = s_im[V7X_SUBLANES - 1:V7X_SUBLANES, :]
        if independent:
            s1re_ref[pl.ds(bi, 1), :] = last_re
            s1im_ref[pl.ds(bi, 1), :] = last_im
            return carry
        return last_re, last_im

    if independent:
        lax.fori_loop(0, nblk, body, 0)
    else:
        c_re, c_im = lax.fori_loop(0, nblk, body, (car_sc[0:1, :], car_sc[1:2, :]))
        car_sc[0:1, :] = c_re
        car_sc[1:2, :] = c_im

        @pl.when(ti == pl.num_programs(1) - 1)
        def _():
            s1re_ref[...] = c_re
            s1im_ref[...] = c_im

    y = _bdot(st_sc[...], cc_ref[...]) + d_ref[...] * u
    z = _bdot(jax.nn.gelu(y), gw_ref[...]) + gb_ref[...]
    y_ref[...] = (z[:, :W_C] * jax.nn.sigmoid(z[:, W_C:])).astype(y_ref.dtype)


def s5_mixer(zc, lam_re, lam_im, bb, cc, layer, d, glu_w, glu_b, s0_re, s0_im, nb, t, row0, prev=None):
    n_tok = zc.shape[0]
    gp = G_C * P_C
    independent = t == V7X_SUBLANES
    aliased = prev is not None
    if independent:
        seqs = S5_SAMPLE_SEQS
        rows = seqs * t
        grid = (nb // seqs, 1)
        st_spec = pl.BlockSpec((seqs, gp), lambda b, i: (b, 0))
    else:
        rows = S5_TILE
        grid = (nb, t // rows)
        st_spec = pl.BlockSpec((1, gp), lambda b, i: (b, 0))
    nt = grid[1]
    rb0 = row0 // rows
    row_map = lambda b, i: (rb0 + b * nt + i, 0)
    const = lambda b, i: (0, 0)
    by_layer = lambda b, i: (layer, 0, 0)
    in_specs = [
        pl.BlockSpec((rows, W_C), row_map),
        pl.BlockSpec((None, 1, gp), by_layer),
        pl.BlockSpec((None, 1, gp), by_layer),
        pl.BlockSpec((None,) + bb.shape[1:], by_layer),
        pl.BlockSpec((None,) + cc.shape[1:], by_layer),
        pl.BlockSpec((1, W_C), const),
        pl.BlockSpec((None,) + glu_w.shape[1:], by_layer),
        pl.BlockSpec((1, 2 * W_C), const),
        st_spec,
        st_spec,
    ]
    args = [zc, lam_re, lam_im, bb, cc, d.reshape(1, W_C), glu_w, glu_b.reshape(1, 2 * W_C), s0_re, s0_im]
    aliases = {}
    if aliased:
        in_specs.append(pl.BlockSpec(memory_space=pl.ANY))
        args.append(prev)
        aliases = {len(args) - 1: 0}
    if independent:
        s1_shape = jax.ShapeDtypeStruct((nb, gp), F32)
    else:
        s1_shape = jax.ShapeDtypeStruct((nb, 1, gp), F32)
        st_out = pl.BlockSpec((None, 1, gp), lambda b, i: (b, 0, 0))
    out_specs = [pl.BlockSpec((rows, W_C), row_map)] + ([st_spec, st_spec] if independent else [st_out, st_out])
    if not independent:
        args[8] = s0_re.reshape(nb, 1, gp)
        args[9] = s0_im.reshape(nb, 1, gp)
        in_specs[8] = in_specs[9] = pl.BlockSpec((None, 1, gp), lambda b, i: (b, 0, 0))
    yc, s1_re, s1_im = pl.pallas_call(
        functools.partial(_s5_kernel, rows=rows, independent=independent, aliased=aliased),
        grid=grid,
        in_specs=in_specs,
        out_specs=out_specs,
        out_shape=[jax.ShapeDtypeStruct((n_tok, W_C), BF16), s1_shape, s1_shape],
        scratch_shapes=[pltpu.VMEM((rows, 2 * gp), F32), pltpu.VMEM((V7X_SUBLANES, gp), F32)],
        input_output_aliases=aliases,
        compiler_params=_cparams("parallel", "arbitrary"),
        name="s5_sample" if independent else "s5_prompt",
    )(*args)
    return yc, s1_re.reshape(nb, G_C, P_C), s1_im.reshape(nb, G_C, P_C)


def _xattn_kernel(q_ref, mk_ref, mv_ref, o_ref):
    q = q_ref[...] * (DH_X ** -0.5)
    outs = []
    for h in range(H_X):
        s = _bdot_nt(q[:, h * DH_X:(h + 1) * DH_X], mk_ref[h])
        m = jnp.max(s, axis=-1, keepdims=True)
        p = jnp.exp(s - m)
        l = jnp.sum(p, axis=-1, keepdims=True)
        outs.append(_bdot(p, mv_ref[h]) / l)
    o_ref[...] = jnp.concatenate(outs, axis=-1).astype(o_ref.dtype)


def _xattn_native_kernel(q_ref, mk_ref, mv_ref, _prev, o_ref, bias_sc, *, tq, seqs, n_mem):
    rows = seqs * tq
    n_keys = seqs * n_mem * H_X

    @pl.when(pl.program_id(0) == 0)
    def _():
        rr = lax.broadcasted_iota(jnp.int32, (H_X * rows, n_keys), 0)
        cc = lax.broadcasted_iota(jnp.int32, (H_X * rows, n_keys), 1)
        own = (cc // (n_mem * H_X) == (rr % rows) // tq) & (cc % H_X == rr // rows)
        bias_sc[...] = jnp.where(own, 0.0, NEG_INF)

    q = q_ref[...] * (DH_X ** -0.5)
    q_all = jnp.concatenate([q[:, h * DH_X:(h + 1) * DH_X] for h in range(H_X)], axis=0)
    k2 = mk_ref[...].reshape(n_keys, DH_X)
    v2 = mv_ref[...].reshape(n_keys, DH_X)
    s = _bdot_nt(q_all, k2) + bias_sc[...]
    m = jnp.max(s, axis=-1, keepdims=True)
    p = jnp.exp(s - m)
    l = jnp.sum(p, axis=-1, keepdims=True)
    o = _bdot(p, v2) / l
    o_ref[...] = jnp.concatenate([o[h * rows:(h + 1) * rows, :] for h in range(H_X)], axis=-1).astype(o_ref.dtype)


def cross_attn_native(qx, mem_k, mem_v, layer, nb, t, row0, prev):
    n_tok, d = qx.shape
    n_mem = mem_k.shape[2]
    seqs = XATTN_SAMPLE_SEQS
    rows = seqs * t
    rb0 = row0 // rows
    mem_spec = pl.BlockSpec((None, seqs, n_mem, H_X, DH_X), lambda b: (layer, b, 0, 0, 0))
    return pl.pallas_call(
        functools.partial(_xattn_native_kernel, tq=t, seqs=seqs, n_mem=n_mem),
        grid=(nb // seqs,),
        in_specs=[pl.BlockSpec((rows, d), lambda b: (rb0 + b, 0)), mem_spec, mem_spec,
                  pl.BlockSpec(memory_space=pl.ANY)],
        out_specs=pl.BlockSpec((rows, d), lambda b: (rb0 + b, 0)),
        out_shape=jax.ShapeDtypeStruct((n_tok, d), prev.dtype),
        scratch_shapes=[pltpu.VMEM((H_X * rows, seqs * n_mem * H_X), F32)],
        input_output_aliases={3: 0},
        compiler_params=_cparams("arbitrary"),
        name="cross_attn_sample",
    )(qx, mem_k, mem_v, prev)


def cross_attn(qx, mk, mv, layer, n_mem, nb, t, out_dtype):
    n_tok, d = qx.shape
    tq = XATTN_TILE
    nq = t // tq
    mem_spec = pl.BlockSpec((None, H_X, n_mem, DH_X), lambda b, i: (layer, 0, b, 0))
    return pl.pallas_call(
        _xattn_kernel,
        grid=(nb, nq),
        in_specs=[pl.BlockSpec((tq, d), lambda b, i: (b * nq + i, 0)), mem_spec, mem_spec],
        out_specs=pl.BlockSpec((tq, d), lambda b, i: (b * nq + i, 0)),
        out_shape=jax.ShapeDtypeStruct((n_tok, d), out_dtype),
        compiler_params=_cparams("parallel", "arbitrary"),
        name="cross_attn_prompt",
    )(qx, mk, mv)


def kernel(x_prompt, x_sample, mem_prompt, cache_k, cache_v, page_table, cache_mem_k, cache_mem_v,
           state_mlstm_c, state_mlstm_n, state_mlstm_m, state_ssm_re, state_ssm_im,
           ln_g, ln_b, ffn1_wg, ffn1_wu, ffn1_wd, ffn2_wg, ffn2_wu, ffn2_wd, w_in, b_in,
           mlstm_norm_g, diff_lam, diff_norm_g, ssm_a_re, ssm_a_im, ssm_log_dt, ssm_b_re, ssm_b_im,
           ssm_c_re, ssm_c_im, ssm_d, ssm_glu_w, ssm_glu_b, w_out, cross_wq, cross_wk, cross_wv, cross_wo):
    bp, tp, d = x_prompt.shape
    bs, ts, _ = x_sample.shape
    depth = ln_g.shape[0]
    n_mem = mem_prompt.shape[1]
    n_p = bp * tp
    gp = G_C * P_C
    alpha = (2.0 * depth) ** 0.25

    cast = lambda w: w.astype(BF16)
    ffn1_wg, ffn1_wu, ffn1_wd = cast(ffn1_wg), cast(ffn1_wu), cast(ffn1_wd)
    ffn2_wg, ffn2_wu, ffn2_wd = cast(ffn2_wg), cast(ffn2_wu), cast(ffn2_wd)
    w_out_b, wq_b, wo_b, glu_w_b = cast(w_out), cast(cross_wq), cast(cross_wo), cast(ssm_glu_w)
    w_in_p, b_in_p = _pack_w_in(w_in, b_in)
    lam_re, lam_im, bb_re, bb_im = s5_discretize(ssm_a_re, ssm_a_im, ssm_log_dt, ssm_b_re, ssm_b_im)
    bb = jnp.concatenate([_block_diag_in(bb_re), _block_diag_in(bb_im)], axis=-1).astype(BF16)
    cc = jnp.concatenate([_block_diag_out(ssm_c_re), -_block_diag_out(ssm_c_im)], axis=1).astype(BF16)
    alibi_q, alibi_k = _alibi_tables(tp)

    p_mk, p_mv, p_mkh, p_mvh = mem_kv(mem_prompt.reshape(bp * n_mem, d), cast(cross_wk), cast(cross_wv))
    p_mk = p_mk.reshape(depth, bp, n_mem, H_X, DH_X)
    p_mv = p_mv.reshape(depth, bp, n_mem, H_X, DH_X)

    x = None
    zeros_s = jnp.zeros((bp, gp), F32)
    p_k = jnp.zeros((depth, n_p, H_B, DV_B), F32)
    p_v = jnp.zeros((depth, n_p, H_B, DV_B), F32)
    p_st, s_st = [], []
    for l in range(depth):
        lam_init = 0.8 - 0.6 * math.exp(-0.3 * l)
        srcs = [x_prompt.reshape(n_p, d), x_sample.reshape(bs * ts, d)] if l == 0 else [x]
        x = ffn_ln(srcs, ffn1_wg, ffn1_wu, ffn1_wd, l, ln_g[l, 0], ln_b[l, 0], alpha)
        za, zb, zc, zg, p_k, p_v, s_k, s_v = proj_in(x, w_in_p, b_in_p, p_k, p_v, l, n_p)

        ya, pc, pn, pm = mlstm_prompt(za, zg, mlstm_norm_g[l], bp, tp)
        ya, sc, sn, sm = mlstm_sample(za, zg, mlstm_norm_g[l], bs, ts, n_p,
                                      state_mlstm_c, state_mlstm_n[l], state_mlstm_m[l], l, ya)

        yb = diff_attn_prompt(zb, alibi_q, alibi_k, diff_lam[l], diff_norm_g[l], bp, tp, lam_init)
        yb = diff_attn_sample(zb, cache_k, cache_v, l, page_table, diff_lam[l], diff_norm_g[l],
                              bs, ts, n_p, lam_init, yb)

        s5_args = (lam_re, lam_im, bb, cc, l, ssm_d[l], glu_w_b, ssm_glu_b[l])
        yc, psr, psi = s5_mixer(zc, *s5_args, zeros_s, zeros_s, bp, tp, 0)
        yc, ssr, ssi = s5_mixer(zc, *s5_args, state_ssm_re[l].reshape(bs, gp), state_ssm_im[l].reshape(bs, gp),
                                bs, ts, n_p, prev=yc)

        x, qx = mix_out_ln(ya, yb, yc, w_out_b, wq_b, l, x, ln_g[l, 1], ln_b[l, 1], alpha)
        o = cross_attn(qx, p_mkh, p_mvh, l, n_mem, bp, tp, BF16)
        o = cross_attn_native(qx, cache_mem_k, cache_mem_v, l, bs, ts, n_p, o)
        x = out_ffn_ln(o, wo_b, ffn2_wg, ffn2_wu, ffn2_wd, l, x, ln_g[l, 2], ln_b[l, 2], ln_g[l, 3], ln_b[l, 3], alpha,
                       split_rows=n_p if l == depth - 1 else None)

        p_st.append((pc, pn, pm, psr, psi))
        s_st.append((s_k.reshape(bs, ts, H_B, DV_B), s_v.reshape(bs, ts, H_B, DV_B), sc, sn, sm, ssr, ssi))

    p_c, p_n, p_m, p_sr, p_si = [jnp.stack(a) for a in zip(*p_st)]
    s_k, s_v, s_c, s_n, s_m, s_sr, s_si = [jnp.stack(a) for a in zip(*s_st)]
    p_k = p_k.reshape(depth, bp, tp, H_B, DV_B)
    p_v = p_v.reshape(depth, bp, tp, H_B, DV_B)
    yp = x[0].reshape(bp, tp, d)
    ys = x[1].reshape(bs, ts, d)
    return (yp, ys, p_k, p_v, p_mk, p_mv, p_c, p_n, p_m, p_sr, p_si, s_k, s_v, s_c, s_n, s_m, s_sr, s_si)
```

```python
import functools
import math

import jax
import jax.numpy as jnp
from jax import lax
from jax.experimental import pallas as pl
from jax.experimental.pallas import tpu as pltpu

F32 = jnp.float32
BF16 = jnp.bfloat16

D_MODEL = 1024
PAGE_SIZE = 128
W_A = D_MODEL // 4
W_B = D_MODEL // 2
W_C = D_MODEL - W_A - W_B
H_A = 4
DH_A = W_A // H_A
H_B = 4
DV_B = W_B // H_B
DK_B = DV_B // 2
GC = 16
G_C = W_C // GC
P_C = 64
H_X = 4
DH_X = D_MODEL // H_X
LN_EPS = 1e-5
NORM_EPS = 1e-6

OFF_AQ = 0
OFF_AK = OFF_AQ + W_A
OFF_AV = OFF_AK + W_A
OFF_AO = OFF_AV + W_A
OFF_AI = OFF_AO + W_A
OFF_AF = OFF_AI + H_A
OFF_BQ = OFF_AF + H_A
OFF_BK = OFF_BQ + H_B * 2 * DK_B
OFF_BV = OFF_BK + H_B * 2 * DK_B
OFF_CU = OFF_BV + W_B
N_IN = OFF_CU + W_C

V7X_LANES = 128
V7X_SUBLANES = 8
V7X_VMEM_LIMIT_BYTES = 56 * 1024 * 1024
V7X_VMEM_LIMIT_LARGE_BYTES = 60 * 1024 * 1024

TOKEN_TILE = 1024
FF_TILE = 256
PROJ_TILE = 512
OUT_TILE = 1024
ATTN_TILE = 512
ATTN_ROWS = 256
XATTN_TILE = 1024
S5_TILE = 1024
PAGES_PER_STEP = 16
ATTN_SAMPLE_SEQS = 2
XATTN_SAMPLE_SEQS = 4
S5_SAMPLE_SEQS = 16
MLSTM_PROMPT_CHUNK = 512
MLSTM_UNITS = 16
ALIBI_SPLIT = 64

NEG_INF = float("-inf")


def _cparams(*sem, vmem_limit=V7X_VMEM_LIMIT_BYTES):
    return pltpu.CompilerParams(dimension_semantics=sem, vmem_limit_bytes=vmem_limit)


def _bdot(a, b):
    return jnp.dot(a.astype(BF16), b.astype(BF16), preferred_element_type=F32)


def _bdot_nt(a, b):
    return lax.dot_general(a.astype(BF16), b.astype(BF16), (((1,), (1,)), ((), ())),
                           preferred_element_type=F32)


def _tile_lanes(x, n):
    return x if n == 1 else jnp.concatenate([x] * n, axis=1)


def _layer_norm(y, g, b):
    mu = jnp.mean(y, axis=-1, keepdims=True)
    yc = y - mu
    var = jnp.mean(yc * yc, axis=-1, keepdims=True)
    return yc * lax.rsqrt(var + LN_EPS) * g + b


def _swiglu(xb, wg_ref, wu_ref, wd_ref, h_sc, tf):
    for c in range(wg_ref.shape[1] // tf):
        cols = slice(c * tf, (c + 1) * tf)
        hg = jnp.dot(xb, wg_ref[:, cols], preferred_element_type=F32)
        hu = jnp.dot(xb, wu_ref[:, cols], preferred_element_type=F32)
        h_sc[:, cols] = ((hg * jax.nn.sigmoid(hg)) * hu).astype(BF16)
    return jnp.dot(h_sc[...], wd_ref[...], preferred_element_type=F32)


def _ffn_ln_kernel(*refs, alpha, tf, n_src, first_tiles):
    srcs = refs[:n_src]
    wg_ref, wu_ref, wd_ref, g_ref, b_ref, o_ref, h_sc = refs[n_src:]

    def run(x_ref):
        x = x_ref[...]
        ff = _swiglu(x.astype(BF16), wg_ref, wu_ref, wd_ref, h_sc, tf)
        o_ref[...] = _layer_norm(alpha * x + 0.5 * ff, g_ref[...], b_ref[...])

    if n_src == 1:
        run(srcs[0])
    else:
        i = pl.program_id(0)
        pl.when(i < first_tiles)(lambda: run(srcs[0]))
        pl.when(i >= first_tiles)(lambda: run(srcs[1]))


def ffn_ln(xs, wg, wu, wd, layer, g, b, alpha):
    d = xs[0].shape[1]
    dff = wg.shape[2]
    tm, tf = TOKEN_TILE, FF_TILE
    first = xs[0].shape[0] // tm
    n = sum(x.shape[0] for x in xs)
    src_specs = [pl.BlockSpec((tm, d), lambda i: (jnp.minimum(i, first - 1), 0))]
    if len(xs) == 2:
        src_specs.append(pl.BlockSpec((tm, d), lambda i: (jnp.maximum(i - first, 0), 0)))
    resident = pl.Buffered(1)
    return pl.pallas_call(
        functools.partial(_ffn_ln_kernel, alpha=alpha, tf=tf, n_src=len(xs), first_tiles=first),
        grid=(n // tm,),
        in_specs=src_specs + [
            pl.BlockSpec((None, d, dff), lambda i: (layer, 0, 0), pipeline_mode=resident),
            pl.BlockSpec((None, d, dff), lambda i: (layer, 0, 0), pipeline_mode=resident),
            pl.BlockSpec((None, dff, d), lambda i: (layer, 0, 0), pipeline_mode=resident),
            pl.BlockSpec((1, d), lambda i: (0, 0)),
            pl.BlockSpec((1, d), lambda i: (0, 0)),
        ],
        out_specs=pl.BlockSpec((tm, d), lambda i: (i, 0)),
        out_shape=jax.ShapeDtypeStruct((n, d), F32),
        scratch_shapes=[pltpu.VMEM((tm, dff), BF16)],
        compiler_params=_cparams("arbitrary"),
        name="ffn_ln",
    )(*xs, wg, wu, wd, g.reshape(1, d), b.reshape(1, d))


def _out_ffn_ln_kernel(o_ref, wo_ref, x_ref, g1_ref, b1_ref, wg_ref, wu_ref, wd_ref, g2_ref, b2_ref, *rest,
                       alpha, tf, split):
    outs, h_sc = rest[:-1], rest[-1]
    y1 = _layer_norm(alpha * x_ref[...] + jnp.dot(o_ref[...].astype(BF16), wo_ref[...], preferred_element_type=F32),
                     g1_ref[...], b1_ref[...])
    ff = _swiglu(y1.astype(BF16), wg_ref, wu_ref, wd_ref, h_sc, tf)
    y2 = _layer_norm(alpha * y1 + 0.5 * ff, g2_ref[...], b2_ref[...])
    if split is None:
        outs[0][...] = y2
    else:
        i = pl.program_id(0)

        @pl.when(i < split)
        def _():
            outs[0][...] = y2

        @pl.when(i >= split)
        def _():
            outs[1][...] = y2


def out_ffn_ln(o, wo, wg, wu, wd, layer, x, g1, b1, g2, b2, alpha, split_rows=None):
    n, d = x.shape
    dff = wg.shape[2]
    tm, tf = TOKEN_TILE, FF_TILE
    resident = pl.Buffered(1)
    by_layer = lambda i: (layer, 0, 0)
    vec = pl.BlockSpec((1, d), lambda i: (0, 0))
    if split_rows is None:
        split = None
        out_specs = [pl.BlockSpec((tm, d), lambda i: (i, 0))]
        out_shape = [jax.ShapeDtypeStruct((n, d), F32)]
    else:
        split = split_rows // tm
        out_specs = [pl.BlockSpec((tm, d), lambda i: (jnp.minimum(i, split - 1), 0)),
                     pl.BlockSpec((tm, d), lambda i: (jnp.maximum(i - split, 0), 0))]
        out_shape = [jax.ShapeDtypeStruct((split_rows, d), F32), jax.ShapeDtypeStruct((n - split_rows, d), F32)]
    res = pl.pallas_call(
        functools.partial(_out_ffn_ln_kernel, alpha=alpha, tf=tf, split=split),
        grid=(n // tm,),
        in_specs=[
            pl.BlockSpec((tm, d), lambda i: (i, 0)),
            pl.BlockSpec((None, d, d), by_layer, pipeline_mode=resident),
            pl.BlockSpec((tm, d), lambda i: (i, 0)),
            vec, vec,
            pl.BlockSpec((None, d, dff), by_layer, pipeline_mode=resident),
            pl.BlockSpec((None, d, dff), by_layer, pipeline_mode=resident),
            pl.BlockSpec((None, dff, d), by_layer, pipeline_mode=resident),
            vec, vec,
        ],
        out_specs=out_specs,
        out_shape=out_shape,
        scratch_shapes=[pltpu.VMEM((tm, dff), BF16)],
        compiler_params=_cparams("arbitrary", vmem_limit=V7X_VMEM_LIMIT_LARGE_BYTES),
        name="out_ffn_ln",
    )(o, wo, x, g1.reshape(1, d), b1.reshape(1, d), wg, wu, wd, g2.reshape(1, d), b2.reshape(1, d))
    return res[0] if split_rows is None else res


N_ZA = 4 * W_A
N_ZB = 3 * W_B
N_ZC = W_C
N_ZG = V7X_LANES
N_PROJ = N_ZA + N_ZB + N_ZC + N_ZG


def _proj_in_kernel(x_ref, w_ref, b_ref, _kp_prev, _vp_prev, za_ref, zb_ref, zc_ref, zg_ref,
                    kp_ref, vp_ref, ks_ref, vs_ref, kv_sc, *, prompt_tiles):
    i = pl.program_id(0)
    xb = x_ref[...].astype(BF16)
    off = 0
    for ref in (za_ref, zb_ref, zc_ref, zg_ref):
        width = ref.shape[1]
        z = jnp.dot(xb, w_ref[:, off:off + width], preferred_element_type=F32) + b_ref[:, off:off + width]
        ref[...] = z.astype(ref.dtype)
        if ref is zb_ref:
            kv_sc[...] = z[:, W_B:]
        off += width

    def write_kv(k_out, v_out):
        for h in range(H_B):
            k_out[:, h, :] = kv_sc[:, h * DV_B:(h + 1) * DV_B]
            v_out[:, h, :] = kv_sc[:, W_B + h * DV_B:W_B + (h + 1) * DV_B]

    @pl.when(i < prompt_tiles)
    def _():
        write_kv(kp_ref, vp_ref)

    @pl.when(i >= prompt_tiles)
    def _():
        write_kv(ks_ref, vs_ref)


def proj_in(x, w, b, kp_prev, vp_prev, layer, n_prompt):
    n, d = x.shape
    tm = PROJ_TILE
    pt = n_prompt // tm
    widths = (N_ZA, N_ZB, N_ZC, N_ZG)
    kv_blk = (tm, H_B, DV_B)
    p_map = lambda i: (layer, jnp.minimum(i, pt - 1), 0, 0)
    s_map = lambda i: (jnp.maximum(i - pt, 0), 0, 0)
    kv_s = jax.ShapeDtypeStruct((n - n_prompt, H_B, DV_B), F32)
    return pl.pallas_call(
        functools.partial(_proj_in_kernel, prompt_tiles=pt),
        grid=(n // tm,),
        in_specs=[
            pl.BlockSpec((tm, d), lambda i: (i, 0)),
            pl.BlockSpec((None, d, N_PROJ), lambda i: (layer, 0, 0)),
            pl.BlockSpec((None, 1, N_PROJ), lambda i: (layer, 0, 0)),
            pl.BlockSpec(memory_space=pl.ANY),
            pl.BlockSpec(memory_space=pl.ANY),
        ],
        out_specs=[pl.BlockSpec((tm, wd), lambda i: (i, 0)) for wd in widths]
        + [pl.BlockSpec((None,) + kv_blk, p_map)] * 2 + [pl.BlockSpec(kv_blk, s_map)] * 2,
        out_shape=[jax.ShapeDtypeStruct((n, wd), BF16 if wd == N_ZB else F32) for wd in widths]
        + [jax.ShapeDtypeStruct(kp_prev.shape, F32)] * 2 + [kv_s, kv_s],
        scratch_shapes=[pltpu.VMEM((tm, 2 * W_B), F32)],
        input_output_aliases={3: 4, 4: 5},
        compiler_params=_cparams("arbitrary"),
        name="proj_in",
    )(x, w, b, kp_prev, vp_prev)


def _pack_w_in(w_in, b_in):
    def cols(a):
        pad = jnp.zeros(a.shape[:-1] + (N_ZG - 2 * H_A,), a.dtype)
        return jnp.concatenate([a[..., OFF_AQ:OFF_AI], a[..., OFF_BQ:OFF_CU], a[..., OFF_CU:N_IN],
                                a[..., OFF_AI:OFF_BQ], pad], axis=-1)
    return cols(w_in).astype(BF16), cols(b_in)[:, None, :]


def _mem_kv_kernel(x_ref, wk_ref, wv_ref, mk_ref, mv_ref, mkh_ref, mvh_ref):
    xb = x_ref[...].astype(BF16)
    for w_ref, o_ref, oh_ref in ((wk_ref, mk_ref, mkh_ref), (wv_ref, mv_ref, mvh_ref)):
        r = jnp.dot(xb, w_ref[...], preferred_element_type=F32)
        for h in range(H_X):
            o_ref[:, h, :] = r[:, h * DH_X:(h + 1) * DH_X]
            oh_ref[h] = r[:, h * DH_X:(h + 1) * DH_X].astype(BF16)


def mem_kv(x, wk, wv):
    m, d = x.shape
    depth = wk.shape[0]
    tm = min(m, TOKEN_TILE)
    shp = jax.ShapeDtypeStruct((depth, m, H_X, DH_X), F32)
    shp_h = jax.ShapeDtypeStruct((depth, H_X, m, DH_X), BF16)
    w_spec = pl.BlockSpec((None, d, d), lambda i, l: (l, 0, 0))
    o_spec = pl.BlockSpec((None, tm, H_X, DH_X), lambda i, l: (l, i, 0, 0))
    oh_spec = pl.BlockSpec((None, H_X, tm, DH_X), lambda i, l: (l, 0, i, 0))
    return pl.pallas_call(
        _mem_kv_kernel,
        grid=(m // tm, depth),
        in_specs=[pl.BlockSpec((tm, d), lambda i, l: (i, 0)), w_spec, w_spec],
        out_specs=[o_spec, o_spec, oh_spec, oh_spec],
        out_shape=[shp, shp, shp_h, shp_h],
        compiler_params=_cparams("parallel", "arbitrary"),
        name="mem_kv",
    )(x, wk, wv)


def _mix_out_ln_kernel(ya_ref, yb_ref, yc_ref, w_ref, x_ref, g_ref, b_ref, wq_ref, o_ref, q_ref, *, alpha):
    acc = None
    off = 0
    for p in (ya_ref, yb_ref, yc_ref):
        width = p.shape[1]
        t = jnp.dot(p[...].astype(BF16), w_ref[off:off + width, :], preferred_element_type=F32)
        acc = t if acc is None else acc + t
        off += width
    y = _layer_norm(alpha * x_ref[...] + acc, g_ref[...], b_ref[...])
    o_ref[...] = y
    q_ref[...] = jnp.dot(y.astype(BF16), wq_ref[...], preferred_element_type=F32).astype(q_ref.dtype)


def mix_out_ln(ya, yb, yc, w_out, wq, layer, x, g, b, alpha):
    n, d = x.shape
    tm = OUT_TILE
    rows = lambda width: pl.BlockSpec((tm, width), lambda i: (i, 0))
    weight = lambda w: pl.BlockSpec((None,) + w.shape[1:], lambda i: (layer, 0, 0))
    vec = pl.BlockSpec((1, d), lambda i: (0, 0))
    return pl.pallas_call(
        functools.partial(_mix_out_ln_kernel, alpha=alpha),
        grid=(n // tm,),
        in_specs=[rows(ya.shape[1]), rows(yb.shape[1]), rows(yc.shape[1]), weight(w_out), rows(d), vec, vec,
                  weight(wq)],
        out_specs=[rows(d), rows(wq.shape[2])],
        out_shape=[jax.ShapeDtypeStruct((n, d), F32),
                   jax.ShapeDtypeStruct((n, wq.shape[2]), BF16)],
        compiler_params=_cparams("parallel"),
        name="mix_out_ln",
    )(ya, yb, yc, w_out, x, g.reshape(1, d), b.reshape(1, d), wq)


def _log_sigmoid(x):
    return jnp.minimum(x, 0.0) - jnp.log1p(jnp.exp(-jnp.abs(x)))


def _mlstm_prompt_kernel(za_ref, zg_ref, g_ref, ya_ref, c1_ref, n1_ref, m1_ref, c_sc, n_sc, m_sc, *, chunk):
    L = chunk
    ti = pl.program_id(1)

    @pl.when(ti == 0)
    def _():
        c_sc[...] = jnp.zeros_like(c_sc)
        n_sc[...] = jnp.zeros_like(n_sc)
        m_sc[...] = jnp.zeros_like(m_sc)

    row = lax.broadcasted_iota(jnp.int32, (L, L), 0)
    col = lax.broadcasted_iota(jnp.int32, (L, L), 1)
    causal = col <= row
    tril = causal.astype(F32)
    sel_r = lax.broadcasted_iota(jnp.int32, (V7X_SUBLANES, V7X_LANES), 0)
    sel_c = lax.broadcasted_iota(jnp.int32, (V7X_SUBLANES, V7X_LANES), 1)
    sel = (sel_r == sel_c).astype(F32)
    lane_g = lax.broadcasted_iota(jnp.int32, (L, N_ZG), 1)
    lane_m = lax.broadcasted_iota(jnp.int32, (1, V7X_LANES), 1)
    rep = lambda col_: jnp.broadcast_to(col_, (L, V7X_LANES))
    wide = lambda x, n: _tile_lanes(x, n // V7X_LANES) if n > V7X_LANES else x[:, :n]

    gates = zg_ref[...]
    gl = jnp.where(lane_g < H_A, gates, _log_sigmoid(gates))
    bcum = jnp.dot(tril, gl, precision=lax.Precision.HIGHEST, preferred_element_type=F32)
    mixed = jnp.where(lane_g < H_A, gates, bcum)
    t_rows = lax.dot_general(sel, mixed, (((1,), (1,)), ((), ())),
                             precision=lax.Precision.HIGHEST, preferred_element_type=F32)
    outs = []
    m_out = jnp.zeros((1, V7X_LANES), F32)
    for h in range(H_A):
        q = za_ref[:, h * DH_A:(h + 1) * DH_A]
        k = za_ref[:, W_A + h * DH_A:W_A + (h + 1) * DH_A] * (DH_A ** -0.5)
        v = za_ref[:, 2 * W_A + h * DH_A:2 * W_A + (h + 1) * DH_A]
        og = za_ref[:, 3 * W_A + h * DH_A:3 * W_A + (h + 1) * DH_A]
        ig_c = rep(gates[:, h:h + 1])
        b_c = rep(bcum[:, H_A + h:H_A + h + 1])
        ig_row = t_rows[h:h + 1, :]
        b_row = t_rows[H_A + h:H_A + h + 1, :]
        c, n, m_prev = c_sc[h], n_sc[h:h + 1, :], m_sc[:, h:h + 1]

        dmat = jnp.where(causal, wide(b_c, L) - b_row + ig_row, NEG_INF)
        a = rep(jnp.max(dmat, axis=-1, keepdims=True))
        s = _bdot_nt(q, k) * jnp.exp(dmat - wide(a, L))
        n_loc = _bdot(s, v)
        d_loc = rep(jnp.sum(s, axis=-1, keepdims=True))
        a_last = a[L - 1:L, :]
        b_last = b_c[L - 1:L, :]
        wk = jnp.exp(b_last - b_c + ig_c - a_last)[:, :DH_A]
        u_loc = lax.dot_general((wk * v).astype(BF16), k.astype(BF16), (((0,), (0,)), ((), ())),
                                preferred_element_type=F32)
        nu_loc = jnp.sum(wk * k, axis=0, keepdims=True)

        inter = b_c + m_prev
        m_row = jnp.maximum(inter, a)
        r = jnp.exp(a - m_row)
        w_inter = jnp.exp(inter - m_row)
        num = r[:, :DH_A] * n_loc + w_inter[:, :DH_A] * _bdot_nt(q, c)
        den = r * d_loc + w_inter * rep(jnp.sum(q * n, axis=-1, keepdims=True))
        hh = num / jnp.maximum(jnp.abs(den), jnp.exp(-m_row))[:, :DH_A]
        m_new = m_row[L - 1:L, 0:1]
        decay = jnp.exp(b_last[:, 0:1] + m_prev - m_new)
        e_loc = jnp.exp(a_last[:, 0:1] - m_new)
        c_sc[h] = decay * c + e_loc * u_loc
        n_sc[h:h + 1, :] = decay * n + e_loc * nu_loc
        m_out = jnp.where(lane_m == h, m_new, m_out)

        hn = hh * lax.rsqrt(jnp.mean(hh * hh, axis=-1, keepdims=True) + NORM_EPS)
        outs.append(jax.nn.sigmoid(og) * hn)
    ya_ref[...] = (jnp.concatenate(outs, axis=-1) * g_ref[...]).astype(ya_ref.dtype)
    m_sc[...] = m_out

    @pl.when(ti == pl.num_programs(1) - 1)
    def _():
        c1_ref[0] = c_sc[...]
        n1_ref[0] = n_sc[...]
        m1_ref[0] = m_sc[...]


def _mlstm_sample_kernel(za_ref, zg_ref, g_ref, c0_ref, n0_ref, m0_ref, _prev, ya_ref, c1_ref, n1_ref, m1_ref,
                         *, t_len, seqs):
    R = seqs * t_len
    row = lax.broadcasted_iota(jnp.int32, (R, R), 0)
    col = lax.broadcasted_iota(jnp.int32, (R, R), 1)
    mask = (row // t_len == col // t_len) & (col <= row)
    tril = mask.astype(F32)
    sel_r = lax.broadcasted_iota(jnp.int32, (V7X_SUBLANES, V7X_LANES), 0)
    sel_c = lax.broadcasted_iota(jnp.int32, (V7X_SUBLANES, V7X_LANES), 1)
    sel = (sel_r == sel_c).astype(F32)
    lane_g = lax.broadcasted_iota(jnp.int32, (R, N_ZG), 1)
    lane_m = lax.broadcasted_iota(jnp.int32, (seqs, 1, V7X_LANES), 2)
    rep = lambda c_: jnp.broadcast_to(c_, (R, V7X_LANES))
    wide = lambda x, n: _tile_lanes(x, n // V7X_LANES) if n > V7X_LANES else x[:, :n]
    per_seq = lambda x: x.reshape(seqs, t_len, x.shape[-1])
    last_rows = lambda x: jnp.broadcast_to(per_seq(x)[:, t_len - 1:t_len, :], (seqs, t_len, x.shape[-1])
                                           ).reshape(R, x.shape[-1])
    seq_rows = lambda x: jnp.broadcast_to(x, (seqs, t_len, x.shape[-1])).reshape(R, x.shape[-1])
    state_rows = lambda x: jnp.broadcast_to(x, (seqs, DH_A, x.shape[-1])).reshape(seqs * DH_A, x.shape[-1])
    own_r = lax.broadcasted_iota(jnp.int32, (R, seqs * DH_A), 0) // t_len
    own_c = lax.broadcasted_iota(jnp.int32, (R, seqs * DH_A), 1) // DH_A
    own = own_r == own_c
    odd_seq = (lax.broadcasted_iota(jnp.int32, (R, DH_A), 0) // t_len) % 2 == 1

    gates = zg_ref[...]
    gl = jnp.where(lane_g < H_A, gates, _log_sigmoid(gates))
    bcum = jnp.dot(tril, gl, precision=lax.Precision.HIGHEST, preferred_element_type=F32)
    mixed = jnp.where(lane_g < H_A, gates, bcum)
    t_rows = lax.dot_general(sel, mixed, (((1,), (1,)), ((), ())),
                             precision=lax.Precision.HIGHEST, preferred_element_type=F32)
    m0 = m0_ref[...]
    norm_g = g_ref[...]
    outs = []
    m_out = jnp.zeros((seqs, 1, V7X_LANES), F32)
    for h in range(H_A):
        q = za_ref[:, h * DH_A:(h + 1) * DH_A]
        k = za_ref[:, W_A + h * DH_A:W_A + (h + 1) * DH_A] * (DH_A ** -0.5)
        v = za_ref[:, 2 * W_A + h * DH_A:2 * W_A + (h + 1) * DH_A]
        og = za_ref[:, 3 * W_A + h * DH_A:3 * W_A + (h + 1) * DH_A]
        ig_c = rep(gates[:, h:h + 1])
        b_c = rep(bcum[:, H_A + h:H_A + h + 1])
        ig_row = t_rows[h:h + 1, :]
        b_row = t_rows[H_A + h:H_A + h + 1, :]
        m_prev = rep(seq_rows(m0)[:, h:h + 1])
        c_stack = c0_ref[:, h].reshape(seqs * DH_A, DH_A)
        n_rows = seq_rows(n0_ref[:, h:h + 1, :])

        dmat = jnp.where(mask, wide(b_c, R) - b_row + ig_row, NEG_INF)
        a = rep(jnp.max(dmat, axis=-1, keepdims=True))
        s = _bdot_nt(q, k) * jnp.exp(dmat - wide(a, R))
        n_loc = _bdot(s, v)
        d_loc = rep(jnp.sum(s, axis=-1, keepdims=True))
        a_last = last_rows(a)
        b_last = last_rows(b_c)
        wk = jnp.exp(b_last - b_c + ig_c - a_last)[:, :DH_A]
        vw = wk * v
        vw2 = jnp.concatenate([vw, vw], axis=1)
        u_stack = lax.dot_general(jnp.where(own, _tile_lanes(vw2, seqs // 2), 0.0).astype(BF16), k.astype(BF16),
                                  (((0,), (0,)), ((), ())), preferred_element_type=F32)
        nu = jnp.sum(per_seq(wk * k), axis=1, keepdims=True)

        inter = b_c + m_prev
        m_row = jnp.maximum(inter, a)
        r = jnp.exp(a - m_row)
        w_inter = jnp.exp(inter - m_row)
        x_all = jnp.where(own, _bdot_nt(q, c_stack), 0.0)
        fold = x_all[:, 0:V7X_LANES]
        for j in range(1, seqs * DH_A // V7X_LANES):
            fold = fold + x_all[:, j * V7X_LANES:(j + 1) * V7X_LANES]
        qc = jnp.where(odd_seq, fold[:, DH_A:], fold[:, :DH_A])
        num = r[:, :DH_A] * n_loc + w_inter[:, :DH_A] * qc
        den = r * d_loc + w_inter * rep(jnp.sum(q * n_rows, axis=-1, keepdims=True))
        hh = num / jnp.maximum(jnp.abs(den), jnp.exp(-m_row))[:, :DH_A]
        hn = hh * lax.rsqrt(jnp.mean(hh * hh, axis=-1, keepdims=True) + NORM_EPS)
        outs.append(jax.nn.sigmoid(og) * hn)

        m_new = per_seq(m_row)[:, t_len - 1:t_len, :]
        decay = jnp.exp(per_seq(b_c + m_prev)[:, t_len - 1:t_len, :] - m_new)
        e_loc = jnp.exp(per_seq(a)[:, t_len - 1:t_len, :] - m_new)
        c_new = (state_rows(decay)[:, :DH_A] * c_stack + state_rows(e_loc)[:, :DH_A] * u_stack)
        c1_ref[:, h] = c_new.reshape(seqs, DH_A, DH_A)
        n1_ref[:, h:h + 1, :] = decay[:, :, :DH_A] * n0_ref[:, h:h + 1, :] + e_loc[:, :, :DH_A] * nu
        m_out = jnp.where(lane_m == h, m_new, m_out)
    ya_ref[...] = (jnp.concatenate(outs, axis=-1) * norm_g).astype(ya_ref.dtype)
    m1_ref[...] = m_out


def _mlstm_out_shapes(n_tok, nb):
    return [
        jax.ShapeDtypeStruct((n_tok, W_A), BF16),
        jax.ShapeDtypeStruct((nb, H_A, DH_A, DH_A), F32),
        jax.ShapeDtypeStruct((nb, H_A, DH_A), F32),
        jax.ShapeDtypeStruct((nb, 1, V7X_LANES), F32),
    ]


def mlstm_prompt(za, zg, norm_g, nb, t):
    L = MLSTM_PROMPT_CHUNK
    nc = t // L
    row_map = lambda b, c: (b * nc + c, 0)
    st4 = lambda b, c: (b, 0, 0, 0)
    st3 = lambda b, c: (b, 0, 0)
    ya, c1, n1, m1 = pl.pallas_call(
        functools.partial(_mlstm_prompt_kernel, chunk=L),
        grid=(nb, nc),
        in_specs=[pl.BlockSpec((L, N_ZA), row_map), pl.BlockSpec((L, N_ZG), row_map),
                  pl.BlockSpec((1, W_A), lambda b, c: (0, 0))],
        out_specs=[pl.BlockSpec((L, W_A), row_map), pl.BlockSpec((1, H_A, DH_A, DH_A), st4),
                   pl.BlockSpec((1, H_A, DH_A), st3), pl.BlockSpec((1, 1, V7X_LANES), st3)],
        out_shape=_mlstm_out_shapes(za.shape[0], nb),
        scratch_shapes=[pltpu.VMEM((H_A, DH_A, DH_A), F32), pltpu.VMEM((H_A, DH_A), F32),
                        pltpu.VMEM((1, V7X_LANES), F32)],
        compiler_params=_cparams("parallel", "arbitrary"),
        name="mlstm_prompt",
    )(za, zg, norm_g.reshape(1, W_A))
    return ya, c1, n1, m1[:, 0, :H_A]


def mlstm_sample(za, zg, norm_g, nb, t, row0, c0, n0, m0, layer, prev):
    seqs = MLSTM_UNITS
    rows = seqs * t
    row_map = lambda b: (row0 // rows + b, 0)
    st4 = lambda b: (b, 0, 0, 0)
    st3 = lambda b: (b, 0, 0)
    m0p = jnp.pad(m0, ((0, 0), (0, V7X_LANES - H_A))).reshape(nb, 1, V7X_LANES)
    state_specs = [pl.BlockSpec((seqs, H_A, DH_A, DH_A), st4), pl.BlockSpec((seqs, H_A, DH_A), st3),
                   pl.BlockSpec((seqs, 1, V7X_LANES), st3)]
    ya, c1, n1, m1 = pl.pallas_call(
        functools.partial(_mlstm_sample_kernel, t_len=t, seqs=seqs),
        grid=(nb // seqs,),
        in_specs=[pl.BlockSpec((rows, N_ZA), row_map), pl.BlockSpec((rows, N_ZG), row_map),
                  pl.BlockSpec((1, W_A), lambda b: (0, 0)),
                  pl.BlockSpec((None, seqs, H_A, DH_A, DH_A), lambda b: (layer, b, 0, 0, 0))] + state_specs[1:]
        + [pl.BlockSpec(memory_space=pl.ANY)],
        out_specs=[pl.BlockSpec((rows, W_A), row_map)] + state_specs,
        out_shape=_mlstm_out_shapes(za.shape[0], nb),
        input_output_aliases={6: 0},
        compiler_params=_cparams("arbitrary"),
        name="mlstm_sample",
    )(za, zg, norm_g.reshape(1, W_A), c0, n0, m0p, prev)
    return ya, c1, n1, m1[:, 0, :H_A]


def _diff_lambda(lam_ref, lam_init):
    lp = lam_ref[...]
    d01 = jnp.sum(lp[0:1, :] * lp[1:2, :], axis=-1, keepdims=True)
    d23 = jnp.sum(lp[2:3, :] * lp[3:4, :], axis=-1, keepdims=True)
    return jnp.exp(d01) - jnp.exp(d23) + lam_init


def _alibi_slope(h):
    return jnp.where(h == 0, 2.0 ** -2, jnp.where(h == 1, 2.0 ** -4, jnp.where(h == 2, 2.0 ** -6, 2.0 ** -8)))


def _alibi_tables(t):
    slopes = (2.0 ** (-8.0 * jnp.arange(1, H_B + 1, dtype=F32) / H_B))[:, None]
    pos = jnp.arange(t, dtype=jnp.int32)
    hi = ((pos // ALIBI_SPLIT) * ALIBI_SPLIT).astype(F32)[None, :]
    lo = (pos % ALIBI_SPLIT).astype(F32)[None, :]
    ones = jnp.ones((H_B, t), F32)
    pad = jnp.zeros((H_B, t, DV_B - 4), F32)
    aq = jnp.concatenate([jnp.stack([slopes * ones, slopes * ones, -slopes * hi, -slopes * lo], axis=-1), pad], -1)
    ak = jnp.concatenate([jnp.stack([hi * ones, lo * ones, ones, ones], axis=-1), pad], -1)
    return aq.astype(BF16), ak.astype(BF16)


def _attn_prompt_kernel(lam_ref, g_ref, q_ref, aq_ref, k_ref, ak_ref, v_ref, o_ref,
                        qs_sc, m_sc, acc_sc, *, lam_init, tq, tk, nq):
    i, j = _tri_pair(pl.program_id(1), nq)
    rq = ATTN_ROWS

    @pl.when(j == 0)
    def _():
        lane = lax.broadcasted_iota(jnp.int32, (tq, DV_B), 1)
        for h in range(H_B):
            q = q_ref[:, h * DV_B:(h + 1) * DV_B] * (DK_B ** -0.5)
            qs_sc[h, 0:tq, 0:DV_B] = jnp.where(lane < DK_B, q, 0.0).astype(BF16)
            qs_sc[h, tq:2 * tq, 0:DV_B] = jnp.where(lane >= DK_B, q, 0.0).astype(BF16)
            qs_sc[h, 0:tq, DV_B:2 * DV_B] = aq_ref[h]
            qs_sc[h, tq:2 * tq, DV_B:2 * DV_B] = aq_ref[h]
        m_sc[...] = jnp.full_like(m_sc, NEG_INF)
        acc_sc[...] = jnp.zeros_like(acc_sc)

    def step(masked):
        ones = jnp.ones((tk, V7X_LANES), BF16)
        for h in range(H_B):
            kaug = jnp.concatenate([k_ref[:, h * DV_B:(h + 1) * DV_B].astype(BF16), ak_ref[h]], axis=1)
            vaug = jnp.concatenate([v_ref[:, h * DV_B:(h + 1) * DV_B].astype(BF16), ones], axis=1)
            for r in range(2 * tq // rq):
                rows = slice(r * rq, (r + 1) * rq)
                s = lax.dot_general(qs_sc[h, rows, :], kaug, (((1,), (1,)), ((), ())), preferred_element_type=F32)
                if masked:
                    qi = (r * rq) % tq + lax.broadcasted_iota(jnp.int32, (rq, tk), 0)
                    kj = lax.broadcasted_iota(jnp.int32, (rq, tk), 1)
                    s = jnp.where(kj <= qi, s, NEG_INF)
                m_old = m_sc[h, rows, :]
                m_new = jnp.maximum(m_old, jnp.max(s, axis=-1, keepdims=True))
                alpha = jnp.exp(m_old - m_new)
                p = jnp.exp(s - _tile_lanes(m_new, tk // V7X_LANES))
                acc_sc[h, rows, :] = (_tile_lanes(alpha, 2) * acc_sc[h, rows, :]
                                      + jnp.dot(p.astype(BF16), vaug, preferred_element_type=F32))
                m_sc[h, rows, :] = m_new

    @pl.when(j < i)
    def _():
        step(False)

    @pl.when(j == i)
    def _():
        step(True)

    @pl.when(j == i)
    def _():
        lam = _diff_lambda(lam_ref, lam_init)
        for h in range(H_B):
            o0 = acc_sc[h, 0:tq, 0:DV_B] / acc_sc[h, 0:tq, DV_B:2 * DV_B]
            o1 = acc_sc[h, tq:2 * tq, 0:DV_B] / acc_sc[h, tq:2 * tq, DV_B:2 * DV_B]
            ob = o0 - lam * o1
            on = ob * lax.rsqrt(jnp.mean(ob * ob, axis=-1, keepdims=True) + NORM_EPS)
            o_ref[:, h * DV_B:(h + 1) * DV_B] = on * g_ref[:, h * DV_B:(h + 1) * DV_B] * (1.0 - lam_init)


def _tri_pair(t, n):
    i = sum((t >= k * (k + 1) // 2).astype(jnp.int32) for k in range(1, n))
    return i, t - i * (i + 1) // 2


def diff_attn_prompt(zb, aq, ak, lam_p, norm_g, nb, t, lam_init):
    n_tok = zb.shape[0]
    tq = tk = ATTN_TILE
    nq = t // tq
    q_map = lambda b, s: (b * nq + _tri_pair(s, nq)[0], 0)
    kv_map = lambda off: (lambda b, s: (b * nq + _tri_pair(s, nq)[1], off))
    return pl.pallas_call(
        functools.partial(_attn_prompt_kernel, lam_init=lam_init, tq=tq, tk=tk, nq=nq),
        grid=(nb, nq * (nq + 1) // 2),
        in_specs=[
            pl.BlockSpec(lam_p.shape, lambda b, s: (0, 0)),
            pl.BlockSpec((1, W_B), lambda b, s: (0, 0)),
            pl.BlockSpec((tq, W_B), q_map),
            pl.BlockSpec((H_B, tq, DV_B), lambda b, s: (0, _tri_pair(s, nq)[0], 0)),
            pl.BlockSpec((tk, W_B), kv_map(1)),
            pl.BlockSpec((H_B, tk, DV_B), lambda b, s: (0, _tri_pair(s, nq)[1], 0)),
            pl.BlockSpec((tk, W_B), kv_map(2)),
        ],
        out_specs=pl.BlockSpec((tq, W_B), q_map),
        out_shape=jax.ShapeDtypeStruct((n_tok, W_B), F32),
        scratch_shapes=[pltpu.VMEM((H_B, 2 * tq, 2 * DV_B), BF16), pltpu.VMEM((H_B, 2 * tq, V7X_LANES), F32),
                        pltpu.VMEM((H_B, 2 * tq, 2 * DV_B), F32)],
        compiler_params=_cparams("parallel", "arbitrary"),
        name="diff_attn_prompt",
    )(lam_p, norm_g.reshape(1, W_B), zb, aq, zb, ak, zb)


def _attn_sample_kernel(*refs, lam_init, past_len, t_new, pages_per_step, seqs):
    G = pages_per_step
    pt_ref, lam_ref, g_ref, q_ref, kn_ref, vn_ref = refs[:6]
    k_refs = refs[6:6 + seqs * G]
    v_refs = refs[6 + seqs * G:6 + 2 * seqs * G]
    _prev, o_ref, qs_sc, bias_sc, m_sc, l_sc, acc_sc = refs[6 + 2 * seqs * G:]
    del pt_ref
    ps = pl.program_id(1)
    rows_per_head = 2 * t_new
    n_rows = H_B * rows_per_head
    page_rows = PAGE_SIZE * H_B
    reps = page_rows // V7X_LANES

    r_lane = lax.broadcasted_iota(jnp.int32, (n_rows, V7X_LANES), 0)
    slope = _alibi_slope(r_lane // rows_per_head).astype(F32)

    @pl.when(ps == 0)
    def _():
        lane = lax.broadcasted_iota(jnp.int32, (t_new, DV_B), 1)
        for u in range(seqs):
            q = q_ref[u * t_new:(u + 1) * t_new, :] * (DK_B ** -0.5)
            for h in range(H_B):
                qh = q[:, h * DV_B:(h + 1) * DV_B]
                qs_sc[u, h * rows_per_head:(h + 1) * rows_per_head, :] = jnp.concatenate(
                    [jnp.where(lane < DK_B, qh, 0.0), jnp.where(lane >= DK_B, qh, 0.0)], axis=0).astype(BF16)
        rr = lax.broadcasted_iota(jnp.int32, (n_rows, page_rows), 0)
        cc = lax.broadcasted_iota(jnp.int32, (n_rows, page_rows), 1)
        rel = cc // H_B - (past_len + rr % t_new)
        bias = _alibi_slope(rr // rows_per_head).astype(F32) * rel.astype(F32)
        bias_sc[...] = jnp.where(cc % H_B == rr // rows_per_head, bias, NEG_INF)
        m_sc[...] = jnp.full_like(m_sc, NEG_INF)
        l_sc[...] = jnp.zeros_like(l_sc)
        acc_sc[...] = jnp.zeros_like(acc_sc)

    lam = _diff_lambda(lam_ref, lam_init)
    for u in range(seqs):
        qs = qs_sc[u]
        m, l, acc = m_sc[u], l_sc[u], acc_sc[u]
        s_pages = []
        m_new = m
        for g in range(G):
            base = ((ps * G + g) * PAGE_SIZE).astype(F32)
            s = _bdot_nt(qs, k_refs[u * G + g][...]) + (bias_sc[...] + _tile_lanes(slope * base, reps))
            m_new = jnp.maximum(m_new, jnp.max(s, axis=-1, keepdims=True))
            s_pages.append(s)
        alpha = jnp.exp(m - m_new)
        l = alpha * l
        acc = alpha * acc
        for g in range(G):
            p = jnp.exp(s_pages[g] - _tile_lanes(m_new, reps))
            l = l + jnp.sum(p, axis=-1, keepdims=True)
            acc = acc + _bdot(p, v_refs[u * G + g][...])
        m = m_new

        @pl.when(ps < pl.num_programs(1) - 1)
        def _(u=u, m=m, l=l, acc=acc):
            m_sc[u] = m
            l_sc[u] = l
            acc_sc[u] = acc

        @pl.when(ps == pl.num_programs(1) - 1)
        def _(u=u, qs=qs, m=m, l=l, acc=acc):
            rnd = lambda a: a.astype(BF16).astype(F32)
            kn = rnd(kn_ref[u * t_new:(u + 1) * t_new, :])
            vn = rnd(vn_ref[u * t_new:(u + 1) * t_new, :])
            qf = qs.astype(F32)
            sn = jnp.concatenate(
                [lax.dot_general(qf[h * rows_per_head:(h + 1) * rows_per_head, :], kn[:, h * DV_B:(h + 1) * DV_B],
                                 (((1,), (1,)), ((), ())), preferred_element_type=F32) for h in range(H_B)],
                axis=0)
            rr = lax.broadcasted_iota(jnp.int32, (n_rows, t_new), 0)
            rel = lax.broadcasted_iota(jnp.int32, (n_rows, t_new), 1) - rr % t_new
            sn = jnp.where(rel <= 0, sn + _alibi_slope(rr // rows_per_head).astype(F32) * rel.astype(F32), NEG_INF)
            m_fin = jnp.maximum(m, jnp.max(sn, axis=-1, keepdims=True))
            alpha = jnp.exp(m - m_fin)
            pn = rnd(jnp.exp(sn - m_fin[:, 0:1]))
            l_fin = alpha * l + jnp.sum(pn, axis=-1, keepdims=True)
            pv = jnp.concatenate(
                [jnp.dot(pn[h * rows_per_head:(h + 1) * rows_per_head, :], vn[:, h * DV_B:(h + 1) * DV_B],
                         preferred_element_type=F32) for h in range(H_B)], axis=0)
            o = (alpha * acc + pv) / l_fin

            outs = []
            for h in range(H_B):
                o0 = o[h * rows_per_head:h * rows_per_head + t_new, :]
                o1 = o[h * rows_per_head + t_new:(h + 1) * rows_per_head, :]
                ob = o0 - lam * o1
                outs.append(ob * lax.rsqrt(jnp.mean(ob * ob, axis=-1, keepdims=True) + NORM_EPS))
            o_ref[u * t_new:(u + 1) * t_new, :] = jnp.concatenate(outs, axis=-1) * g_ref[...] * (1.0 - lam_init)


def diff_attn_sample(zb, cache_k, cache_v, layer, page_table, lam_p, norm_g, nb, t_new, row0, lam_init, prev):
    n_tok = zb.shape[0]
    n_pages = page_table.shape[1]
    G = PAGES_PER_STEP
    S = ATTN_SAMPLE_SEQS
    rows = S * t_new
    rb0 = row0 // rows
    n_rows = 2 * H_B * t_new
    page_rows = PAGE_SIZE * H_B
    ck = cache_k.reshape(cache_k.shape[0], cache_k.shape[1], page_rows, DV_B)
    cv = cache_v.reshape(cache_v.shape[0], cache_v.shape[1], page_rows, DV_B)

    def page_map(u, g):
        return lambda b, p, pt: (layer, pt[b * S + u, p * G + g], 0, 0)

    page_blk = (None, None, page_rows, DV_B)
    page_specs = [pl.BlockSpec(page_blk, page_map(u, g)) for u in range(S) for g in range(G)]
    in_specs = [
        pl.BlockSpec(lam_p.shape, lambda b, p, pt: (0, 0)),
        pl.BlockSpec((1, W_B), lambda b, p, pt: (0, 0)),
        pl.BlockSpec((rows, W_B), lambda b, p, pt: (rb0 + b, 0)),
        pl.BlockSpec((rows, W_B), lambda b, p, pt: (rb0 + b, 1)),
        pl.BlockSpec((rows, W_B), lambda b, p, pt: (rb0 + b, 2)),
    ] + page_specs + page_specs + [pl.BlockSpec(memory_space=pl.ANY)]
    args = [page_table, lam_p, norm_g.reshape(1, W_B), zb, zb, zb] + [ck] * (S * G) + [cv] * (S * G) + [prev]
    grid_spec = pltpu.PrefetchScalarGridSpec(
        num_scalar_prefetch=1,
        grid=(nb // S, n_pages // G),
        in_specs=in_specs,
        out_specs=pl.BlockSpec((rows, W_B), lambda b, p, pt: (rb0 + b, 0)),
        scratch_shapes=[pltpu.VMEM((S, n_rows, DV_B), BF16), pltpu.VMEM((n_rows, page_rows), F32),
                        pltpu.VMEM((S, n_rows, V7X_LANES), F32), pltpu.VMEM((S, n_rows, V7X_LANES), F32),
                        pltpu.VMEM((S, n_rows, DV_B), F32)],
    )
    return pl.pallas_call(
        functools.partial(_attn_sample_kernel, lam_init=lam_init, past_len=n_pages * PAGE_SIZE,
                          t_new=t_new, pages_per_step=G, seqs=S),
        grid_spec=grid_spec,
        out_shape=jax.ShapeDtypeStruct((n_tok, W_B), F32),
        input_output_aliases={len(args) - 1: 0},
        compiler_params=_cparams("parallel", "arbitrary"),
        name="diff_attn_sample",
    )(*args)


def _s5_disc_kernel(are_ref, aim_ref, ldt_ref, bre_ref, bim_ref, lre_ref, lim_ref, bbre_ref, bbim_ref):
    a_re = are_ref[...]
    a_im = aim_ref[...]
    dt = jnp.exp(ldt_ref[...])
    mag = jnp.exp(a_re * dt)
    lb_re = mag * jnp.cos(a_im * dt)
    lb_im = mag * jnp.sin(a_im * dt)
    den = a_re * a_re + a_im * a_im
    xr = lb_re - 1.0
    fr = (xr * a_re + lb_im * a_im) / den
    fi = (lb_im * a_re - xr * a_im) / den
    lre_ref[...] = lb_re
    lim_ref[...] = lb_im
    b_re = bre_ref[...]
    b_im = bim_ref[...]
    bbre_ref[...] = fr * b_re - fi * b_im
    bbim_ref[...] = fr * b_im + fi * b_re


def s5_discretize(a_re, a_im, log_dt, b_re, b_im):
    depth = a_re.shape[0]
    gp = G_C * P_C
    flat = lambda a: a.reshape(depth, 1, gp)
    ldt = jnp.broadcast_to(log_dt[:, :, None], (depth, G_C, P_C)).reshape(depth, 1, gp)
    tr = lambda b: jnp.transpose(b, (0, 3, 1, 2)).reshape(depth, GC, gp)
    shp1 = jax.ShapeDtypeStruct((depth, 1, gp), F32)
    shpb = jax.ShapeDtypeStruct((depth, GC, gp), F32)
    return pl.pallas_call(_s5_disc_kernel, out_shape=[shp1, shp1, shpb, shpb], name="s5_discretize")(
        flat(a_re), flat(a_im), ldt, tr(b_re), tr(b_im))


def _block_diag_in(bb):
    depth, _, gp = bb.shape
    tiled = jnp.tile(bb, (1, G_C, 1)).reshape(depth, G_C, GC, gp)
    grp_r = jnp.arange(G_C)[:, None, None]
    grp_c = (jnp.arange(gp) // P_C)[None, None, :]
    return jnp.where(grp_r == grp_c, tiled, 0.0).reshape(depth, G_C * GC, gp)


def _block_diag_out(c):
    depth = c.shape[0]
    ct = jnp.transpose(c, (0, 1, 3, 2)).reshape(depth, G_C * P_C, GC)
    tiled = jnp.tile(ct, (1, 1, G_C))
    grp_r = (jnp.arange(G_C * P_C) // P_C)[:, None]
    grp_c = (jnp.arange(G_C * GC) // GC)[None, :]
    return jnp.where(grp_r == grp_c, tiled, 0.0)


def _cmul_add(a_re, a_im, x_re, x_im, y_re, y_im):
    return y_re + (a_re * x_re - a_im * x_im), y_im + (a_re * x_im + a_im * x_re)


def _block_scan(x_re, x_im, pw):
    for d, (a_re, a_im) in zip((1, 2, 4), pw):
        x_re, x_im = _cmul_add(a_re, a_im, pltpu.roll(x_re, d, 0), pltpu.roll(x_im, d, 0), x_re, x_im)
    return x_re, x_im


def _s5_kernel(*refs, rows, independent, aliased):
    (u_ref, lre_ref, lim_ref, bb_ref, cc_ref, d_ref, gw_ref, gb_ref, s0re_ref, s0im_ref) = refs[:10]
    rest = refs[10:]
    if aliased:
        rest = rest[1:]
    y_ref, s1re_ref, s1im_ref, st_sc, car_sc = rest
    gp = G_C * P_C
    nblk = rows // V7X_SUBLANES
    ti = pl.program_id(1)

    lam_re = lre_ref[...]
    lam_im = lim_ref[...]
    l2_re, l2_im = lam_re * lam_re - lam_im * lam_im, 2.0 * lam_re * lam_im
    l4_re, l4_im = l2_re * l2_re - l2_im * l2_im, 2.0 * l2_re * l2_im
    row8 = lax.broadcasted_iota(jnp.int32, (V7X_SUBLANES, gp), 0)
    pw = tuple((jnp.where(row8 >= d, a_re, 0.0), jnp.where(row8 >= d, a_im, 0.0))
               for d, (a_re, a_im) in zip((1, 2, 4), ((lam_re, lam_im), (l2_re, l2_im), (l4_re, l4_im))))
    pk_re, pk_im = _block_scan(jnp.where(row8 == 0, lam_re, 0.0), jnp.where(row8 == 0, lam_im, 0.0), pw)

    u = u_ref[...]
    st_sc[...] = _bdot(u, bb_ref[...])

    if not independent:
        @pl.when(ti == 0)
        def _():
            car_sc[0:1, :] = s0re_ref[...]
            car_sc[1:2, :] = s0im_ref[...]

    def body(bi, carry):
        r0 = pl.multiple_of(bi * V7X_SUBLANES, V7X_SUBLANES)
        x_re = st_sc[pl.ds(r0, V7X_SUBLANES), 0:gp]
        x_im = st_sc[pl.ds(r0, V7X_SUBLANES), gp:2 * gp]
        x_re, x_im = _block_scan(x_re, x_im, pw)
        if independent:
            c_re = s0re_ref[pl.ds(bi, 1), :]
            c_im = s0im_ref[pl.ds(bi, 1), :]
        else:
            c_re, c_im = carry
        s_re, s_im = _cmul_add(pk_re, pk_im, c_re, c_im, x_re, x_im)
        st_sc[pl.ds(r0, V7X_SUBLANES), 0:gp] = s_re
        st_sc[pl.ds(r0, V7X_SUBLANES), gp:2 * gp] = s_im
        last_re = s_re[V7X_SUBLANES - 1:V7X_SUBLANES, :]
        last_im = s_im[V7X_SUBLANES - 1:V7X_SUBLANES, :]
        if independent:
            s1re_ref[pl.ds(bi, 1), :] = last_re
            s1im_ref[pl.ds(bi, 1), :] = last_im
            return carry
        return last_re, last_im

    if independent:
        lax.fori_loop(0, nblk, body, 0)
    else:
        c_re, c_im = lax.fori_loop(0, nblk, body, (car_sc[0:1, :], car_sc[1:2, :]))
        car_sc[0:1, :] = c_re
        car_sc[1:2, :] = c_im

        @pl.when(ti == pl.num_programs(1) - 1)
        def _():
            s1re_ref[...] = c_re
            s1im_ref[...] = c_im

    y = _bdot(st_sc[...], cc_ref[...]) + d_ref[...] * u
    z = _bdot(jax.nn.gelu(y), gw_ref[...]) + gb_ref[...]
    y_ref[...] = (z[:, :W_C] * jax.nn.sigmoid(z[:, W_C:])).astype(y_ref.dtype)


def s5_mixer(zc, lam_re, lam_im, bb, cc, layer, d, glu_w, glu_b, s0_re, s0_im, nb, t, row0, prev=None):
    n_tok = zc.shape[0]
    gp = G_C * P_C
    independent = t == V7X_SUBLANES
    aliased = prev is not None
    if independent:
        seqs = S5_SAMPLE_SEQS
        rows = seqs * t
        grid = (nb // seqs, 1)
        st_spec = pl.BlockSpec((seqs, gp), lambda b, i: (b, 0))
    else:
        rows = S5_TILE
        grid = (nb, t // rows)
        st_spec = pl.BlockSpec((1, gp), lambda b, i: (b, 0))
    nt = grid[1]
    rb0 = row0 // rows
    row_map = lambda b, i: (rb0 + b * nt + i, 0)
    const = lambda b, i: (0, 0)
    by_layer = lambda b, i: (layer, 0, 0)
    in_specs = [
        pl.BlockSpec((rows, W_C), row_map),
        pl.BlockSpec((None, 1, gp), by_layer),
        pl.BlockSpec((None, 1, gp), by_layer),
        pl.BlockSpec((None,) + bb.shape[1:], by_layer),
        pl.BlockSpec((None,) + cc.shape[1:], by_layer),
        pl.BlockSpec((1, W_C), const),
        pl.BlockSpec((None,) + glu_w.shape[1:], by_layer),
        pl.BlockSpec((1, 2 * W_C), const),
        st_spec,
        st_spec,
    ]
    args = [zc, lam_re, lam_im, bb, cc, d.reshape(1, W_C), glu_w, glu_b.reshape(1, 2 * W_C), s0_re, s0_im]
    aliases = {}
    if aliased:
        in_specs.append(pl.BlockSpec(memory_space=pl.ANY))
        args.append(prev)
        aliases = {len(args) - 1: 0}
    if independent:
        s1_shape = jax.ShapeDtypeStruct((nb, gp), F32)
    else:
        s1_shape = jax.ShapeDtypeStruct((nb, 1, gp), F32)
        st_out = pl.BlockSpec((None, 1, gp), lambda b, i: (b, 0, 0))
    out_specs = [pl.BlockSpec((rows, W_C), row_map)] + ([st_spec, st_spec] if independent else [st_out, st_out])
    if not independent:
        args[8] = s0_re.reshape(nb, 1, gp)
        args[9] = s0_im.reshape(nb, 1, gp)
        in_specs[8] = in_specs[9] = pl.BlockSpec((None, 1, gp), lambda b, i: (b, 0, 0))
    yc, s1_re, s1_im = pl.pallas_call(
        functools.partial(_s5_kernel, rows=rows, independent=independent, aliased=aliased),
        grid=grid,
        in_specs=in_specs,
        out_specs=out_specs,
        out_shape=[jax.ShapeDtypeStruct((n_tok, W_C), BF16), s1_shape, s1_shape],
        scratch_shapes=[pltpu.VMEM((rows, 2 * gp), F32), pltpu.VMEM((V7X_SUBLANES, gp), F32)],
        input_output_aliases=aliases,
        compiler_params=_cparams("parallel", "arbitrary"),
        name="s5_sample" if independent else "s5_prompt",
    )(*args)
    return yc, s1_re.reshape(nb, G_C, P_C), s1_im.reshape(nb, G_C, P_C)


def _xattn_kernel(q_ref, mk_ref, mv_ref, o_ref):
    q = q_ref[...] * (DH_X ** -0.5)
    outs = []
    for h in range(H_X):
        s = _bdot_nt(q[:, h * DH_X:(h + 1) * DH_X], mk_ref[h])
        m = jnp.max(s, axis=-1, keepdims=True)
        p = jnp.exp(s - m)
        l = jnp.sum(p, axis=-1, keepdims=True)
        outs.append(_bdot(p, mv_ref[h]) / l)
    o_ref[...] = jnp.concatenate(outs, axis=-1).astype(o_ref.dtype)


def _xattn_native_kernel(q_ref, mk_ref, mv_ref, _prev, o_ref, bias_sc, *, tq, seqs, n_mem):
    rows = seqs * tq
    n_keys = seqs * n_mem * H_X

    @pl.when(pl.program_id(0) == 0)
    def _():
        rr = lax.broadcasted_iota(jnp.int32, (H_X * rows, n_keys), 0)
        cc = lax.broadcasted_iota(jnp.int32, (H_X * rows, n_keys), 1)
        own = (cc // (n_mem * H_X) == (rr % rows) // tq) & (cc % H_X == rr // rows)
        bias_sc[...] = jnp.where(own, 0.0, NEG_INF)

    q = q_ref[...] * (DH_X ** -0.5)
    q_all = jnp.concatenate([q[:, h * DH_X:(h + 1) * DH_X] for h in range(H_X)], axis=0)
    k2 = mk_ref[...].reshape(n_keys, DH_X)
    v2 = mv_ref[...].reshape(n_keys, DH_X)
    s = _bdot_nt(q_all, k2) + bias_sc[...]
    m = jnp.max(s, axis=-1, keepdims=True)
    p = jnp.exp(s - m)
    l = jnp.sum(p, axis=-1, keepdims=True)
    o = _bdot(p, v2) / l
    o_ref[...] = jnp.concatenate([o[h * rows:(h + 1) * rows, :] for h in range(H_X)], axis=-1).astype(o_ref.dtype)


def cross_attn_native(qx, mem_k, mem_v, layer, nb, t, row0, prev):
    n_tok, d = qx.shape
    n_mem = mem_k.shape[2]
    seqs = XATTN_SAMPLE_SEQS
    rows = seqs * t
    rb0 = row0 // rows
    mem_spec = pl.BlockSpec((None, seqs, n_mem, H_X, DH_X), lambda b: (layer, b, 0, 0, 0))
    return pl.pallas_call(
        functools.partial(_xattn_native_kernel, tq=t, seqs=seqs, n_mem=n_mem),
        grid=(nb // seqs,),
        in_specs=[pl.BlockSpec((rows, d), lambda b: (rb0 + b, 0)), mem_spec, mem_spec,
                  pl.BlockSpec(memory_space=pl.ANY)],
        out_specs=pl.BlockSpec((rows, d), lambda b: (rb0 + b, 0)),
        out_shape=jax.ShapeDtypeStruct((n_tok, d), prev.dtype),
        scratch_shapes=[pltpu.VMEM((H_X * rows, seqs * n_mem * H_X), F32)],
        input_output_aliases={3: 0},
        compiler_params=_cparams("arbitrary"),
        name="cross_attn_sample",
    )(qx, mem_k, mem_v, prev)


def cross_attn(qx, mk, mv, layer, n_mem, nb, t, out_dtype):
    n_tok, d = qx.shape
    tq = XATTN_TILE
    nq = t // tq
    mem_spec = pl.BlockSpec((None, H_X, n_mem, DH_X), lambda b, i: (layer, 0, b, 0))
    return pl.pallas_call(
        _xattn_kernel,
        grid=(nb, nq),
        in_specs=[pl.BlockSpec((tq, d), lambda b, i: (b * nq + i, 0)), mem_spec, mem_spec],
        out_specs=pl.BlockSpec((tq, d), lambda b, i: (b * nq + i, 0)),
        out_shape=jax.ShapeDtypeStruct((n_tok, d), out_dtype),
        compiler_params=_cparams("parallel", "arbitrary"),
        name="cross_attn_prompt",
    )(qx, mk, mv)


def kernel(x_prompt, x_sample, mem_prompt, cache_k, cache_v, page_table, cache_mem_k, cache_mem_v,
           state_mlstm_c, state_mlstm_n, state_mlstm_m, state_ssm_re, state_ssm_im,
           ln_g, ln_b, ffn1_wg, ffn1_wu, ffn1_wd, ffn2_wg, ffn2_wu, ffn2_wd, w_in, b_in,
           mlstm_norm_g, diff_lam, diff_norm_g, ssm_a_re, ssm_a_im, ssm_log_dt, ssm_b_re, ssm_b_im,
           ssm_c_re, ssm_c_im, ssm_d, ssm_glu_w, ssm_glu_b, w_out, cross_wq, cross_wk, cross_wv, cross_wo):
    bp, tp, d = x_prompt.shape
    bs, ts, _ = x_sample.shape
    depth = ln_g.shape[0]
    n_mem = mem_prompt.shape[1]
    n_p = bp * tp
    gp = G_C * P_C
    alpha = (2.0 * depth) ** 0.25

    cast = lambda w: w.astype(BF16)
    ffn1_wg, ffn1_wu, ffn1_wd = cast(ffn1_wg), cast(ffn1_wu), cast(ffn1_wd)
    ffn2_wg, ffn2_wu, ffn2_wd = cast(ffn2_wg), cast(ffn2_wu), cast(ffn2_wd)
    w_out_b, wq_b, wo_b, glu_w_b = cast(w_out), cast(cross_wq), cast(cross_wo), cast(ssm_glu_w)
    w_in_p, b_in_p = _pack_w_in(w_in, b_in)
    lam_re, lam_im, bb_re, bb_im = s5_discretize(ssm_a_re, ssm_a_im, ssm_log_dt, ssm_b_re, ssm_b_im)
    bb = jnp.concatenate([_block_diag_in(bb_re), _block_diag_in(bb_im)], axis=-1).astype(BF16)
    cc = jnp.concatenate([_block_diag_out(ssm_c_re), -_block_diag_out(ssm_c_im)], axis=1).astype(BF16)
    alibi_q, alibi_k = _alibi_tables(tp)

    p_mk, p_mv, p_mkh, p_mvh = mem_kv(mem_prompt.reshape(bp * n_mem, d), cast(cross_wk), cast(cross_wv))
    p_mk = p_mk.reshape(depth, bp, n_mem, H_X, DH_X)
    p_mv = p_mv.reshape(depth, bp, n_mem, H_X, DH_X)

    x = None
    zeros_s = jnp.zeros((bp, gp), F32)
    p_k = jnp.zeros((depth, n_p, H_B, DV_B), F32)
    p_v = jnp.zeros((depth, n_p, H_B, DV_B), F32)
    p_st, s_st = [], []
    for l in range(depth):
        lam_init = 0.8 - 0.6 * math.exp(-0.3 * l)
        srcs = [x_prompt.reshape(n_p, d), x_sample.reshape(bs * ts, d)] if l == 0 else [x]
        x = ffn_ln(srcs, ffn1_wg, ffn1_wu, ffn1_wd, l, ln_g[l, 0], ln_b[l, 0], alpha)
        za, zb, zc, zg, p_k, p_v, s_k, s_v = proj_in(x, w_in_p, b_in_p, p_k, p_v, l, n_p)

        ya, pc, pn, pm = mlstm_prompt(za, zg, mlstm_norm_g[l], bp, tp)
        ya, sc, sn, sm = mlstm_sample(za, zg, mlstm_norm_g[l], bs, ts, n_p,
                                      state_mlstm_c, state_mlstm_n[l], state_mlstm_m[l], l, ya)

        yb = diff_attn_prompt(zb, alibi_q, alibi_k, diff_lam[l], diff_norm_g[l], bp, tp, lam_init)
        yb = diff_attn_sample(zb, cache_k, cache_v, l, page_table, diff_lam[l], diff_norm_g[l],
                              bs, ts, n_p, lam_init, yb)

        s5_args = (lam_re, lam_im, bb, cc, l, ssm_d[l], glu_w_b, ssm_glu_b[l])
        yc, psr, psi = s5_mixer(zc, *s5_args, zeros_s, zeros_s, bp, tp, 0)
        yc, ssr, ssi = s5_mixer(zc, *s5_args, state_ssm_re[l].reshape(bs, gp), state_ssm_im[l].reshape(bs, gp),
                                bs, ts, n_p, prev=yc)

        x, qx = mix_out_ln(ya, yb, yc, w_out_b, wq_b, l, x, ln_g[l, 1], ln_b[l, 1], alpha)
        o = cross_attn(qx, p_mkh, p_mvh, l, n_mem, bp, tp, BF16)
        o = cross_attn_native(qx, cache_mem_k, cache_mem_v, l, bs, ts, n_p, o)
        x = out_ffn_ln(o, wo_b, ffn2_wg, ffn2_wu, ffn2_wd, l, x, ln_g[l, 2], ln_b[l, 2], ln_g[l, 3], ln_b[l, 3], alpha,
                       split_rows=n_p if l == depth - 1 else None)

        p_st.append((pc, pn, pm, psr, psi))
        s_st.append((s_k.reshape(bs, ts, H_B, DV_B), s_v.reshape(bs, ts, H_B, DV_B), sc, sn, sm, ssr, ssi))

    p_c, p_n, p_m, p_sr, p_si = [jnp.stack(a) for a in zip(*p_st)]
    s_k, s_v, s_c, s_n, s_m, s_sr, s_si = [jnp.stack(a) for a in zip(*s_st)]
    p_k = p_k.reshape(depth, bp, tp, H_B, DV_B)
    p_v = p_v.reshape(depth, bp, tp, H_B, DV_B)
    yp = x[0].reshape(bp, tp, d)
    ys = x[1].reshape(bs, ts, d)
    return (yp, ys, p_k, p_v, p_mk, p_mv, p_c, p_n, p_m, p_sr, p_si, s_k, s_v, s_c, s_n, s_m, s_sr, s_si)
```

```python
import functools
import math

import jax
import jax.numpy as jnp
from jax import lax
from jax.experimental import pallas as pl
from jax.experimental.pallas import tpu as pltpu

F32 = jnp.float32
BF16 = jnp.bfloat16

D_MODEL = 1024
PAGE_SIZE = 128
W_A = D_MODEL // 4
W_B = D_MODEL // 2
W_C = D_MODEL - W_A - W_B
H_A = 4
DH_A = W_A // H_A
H_B = 4
DV_B = W_B // H_B
DK_B = DV_B // 2
GC = 16
G_C = W_C // GC
P_C = 64
H_X = 4
DH_X = D_MODEL // H_X
LN_EPS = 1e-5
NORM_EPS = 1e-6

OFF_AQ = 0
OFF_AK = OFF_AQ + W_A
OFF_AV = OFF_AK + W_A
OFF_AO = OFF_AV + W_A
OFF_AI = OFF_AO + W_A
OFF_AF = OFF_AI + H_A
OFF_BQ = OFF_AF + H_A
OFF_BK = OFF_BQ + H_B * 2 * DK_B
OFF_BV = OFF_BK + H_B * 2 * DK_B
OFF_CU = OFF_BV + W_B
N_IN = OFF_CU + W_C

V7X_LANES = 128
V7X_SUBLANES = 8
V7X_VMEM_LIMIT_BYTES = 56 * 1024 * 1024
V7X_VMEM_LIMIT_LARGE_BYTES = 60 * 1024 * 1024

TOKEN_TILE = 1024
FF_TILE = 256
PROJ_TILE = 512
OUT_TILE = 1024
ATTN_TILE = 512
ATTN_ROWS = 256
XATTN_TILE = 2048
S5_TILE = 2048
PAGES_PER_STEP = 16
ATTN_SAMPLE_SEQS = 2
XATTN_SAMPLE_SEQS = 4
S5_SAMPLE_SEQS = 16
MLSTM_PROMPT_CHUNK = 512
MLSTM_UNITS = 16
ALIBI_SPLIT = 64

NEG_INF = float("-inf")


def _cparams(*sem, vmem_limit=V7X_VMEM_LIMIT_BYTES):
    return pltpu.CompilerParams(dimension_semantics=sem, vmem_limit_bytes=vmem_limit)


def _bdot(a, b):
    return jnp.dot(a.astype(BF16), b.astype(BF16), preferred_element_type=F32)


def _bdot_nt(a, b):
    return lax.dot_general(a.astype(BF16), b.astype(BF16), (((1,), (1,)), ((), ())),
                           preferred_element_type=F32)


def _tile_lanes(x, n):
    return x if n == 1 else jnp.concatenate([x] * n, axis=1)


def _layer_norm(y, g, b):
    mu = jnp.mean(y, axis=-1, keepdims=True)
    yc = y - mu
    var = jnp.mean(yc * yc, axis=-1, keepdims=True)
    return yc * lax.rsqrt(var + LN_EPS) * g + b


def _swiglu(xb, wg_ref, wu_ref, wd_ref, h_sc, tf):
    for c in range(wg_ref.shape[1] // tf):
        cols = slice(c * tf, (c + 1) * tf)
        hg = jnp.dot(xb, wg_ref[:, cols], preferred_element_type=F32)
        hu = jnp.dot(xb, wu_ref[:, cols], preferred_element_type=F32)
        h_sc[:, cols] = ((hg * jax.nn.sigmoid(hg)) * hu).astype(BF16)
    return jnp.dot(h_sc[...], wd_ref[...], preferred_element_type=F32)


def _ffn_ln_kernel(*refs, alpha, tf, n_src, first_tiles):
    srcs = refs[:n_src]
    wg_ref, wu_ref, wd_ref, g_ref, b_ref, o_ref, h_sc = refs[n_src:]

    def run(x_ref):
        x = x_ref[...]
        ff = _swiglu(x.astype(BF16), wg_ref, wu_ref, wd_ref, h_sc, tf)
        o_ref[...] = _layer_norm(alpha * x + 0.5 * ff, g_ref[...], b_ref[...])

    if n_src == 1:
        run(srcs[0])
    else:
        i = pl.program_id(0)
        pl.when(i < first_tiles)(lambda: run(srcs[0]))
        pl.when(i >= first_tiles)(lambda: run(srcs[1]))


def ffn_ln(xs, wg, wu, wd, layer, g, b, alpha):
    d = xs[0].shape[1]
    dff = wg.shape[2]
    tm, tf = TOKEN_TILE, FF_TILE
    first = xs[0].shape[0] // tm
    n = sum(x.shape[0] for x in xs)
    src_specs = [pl.BlockSpec((tm, d), lambda i: (jnp.minimum(i, first - 1), 0))]
    if len(xs) == 2:
        src_specs.append(pl.BlockSpec((tm, d), lambda i: (jnp.maximum(i - first, 0), 0)))
    resident = pl.Buffered(1)
    return pl.pallas_call(
        functools.partial(_ffn_ln_kernel, alpha=alpha, tf=tf, n_src=len(xs), first_tiles=first),
        grid=(n // tm,),
        in_specs=src_specs + [
            pl.BlockSpec((None, d, dff), lambda i: (layer, 0, 0), pipeline_mode=resident),
            pl.BlockSpec((None, d, dff), lambda i: (layer, 0, 0), pipeline_mode=resident),
            pl.BlockSpec((None, dff, d), lambda i: (layer, 0, 0), pipeline_mode=resident),
            pl.BlockSpec((1, d), lambda i: (0, 0)),
            pl.BlockSpec((1, d), lambda i: (0, 0)),
        ],
        out_specs=pl.BlockSpec((tm, d), lambda i: (i, 0)),
        out_shape=jax.ShapeDtypeStruct((n, d), F32),
        scratch_shapes=[pltpu.VMEM((tm, dff), BF16)],
        compiler_params=_cparams("arbitrary"),
        name="ffn_ln",
    )(*xs, wg, wu, wd, g.reshape(1, d), b.reshape(1, d))


def _out_ffn_ln_kernel(o_ref, wo_ref, x_ref, g1_ref, b1_ref, wg_ref, wu_ref, wd_ref, g2_ref, b2_ref, *rest,
                       alpha, tf, split):
    outs, h_sc = rest[:-1], rest[-1]
    y1 = _layer_norm(alpha * x_ref[...] + jnp.dot(o_ref[...].astype(BF16), wo_ref[...], preferred_element_type=F32),
                     g1_ref[...], b1_ref[...])
    ff = _swiglu(y1.astype(BF16), wg_ref, wu_ref, wd_ref, h_sc, tf)
    y2 = _layer_norm(alpha * y1 + 0.5 * ff, g2_ref[...], b2_ref[...])
    if split is None:
        outs[0][...] = y2
    else:
        i = pl.program_id(0)

        @pl.when(i < split)
        def _():
            outs[0][...] = y2

        @pl.when(i >= split)
        def _():
            outs[1][...] = y2


def out_ffn_ln(o, wo, wg, wu, wd, layer, x, g1, b1, g2, b2, alpha, split_rows=None):
    n, d = x.shape
    dff = wg.shape[2]
    tm, tf = TOKEN_TILE, FF_TILE
    resident = pl.Buffered(1)
    by_layer = lambda i: (layer, 0, 0)
    vec = pl.BlockSpec((1, d), lambda i: (0, 0))
    if split_rows is None:
        split = None
        out_specs = [pl.BlockSpec((tm, d), lambda i: (i, 0))]
        out_shape = [jax.ShapeDtypeStruct((n, d), F32)]
    else:
        split = split_rows // tm
        out_specs = [pl.BlockSpec((tm, d), lambda i: (jnp.minimum(i, split - 1), 0)),
                     pl.BlockSpec((tm, d), lambda i: (jnp.maximum(i - split, 0), 0))]
        out_shape = [jax.ShapeDtypeStruct((split_rows, d), F32), jax.ShapeDtypeStruct((n - split_rows, d), F32)]
    res = pl.pallas_call(
        functools.partial(_out_ffn_ln_kernel, alpha=alpha, tf=tf, split=split),
        grid=(n // tm,),
        in_specs=[
            pl.BlockSpec((tm, d), lambda i: (i, 0)),
            pl.BlockSpec((None, d, d), by_layer, pipeline_mode=resident),
            pl.BlockSpec((tm, d), lambda i: (i, 0)),
            vec, vec,
            pl.BlockSpec((None, d, dff), by_layer, pipeline_mode=resident),
            pl.BlockSpec((None, d, dff), by_layer, pipeline_mode=resident),
            pl.BlockSpec((None, dff, d), by_layer, pipeline_mode=resident),
            vec, vec,
        ],
        out_specs=out_specs,
        out_shape=out_shape,
        scratch_shapes=[pltpu.VMEM((tm, dff), BF16)],
        compiler_params=_cparams("arbitrary", vmem_limit=V7X_VMEM_LIMIT_LARGE_BYTES),
        name="out_ffn_ln",
    )(o, wo, x, g1.reshape(1, d), b1.reshape(1, d), wg, wu, wd, g2.reshape(1, d), b2.reshape(1, d))
    return res[0] if split_rows is None else res


N_ZA = 4 * W_A
N_ZB = 3 * W_B
N_ZC = W_C
N_ZG = V7X_LANES
N_PROJ = N_ZA + N_ZB + N_ZC + N_ZG


def _proj_in_kernel(x_ref, w_ref, b_ref, _kp_prev, _vp_prev, za_ref, zb_ref, zc_ref, zg_ref,
                    kp_ref, vp_ref, ks_ref, vs_ref, kv_sc, *, prompt_tiles):
    i = pl.program_id(0)
    xb = x_ref[...].astype(BF16)
    off = 0
    for ref in (za_ref, zb_ref, zc_ref, zg_ref):
        width = ref.shape[1]
        z = jnp.dot(xb, w_ref[:, off:off + width], preferred_element_type=F32) + b_ref[:, off:off + width]
        ref[...] = z.astype(ref.dtype)
        if ref is zb_ref:
            kv_sc[...] = z[:, W_B:]
        off += width

    def write_kv(k_out, v_out):
        for h in range(H_B):
            k_out[:, h, :] = kv_sc[:, h * DV_B:(h + 1) * DV_B]
            v_out[:, h, :] = kv_sc[:, W_B + h * DV_B:W_B + (h + 1) * DV_B]

    @pl.when(i < prompt_tiles)
    def _():
        write_kv(kp_ref, vp_ref)

    @pl.when(i >= prompt_tiles)
    def _():
        write_kv(ks_ref, vs_ref)


def proj_in(x, w, b, kp_prev, vp_prev, layer, n_prompt):
    n, d = x.shape
    tm = PROJ_TILE
    pt = n_prompt // tm
    widths = (N_ZA, N_ZB, N_ZC, N_ZG)
    kv_blk = (tm, H_B, DV_B)
    p_map = lambda i: (layer, jnp.minimum(i, pt - 1), 0, 0)
    s_map = lambda i: (jnp.maximum(i - pt, 0), 0, 0)
    kv_s = jax.ShapeDtypeStruct((n - n_prompt, H_B, DV_B), F32)
    return pl.pallas_call(
        functools.partial(_proj_in_kernel, prompt_tiles=pt),
        grid=(n // tm,),
        in_specs=[
            pl.BlockSpec((tm, d), lambda i: (i, 0)),
            pl.BlockSpec((None, d, N_PROJ), lambda i: (layer, 0, 0)),
            pl.BlockSpec((None, 1, N_PROJ), lambda i: (layer, 0, 0)),
            pl.BlockSpec(memory_space=pl.ANY),
            pl.BlockSpec(memory_space=pl.ANY),
        ],
        out_specs=[pl.BlockSpec((tm, wd), lambda i: (i, 0)) for wd in widths]
        + [pl.BlockSpec((None,) + kv_blk, p_map)] * 2 + [pl.BlockSpec(kv_blk, s_map)] * 2,
        out_shape=[jax.ShapeDtypeStruct((n, wd), BF16 if wd == N_ZB else F32) for wd in widths]
        + [jax.ShapeDtypeStruct(kp_prev.shape, F32)] * 2 + [kv_s, kv_s],
        scratch_shapes=[pltpu.VMEM((tm, 2 * W_B), F32)],
        input_output_aliases={3: 4, 4: 5},
        compiler_params=_cparams("arbitrary"),
        name="proj_in",
    )(x, w, b, kp_prev, vp_prev)


def _pack_w_in(w_in, b_in):
    def cols(a):
        pad = jnp.zeros(a.shape[:-1] + (N_ZG - 2 * H_A,), a.dtype)
        return jnp.concatenate([a[..., OFF_AQ:OFF_AI], a[..., OFF_BQ:OFF_CU], a[..., OFF_CU:N_IN],
                                a[..., OFF_AI:OFF_BQ], pad], axis=-1)
    return cols(w_in).astype(BF16), cols(b_in)[:, None, :]


def _mem_kv_kernel(x_ref, wk_ref, wv_ref, mk_ref, mv_ref, mkh_ref, mvh_ref):
    xb = x_ref[...].astype(BF16)
    for w_ref, o_ref, oh_ref in ((wk_ref, mk_ref, mkh_ref), (wv_ref, mv_ref, mvh_ref)):
        r = jnp.dot(xb, w_ref[...], preferred_element_type=F32)
        for h in range(H_X):
            o_ref[:, h, :] = r[:, h * DH_X:(h + 1) * DH_X]
            oh_ref[h] = r[:, h * DH_X:(h + 1) * DH_X].astype(BF16)


def mem_kv(x, wk, wv):
    m, d = x.shape
    depth = wk.shape[0]
    tm = min(m, TOKEN_TILE)
    shp = jax.ShapeDtypeStruct((depth, m, H_X, DH_X), F32)
    shp_h = jax.ShapeDtypeStruct((depth, H_X, m, DH_X), BF16)
    w_spec = pl.BlockSpec((None, d, d), lambda i, l: (l, 0, 0))
    o_spec = pl.BlockSpec((None, tm, H_X, DH_X), lambda i, l: (l, i, 0, 0))
    oh_spec = pl.BlockSpec((None, H_X, tm, DH_X), lambda i, l: (l, 0, i, 0))
    return pl.pallas_call(
        _mem_kv_kernel,
        grid=(m // tm, depth),
        in_specs=[pl.BlockSpec((tm, d), lambda i, l: (i, 0)), w_spec, w_spec],
        out_specs=[o_spec, o_spec, oh_spec, oh_spec],
        out_shape=[shp, shp, shp_h, shp_h],
        compiler_params=_cparams("parallel", "arbitrary"),
        name="mem_kv",
    )(x, wk, wv)


def _mix_out_ln_kernel(ya_ref, yb_ref, yc_ref, w_ref, x_ref, g_ref, b_ref, wq_ref, o_ref, q_ref, *, alpha):
    acc = None
    off = 0
    for p in (ya_ref, yb_ref, yc_ref):
        width = p.shape[1]
        t = jnp.dot(p[...].astype(BF16), w_ref[off:off + width, :], preferred_element_type=F32)
        acc = t if acc is None else acc + t
        off += width
    y = _layer_norm(alpha * x_ref[...] + acc, g_ref[...], b_ref[...])
    o_ref[...] = y
    q_ref[...] = jnp.dot(y.astype(BF16), wq_ref[...], preferred_element_type=F32).astype(q_ref.dtype)


def mix_out_ln(ya, yb, yc, w_out, wq, layer, x, g, b, alpha):
    n, d = x.shape
    tm = OUT_TILE
    rows = lambda width: pl.BlockSpec((tm, width), lambda i: (i, 0))
    weight = lambda w: pl.BlockSpec((None,) + w.shape[1:], lambda i: (layer, 0, 0))
    vec = pl.BlockSpec((1, d), lambda i: (0, 0))
    return pl.pallas_call(
        functools.partial(_mix_out_ln_kernel, alpha=alpha),
        grid=(n // tm,),
        in_specs=[rows(ya.shape[1]), rows(yb.shape[1]), rows(yc.shape[1]), weight(w_out), rows(d), vec, vec,
                  weight(wq)],
        out_specs=[rows(d), rows(wq.shape[2])],
        out_shape=[jax.ShapeDtypeStruct((n, d), F32),
                   jax.ShapeDtypeStruct((n, wq.shape[2]), BF16)],
        compiler_params=_cparams("parallel"),
        name="mix_out_ln",
    )(ya, yb, yc, w_out, x, g.reshape(1, d), b.reshape(1, d), wq)


def _log_sigmoid(x):
    return jnp.minimum(x, 0.0) - jnp.log1p(jnp.exp(-jnp.abs(x)))


def _mlstm_prompt_kernel(za_ref, zg_ref, g_ref, ya_ref, c1_ref, n1_ref, m1_ref, c_sc, n_sc, m_sc, *, chunk):
    L = chunk
    ti = pl.program_id(1)

    @pl.when(ti == 0)
    def _():
        c_sc[...] = jnp.zeros_like(c_sc)
        n_sc[...] = jnp.zeros_like(n_sc)
        m_sc[...] = jnp.zeros_like(m_sc)

    row = lax.broadcasted_iota(jnp.int32, (L, L), 0)
    col = lax.broadcasted_iota(jnp.int32, (L, L), 1)
    causal = col <= row
    tril = causal.astype(F32)
    sel_r = lax.broadcasted_iota(jnp.int32, (V7X_SUBLANES, V7X_LANES), 0)
    sel_c = lax.broadcasted_iota(jnp.int32, (V7X_SUBLANES, V7X_LANES), 1)
    sel = (sel_r == sel_c).astype(F32)
    lane_g = lax.broadcasted_iota(jnp.int32, (L, N_ZG), 1)
    lane_m = lax.broadcasted_iota(jnp.int32, (1, V7X_LANES), 1)
    rep = lambda col_: jnp.broadcast_to(col_, (L, V7X_LANES))
    wide = lambda x, n: _tile_lanes(x, n // V7X_LANES) if n > V7X_LANES else x[:, :n]

    gates = zg_ref[...]
    gl = jnp.where(lane_g < H_A, gates, _log_sigmoid(gates))
    bcum = jnp.dot(tril, gl, precision=lax.Precision.HIGHEST, preferred_element_type=F32)
    mixed = jnp.where(lane_g < H_A, gates, bcum)
    t_rows = lax.dot_general(sel, mixed, (((1,), (1,)), ((), ())),
                             precision=lax.Precision.HIGHEST, preferred_element_type=F32)
    outs = []
    m_out = jnp.zeros((1, V7X_LANES), F32)
    for h in range(H_A):
        q = za_ref[:, h * DH_A:(h + 1) * DH_A]
        k = za_ref[:, W_A + h * DH_A:W_A + (h + 1) * DH_A] * (DH_A ** -0.5)
        v = za_ref[:, 2 * W_A + h * DH_A:2 * W_A + (h + 1) * DH_A]
        og = za_ref[:, 3 * W_A + h * DH_A:3 * W_A + (h + 1) * DH_A]
        ig_c = rep(gates[:, h:h + 1])
        b_c = rep(bcum[:, H_A + h:H_A + h + 1])
        ig_row = t_rows[h:h + 1, :]
        b_row = t_rows[H_A + h:H_A + h + 1, :]
        c, n, m_prev = c_sc[h], n_sc[h:h + 1, :], m_sc[:, h:h + 1]

        dmat = jnp.where(causal, wide(b_c, L) - b_row + ig_row, NEG_INF)
        a = rep(jnp.max(dmat, axis=-1, keepdims=True))
        s = _bdot_nt(q, k) * jnp.exp(dmat - wide(a, L))
        n_loc = _bdot(s, v)
        d_loc = rep(jnp.sum(s, axis=-1, keepdims=True))
        a_last = a[L - 1:L, :]
        b_last = b_c[L - 1:L, :]
        wk = jnp.exp(b_last - b_c + ig_c - a_last)[:, :DH_A]
        u_loc = lax.dot_general((wk * v).astype(BF16), k.astype(BF16), (((0,), (0,)), ((), ())),
                                preferred_element_type=F32)
        nu_loc = jnp.sum(wk * k, axis=0, keepdims=True)

        inter = b_c + m_prev
        m_row = jnp.maximum(inter, a)
        r = jnp.exp(a - m_row)
        w_inter = jnp.exp(inter - m_row)
        num = r[:, :DH_A] * n_loc + w_inter[:, :DH_A] * _bdot_nt(q, c)
        den = r * d_loc + w_inter * rep(jnp.sum(q * n, axis=-1, keepdims=True))
        hh = num / jnp.maximum(jnp.abs(den), jnp.exp(-m_row))[:, :DH_A]
        m_new = m_row[L - 1:L, 0:1]
        decay = jnp.exp(b_last[:, 0:1] + m_prev - m_new)
        e_loc = jnp.exp(a_last[:, 0:1] - m_new)
        c_sc[h] = decay * c + e_loc * u_loc
        n_sc[h:h + 1, :] = decay * n + e_loc * nu_loc
        m_out = jnp.where(lane_m == h, m_new, m_out)

        hn = hh * lax.rsqrt(jnp.mean(hh * hh, axis=-1, keepdims=True) + NORM_EPS)
        outs.append(jax.nn.sigmoid(og) * hn)
    ya_ref[...] = (jnp.concatenate(outs, axis=-1) * g_ref[...]).astype(ya_ref.dtype)
    m_sc[...] = m_out

    @pl.when(ti == pl.num_programs(1) - 1)
    def _():
        c1_ref[0] = c_sc[...]
        n1_ref[0] = n_sc[...]
        m1_ref[0] = m_sc[...]


def _mlstm_sample_kernel(za_ref, zg_ref, g_ref, c0_ref, n0_ref, m0_ref, _prev, ya_ref, c1_ref, n1_ref, m1_ref,
                         *, t_len, seqs):
    R = seqs * t_len
    row = lax.broadcasted_iota(jnp.int32, (R, R), 0)
    col = lax.broadcasted_iota(jnp.int32, (R, R), 1)
    mask = (row // t_len == col // t_len) & (col <= row)
    tril = mask.astype(F32)
    sel_r = lax.broadcasted_iota(jnp.int32, (V7X_SUBLANES, V7X_LANES), 0)
    sel_c = lax.broadcasted_iota(jnp.int32, (V7X_SUBLANES, V7X_LANES), 1)
    sel = (sel_r == sel_c).astype(F32)
    lane_g = lax.broadcasted_iota(jnp.int32, (R, N_ZG), 1)
    lane_m = lax.broadcasted_iota(jnp.int32, (seqs, 1, V7X_LANES), 2)
    rep = lambda c_: jnp.broadcast_to(c_, (R, V7X_LANES))
    wide = lambda x, n: _tile_lanes(x, n // V7X_LANES) if n > V7X_LANES else x[:, :n]
    per_seq = lambda x: x.reshape(seqs, t_len, x.shape[-1])
    last_rows = lambda x: jnp.broadcast_to(per_seq(x)[:, t_len - 1:t_len, :], (seqs, t_len, x.shape[-1])
                                           ).reshape(R, x.shape[-1])
    seq_rows = lambda x: jnp.broadcast_to(x, (seqs, t_len, x.shape[-1])).reshape(R, x.shape[-1])
    state_rows = lambda x: jnp.broadcast_to(x, (seqs, DH_A, x.shape[-1])).reshape(seqs * DH_A, x.shape[-1])
    own_r = lax.broadcasted_iota(jnp.int32, (R, seqs * DH_A), 0) // t_len
    own_c = lax.broadcasted_iota(jnp.int32, (R, seqs * DH_A), 1) // DH_A
    own = own_r == own_c
    odd_seq = (lax.broadcasted_iota(jnp.int32, (R, DH_A), 0) // t_len) % 2 == 1

    gates = zg_ref[...]
    gl = jnp.where(lane_g < H_A, gates, _log_sigmoid(gates))
    bcum = jnp.dot(tril, gl, precision=lax.Precision.HIGHEST, preferred_element_type=F32)
    mixed = jnp.where(lane_g < H_A, gates, bcum)
    t_rows = lax.dot_general(sel, mixed, (((1,), (1,)), ((), ())),
                             precision=lax.Precision.HIGHEST, preferred_element_type=F32)
    m0 = m0_ref[...]
    norm_g = g_ref[...]
    outs = []
    m_out = jnp.zeros((seqs, 1, V7X_LANES), F32)
    for h in range(H_A):
        q = za_ref[:, h * DH_A:(h + 1) * DH_A]
        k = za_ref[:, W_A + h * DH_A:W_A + (h + 1) * DH_A] * (DH_A ** -0.5)
        v = za_ref[:, 2 * W_A + h * DH_A:2 * W_A + (h + 1) * DH_A]
        og = za_ref[:, 3 * W_A + h * DH_A:3 * W_A + (h + 1) * DH_A]
        ig_c = rep(gates[:, h:h + 1])
        b_c = rep(bcum[:, H_A + h:H_A + h + 1])
        ig_row = t_rows[h:h + 1, :]
        b_row = t_rows[H_A + h:H_A + h + 1, :]
        m_prev = rep(seq_rows(m0)[:, h:h + 1])
        c_stack = c0_ref[:, h].reshape(seqs * DH_A, DH_A)
        n_rows = seq_rows(n0_ref[:, h:h + 1, :])

        dmat = jnp.where(mask, wide(b_c, R) - b_row + ig_row, NEG_INF)
        a = rep(jnp.max(dmat, axis=-1, keepdims=True))
        s = _bdot_nt(q, k) * jnp.exp(dmat - wide(a, R))
        n_loc = _bdot(s, v)
        d_loc = rep(jnp.sum(s, axis=-1, keepdims=True))
        a_last = last_rows(a)
        b_last = last_rows(b_c)
        wk = jnp.exp(b_last - b_c + ig_c - a_last)[:, :DH_A]
        vw = wk * v
        vw2 = jnp.concatenate([vw, vw], axis=1)
        u_stack = lax.dot_general(jnp.where(own, _tile_lanes(vw2, seqs // 2), 0.0).astype(BF16), k.astype(BF16),
                                  (((0,), (0,)), ((), ())), preferred_element_type=F32)
        nu = jnp.sum(per_seq(wk * k), axis=1, keepdims=True)

        inter = b_c + m_prev
        m_row = jnp.maximum(inter, a)
        r = jnp.exp(a - m_row)
        w_inter = jnp.exp(inter - m_row)
        x_all = jnp.where(own, _bdot_nt(q, c_stack), 0.0)
        fold = x_all[:, 0:V7X_LANES]
        for j in range(1, seqs * DH_A // V7X_LANES):
            fold = fold + x_all[:, j * V7X_LANES:(j + 1) * V7X_LANES]
        qc = jnp.where(odd_seq, fold[:, DH_A:], fold[:, :DH_A])
        num = r[:, :DH_A] * n_loc + w_inter[:, :DH_A] * qc
        den = r * d_loc + w_inter * rep(jnp.sum(q * n_rows, axis=-1, keepdims=True))
        hh = num / jnp.maximum(jnp.abs(den), jnp.exp(-m_row))[:, :DH_A]
        hn = hh * lax.rsqrt(jnp.mean(hh * hh, axis=-1, keepdims=True) + NORM_EPS)
        outs.append(jax.nn.sigmoid(og) * hn)

        m_new = per_seq(m_row)[:, t_len - 1:t_len, :]
        decay = jnp.exp(per_seq(b_c + m_prev)[:, t_len - 1:t_len, :] - m_new)
        e_loc = jnp.exp(per_seq(a)[:, t_len - 1:t_len, :] - m_new)
        c_new = (state_rows(decay)[:, :DH_A] * c_stack + state_rows(e_loc)[:, :DH_A] * u_stack)
        c1_ref[:, h] = c_new.reshape(seqs, DH_A, DH_A)
        n1_ref[:, h:h + 1, :] = decay[:, :, :DH_A] * n0_ref[:, h:h + 1, :] + e_loc[:, :, :DH_A] * nu
        m_out = jnp.where(lane_m == h, m_new, m_out)
    ya_ref[...] = (jnp.concatenate(outs, axis=-1) * norm_g).astype(ya_ref.dtype)
    m1_ref[...] = m_out


def _mlstm_out_shapes(n_tok, nb):
    return [
        jax.ShapeDtypeStruct((n_tok, W_A), BF16),
        jax.ShapeDtypeStruct((nb, H_A, DH_A, DH_A), F32),
        jax.ShapeDtypeStruct((nb, H_A, DH_A), F32),
        jax.ShapeDtypeStruct((nb, 1, V7X_LANES), F32),
    ]


def mlstm_prompt(za, zg, norm_g, nb, t):
    L = MLSTM_PROMPT_CHUNK
    nc = t // L
    row_map = lambda b, c: (b * nc + c, 0)
    st4 = lambda b, c: (b, 0, 0, 0)
    st3 = lambda b, c: (b, 0, 0)
    ya, c1, n1, m1 = pl.pallas_call(
        functools.partial(_mlstm_prompt_kernel, chunk=L),
        grid=(nb, nc),
        in_specs=[pl.BlockSpec((L, N_ZA), row_map), pl.BlockSpec((L, N_ZG), row_map),
                  pl.BlockSpec((1, W_A), lambda b, c: (0, 0))],
        out_specs=[pl.BlockSpec((L, W_A), row_map), pl.BlockSpec((1, H_A, DH_A, DH_A), st4),
                   pl.BlockSpec((1, H_A, DH_A), st3), pl.BlockSpec((1, 1, V7X_LANES), st3)],
        out_shape=_mlstm_out_shapes(za.shape[0], nb),
        scratch_shapes=[pltpu.VMEM((H_A, DH_A, DH_A), F32), pltpu.VMEM((H_A, DH_A), F32),
                        pltpu.VMEM((1, V7X_LANES), F32)],
        compiler_params=_cparams("parallel", "arbitrary"),
        name="mlstm_prompt",
    )(za, zg, norm_g.reshape(1, W_A))
    return ya, c1, n1, m1[:, 0, :H_A]


def mlstm_sample(za, zg, norm_g, nb, t, row0, c0, n0, m0, layer, prev):
    seqs = MLSTM_UNITS
    rows = seqs * t
    row_map = lambda b: (row0 // rows + b, 0)
    st4 = lambda b: (b, 0, 0, 0)
    st3 = lambda b: (b, 0, 0)
    m0p = jnp.pad(m0, ((0, 0), (0, V7X_LANES - H_A))).reshape(nb, 1, V7X_LANES)
    state_specs = [pl.BlockSpec((seqs, H_A, DH_A, DH_A), st4), pl.BlockSpec((seqs, H_A, DH_A), st3),
                   pl.BlockSpec((seqs, 1, V7X_LANES), st3)]
    ya, c1, n1, m1 = pl.pallas_call(
        functools.partial(_mlstm_sample_kernel, t_len=t, seqs=seqs),
        grid=(nb // seqs,),
        in_specs=[pl.BlockSpec((rows, N_ZA), row_map), pl.BlockSpec((rows, N_ZG), row_map),
                  pl.BlockSpec((1, W_A), lambda b: (0, 0)),
                  pl.BlockSpec((None, seqs, H_A, DH_A, DH_A), lambda b: (layer, b, 0, 0, 0))] + state_specs[1:]
        + [pl.BlockSpec(memory_space=pl.ANY)],
        out_specs=[pl.BlockSpec((rows, W_A), row_map)] + state_specs,
        out_shape=_mlstm_out_shapes(za.shape[0], nb),
        input_output_aliases={6: 0},
        compiler_params=_cparams("arbitrary"),
        name="mlstm_sample",
    )(za, zg, norm_g.reshape(1, W_A), c0, n0, m0p, prev)
    return ya, c1, n1, m1[:, 0, :H_A]


def _diff_lambda(lam_ref, lam_init):
    lp = lam_ref[...]
    d01 = jnp.sum(lp[0:1, :] * lp[1:2, :], axis=-1, keepdims=True)
    d23 = jnp.sum(lp[2:3, :] * lp[3:4, :], axis=-1, keepdims=True)
    return jnp.exp(d01) - jnp.exp(d23) + lam_init


def _alibi_slope(h):
    return jnp.where(h == 0, 2.0 ** -2, jnp.where(h == 1, 2.0 ** -4, jnp.where(h == 2, 2.0 ** -6, 2.0 ** -8)))


def _alibi_tables(t):
    slopes = (2.0 ** (-8.0 * jnp.arange(1, H_B + 1, dtype=F32) / H_B))[:, None]
    pos = jnp.arange(t, dtype=jnp.int32)
    hi = ((pos // ALIBI_SPLIT) * ALIBI_SPLIT).astype(F32)[None, :]
    lo = (pos % ALIBI_SPLIT).astype(F32)[None, :]
    ones = jnp.ones((H_B, t), F32)
    pad = jnp.zeros((H_B, t, DV_B - 4), F32)
    aq = jnp.concatenate([jnp.stack([slopes * ones, slopes * ones, -slopes * hi, -slopes * lo], axis=-1), pad], -1)
    ak = jnp.concatenate([jnp.stack([hi * ones, lo * ones, ones, ones], axis=-1), pad], -1)
    return aq.astype(BF16), ak.astype(BF16)


def _attn_prompt_kernel(lam_ref, g_ref, q_ref, aq_ref, k_ref, ak_ref, v_ref, o_ref,
                        qs_sc, m_sc, acc_sc, *, lam_init, tq, tk, nq):
    i, j = _tri_pair(pl.program_id(1), nq)
    rq = ATTN_ROWS

    @pl.when(j == 0)
    def _():
        lane = lax.broadcasted_iota(jnp.int32, (tq, DV_B), 1)
        for h in range(H_B):
            q = q_ref[:, h * DV_B:(h + 1) * DV_B] * (DK_B ** -0.5)
            qs_sc[h, 0:tq, 0:DV_B] = jnp.where(lane < DK_B, q, 0.0).astype(BF16)
            qs_sc[h, tq:2 * tq, 0:DV_B] = jnp.where(lane >= DK_B, q, 0.0).astype(BF16)
            qs_sc[h, 0:tq, DV_B:2 * DV_B] = aq_ref[h]
            qs_sc[h, tq:2 * tq, DV_B:2 * DV_B] = aq_ref[h]
        m_sc[...] = jnp.full_like(m_sc, NEG_INF)
        acc_sc[...] = jnp.zeros_like(acc_sc)

    def step(masked):
        ones = jnp.ones((tk, V7X_LANES), BF16)
        for h in range(H_B):
            kaug = jnp.concatenate([k_ref[:, h * DV_B:(h + 1) * DV_B].astype(BF16), ak_ref[h]], axis=1)
            vaug = jnp.concatenate([v_ref[:, h * DV_B:(h + 1) * DV_B].astype(BF16), ones], axis=1)
            for r in range(2 * tq // rq):
                rows = slice(r * rq, (r + 1) * rq)
                s = lax.dot_general(qs_sc[h, rows, :], kaug, (((1,), (1,)), ((), ())), preferred_element_type=F32)
                if masked:
                    qi = (r * rq) % tq + lax.broadcasted_iota(jnp.int32, (rq, tk), 0)
                    kj = lax.broadcasted_iota(jnp.int32, (rq, tk), 1)
                    s = jnp.where(kj <= qi, s, NEG_INF)
                m_old = m_sc[h, rows, :]
                m_new = jnp.maximum(m_old, jnp.max(s, axis=-1, keepdims=True))
                alpha = jnp.exp(m_old - m_new)
                p = jnp.exp(s - _tile_lanes(m_new, tk // V7X_LANES))
                acc_sc[h, rows, :] = (_tile_lanes(alpha, 2) * acc_sc[h, rows, :]
                                      + jnp.dot(p.astype(BF16), vaug, preferred_element_type=F32))
                m_sc[h, rows, :] = m_new

    @pl.when(j < i)
    def _():
        step(False)

    @pl.when(j == i)
    def _():
        step(True)

    @pl.when(j == i)
    def _():
        lam = _diff_lambda(lam_ref, lam_init)
        for h in range(H_B):
            o0 = acc_sc[h, 0:tq, 0:DV_B] / acc_sc[h, 0:tq, DV_B:2 * DV_B]
            o1 = acc_sc[h, tq:2 * tq, 0:DV_B] / acc_sc[h, tq:2 * tq, DV_B:2 * DV_B]
            ob = o0 - lam * o1
            on = ob * lax.rsqrt(jnp.mean(ob * ob, axis=-1, keepdims=True) + NORM_EPS)
            o_ref[:, h * DV_B:(h + 1) * DV_B] = on * g_ref[:, h * DV_B:(h + 1) * DV_B] * (1.0 - lam_init)


def _tri_pair(t, n):
    i = sum((t >= k * (k + 1) // 2).astype(jnp.int32) for k in range(1, n))
    return i, t - i * (i + 1) // 2


def diff_attn_prompt(zb, aq, ak, lam_p, norm_g, nb, t, lam_init):
    n_tok = zb.shape[0]
    tq = tk = ATTN_TILE
    nq = t // tq
    q_map = lambda b, s: (b * nq + _tri_pair(s, nq)[0], 0)
    kv_map = lambda off: (lambda b, s: (b * nq + _tri_pair(s, nq)[1], off))
    return pl.pallas_call(
        functools.partial(_attn_prompt_kernel, lam_init=lam_init, tq=tq, tk=tk, nq=nq),
        grid=(nb, nq * (nq + 1) // 2),
        in_specs=[
            pl.BlockSpec(lam_p.shape, lambda b, s: (0, 0)),
            pl.BlockSpec((1, W_B), lambda b, s: (0, 0)),
            pl.BlockSpec((tq, W_B), q_map),
            pl.BlockSpec((H_B, tq, DV_B), lambda b, s: (0, _tri_pair(s, nq)[0], 0)),
            pl.BlockSpec((tk, W_B), kv_map(1)),
            pl.BlockSpec((H_B, tk, DV_B), lambda b, s: (0, _tri_pair(s, nq)[1], 0)),
            pl.BlockSpec((tk, W_B), kv_map(2)),
        ],
        out_specs=pl.BlockSpec((tq, W_B), q_map),
        out_shape=jax.ShapeDtypeStruct((n_tok, W_B), F32),
        scratch_shapes=[pltpu.VMEM((H_B, 2 * tq, 2 * DV_B), BF16), pltpu.VMEM((H_B, 2 * tq, V7X_LANES), F32),
                        pltpu.VMEM((H_B, 2 * tq, 2 * DV_B), F32)],
        compiler_params=_cparams("parallel", "arbitrary"),
        name="diff_attn_prompt",
    )(lam_p, norm_g.reshape(1, W_B), zb, aq, zb, ak, zb)


def _attn_sample_kernel(*refs, lam_init, past_len, t_new, pages_per_step, seqs):
    G = pages_per_step
    pt_ref, lam_ref, g_ref, q_ref, kn_ref, vn_ref = refs[:6]
    k_refs = refs[6:6 + seqs * G]
    v_refs = refs[6 + seqs * G:6 + 2 * seqs * G]
    _prev, o_ref, qs_sc, bias_sc, m_sc, l_sc, acc_sc = refs[6 + 2 * seqs * G:]
    del pt_ref
    ps = pl.program_id(1)
    rows_per_head = 2 * t_new
    n_rows = H_B * rows_per_head
    page_rows = PAGE_SIZE * H_B
    reps = page_rows // V7X_LANES

    r_lane = lax.broadcasted_iota(jnp.int32, (n_rows, V7X_LANES), 0)
    slope = _alibi_slope(r_lane // rows_per_head).astype(F32)

    @pl.when(ps == 0)
    def _():
        lane = lax.broadcasted_iota(jnp.int32, (t_new, DV_B), 1)
        for u in range(seqs):
            q = q_ref[u * t_new:(u + 1) * t_new, :] * (DK_B ** -0.5)
            for h in range(H_B):
                qh = q[:, h * DV_B:(h + 1) * DV_B]
                qs_sc[u, h * rows_per_head:(h + 1) * rows_per_head, :] = jnp.concatenate(
                    [jnp.where(lane < DK_B, qh, 0.0), jnp.where(lane >= DK_B, qh, 0.0)], axis=0).astype(BF16)
        rr = lax.broadcasted_iota(jnp.int32, (n_rows, page_rows), 0)
        cc = lax.broadcasted_iota(jnp.int32, (n_rows, page_rows), 1)
        rel = cc // H_B - (past_len + rr % t_new)
        bias = _alibi_slope(rr // rows_per_head).astype(F32) * rel.astype(F32)
        bias_sc[...] = jnp.where(cc % H_B == rr // rows_per_head, bias, NEG_INF)
        m_sc[...] = jnp.full_like(m_sc, NEG_INF)
        l_sc[...] = jnp.zeros_like(l_sc)
        acc_sc[...] = jnp.zeros_like(acc_sc)

    lam = _diff_lambda(lam_ref, lam_init)
    for u in range(seqs):
        qs = qs_sc[u]
        m, l, acc = m_sc[u], l_sc[u], acc_sc[u]
        s_pages = []
        m_new = m
        for g in range(G):
            base = ((ps * G + g) * PAGE_SIZE).astype(F32)
            s = _bdot_nt(qs, k_refs[u * G + g][...]) + (bias_sc[...] + _tile_lanes(slope * base, reps))
            m_new = jnp.maximum(m_new, jnp.max(s, axis=-1, keepdims=True))
            s_pages.append(s)
        alpha = jnp.exp(m - m_new)
        l = alpha * l
        acc = alpha * acc
        for g in range(G):
            p = jnp.exp(s_pages[g] - _tile_lanes(m_new, reps))
            l = l + jnp.sum(p, axis=-1, keepdims=True)
            acc = acc + _bdot(p, v_refs[u * G + g][...])
        m = m_new

        @pl.when(ps < pl.num_programs(1) - 1)
        def _(u=u, m=m, l=l, acc=acc):
            m_sc[u] = m
            l_sc[u] = l
            acc_sc[u] = acc

        @pl.when(ps == pl.num_programs(1) - 1)
        def _(u=u, qs=qs, m=m, l=l, acc=acc):
            rnd = lambda a: a.astype(BF16).astype(F32)
            kn = rnd(kn_ref[u * t_new:(u + 1) * t_new, :])
            vn = rnd(vn_ref[u * t_new:(u + 1) * t_new, :])
            qf = qs.astype(F32)
            sn = jnp.concatenate(
                [lax.dot_general(qf[h * rows_per_head:(h + 1) * rows_per_head, :], kn[:, h * DV_B:(h + 1) * DV_B],
                                 (((1,), (1,)), ((), ())), preferred_element_type=F32) for h in range(H_B)],
                axis=0)
            rr = lax.broadcasted_iota(jnp.int32, (n_rows, t_new), 0)
            rel = lax.broadcasted_iota(jnp.int32, (n_rows, t_new), 1) - rr % t_new
            sn = jnp.where(rel <= 0, sn + _alibi_slope(rr // rows_per_head).astype(F32) * rel.astype(F32), NEG_INF)
            m_fin = jnp.maximum(m, jnp.max(sn, axis=-1, keepdims=True))
            alpha = jnp.exp(m - m_fin)
            pn = rnd(jnp.exp(sn - m_fin[:, 0:1]))
            l_fin = alpha * l + jnp.sum(pn, axis=-1, keepdims=True)
            pv = jnp.concatenate(
                [jnp.dot(pn[h * rows_per_head:(h + 1) * rows_per_head, :], vn[:, h * DV_B:(h + 1) * DV_B],
                         preferred_element_type=F32) for h in range(H_B)], axis=0)
            o = (alpha * acc + pv) / l_fin

            outs = []
            for h in range(H_B):
                o0 = o[h * rows_per_head:h * rows_per_head + t_new, :]
                o1 = o[h * rows_per_head + t_new:(h + 1) * rows_per_head, :]
                ob = o0 - lam * o1
                outs.append(ob * lax.rsqrt(jnp.mean(ob * ob, axis=-1, keepdims=True) + NORM_EPS))
            o_ref[u * t_new:(u + 1) * t_new, :] = jnp.concatenate(outs, axis=-1) * g_ref[...] * (1.0 - lam_init)


def diff_attn_sample(zb, cache_k, cache_v, layer, page_table, lam_p, norm_g, nb, t_new, row0, lam_init, prev):
    n_tok = zb.shape[0]
    n_pages = page_table.shape[1]
    G = PAGES_PER_STEP
    S = ATTN_SAMPLE_SEQS
    rows = S * t_new
    rb0 = row0 // rows
    n_rows = 2 * H_B * t_new
    page_rows = PAGE_SIZE * H_B
    ck = cache_k.reshape(cache_k.shape[0], cache_k.shape[1], page_rows, DV_B)
    cv = cache_v.reshape(cache_v.shape[0], cache_v.shape[1], page_rows, DV_B)

    def page_map(u, g):
        return lambda b, p, pt: (layer, pt[b * S + u, p * G + g], 0, 0)

    page_blk = (None, None, page_rows, DV_B)
    page_specs = [pl.BlockSpec(page_blk, page_map(u, g)) for u in range(S) for g in range(G)]
    in_specs = [
        pl.BlockSpec(lam_p.shape, lambda b, p, pt: (0, 0)),
        pl.BlockSpec((1, W_B), lambda b, p, pt: (0, 0)),
        pl.BlockSpec((rows, W_B), lambda b, p, pt: (rb0 + b, 0)),
        pl.BlockSpec((rows, W_B), lambda b, p, pt: (rb0 + b, 1)),
        pl.BlockSpec((rows, W_B), lambda b, p, pt: (rb0 + b, 2)),
    ] + page_specs + page_specs + [pl.BlockSpec(memory_space=pl.ANY)]
    args = [page_table, lam_p, norm_g.reshape(1, W_B), zb, zb, zb] + [ck] * (S * G) + [cv] * (S * G) + [prev]
    grid_spec = pltpu.PrefetchScalarGridSpec(
        num_scalar_prefetch=1,
        grid=(nb // S, n_pages // G),
        in_specs=in_specs,
        out_specs=pl.BlockSpec((rows, W_B), lambda b, p, pt: (rb0 + b, 0)),
        scratch_shapes=[pltpu.VMEM((S, n_rows, DV_B), BF16), pltpu.VMEM((n_rows, page_rows), F32),
                        pltpu.VMEM((S, n_rows, V7X_LANES), F32), pltpu.VMEM((S, n_rows, V7X_LANES), F32),
                        pltpu.VMEM((S, n_rows, DV_B), F32)],
    )
    return pl.pallas_call(
        functools.partial(_attn_sample_kernel, lam_init=lam_init, past_len=n_pages * PAGE_SIZE,
                          t_new=t_new, pages_per_step=G, seqs=S),
        grid_spec=grid_spec,
        out_shape=jax.ShapeDtypeStruct((n_tok, W_B), F32),
        input_output_aliases={len(args) - 1: 0},
        compiler_params=_cparams("parallel", "arbitrary"),
        name="diff_attn_sample",
    )(*args)


def _s5_disc_kernel(are_ref, aim_ref, ldt_ref, bre_ref, bim_ref, lre_ref, lim_ref, bbre_ref, bbim_ref):
    a_re = are_ref[...]
    a_im = aim_ref[...]
    dt = jnp.exp(ldt_ref[...])
    mag = jnp.exp(a_re * dt)
    lb_re = mag * jnp.cos(a_im * dt)
    lb_im = mag * jnp.sin(a_im * dt)
    den = a_re * a_re + a_im * a_im
    xr = lb_re - 1.0
    fr = (xr * a_re + lb_im * a_im) / den
    fi = (lb_im * a_re - xr * a_im) / den
    lre_ref[...] = lb_re
    lim_ref[...] = lb_im
    b_re = bre_ref[...]
    b_im = bim_ref[...]
    bbre_ref[...] = fr * b_re - fi * b_im
    bbim_ref[...] = fr * b_im + fi * b_re


def s5_discretize(a_re, a_im, log_dt, b_re, b_im):
    depth = a_re.shape[0]
    gp = G_C * P_C
    flat = lambda a: a.reshape(depth, 1, gp)
    ldt = jnp.broadcast_to(log_dt[:, :, None], (depth, G_C, P_C)).reshape(depth, 1, gp)
    tr = lambda b: jnp.transpose(b, (0, 3, 1, 2)).reshape(depth, GC, gp)
    shp1 = jax.ShapeDtypeStruct((depth, 1, gp), F32)
    shpb = jax.ShapeDtypeStruct((depth, GC, gp), F32)
    return pl.pallas_call(_s5_disc_kernel, out_shape=[shp1, shp1, shpb, shpb], name="s5_discretize")(
        flat(a_re), flat(a_im), ldt, tr(b_re), tr(b_im))


def _block_diag_in(bb):
    depth, _, gp = bb.shape
    tiled = jnp.tile(bb, (1, G_C, 1)).reshape(depth, G_C, GC, gp)
    grp_r = jnp.arange(G_C)[:, None, None]
    grp_c = (jnp.arange(gp) // P_C)[None, None, :]
    return jnp.where(grp_r == grp_c, tiled, 0.0).reshape(depth, G_C * GC, gp)


def _block_diag_out(c):
    depth = c.shape[0]
    ct = jnp.transpose(c, (0, 1, 3, 2)).reshape(depth, G_C * P_C, GC)
    tiled = jnp.tile(ct, (1, 1, G_C))
    grp_r = (jnp.arange(G_C * P_C) // P_C)[:, None]
    grp_c = (jnp.arange(G_C * GC) // GC)[None, :]
    return jnp.where(grp_r == grp_c, tiled, 0.0)


def _cmul_add(a_re, a_im, x_re, x_im, y_re, y_im):
    return y_re + (a_re * x_re - a_im * x_im), y_im + (a_re * x_im + a_im * x_re)


def _block_scan(x_re, x_im, pw):
    for d, (a_re, a_im) in zip((1, 2, 4), pw):
        x_re, x_im = _cmul_add(a_re, a_im, pltpu.roll(x_re, d, 0), pltpu.roll(x_im, d, 0), x_re, x_im)
    return x_re, x_im


def _s5_kernel(*refs, rows, independent, aliased):
    (u_ref, lre_ref, lim_ref, bb_ref, cc_ref, d_ref, gw_ref, gb_ref, s0re_ref, s0im_ref) = refs[:10]
    rest = refs[10:]
    if aliased:
        rest = rest[1:]
    y_ref, s1re_ref, s1im_ref, st_sc, car_sc = rest
    gp = G_C * P_C
    nblk = rows // V7X_SUBLANES
    ti = pl.program_id(1)

    lam_re = lre_ref[...]
    lam_im = lim_ref[...]
    l2_re, l2_im = lam_re * lam_re - lam_im * lam_im, 2.0 * lam_re * lam_im
    l4_re, l4_im = l2_re * l2_re - l2_im * l2_im, 2.0 * l2_re * l2_im
    row8 = lax.broadcasted_iota(jnp.int32, (V7X_SUBLANES, gp), 0)
    pw = tuple((jnp.where(row8 >= d, a_re, 0.0), jnp.where(row8 >= d, a_im, 0.0))
               for d, (a_re, a_im) in zip((1, 2, 4), ((lam_re, lam_im), (l2_re, l2_im), (l4_re, l4_im))))
    pk_re, pk_im = _block_scan(jnp.where(row8 == 0, lam_re, 0.0), jnp.where(row8 == 0, lam_im, 0.0), pw)

    u = u_ref[...]
    st_sc[...] = _bdot(u, bb_ref[...])

    if not independent:
        @pl.when(ti == 0)
        def _():
            car_sc[0:1, :] = s0re_ref[...]
            car_sc[1:2, :] = s0im_ref[...]

    def body(bi, carry):
        r0 = pl.multiple_of(bi * V7X_SUBLANES, V7X_SUBLANES)
        x_re = st_sc[pl.ds(r0, V7X_SUBLANES), 0:gp]
        x_im = st_sc[pl.ds(r0, V7X_SUBLANES), gp:2 * gp]
        x_re, x_im = _block_scan(x_re, x_im, pw)
        if independent:
            c_re = s0re_ref[pl.ds(bi, 1), :]
            c_im = s0im_ref[pl.ds(bi, 1), :]
        else:
            c_re, c_im = carry
        s_re, s_im = _cmul_add(pk_re, pk_im, c_re, c_im, x_re, x_im)
        st_sc[pl.ds(r0, V7X_SUBLANES), 0:gp] = s_re
        st_sc[pl.ds(r0, V7X_SUBLANES), gp:2 * gp] = s_im
        last_re = s_re[V7X_SUBLANES - 1:V7X_SUBLANES, :]
        last_im = s_im[V7X_SUBLANES - 1:V7X_SUBLANES, :]
        if independent:
            s1re_ref[pl.ds(bi, 1), :] = last_re
            s1im_ref[pl.ds(bi, 1), :] = last_im
            return carry
        return last_re, last_im

    if independent:
        lax.fori_loop(0, nblk, body, 0)
    else:
        c_re, c_im = lax.fori_loop(0, nblk, body, (car_sc[0:1, :], car_sc[1:2, :]))
        car_sc[0:1, :] = c_re
        car_sc[1:2, :] = c_im

        @pl.when(ti == pl.num_programs(1) - 1)
        def _():
            s1re_ref[...] = c_re
            s1im_ref[...] = c_im

    y = _bdot(st_sc[...], cc_ref[...]) + d_ref[...] * u
    z = _bdot(jax.nn.gelu(y), gw_ref[...]) + gb_ref[...]
    y_ref[...] = (z[:, :W_C] * jax.nn.sigmoid(z[:, W_C:])).astype(y_ref.dtype)


def s5_mixer(zc, lam_re, lam_im, bb, cc, layer, d, glu_w, glu_b, s0_re, s0_im, nb, t, row0, prev=None):
    n_tok = zc.shape[0]
    gp = G_C * P_C
    independent = t == V7X_SUBLANES
    aliased = prev is not None
    if independent:
        seqs = S5_SAMPLE_SEQS
        rows = seqs * t
        grid = (nb // seqs, 1)
        st_spec = pl.BlockSpec((seqs, gp), lambda b, i: (b, 0))
    else:
        rows = S5_TILE
        grid = (nb, t // rows)
        st_spec = pl.BlockSpec((1, gp), lambda b, i: (b, 0))
    nt = grid[1]
    rb0 = row0 // rows
    row_map = lambda b, i: (rb0 + b * nt + i, 0)
    const = lambda b, i: (0, 0)
    by_layer = lambda b, i: (layer, 0, 0)
    in_specs = [
        pl.BlockSpec((rows, W_C), row_map),
        pl.BlockSpec((None, 1, gp), by_layer),
        pl.BlockSpec((None, 1, gp), by_layer),
        pl.BlockSpec((None,) + bb.shape[1:], by_layer),
        pl.BlockSpec((None,) + cc.shape[1:], by_layer),
        pl.BlockSpec((1, W_C), const),
        pl.BlockSpec((None,) + glu_w.shape[1:], by_layer),
        pl.BlockSpec((1, 2 * W_C), const),
        st_spec,
        st_spec,
    ]
    args = [zc, lam_re, lam_im, bb, cc, d.reshape(1, W_C), glu_w, glu_b.reshape(1, 2 * W_C), s0_re, s0_im]
    aliases = {}
    if aliased:
        in_specs.append(pl.BlockSpec(memory_space=pl.ANY))
        args.append(prev)
        aliases = {len(args) - 1: 0}
    if independent:
        s1_shape = jax.ShapeDtypeStruct((nb, gp), F32)
    else:
        s1_shape = jax.ShapeDtypeStruct((nb, 1, gp), F32)
        st_out = pl.BlockSpec((None, 1, gp), lambda b, i: (b, 0, 0))
    out_specs = [pl.BlockSpec((rows, W_C), row_map)] + ([st_spec, st_spec] if independent else [st_out, st_out])
    if not independent:
        args[8] = s0_re.reshape(nb, 1, gp)
        args[9] = s0_im.reshape(nb, 1, gp)
        in_specs[8] = in_specs[9] = pl.BlockSpec((None, 1, gp), lambda b, i: (b, 0, 0))
    yc, s1_re, s1_im = pl.pallas_call(
        functools.partial(_s5_kernel, rows=rows, independent=independent, aliased=aliased),
        grid=grid,
        in_specs=in_specs,
        out_specs=out_specs,
        out_shape=[jax.ShapeDtypeStruct((n_tok, W_C), BF16), s1_shape, s1_shape],
        scratch_shapes=[pltpu.VMEM((rows, 2 * gp), F32), pltpu.VMEM((V7X_SUBLANES, gp), F32)],
        input_output_aliases=aliases,
        compiler_params=_cparams("parallel", "arbitrary"),
        name="s5_sample" if independent else "s5_prompt",
    )(*args)
    return yc, s1_re.reshape(nb, G_C, P_C), s1_im.reshape(nb, G_C, P_C)


def _xattn_kernel(q_ref, mk_ref, mv_ref, o_ref):
    q = q_ref[...] * (DH_X ** -0.5)
    outs = []
    for h in range(H_X):
        s = _bdot_nt(q[:, h * DH_X:(h + 1) * DH_X], mk_ref[h])
        m = jnp.max(s, axis=-1, keepdims=True)
        p = jnp.exp(s - m)
        l = jnp.sum(p, axis=-1, keepdims=True)
        outs.append(_bdot(p, mv_ref[h]) / l)
    o_ref[...] = jnp.concatenate(outs, axis=-1).astype(o_ref.dtype)


def _xattn_native_kernel(q_ref, mk_ref, mv_ref, _prev, o_ref, bias_sc, *, tq, seqs, n_mem):
    rows = seqs * tq
    n_keys = seqs * n_mem * H_X

    @pl.when(pl.program_id(0) == 0)
    def _():
        rr = lax.broadcasted_iota(jnp.int32, (H_X * rows, n_keys), 0)
        cc = lax.broadcasted_iota(jnp.int32, (H_X * rows, n_keys), 1)
        own = (cc // (n_mem * H_X) == (rr % rows) // tq) & (cc % H_X == rr // rows)
        bias_sc[...] = jnp.where(own, 0.0, NEG_INF)

    q = q_ref[...] * (DH_X ** -0.5)
    q_all = jnp.concatenate([q[:, h * DH_X:(h + 1) * DH_X] for h in range(H_X)], axis=0)
    k2 = mk_ref[...].reshape(n_keys, DH_X)
    v2 = mv_ref[...].reshape(n_keys, DH_X)
    s = _bdot_nt(q_all, k2) + bias_sc[...]
    m = jnp.max(s, axis=-1, keepdims=True)
    p = jnp.exp(s - m)
    l = jnp.sum(p, axis=-1, keepdims=True)
    o = _bdot(p, v2) / l
    o_ref[...] = jnp.concatenate([o[h * rows:(h + 1) * rows, :] for h in range(H_X)], axis=-1).astype(o_ref.dtype)


def cross_attn_native(qx, mem_k, mem_v, layer, nb, t, row0, prev):
    n_tok, d = qx.shape
    n_mem = mem_k.shape[2]
    seqs = XATTN_SAMPLE_SEQS
    rows = seqs * t
    rb0 = row0 // rows
    mem_spec = pl.BlockSpec((None, seqs, n_mem, H_X, DH_X), lambda b: (layer, b, 0, 0, 0))
    return pl.pallas_call(
        functools.partial(_xattn_native_kernel, tq=t, seqs=seqs, n_mem=n_mem),
        grid=(nb // seqs,),
        in_specs=[pl.BlockSpec((rows, d), lambda b: (rb0 + b, 0)), mem_spec, mem_spec,
                  pl.BlockSpec(memory_space=pl.ANY)],
        out_specs=pl.BlockSpec((rows, d), lambda b: (rb0 + b, 0)),
        out_shape=jax.ShapeDtypeStruct((n_tok, d), prev.dtype),
        scratch_shapes=[pltpu.VMEM((H_X * rows, seqs * n_mem * H_X), F32)],
        input_output_aliases={3: 0},
        compiler_params=_cparams("arbitrary"),
        name="cross_attn_sample",
    )(qx, mem_k, mem_v, prev)


def cross_attn(qx, mk, mv, layer, n_mem, nb, t, out_dtype):
    n_tok, d = qx.shape
    tq = XATTN_TILE
    nq = t // tq
    mem_spec = pl.BlockSpec((None, H_X, n_mem, DH_X), lambda b, i: (layer, 0, b, 0))
    return pl.pallas_call(
        _xattn_kernel,
        grid=(nb, nq),
        in_specs=[pl.BlockSpec((tq, d), lambda b, i: (b * nq + i, 0)), mem_spec, mem_spec],
        out_specs=pl.BlockSpec((tq, d), lambda b, i: (b * nq + i, 0)),
        out_shape=jax.ShapeDtypeStruct((n_tok, d), out_dtype),
        compiler_params=_cparams("parallel", "arbitrary"),
        name="cross_attn_prompt",
    )(qx, mk, mv)


def kernel(x_prompt, x_sample, mem_prompt, cache_k, cache_v, page_table, cache_mem_k, cache_mem_v,
           state_mlstm_c, state_mlstm_n, state_mlstm_m, state_ssm_re, state_ssm_im,
           ln_g, ln_b, ffn1_wg, ffn1_wu, ffn1_wd, ffn2_wg, ffn2_wu, ffn2_wd, w_in, b_in,
           mlstm_norm_g, diff_lam, diff_norm_g, ssm_a_re, ssm_a_im, ssm_log_dt, ssm_b_re, ssm_b_im,
           ssm_c_re, ssm_c_im, ssm_d, ssm_glu_w, ssm_glu_b, w_out, cross_wq, cross_wk, cross_wv, cross_wo):
    bp, tp, d = x_prompt.shape
    bs, ts, _ = x_sample.shape
    depth = ln_g.shape[0]
    n_mem = mem_prompt.shape[1]
    n_p = bp * tp
    gp = G_C * P_C
    alpha = (2.0 * depth) ** 0.25

    cast = lambda w: w.astype(BF16)
    ffn1_wg, ffn1_wu, ffn1_wd = cast(ffn1_wg), cast(ffn1_wu), cast(ffn1_wd)
    ffn2_wg, ffn2_wu, ffn2_wd = cast(ffn2_wg), cast(ffn2_wu), cast(ffn2_wd)
    w_out_b, wq_b, wo_b, glu_w_b = cast(w_out), cast(cross_wq), cast(cross_wo), cast(ssm_glu_w)
    w_in_p, b_in_p = _pack_w_in(w_in, b_in)
    lam_re, lam_im, bb_re, bb_im = s5_discretize(ssm_a_re, ssm_a_im, ssm_log_dt, ssm_b_re, ssm_b_im)
    bb = jnp.concatenate([_block_diag_in(bb_re), _block_diag_in(bb_im)], axis=-1).astype(BF16)
    cc = jnp.concatenate([_block_diag_out(ssm_c_re), -_block_diag_out(ssm_c_im)], axis=1).astype(BF16)
    alibi_q, alibi_k = _alibi_tables(tp)

    p_mk, p_mv, p_mkh, p_mvh = mem_kv(mem_prompt.reshape(bp * n_mem, d), cast(cross_wk), cast(cross_wv))
    p_mk = p_mk.reshape(depth, bp, n_mem, H_X, DH_X)
    p_mv = p_mv.reshape(depth, bp, n_mem, H_X, DH_X)

    x = None
    zeros_s = jnp.zeros((bp, gp), F32)
    p_k = jnp.zeros((depth, n_p, H_B, DV_B), F32)
    p_v = jnp.zeros((depth, n_p, H_B, DV_B), F32)
    p_st, s_st = [], []
    for l in range(depth):
        lam_init = 0.8 - 0.6 * math.exp(-0.3 * l)
        srcs = [x_prompt.reshape(n_p, d), x_sample.reshape(bs * ts, d)] if l == 0 else [x]
        x = ffn_ln(srcs, ffn1_wg, ffn1_wu, ffn1_wd, l, ln_g[l, 0], ln_b[l, 0], alpha)
        za, zb, zc, zg, p_k, p_v, s_k, s_v = proj_in(x, w_in_p, b_in_p, p_k, p_v, l, n_p)

        ya, pc, pn, pm = mlstm_prompt(za, zg, mlstm_norm_g[l], bp, tp)
        ya, sc, sn, sm = mlstm_sample(za, zg, mlstm_norm_g[l], bs, ts, n_p,
                                      state_mlstm_c, state_mlstm_n[l], state_mlstm_m[l], l, ya)

        yb = diff_attn_prompt(zb, alibi_q, alibi_k, diff_lam[l], diff_norm_g[l], bp, tp, lam_init)
        yb = diff_attn_sample(zb, cache_k, cache_v, l, page_table, diff_lam[l], diff_norm_g[l],
                              bs, ts, n_p, lam_init, yb)

        s5_args = (lam_re, lam_im, bb, cc, l, ssm_d[l], glu_w_b, ssm_glu_b[l])
        yc, psr, psi = s5_mixer(zc, *s5_args, zeros_s, zeros_s, bp, tp, 0)
        yc, ssr, ssi = s5_mixer(zc, *s5_args, state_ssm_re[l].reshape(bs, gp), state_ssm_im[l].reshape(bs, gp),
                                bs, ts, n_p, prev=yc)

        x, qx = mix_out_ln(ya, yb, yc, w_out_b, wq_b, l, x, ln_g[l, 1], ln_b[l, 1], alpha)
        o = cross_attn(qx, p_mkh, p_mvh, l, n_mem, bp, tp, BF16)
        o = cross_attn_native(qx, cache_mem_k, cache_mem_v, l, bs, ts, n_p, o)
        x = out_ffn_ln(o, wo_b, ffn2_wg, ffn2_wu, ffn2_wd, l, x, ln_g[l, 2], ln_b[l, 2], ln_g[l, 3], ln_b[l, 3], alpha,
                       split_rows=n_p if l == depth - 1 else None)

        p_st.append((pc, pn, pm, psr, psi))
        s_st.append((s_k.reshape(bs, ts, H_B, DV_B), s_v.reshape(bs, ts, H_B, DV_B), sc, sn, sm, ssr, ssi))

    p_c, p_n, p_m, p_sr, p_si = [jnp.stack(a) for a in zip(*p_st)]
    s_k, s_v, s_c, s_n, s_m, s_sr, s_si = [jnp.stack(a) for a in zip(*s_st)]
    p_k = p_k.reshape(depth, bp, tp, H_B, DV_B)
    p_v = p_v.reshape(depth, bp, tp, H_B, DV_B)
    yp = x[0].reshape(bp, tp, d)
    ys = x[1].reshape(bs, ts, d)
    return (yp, ys, p_k, p_v, p_mk, p_mv, p_c, p_n, p_m, p_sr, p_si, s_k, s_v, s_c, s_n, s_m, s_sr, s_si)
```
